```python
import math
import jax
import jax.numpy as jnp
from jax import lax
import numpy as np

D_MODEL = 2048
BATCH = 2
SEQ = 4096
DEPTH = 1
DEC_BATCH = 8
DEC_SEQ = 4
PAST_LEN = 16384
PAGE_SIZE = 128

HEAD_DIM = 128
N_HEADS = D_MODEL // HEAD_DIM
NSA_HEADS = N_HEADS // 2
FOX_HEADS = N_HEADS - NSA_HEADS
NSA_KV_HEADS = 4
NSA_GROUP = NSA_HEADS // NSA_KV_HEADS
NSA_WIDTH = NSA_HEADS * HEAD_DIM
FOX_WIDTH = FOX_HEADS * HEAD_DIM
CMP_LEN = 32
CMP_STRIDE = 16
CMP_RATIO = CMP_LEN // CMP_STRIDE
CMP_HIDDEN = 512
SEL_BLOCK = 64
SEL_TOPK = 16
WINDOW = 512
QBLOCK = 128
SEL_QBLOCK = 64
REL_BUCKETS = 32
REL_MAX_DIST = 128
D_FF = 5632
RMS_EPS = 1e-6
FORGET_BIAS_INIT = 3.0

OFF_KV_NSA = NSA_HEADS * HEAD_DIM
OFF_GATE = OFF_KV_NSA + 6 * NSA_KV_HEADS * HEAD_DIM
OFF_FOX = OFF_GATE + 3 * NSA_HEADS
OFF_FORGET = OFF_FOX + 3 * FOX_HEADS * HEAD_DIM
IN_COLS = OFF_FORGET + FOX_HEADS

kernel_name = "nsa_fox_parallel_heads_macaron_step"


def rms_norm(x, g):
    xf = x.astype(jnp.float32)
    xf = xf * lax.rsqrt(jnp.mean(jnp.square(xf), axis=-1, keepdims=True) + RMS_EPS)
    return (xf * g.astype(jnp.float32)).astype(x.dtype)


def swiglu(x, w_gate, w_up, w_down):
    return (jax.nn.silu(x @ w_gate) * (x @ w_up)) @ w_down


def masked_softmax(s, mask):
    s = jnp.where(mask, s, -jnp.inf)
    m = jnp.max(s, axis=-1, keepdims=True)
    m = jnp.where(jnp.isfinite(m), m, 0.0)
    e = jnp.exp(s - m)
    return e / jnp.maximum(jnp.sum(e, axis=-1, keepdims=True), 1e-30)


def rel_bucket(dist):
    n = jnp.maximum(dist, 0)
    max_exact = REL_BUCKETS // 2
    nf = jnp.maximum(n, 1).astype(jnp.float32)
    large = max_exact + (jnp.log(nf / max_exact) / math.log(REL_MAX_DIST / max_exact)
                         * (REL_BUCKETS - max_exact)).astype(jnp.int32)
    large = jnp.minimum(large, REL_BUCKETS - 1)
    return jnp.where(n < max_exact, n, large)


def _split_q(a, qb, axis):
    shp = a.shape
    a = a.reshape(shp[:axis] + (shp[axis] // qb, qb) + shp[axis + 1:])
    return jnp.moveaxis(a, axis, 0)


def _merge_q(a, axis):
    a = jnp.moveaxis(a, 0, axis)
    shp = a.shape
    return a.reshape(shp[:axis] + (shp[axis] * shp[axis + 1],) + shp[axis + 2:])


def compress_blocks(rows, pos_emb, w1, w2):
    B, L, G, D = rows.shape
    n_cmp = (L - CMP_LEN) // CMP_STRIDE + 1
    n_chunk = n_cmp + CMP_RATIO - 1
    chunks = rows[:, :n_chunk * CMP_STRIDE].reshape(B, n_chunk, CMP_STRIDE, G, D)
    pe = pos_emb.reshape(CMP_RATIO, CMP_STRIDE, 1, D)
    w1r = w1.reshape(CMP_RATIO, CMP_STRIDE, D, CMP_HIDDEN)
    hid = jnp.einsum("bnsgd,sdh->bngh", chunks[:, 0:n_cmp] + pe[0], w1r[0])
    for r in range(1, CMP_RATIO):
        hid = hid + jnp.einsum("bnsgd,sdh->bngh", chunks[:, r:r + n_cmp] + pe[r], w1r[r])
    return jax.nn.silu(hid) @ w2


def cmp_to_sel_matrix(n_cmp, n_sel):
    c0 = np.arange(n_cmp)[:, None] * CMP_STRIDE
    s0 = np.arange(n_sel)[None, :] * SEL_BLOCK
    inter = np.clip(np.minimum(c0 + CMP_LEN, s0 + SEL_BLOCK) - np.maximum(c0, s0), 0, None)
    return jnp.asarray(inter / CMP_LEN, dtype=jnp.float32)


def nsa_attention(q, gates, nsa_full, win_pad, q_pos, rel_table, k_norm,
                  cmp_pos_k, cmp_w1_k, cmp_w2_k, cmp_pos_v, cmp_w1_v, cmp_w2_v):
    B, T = q.shape[:2]
    L = nsa_full.shape[1]
    G, Hg = NSA_KV_HEADS, NSA_GROUP
    scale = HEAD_DIM ** -0.5
    qg = q.reshape(B, T, G, Hg, HEAD_DIM)
    tbl = rel_table.reshape(REL_BUCKETS, G, Hg)

    kc = rms_norm(compress_blocks(nsa_full[:, :, 0], cmp_pos_k, cmp_w1_k, cmp_w2_k), k_norm)
    vc = compress_blocks(nsa_full[:, :, 1], cmp_pos_v, cmp_w1_v, cmp_w2_v)
    n_cmp = kc.shape[1]
    blk_end = jnp.arange(n_cmp, dtype=jnp.int32) * CMP_STRIDE + (CMP_LEN - 1)
    d_cmp = q_pos[:, None] - blk_end[None, :]
    bias = tbl[rel_bucket(d_cmp)].astype(jnp.float32).transpose(2, 3, 0, 1)
    s = jnp.einsum("btghd,bngd->bghtn", qg, kc).astype(jnp.float32) * scale + bias
    p_cmp = masked_softmax(s, d_cmp >= 0)
    o_cmp = jnp.einsum("bghtn,bngd->btghd", p_cmp.astype(vc.dtype), vc)

    n_sel = -(-L // SEL_BLOCK)
    imp = jnp.einsum("bghtn,nj->bgtj", p_cmp, cmp_to_sel_matrix(n_cmp, n_sel))
    blk_q = (q_pos // SEL_BLOCK)[:, None]
    j = jnp.arange(n_sel, dtype=jnp.int32)[None, :]
    forced = (j == 0) | (j == blk_q) | (j == blk_q - 1)
    score = jnp.where(forced, jnp.inf, jnp.where(j <= blk_q, imp, -jnp.inf))
    _, sel_idx = lax.top_k(score, min(SEL_TOPK, n_sel))

    pad = n_sel * SEL_BLOCK - L

    def blockify(rows):
        rows = jnp.pad(rows, ((0, 0), (0, pad), (0, 0), (0, 0)))
        return rows.reshape(B, n_sel, SEL_BLOCK, G, HEAD_DIM).transpose(0, 3, 1, 2, 4)

    ks = blockify(nsa_full[:, :, 2])
    vs = blockify(nsa_full[:, :, 3])
    tbl_g = tbl.transpose(1, 0, 2)
    b_i = jnp.arange(B)[:, None, None, None]
    g_i = jnp.arange(G)[None, :, None, None]
    n_keys = sel_idx.shape[-1] * SEL_BLOCK

    def sel_block(args):
        qb, ib, pb = args
        qs = pb.shape[0]
        kg = ks[b_i, g_i, ib].reshape(B, G, qs, n_keys, HEAD_DIM)
        vg = vs[b_i, g_i, ib].reshape(B, G, qs, n_keys, HEAD_DIM)
        kpos = (ib[..., None] * SEL_BLOCK + jnp.arange(SEL_BLOCK, dtype=jnp.int32)).reshape(B, G, qs, n_keys)
        d_sel = pb[None, None, :, None] - kpos
        bias_s = tbl_g[g_i, rel_bucket(d_sel)].astype(jnp.float32).transpose(0, 1, 4, 2, 3)
        s_s = jnp.einsum("bqghd,bgqkd->bghqk", qb, kg).astype(jnp.float32) * scale + bias_s
        p_s = masked_softmax(s_s, (d_sel >= 0)[:, :, None])
        return jnp.einsum("bghqk,bgqkd->bqghd", p_s.astype(vg.dtype), vg)

    qs = math.gcd(T, SEL_QBLOCK)
    o_sel = _merge_q(lax.map(sel_block, (_split_q(qg, qs, 1), _split_q(sel_idx, qs, 2),
                                         _split_q(q_pos, qs, 0))), 1)

    qw = math.gcd(T, QBLOCK)
    nb = T // qw
    span = WINDOW + qw
    kidx = jnp.arange(nb, dtype=jnp.int32)[:, None] * qw + jnp.arange(span, dtype=jnp.int32)[None, :]
    kpos_w = q_pos[0] - WINDOW + kidx
    d_win = q_pos.reshape(nb, qw)[:, :, None] - kpos_w[:, None, :]
    win_mask = (d_win >= 0) & (d_win < WINDOW) & (kpos_w[:, None, :] >= 0)
    kw = win_pad[:, :, 0][:, kidx]
    vw = win_pad[:, :, 1][:, kidx]
    bias_w = tbl[rel_bucket(d_win)].astype(jnp.float32).transpose(0, 3, 4, 1, 2)
    s_w = jnp.einsum("bnqghd,bnkgd->bnghqk", qg.reshape(B, nb, qw, G, Hg, HEAD_DIM), kw).astype(jnp.float32) * scale + bias_w[None]
    p_w = masked_softmax(s_w, win_mask[None, :, None, None])
    o_win = jnp.einsum("bnghqk,bnkgd->bnqghd", p_w.astype(vw.dtype), vw).reshape(B, T, G, Hg, HEAD_DIM)

    shp = (B, T, NSA_HEADS, HEAD_DIM)
    return (gates[..., 0:1] * o_cmp.reshape(shp) + gates[..., 1:2] * o_sel.reshape(shp)
            + gates[..., 2:3] * o_win.reshape(shp))


def fox_attention(q, k, v, logf, q_pos):
    B, T = q.shape[:2]
    L = k.shape[1]
    scale = HEAD_DIM ** -0.5
    r = lax.cumsum(logf, axis=1, reverse=True) - logf
    r_k = r.transpose(0, 2, 1)
    k_pos = jnp.arange(L, dtype=jnp.int32)

    def block(args):
        qb, rq, pb = args
        s = jnp.einsum("bqhd,bkhd->bhqk", qb, k).astype(jnp.float32) * scale
        s = s + (r_k[:, :, None, :] - rq.transpose(0, 2, 1)[:, :, :, None])
        p = masked_softmax(s, k_pos[None, :] <= pb[:, None])
        return jnp.einsum("bhqk,bkhd->bqhd", p.astype(v.dtype), v)

    qb = math.gcd(T, QBLOCK)
    out = lax.map(block, (_split_q(q, qb, 1), _split_q(r[:, L - T:], qb, 1), _split_q(q_pos, qb, 0)))
    return _merge_q(out, 1)


def decoder_layer(x, q_pos, past_nsa, past_fox, past_logf, win_buf, rel_table, lw):
    (norm_ffn1, ffn1_gate, ffn1_up, ffn1_down, norm_mix, w_in, nsa_gate_bias,
     fox_forget_bias, q_norm_nsa, k_norm_nsa, q_norm_fox, k_norm_fox,
     cmp_pos_k, cmp_w1_k, cmp_w2_k, cmp_pos_v, cmp_w1_v, cmp_w2_v,
     out_norm_nsa, out_norm_fox, w_out, norm_ffn2, ffn2_gate, ffn2_up, ffn2_down) = lw
    B, T, _ = x.shape
    x = x + 0.5 * swiglu(rms_norm(x, norm_ffn1), ffn1_gate, ffn1_up, ffn1_down)

    h = rms_norm(x, norm_mix) @ w_in
    q_nsa = rms_norm(h[..., :OFF_KV_NSA].reshape(B, T, NSA_HEADS, HEAD_DIM), q_norm_nsa)
    kv6 = h[..., OFF_KV_NSA:OFF_GATE].reshape(B, T, 6, NSA_KV_HEADS, HEAD_DIM)
    nsa_rows = jnp.stack([kv6[:, :, 0], kv6[:, :, 1], rms_norm(kv6[:, :, 2], k_norm_nsa), kv6[:, :, 3]], axis=2)
    win_rows = jnp.stack([rms_norm(kv6[:, :, 4], k_norm_nsa), kv6[:, :, 5]], axis=2)
    gates = jax.nn.sigmoid(h[..., OFF_GATE:OFF_FOX].reshape(B, T, NSA_HEADS, 3) + nsa_gate_bias)
    qkv_fox = h[..., OFF_FOX:OFF_FORGET].reshape(B, T, 3, FOX_HEADS, HEAD_DIM)
    q_fox = rms_norm(qkv_fox[:, :, 0], q_norm_fox)
    fox_rows = jnp.stack([rms_norm(qkv_fox[:, :, 1], k_norm_fox), qkv_fox[:, :, 2]], axis=2)
    logf = jax.nn.log_sigmoid((h[..., OFF_FORGET:] + fox_forget_bias).astype(jnp.float32))

    if past_nsa is None:
        nsa_full, fox_full, logf_full = nsa_rows, fox_rows, logf
        win_seq = win_rows
        win_keep = min(WINDOW, T)
    else:
        nsa_full = jnp.concatenate([past_nsa, nsa_rows], axis=1)
        fox_full = jnp.concatenate([past_fox, fox_rows], axis=1)
        logf_full = jnp.concatenate([past_logf.astype(jnp.float32), logf], axis=1)
        win_seq = jnp.concatenate([win_buf, win_rows], axis=1)
        win_keep = win_buf.shape[1]
    win_pad = jnp.pad(win_seq, ((0, 0), (WINDOW + T - win_seq.shape[1], 0), (0, 0), (0, 0), (0, 0)))
    win_new = win_seq[:, win_seq.shape[1] - win_keep:]

    o_nsa = nsa_attention(q_nsa, gates, nsa_full, win_pad, q_pos, rel_table, k_norm_nsa,
                          cmp_pos_k, cmp_w1_k, cmp_w2_k, cmp_pos_v, cmp_w1_v, cmp_w2_v)
    o_fox = fox_attention(q_fox, fox_full[:, :, 0], fox_full[:, :, 1], logf_full, q_pos)
    mixed = jnp.concatenate([rms_norm(o_nsa.reshape(B, T, NSA_WIDTH), out_norm_nsa),
                             rms_norm(o_fox.reshape(B, T, FOX_WIDTH), out_norm_fox)], axis=-1)
    x = x + mixed @ w_out
    x = x + 0.5 * swiglu(rms_norm(x, norm_ffn2), ffn2_gate, ffn2_up, ffn2_down)
    return x, nsa_rows, fox_rows, logf, win_new


def setup_inputs(seed: int = 0) -> dict:
    key = jax.random.key(seed)
    keys = jax.random.split(key, 48)
    counter = [0]

    def nk():
        counter[0] += 1
        return keys[counter[0] - 1]

    def normal(shape, scale):
        return scale * jax.random.normal(nk(), shape, jnp.float32)

    def gain(shape):
        return 1.0 + normal(shape, 0.02)

    n_pages = PAST_LEN // PAGE_SIZE
    n_used = DEC_BATCH * n_pages
    n_pool = n_used + max(1, n_used // 4)
    win_len = min(WINDOW, PAST_LEN)
    L = DEPTH
    page_table = jax.random.permutation(nk(), n_pool)[:n_used].reshape(DEC_BATCH, n_pages).astype(jnp.int32)
    return {
        "x_prompt": normal((BATCH, SEQ, D_MODEL), 1.0),
        "x_sample": normal((DEC_BATCH, DEC_SEQ, D_MODEL), 1.0),
        "cache_nsa_kv": normal((L, n_pool, PAGE_SIZE, 4, NSA_KV_HEADS, HEAD_DIM), 1.0),
        "cache_fox_kv": normal((L, n_pool, PAGE_SIZE, 2, FOX_HEADS, HEAD_DIM), 1.0),
        "cache_fox_logf": jax.nn.log_sigmoid(FORGET_BIAS_INIT + normal((L, n_pool, PAGE_SIZE, FOX_HEADS), 1.0)),
        "state_win_kv": normal((L, DEC_BATCH, win_len, 2, NSA_KV_HEADS, HEAD_DIM), 1.0),
        "page_table": page_table,
        "rel_table": normal((REL_BUCKETS, NSA_HEADS), 0.5),
        "norm_ffn1": gain((L, D_MODEL)),
        "ffn1_gate": normal((L, D_MODEL, D_FF), D_MODEL ** -0.5),
        "ffn1_up": normal((L, D_MODEL, D_FF), D_MODEL ** -0.5),
        "ffn1_down": normal((L, D_FF, D_MODEL), D_FF ** -0.5),
        "norm_mix": gain((L, D_MODEL)),
        "w_in": normal((L, D_MODEL, IN_COLS), D_MODEL ** -0.5),
        "nsa_gate_bias": normal((L, NSA_HEADS, 3), 0.1),
        "fox_forget_bias": FORGET_BIAS_INIT + normal((L, FOX_HEADS), 0.5),
        "q_norm_nsa": gain((L, HEAD_DIM)),
        "k_norm_nsa": gain((L, HEAD_DIM)),
        "q_norm_fox": gain((L, HEAD_DIM)),
        "k_norm_fox": gain((L, HEAD_DIM)),
        "cmp_pos_k": normal((L, CMP_LEN, HEAD_DIM), 0.1),
        "cmp_w1_k": normal((L, CMP_LEN * HEAD_DIM, CMP_HIDDEN), (CMP_LEN * HEAD_DIM) ** -0.5),
        "cmp_w2_k": normal((L, CMP_HIDDEN, HEAD_DIM), CMP_HIDDEN ** -0.5),
        "cmp_pos_v": normal((L, CMP_LEN, HEAD_DIM), 0.1),
        "cmp_w1_v": normal((L, CMP_LEN * HEAD_DIM, CMP_HIDDEN), (CMP_LEN * HEAD_DIM) ** -0.5),
        "cmp_w2_v": normal((L, CMP_HIDDEN, HEAD_DIM), CMP_HIDDEN ** -0.5),
        "out_norm_nsa": gain((L, NSA_WIDTH)),
        "out_norm_fox": gain((L, FOX_WIDTH)),
        "w_out": normal((L, D_MODEL, D_MODEL), D_MODEL ** -0.5),
        "norm_ffn2": gain((L, D_MODEL)),
        "ffn2_gate": normal((L, D_MODEL, D_FF), D_MODEL ** -0.5),
        "ffn2_up": normal((L, D_MODEL, D_FF), D_MODEL ** -0.5),
        "ffn2_down": normal((L, D_FF, D_MODEL), D_FF ** -0.5),
    }


def reference(x_prompt, x_sample, cache_nsa_kv, cache_fox_kv, cache_fox_logf, state_win_kv,
              page_table, rel_table, norm_ffn1, ffn1_gate, ffn1_up, ffn1_down, norm_mix, w_in,
              nsa_gate_bias, fox_forget_bias, q_norm_nsa, k_norm_nsa, q_norm_fox, k_norm_fox,
              cmp_pos_k, cmp_w1_k, cmp_w2_k, cmp_pos_v, cmp_w1_v, cmp_w2_v,
              out_norm_nsa, out_norm_fox, w_out, norm_ffn2, ffn2_gate, ffn2_up, ffn2_down):
    n_pages = page_table.shape[1]
    past_len = n_pages * PAGE_SIZE
    pos_prompt = jnp.arange(x_prompt.shape[1], dtype=jnp.int32)
    pos_sample = past_len + jnp.arange(x_sample.shape[1], dtype=jnp.int32)

    def gather_pages(pool):
        rows = pool[page_table]
        return rows.reshape((rows.shape[0], past_len) + rows.shape[3:])

    y_p, y_s = x_prompt, x_sample
    nsa_p, fox_p, logf_p, win_p = [], [], [], []
    nsa_s, fox_s, logf_s, win_s = [], [], [], []
    for layer in range(DEPTH):
        lw = (norm_ffn1[layer], ffn1_gate[layer], ffn1_up[layer], ffn1_down[layer], norm_mix[layer],
              w_in[layer], nsa_gate_bias[layer], fox_forget_bias[layer], q_norm_nsa[layer],
              k_norm_nsa[layer], q_norm_fox[layer], k_norm_fox[layer], cmp_pos_k[layer],
              cmp_w1_k[layer], cmp_w2_k[layer], cmp_pos_v[layer], cmp_w1_v[layer], cmp_w2_v[layer],
              out_norm_nsa[layer], out_norm_fox[layer], w_out[layer], norm_ffn2[layer],
              ffn2_gate[layer], ffn2_up[layer], ffn2_down[layer])
        y_p, a, b, c, d = decoder_layer(y_p, pos_prompt, None, None, None, None, rel_table, lw)
        nsa_p.append(a)
        fox_p.append(b)
        logf_p.append(c)
        win_p.append(d)
        y_s, a, b, c, d = decoder_layer(y_s, pos_sample, gather_pages(cache_nsa_kv[layer]),
                                        gather_pages(cache_fox_kv[layer]),
                                        gather_pages(cache_fox_logf[layer]),
                                        state_win_kv[layer], rel_table, lw)
        nsa_s.append(a)
        fox_s.append(b)
        logf_s.append(c)
        win_s.append(d)
    nsa_kv_prompt = jnp.stack(nsa_p, axis=0)
    fox_kv_prompt = jnp.stack(fox_p, axis=0)
    fox_logf_prompt = jnp.stack(logf_p, axis=0)
    win_kv_prompt = jnp.stack(win_p, axis=0)
    nsa_kv_sample = jnp.stack(nsa_s, axis=0)
    fox_kv_sample = jnp.stack(fox_s, axis=0)
    fox_logf_sample = jnp.stack(logf_s, axis=0)
    win_kv_sample = jnp.stack(win_s, axis=0)
    return (y_p, y_s, nsa_kv_prompt, fox_kv_prompt, fox_logf_prompt, win_kv_prompt,
            nsa_kv_sample, fox_kv_sample, fox_logf_sample, win_kv_sample)
```

```python
import functools
import math

import numpy as np
import jax
import jax.numpy as jnp
from jax import lax
from jax.experimental import pallas as pl
from jax.experimental.pallas import tpu as pltpu

HEAD_DIM = 128
NSA_HEADS = 8
FOX_HEADS = 8
NSA_KV_HEADS = 4
NSA_GROUP = NSA_HEADS // NSA_KV_HEADS
CMP_LEN = 32
CMP_STRIDE = 16
CMP_HIDDEN = 512
SEL_BLOCK = 64
SEL_TOPK = 16
WINDOW = 512
REL_BUCKETS = 32
REL_MAX_DIST = 128
RMS_EPS = 1e-6
PAGE_SIZE = 128

LANE = 128
NEG = -1e30
FORCED_SCORE = 1e30
VMEM_LIMIT = 56 * 1024 * 1024

BF16 = jnp.bfloat16
F32 = jnp.float32
NT_DIMS = (((1,), (1,)), ((), ()))


def _bucket_thresholds():
    n = np.arange(0, 4 * REL_MAX_DIST)
    max_exact = REL_BUCKETS // 2
    nf = np.maximum(n, 1).astype(np.float32)
    large = max_exact + (np.log(nf / max_exact) / math.log(REL_MAX_DIST / max_exact)
                         * (REL_BUCKETS - max_exact)).astype(np.int32)
    bucket = np.where(n < max_exact, n, np.minimum(large, REL_BUCKETS - 1))
    return [int(np.min(n[bucket >= k])) for k in range(1, REL_BUCKETS)]


BUCKET_THR = _bucket_thresholds()
FAR_DIST = BUCKET_THR[-1]


def _cparams(sem):
    return pltpu.CompilerParams(dimension_semantics=sem, vmem_limit_bytes=VMEM_LIMIT)


def _rms_rows(x, gain):
    ms = jnp.mean(x * x, axis=-1, keepdims=True)
    return x * lax.rsqrt(ms + RMS_EPS) * gain


def _ffn_kernel(x_ref, g_ref, wg_ref, wu_ref, wd_ref, o_ref, xn_ref):
    @pl.when(pl.program_id(1) == 0)
    def _():
        x = x_ref[...]
        xn_ref[...] = _rms_rows(x, g_ref[...]).astype(BF16)
        o_ref[...] = x

    xn = xn_ref[...]
    a = jnp.dot(xn, wg_ref[...], preferred_element_type=F32)
    u = jnp.dot(xn, wu_ref[...], preferred_element_type=F32)
    h = (a / (1.0 + jnp.exp(-a))) * u * 0.5
    o_ref[...] += jnp.dot(h.astype(BF16), wd_ref[...], preferred_element_type=F32)


def _ffn(x, gain, wg, wu, wd, tm, tf):
    m, d = x.shape
    f = wg.shape[1]
    return pl.pallas_call(
        _ffn_kernel,
        out_shape=jax.ShapeDtypeStruct((m, d), F32),
        grid=(m // tm, f // tf),
        in_specs=[
            pl.BlockSpec((tm, d), lambda i, j: (i, 0)),
            pl.BlockSpec((1, d), lambda i, j: (0, 0)),
            pl.BlockSpec((d, tf), lambda i, j: (0, j)),
            pl.BlockSpec((d, tf), lambda i, j: (0, j)),
            pl.BlockSpec((tf, d), lambda i, j: (j, 0)),
        ],
        out_specs=pl.BlockSpec((tm, d), lambda i, j: (i, 0)),
        scratch_shapes=[pltpu.VMEM((tm, d), BF16)],
        compiler_params=_cparams(("parallel", "arbitrary")),
        name="ffn",
    )(x, gain, wg, wu, wd)


IN_TN = 4 * HEAD_DIM
J_NSA = (2, 6)
J_WIN = (6, 8)
J_FOX = (10, 14)
N_HEAD_COLS = 56
HM_Q_NSA, HM_K_CMP, HM_K_SLC, HM_V_SLC, HM_K_WIN, HM_V_WIN = 0, 8, 16, 20, 24, 28
HM_Q_FOX, HM_K_FOX, HM_V_FOX = 32, 40, 48


def _inproj_kernel(x_ref, g_ref, w_ref, cg_ref, cf_ref, nsa_ref, win_ref, fox_ref, hm_ref, xn_ref):
    j = pl.program_id(1)

    @pl.when(j == 0)
    def _():
        xn_ref[...] = _rms_rows(x_ref[...], g_ref[...]).astype(BF16)

    res = jnp.dot(xn_ref[...], w_ref[...], preferred_element_type=F32)
    pieces = []
    for hh in range(IN_TN // HEAD_DIM):
        sl = slice(hh * HEAD_DIM, (hh + 1) * HEAD_DIM)
        r = res[:, sl]
        flag = cf_ref[:, sl]
        ms = jnp.mean(r * r, axis=-1, keepdims=True)
        r = r * (flag * lax.rsqrt(ms + RMS_EPS) + (1.0 - flag)) * cg_ref[:, sl]
        hm_ref[hh] = r.astype(BF16)
        pieces.append(r)
    full = jnp.concatenate(pieces, axis=-1)

    @pl.when((j >= J_NSA[0]) & (j < J_NSA[1]))
    def _():
        nsa_ref[...] = full

    @pl.when((j >= J_WIN[0]) & (j < J_WIN[1]))
    def _():
        win_ref[...] = full

    @pl.when((j >= J_FOX[0]) & (j < J_FOX[1]))
    def _():
        fox_ref[...] = full


def _inproj(x, gain, w_main, colgain, colflag, tm):
    m, d = x.shape
    ncol = w_main.shape[1]
    nj = ncol // IN_TN

    def seg(lo, hi):
        return lambda i, j: (i, jnp.clip(j - lo, 0, hi - lo - 1))

    return pl.pallas_call(
        _inproj_kernel,
        out_shape=(
            jax.ShapeDtypeStruct((m, (J_NSA[1] - J_NSA[0]) * IN_TN), F32),
            jax.ShapeDtypeStruct((m, (J_WIN[1] - J_WIN[0]) * IN_TN), F32),
            jax.ShapeDtypeStruct((m, (J_FOX[1] - J_FOX[0]) * IN_TN), F32),
            jax.ShapeDtypeStruct((N_HEAD_COLS, m, HEAD_DIM), BF16),
        ),
        grid=(m // tm, nj),
        in_specs=[
            pl.BlockSpec((tm, d), lambda i, j: (i, 0)),
            pl.BlockSpec((1, d), lambda i, j: (0, 0)),
            pl.BlockSpec((d, IN_TN), lambda i, j: (0, j)),
            pl.BlockSpec((1, IN_TN), lambda i, j: (0, j)),
            pl.BlockSpec((1, IN_TN), lambda i, j: (0, j)),
        ],
        out_specs=(
            pl.BlockSpec((tm, IN_TN), seg(*J_NSA)),
            pl.BlockSpec((tm, IN_TN), seg(*J_WIN)),
            pl.BlockSpec((tm, IN_TN), seg(*J_FOX)),
            pl.BlockSpec((IN_TN // HEAD_DIM, tm, HEAD_DIM), lambda i, j: (j, i, 0)),
        ),
        scratch_shapes=[pltpu.VMEM((tm, d), BF16)],
        compiler_params=_cparams(("parallel", "arbitrary")),
        name="inproj",
    )(x, gain, w_main, colgain, colflag)


N_GATE_COLS = 3 * NSA_HEADS


def _small_kernel(x_ref, g_ref, w_ref, b_ref, o_ref):
    xn = _rms_rows(x_ref[...], g_ref[...]).astype(BF16)
    z = jnp.dot(xn, w_ref[...], preferred_element_type=F32) + b_ref[...]
    lane = lax.broadcasted_iota(jnp.int32, z.shape, 1)
    sig = 1.0 / (1.0 + jnp.exp(-z))
    logsig = jnp.minimum(z, 0.0) - jnp.log(1.0 + jnp.exp(-jnp.abs(z)))
    o_ref[...] = jnp.where(lane < N_GATE_COLS, sig,
                           jnp.where(lane < N_GATE_COLS + FOX_HEADS, logsig, 0.0))


def _small(x, gain, w_small, b_small, tm):
    m, d = x.shape
    return pl.pallas_call(
        _small_kernel,
        out_shape=jax.ShapeDtypeStruct((m, LANE), F32),
        grid=(m // tm,),
        in_specs=[
            pl.BlockSpec((tm, d), lambda i: (i, 0)),
            pl.BlockSpec((1, d), lambda i: (0, 0)),
            pl.BlockSpec((d, LANE), lambda i: (0, 0)),
            pl.BlockSpec((1, LANE), lambda i: (0, 0)),
        ],
        out_specs=pl.BlockSpec((tm, LANE), lambda i: (i, 0)),
        compiler_params=_cparams(("parallel",)),
        name="gates",
    )(x, gain, w_small, b_small)


def _outproj_kernel(on_ref, of_ref, gn_ref, gf_ref, w_ref, x_ref, y_ref):
    a = _rms_rows(on_ref[...], gn_ref[...]).astype(BF16)
    b = _rms_rows(of_ref[...], gf_ref[...]).astype(BF16)
    half = a.shape[1]
    y = jnp.dot(a, w_ref[:half, :], preferred_element_type=F32)
    y = y + jnp.dot(b, w_ref[half:, :], preferred_element_type=F32)
    y_ref[...] = x_ref[...] + y


def _outproj(o_nsa, o_fox, g_nsa, g_fox, w_out, x, tm):
    m, d = x.shape
    wn = o_nsa.shape[1]
    wf = o_fox.shape[1]
    return pl.pallas_call(
        _outproj_kernel,
        out_shape=jax.ShapeDtypeStruct((m, d), F32),
        grid=(m // tm,),
        in_specs=[
            pl.BlockSpec((tm, wn), lambda i: (i, 0)),
            pl.BlockSpec((tm, wf), lambda i: (i, 0)),
            pl.BlockSpec((1, wn), lambda i: (0, 0)),
            pl.BlockSpec((1, wf), lambda i: (0, 0)),
            pl.BlockSpec((wn + wf, d), lambda i: (0, 0)),
            pl.BlockSpec((tm, d), lambda i: (i, 0)),
        ],
        out_specs=pl.BlockSpec((tm, d), lambda i: (i, 0)),
        compiler_params=_cparams(("parallel",)),
        name="outproj",
    )(o_nsa, o_fox, g_nsa, g_fox, w_out, x)


def _cumsum_kernel(x_ref, o_ref):
    n_chunk = x_ref.shape[2] // LANE
    r = lax.broadcasted_iota(jnp.int32, (LANE, LANE), 0)
    c = lax.broadcasted_iota(jnp.int32, (LANE, LANE), 1)
    upper = (r <= c).astype(F32)

    def body(k, carry):
        st = pl.multiple_of(k * LANE, LANE)
        x = x_ref[0, :, pl.ds(st, LANE)]
        cs = jnp.dot(x, upper, preferred_element_type=F32, precision=lax.Precision.HIGHEST) + carry
        o_ref[0, :, pl.ds(st, LANE)] = cs
        return jnp.broadcast_to(cs[:, LANE - 1:LANE], cs.shape)

    lax.fori_loop(0, n_chunk, body, jnp.zeros((x_ref.shape[1], LANE), F32))


def _cumsum(x):
    b, h, length = x.shape
    return pl.pallas_call(
        _cumsum_kernel,
        out_shape=jax.ShapeDtypeStruct(x.shape, F32),
        grid=(b,),
        in_specs=[pl.BlockSpec((1, h, length), lambda i: (i, 0, 0))],
        out_specs=pl.BlockSpec((1, h, length), lambda i: (i, 0, 0)),
        compiler_params=_cparams(("parallel",)),
        name="logf_cumsum",
    )(x)


PE_ROWS = 16


def _compress_kernel(x_ref, w1_ref, w2_ref, pe_ref, kn_ref, o_ref):
    kind = pl.program_id(0) // NSA_KV_HEADS
    n = x_ref.shape[2]
    w1 = w1_ref[0]
    h = jnp.dot(x_ref[0, 0], w1, preferred_element_type=F32)
    pw = jnp.dot(pe_ref[0], w1, preferred_element_type=F32)
    const = pw[0:1, :CMP_HIDDEN] + pw[1:2, CMP_HIDDEN:]
    hid = h[:, :CMP_HIDDEN] + pltpu.roll(h[:, CMP_HIDDEN:], n - 1, 0) + const
    act = hid / (1.0 + jnp.exp(-hid))
    out = jnp.dot(act.astype(BF16), w2_ref[0], preferred_element_type=F32)
    normed = _rms_rows(out, kn_ref[...])
    o_ref[0, 0] = jnp.where(kind == 0, normed, out).astype(BF16)


def _compress(xc, c_off, w1cat, w2, pe, k_norm, n_rows):
    nb = xc.shape[1]
    return pl.pallas_call(
        _compress_kernel,
        out_shape=jax.ShapeDtypeStruct((2 * NSA_KV_HEADS, nb, n_rows, HEAD_DIM), BF16),
        grid=(2 * NSA_KV_HEADS, nb),
        in_specs=[
            pl.BlockSpec((1, 1, n_rows, CMP_STRIDE * HEAD_DIM), lambda c, b: (c_off + c, b, 0, 0)),
            pl.BlockSpec((1, CMP_STRIDE * HEAD_DIM, 2 * CMP_HIDDEN), lambda c, b: (c // NSA_KV_HEADS, 0, 0)),
            pl.BlockSpec((1, CMP_HIDDEN, HEAD_DIM), lambda c, b: (c // NSA_KV_HEADS, 0, 0)),
            pl.BlockSpec((1, PE_ROWS, CMP_STRIDE * HEAD_DIM), lambda c, b: (c // NSA_KV_HEADS, 0, 0)),
            pl.BlockSpec((1, HEAD_DIM), lambda c, b: (0, 0)),
        ],
        out_specs=pl.BlockSpec((1, 1, n_rows, HEAD_DIM), lambda c, b: (c, b, 0, 0)),
        compiler_params=_cparams(("parallel", "parallel")),
        name="compress",
    )(xc, w1cat, w2, pe, k_norm)


def _online_update(state, s, v):
    m, l, acc = state
    m_new = jnp.maximum(m, jnp.max(s, axis=-1, keepdims=True))
    alpha = jnp.exp(m - m_new)
    p = jnp.exp(s - m_new)
    l = alpha * l + jnp.sum(p, axis=-1, keepdims=True)
    acc = alpha * acc + jnp.dot(p.astype(BF16), v, preferred_element_type=F32)
    return m_new, l, acc


def _init_state(rows):
    return (jnp.full((rows, 1), NEG, F32), jnp.zeros((rows, 1), F32), jnp.zeros((rows, HEAD_DIM), F32))


def _rel_bias(dist, tbl_ref, head):
    out = jnp.full(dist.shape, tbl_ref[0, head], F32)
    for k, thr in enumerate(BUCKET_THR, start=1):
        out = jnp.where(dist >= thr, tbl_ref[k, head], out)
    return out


def _fox_kernel(q_ref, k_ref, v_ref, crow_ref, ccol_ref, o_ref, *, tq, tk, td, qoff):
    scale = HEAD_DIM ** -0.5
    q0 = qoff + pl.program_id(2) * tq
    q = q_ref[0]
    cq = ccol_ref[0, 0]

    def logits(start, size):
        k = k_ref[0, 0, pl.ds(start, size), :]
        s = lax.dot_general(q, k, NT_DIMS, preferred_element_type=F32) * scale
        return s + (cq - crow_ref[0, 0, :, pl.ds(start, size)])

    def body(kt, state):
        start = pl.multiple_of(kt * tk, tk)
        return _online_update(state, logits(start, tk), v_ref[0, 0, pl.ds(start, tk), :])

    state = lax.fori_loop(0, q0 // tk, body, _init_state(tq))
    start = pl.multiple_of(q0, LANE)
    s = logits(start, td)
    row = lax.broadcasted_iota(jnp.int32, (tq, td), 0)
    col = lax.broadcasted_iota(jnp.int32, (tq, td), 1)
    s = jnp.where(col <= row, s, NEG)
    _, l, acc = _online_update(state, s, v_ref[0, 0, pl.ds(start, td), :])
    o_ref[...] = acc / l


def _fox_attention(qarr, kv, crow, ccol, *, q_head0, k_head0, v_head0, nb, nq, tq, tk, td, qoff):
    lk = kv.shape[2]
    kern = functools.partial(_fox_kernel, tq=tq, tk=tk, td=td, qoff=qoff)
    return pl.pallas_call(
        kern,
        out_shape=jax.ShapeDtypeStruct((nb * nq * tq, FOX_HEADS * HEAD_DIM), F32),
        grid=(nb, FOX_HEADS, nq),
        in_specs=[
            pl.BlockSpec((1, tq, HEAD_DIM), lambda b, h, t: (q_head0 + h, b * nq + t, 0)),
            pl.BlockSpec((1, 1, lk, HEAD_DIM), lambda b, h, t: (k_head0 + h, b, 0, 0)),
            pl.BlockSpec((1, 1, lk, HEAD_DIM), lambda b, h, t: (v_head0 + h, b, 0, 0)),
            pl.BlockSpec((1, 1, 1, lk), lambda b, h, t: (b, h, 0, 0)),
            pl.BlockSpec((1, 1, tq, 1), lambda b, h, t: (b, h, t, 0)),
        ],
        out_specs=pl.BlockSpec((tq, HEAD_DIM), lambda b, h, t: (b * nq + t, h)),
        compiler_params=_cparams(("parallel", "parallel", "arbitrary")),
        name="fox_attention",
    )(qarr, kv, kv, crow, ccol)


def _nsa_kernel(tbl_ref, q_ref, kc_ref, vc_ref, msel_ref, ks_ref, vs_ref, kw_ref, vw_ref, gate_ref, o_ref,
                *, tq, tk, td, qoff, wpos0, n_sel):
    scale = HEAD_DIM ** -0.5
    g = pl.program_id(1)
    q0 = qoff + pl.program_id(2) * tq
    rows = NSA_GROUP * tq
    q2 = q_ref[...].reshape(rows, HEAD_DIM)
    heads = [NSA_GROUP * g + hh for hh in range(NSA_GROUP)]
    pos_col = q0 + lax.broadcasted_iota(jnp.int32, (tq, 1), 0)

    def per_head(fn):
        return jnp.concatenate([fn(hh) for hh in range(NSA_GROUP)], axis=0)

    ncp = kc_ref.shape[2]
    blk_end = lax.broadcasted_iota(jnp.int32, (tq, ncp), 1) * CMP_STRIDE + (CMP_LEN - 1)
    d_cmp = pos_col - blk_end
    ok_cmp = d_cmp >= 0
    n_cmp = jnp.maximum(d_cmp, 0)
    s = lax.dot_general(q2, kc_ref[0, 0], NT_DIMS, preferred_element_type=F32) * scale
    p_heads = []
    for hh in range(NSA_GROUP):
        sh = s[hh * tq:(hh + 1) * tq] + _rel_bias(n_cmp, tbl_ref, heads[hh])
        sh = jnp.where(ok_cmp, sh, NEG)
        e = jnp.where(ok_cmp, jnp.exp(sh - jnp.max(sh, axis=-1, keepdims=True)), 0.0)
        p_heads.append(e / jnp.maximum(jnp.sum(e, axis=-1, keepdims=True), 1e-30))
    o_cmp = jnp.dot(jnp.concatenate(p_heads, axis=0).astype(BF16), vc_ref[0, 0], preferred_element_type=F32)
    imp = jnp.dot(sum(p_heads), msel_ref[...], preferred_element_type=F32,
                  precision=lax.Precision.HIGHEST)

    nj = imp.shape[1]
    jj = lax.broadcasted_iota(jnp.int32, (tq, nj), 1)
    blk_q = pos_col // SEL_BLOCK
    forced = (jj == 0) | (jj == blk_q) | (jj == blk_q - 1)
    score = jnp.where(forced, FORCED_SCORE, jnp.where(jj <= blk_q, imp, -1.0))
    score = jnp.where(jj < n_sel, score, -2.0)
    rank = jnp.zeros((tq, nj), F32)
    for other in range(n_sel):
        col = jnp.broadcast_to(score[:, other:other + 1], score.shape)
        tie = jnp.where(jj > other, 1.0, 0.0)
        rank = rank + jnp.where(col > score, 1.0, jnp.where(col == score, tie, 0.0))
    sel = jnp.where((rank < SEL_TOPK) & (jj < n_sel), 1.0, 0.0).astype(BF16)

    def bias_and_mask(start, size, kpos0, near, window):
        def near_fn():
            kpos = kpos0 + start + lax.broadcasted_iota(jnp.int32, (tq, size), 1)
            d = pos_col - kpos
            ok = (d >= 0) & (d < WINDOW) if window else d >= 0
            n = jnp.maximum(d, 0)
            return per_head(lambda hh: jnp.where(ok, _rel_bias(n, tbl_ref, heads[hh]), NEG))

        def far_fn():
            return per_head(lambda hh: jnp.full((tq, size), tbl_ref[REL_BUCKETS - 1, heads[hh]], F32))

        if near is True:
            return near_fn()
        return lax.cond(near, near_fn, far_fn)

    def slc_tile(state, start, size, near):
        blk_of_key = (start + lax.broadcasted_iota(jnp.int32, (nj, size), 1)) // SEL_BLOCK
        expand = jnp.where(blk_of_key == lax.broadcasted_iota(jnp.int32, (nj, size), 0), 1.0, 0.0).astype(BF16)
        picked = jnp.dot(sel, expand, preferred_element_type=F32)
        drop = (picked - 1.0) * (-NEG)
        sc = lax.dot_general(q2, ks_ref[0, 0, pl.ds(start, size), :], NT_DIMS, preferred_element_type=F32) * scale
        sc = sc + bias_and_mask(start, size, 0, near, False) + jnp.concatenate([drop] * NSA_GROUP, axis=0)
        return _online_update(state, sc, vs_ref[0, 0, pl.ds(start, size), :])

    def slc_body(kt, state):
        start = pl.multiple_of(kt * tk, tk)
        return slc_tile(state, start, tk, q0 - (start + tk - 1) < FAR_DIST)

    state = lax.fori_loop(0, q0 // tk, slc_body, _init_state(rows))
    _, l, acc = slc_tile(state, pl.multiple_of(q0, LANE), td, True)
    o_slc = acc / l

    def win_body(wt, state):
        start = pl.multiple_of(wt * LANE, LANE)
        near = q0 - (wpos0 + start + LANE - 1) < FAR_DIST
        sc = lax.dot_general(q2, kw_ref[0, 0, pl.ds(start, LANE), :], NT_DIMS, preferred_element_type=F32) * scale
        kpos = wpos0 + start + lax.broadcasted_iota(jnp.int32, (tq, LANE), 1)
        d = pos_col - kpos
        edge = jnp.where((d >= 0) & (d < WINDOW), 0.0, NEG)
        sc = sc + bias_and_mask(start, LANE, wpos0, near, True) + jnp.concatenate([edge] * NSA_GROUP, axis=0)
        return _online_update(state, sc, vw_ref[0, 0, pl.ds(start, LANE), :])

    w_lo = jnp.maximum(q0 - wpos0 - WINDOW, 0) // LANE
    w_hi = (q0 + tq - 1 - wpos0) // LANE
    _, l, acc = lax.fori_loop(w_lo, w_hi + 1, win_body, _init_state(rows))
    o_win = acc / l

    gates = gate_ref[...]
    lane = lax.broadcasted_iota(jnp.int32, gates.shape, 1)

    def gate(hh, branch):
        return jnp.sum(jnp.where(lane == heads[hh] * 3 + branch, gates, 0.0), axis=-1, keepdims=True)

    for hh in range(NSA_GROUP):
        sl = slice(hh * tq, (hh + 1) * tq)
        o_ref[:, hh * HEAD_DIM:(hh + 1) * HEAD_DIM] = (
            gate(hh, 0) * o_cmp[sl] + gate(hh, 1) * o_slc[sl] + gate(hh, 2) * o_win[sl])


def _nsa_attention(rel_table, qarr, kcv, msel, kv, kwin, gates, *, ks_head0, vs_head0, kw_head0, vw_head0,
                   nb, nq, tq, tk, td, qoff, wpos0, n_sel):
    lk = kv.shape[2]
    lw = kwin.shape[2]
    ncp = kcv.shape[2]
    nj = msel.shape[1]
    kern = functools.partial(_nsa_kernel, tq=tq, tk=tk, td=td, qoff=qoff, wpos0=wpos0, n_sel=n_sel)
    g_w = NSA_GROUP * HEAD_DIM
    return pl.pallas_call(
        kern,
        out_shape=jax.ShapeDtypeStruct((nb * nq * tq, NSA_HEADS * HEAD_DIM), F32),
        grid=(nb, NSA_KV_HEADS, nq),
        in_specs=[
            pl.BlockSpec(memory_space=pltpu.SMEM),
            pl.BlockSpec((NSA_GROUP, tq, HEAD_DIM), lambda b, g, t: (g, b * nq + t, 0)),
            pl.BlockSpec((1, 1, ncp, HEAD_DIM), lambda b, g, t: (g, b, 0, 0)),
            pl.BlockSpec((1, 1, ncp, HEAD_DIM), lambda b, g, t: (NSA_KV_HEADS + g, b, 0, 0)),
            pl.BlockSpec((ncp, nj), lambda b, g, t: (0, 0)),
            pl.BlockSpec((1, 1, lk, HEAD_DIM), lambda b, g, t: (ks_head0 + g, b, 0, 0)),
            pl.BlockSpec((1, 1, lk, HEAD_DIM), lambda b, g, t: (vs_head0 + g, b, 0, 0)),
            pl.BlockSpec((1, 1, lw, HEAD_DIM), lambda b, g, t: (kw_head0 + g, b, 0, 0)),
            pl.BlockSpec((1, 1, lw, HEAD_DIM), lambda b, g, t: (vw_head0 + g, b, 0, 0)),
            pl.BlockSpec((tq, LANE), lambda b, g, t: (b * nq + t, 0)),
        ],
        out_specs=pl.BlockSpec((tq, g_w), lambda b, g, t: (b * nq + t, g)),
        compiler_params=_cparams(("parallel", "parallel", "arbitrary")),
        name="nsa_attention",
    )(rel_table, qarr, kcv, kcv, msel, kv, kv, kwin, kwin, gates)


def _cmp_to_sel(n_cmp, n_sel, rows, cols):
    c0 = np.arange(n_cmp)[:, None] * CMP_STRIDE
    s0 = np.arange(n_sel)[None, :] * SEL_BLOCK
    inter = np.clip(np.minimum(c0 + CMP_LEN, s0 + SEL_BLOCK) - np.maximum(c0, s0), 0, None)
    m = np.zeros((rows, cols), np.float32)
    m[:n_cmp, :n_sel] = inter / CMP_LEN
    return jnp.asarray(m)


REGROUP_PAGES = 4


def _regroup_kernel(pt_ref, *refs):
    del pt_ref
    npg = REGROUP_PAGES
    nsa_in, fox_in, lf_in = refs[:npg], refs[npg:2 * npg], refs[2 * npg:3 * npg]
    nsa_out, fox_out, lf_out = refs[3 * npg:]
    is_tail = pl.program_id(1) == pl.num_programs(1) - 1

    @pl.when(is_tail)
    def _():
        nsa_out[...] = jnp.zeros(nsa_out.shape, nsa_out.dtype)
        fox_out[...] = jnp.zeros(fox_out.shape, fox_out.dtype)
        lf_out[...] = jnp.zeros(lf_out.shape, lf_out.dtype)

    @pl.when(jnp.logical_not(is_tail))
    def _():
        n_heads = nsa_out.shape[0]
        for p in range(npg):
            rows = slice(p * PAGE_SIZE, (p + 1) * PAGE_SIZE)
            for c in range(n_heads):
                cols = slice(c * HEAD_DIM, (c + 1) * HEAD_DIM)
                nsa_out[c, 0, rows, :] = nsa_in[p][0, :, cols].astype(BF16)
                fox_out[c, 0, rows, :] = fox_in[p][0, :, cols].astype(BF16)
            lf_out[0, rows, :] = lf_in[p][0]


def _regroup(page_table, cache_nsa, cache_fox, cache_logf, lk):
    nb, n_pages = page_table.shape
    npg = REGROUP_PAGES
    n_steps = n_pages // npg
    rows = npg * PAGE_SIZE
    n_heads = cache_nsa.shape[2] // HEAD_DIM
    last = n_steps - 1

    def page_map(p):
        return lambda b, s, pt: (pt[b, jnp.minimum(s, last) * npg + p], 0, 0)

    def specs(width):
        return [pl.BlockSpec((1, PAGE_SIZE, width), page_map(p)) for p in range(npg)]

    grid_spec = pltpu.PrefetchScalarGridSpec(
        num_scalar_prefetch=1,
        grid=(nb, n_steps + 1),
        in_specs=specs(cache_nsa.shape[2]) + specs(cache_fox.shape[2]) + specs(cache_logf.shape[2]),
        out_specs=(
            pl.BlockSpec((n_heads, 1, rows, HEAD_DIM), lambda b, s, pt: (0, b, s, 0)),
            pl.BlockSpec((n_heads, 1, rows, HEAD_DIM), lambda b, s, pt: (0, b, s, 0)),
            pl.BlockSpec((1, rows, cache_logf.shape[2]), lambda b, s, pt: (b, s, 0)),
        ),
    )
    return pl.pallas_call(
        _regroup_kernel,
        out_shape=(
            jax.ShapeDtypeStruct((n_heads, nb, lk, HEAD_DIM), BF16),
            jax.ShapeDtypeStruct((n_heads, nb, lk, HEAD_DIM), BF16),
            jax.ShapeDtypeStruct((nb, lk, cache_logf.shape[2]), F32),
        ),
        grid_spec=grid_spec,
        compiler_params=_cparams(("parallel", "arbitrary")),
        name="cache_regroup",
    )(page_table, *([cache_nsa] * npg), *([cache_fox] * npg), *([cache_logf] * npg))


def _largest_tile(n, candidates):
    for c in candidates:
        if n % c == 0:
            return c
    raise ValueError(f"no tile in {candidates} divides {n}")


def _token_stage_in(x, p, tm):
    x1 = _ffn(x, p["norm_ffn1"], p["wg1"], p["wu1"], p["wd1"], tm, p["tf"])
    nsa_rows, win_rows, fox_rows, hm = _inproj(x1, p["norm_mix"], p["w_main"], p["colgain"], p["colflag"], tm)
    small = _small(x1, p["norm_mix"], p["w_small"], p["b_small"], tm)
    return x1, nsa_rows, win_rows, fox_rows, hm, small


def _token_stage_out(x1, o_nsa, o_fox, p, tm):
    x2 = _outproj(o_nsa, o_fox, p["out_norm_nsa"], p["out_norm_fox"], p["w_out"], x1, min(tm, 256))
    return _ffn(x2, p["norm_ffn2"], p["wg2"], p["wu2"], p["wd2"], tm, p["tf"])


def kernel(x_prompt, x_sample, cache_nsa_kv, cache_fox_kv, cache_fox_logf, state_win_kv, page_table, rel_table, norm_ffn1, ffn1_gate, ffn1_up, ffn1_down, norm_mix, w_in, nsa_gate_bias, fox_forget_bias, q_norm_nsa, k_norm_nsa, q_norm_fox, k_norm_fox, cmp_pos_k, cmp_w1_k, cmp_w2_k, cmp_pos_v, cmp_w1_v, cmp_w2_v, out_norm_nsa, out_norm_fox, w_out, norm_ffn2, ffn2_gate, ffn2_up, ffn2_down):
    depth = w_in.shape[0]
    assert depth == 1, "single-layer trunk"
    nbp, seq, d_model = x_prompt.shape
    nbd, dseq, _ = x_sample.shape
    n_pages = page_table.shape[1]
    past = n_pages * PAGE_SIZE
    d_ff = ffn1_gate.shape[2]
    nsa_w = NSA_HEADS * HEAD_DIM
    kv6_w = 6 * NSA_KV_HEADS * HEAD_DIM
    fox_w = 3 * FOX_HEADS * HEAD_DIM
    off_gate = nsa_w + kv6_w
    off_fox = off_gate + N_GATE_COLS
    off_forget = off_fox + fox_w
    assert w_in.shape[2] == off_forget + FOX_HEADS and d_model == nsa_w + FOX_HEADS * HEAD_DIM
    assert seq % LANE == 0 and seq >= WINDOW and past % LANE == 0 and n_pages % REGROUP_PAGES == 0
    assert dseq <= 16 and state_win_kv.shape[2] == WINDOW

    w0 = w_in[0]
    ones = lambda n: jnp.ones((n,), F32)
    zeros = lambda n: jnp.zeros((n,), F32)
    kn, kvw = NSA_KV_HEADS, NSA_KV_HEADS * HEAD_DIM
    p = {
        "tf": _largest_tile(d_ff, (512, 256, 128)),
        "norm_ffn1": norm_ffn1[0][None], "norm_mix": norm_mix[0][None], "norm_ffn2": norm_ffn2[0][None],
        "wg1": ffn1_gate[0].astype(BF16), "wu1": ffn1_up[0].astype(BF16), "wd1": ffn1_down[0].astype(BF16),
        "wg2": ffn2_gate[0].astype(BF16), "wu2": ffn2_up[0].astype(BF16), "wd2": ffn2_down[0].astype(BF16),
        "w_main": jnp.concatenate([w0[:, :off_gate], w0[:, off_fox:off_forget]], axis=1).astype(BF16),
        "w_small": jnp.concatenate([w0[:, off_gate:off_fox], w0[:, off_forget:],
                                    jnp.zeros((d_model, LANE - N_GATE_COLS - FOX_HEADS), F32)], axis=1).astype(BF16),
        "b_small": jnp.concatenate([nsa_gate_bias[0].reshape(-1), fox_forget_bias[0],
                                    zeros(LANE - N_GATE_COLS - FOX_HEADS)])[None],
        "colgain": jnp.concatenate([
            jnp.tile(q_norm_nsa[0], NSA_HEADS), ones(2 * kvw), jnp.tile(k_norm_nsa[0], kn), ones(kvw),
            jnp.tile(k_norm_nsa[0], kn), ones(kvw), jnp.tile(q_norm_fox[0], FOX_HEADS),
            jnp.tile(k_norm_fox[0], FOX_HEADS), ones(FOX_HEADS * HEAD_DIM)])[None],
        "colflag": jnp.concatenate([
            ones(nsa_w), zeros(2 * kvw), ones(kvw), zeros(kvw), ones(kvw), zeros(kvw),
            ones(2 * FOX_HEADS * HEAD_DIM), zeros(FOX_HEADS * HEAD_DIM)])[None],
        "out_norm_nsa": out_norm_nsa[0][None], "out_norm_fox": out_norm_fox[0][None],
        "w_out": w_out[0].astype(BF16),
    }
    half = CMP_STRIDE * HEAD_DIM

    def cmp_w1(w):
        return jnp.concatenate([w[0, :half], w[0, half:]], axis=1)

    def cmp_pe(pe):
        return jnp.concatenate([pe[0].reshape(CMP_LEN // CMP_STRIDE, half), jnp.zeros((PE_ROWS - CMP_LEN // CMP_STRIDE, half), F32)], axis=0)

    w1cat = jnp.stack([cmp_w1(cmp_w1_k), cmp_w1(cmp_w1_v)]).astype(BF16)
    w2cat = jnp.stack([cmp_w2_k[0], cmp_w2_v[0]]).astype(BF16)
    pecat = jnp.stack([cmp_pe(cmp_pos_k), cmp_pe(cmp_pos_v)]).astype(BF16)
    k_norm_row = k_norm_nsa[0][None]

    mp = nbp * seq
    tm_p = _largest_tile(mp, (512, 256, 128))
    x1, nsa_rows, win_rows, fox_rows, hm, small = _token_stage_in(x_prompt.reshape(mp, d_model), p, tm_p)
    hm4 = hm.reshape(N_HEAD_COLS, nbp, seq, HEAD_DIM)

    logf = small[:, N_GATE_COLS:N_GATE_COLS + FOX_HEADS]
    csum = _cumsum(logf.reshape(nbp, seq, FOX_HEADS).transpose(0, 2, 1))
    tq_f = _largest_tile(seq, (256, 128))
    o_fox = _fox_attention(hm, hm4, csum[:, :, None, :], csum[:, :, :, None],
                           q_head0=HM_Q_FOX, k_head0=HM_K_FOX, v_head0=HM_V_FOX,
                           nb=nbp, nq=seq // tq_f, tq=tq_f, tk=tq_f, td=tq_f, qoff=0)

    n_chunk = seq // CMP_STRIDE
    n_cmp = (seq - CMP_LEN) // CMP_STRIDE + 1
    n_sel = -(-seq // SEL_BLOCK)
    kcv = _compress(hm.reshape(N_HEAD_COLS, nbp, n_chunk, half), HM_K_CMP, w1cat, w2cat, pecat, k_norm_row, n_chunk)
    msel = _cmp_to_sel(n_cmp, n_sel, n_chunk, -(-n_sel // LANE) * LANE)
    o_nsa = _nsa_attention(rel_table, hm, kcv, msel, hm4, hm4, small,
                           ks_head0=HM_K_SLC, vs_head0=HM_V_SLC, kw_head0=HM_K_WIN, vw_head0=HM_V_WIN,
                           nb=nbp, nq=seq // LANE, tq=LANE, tk=LANE, td=LANE, qoff=0, wpos0=0, n_sel=n_sel)
    y_p = _token_stage_out(x1, o_nsa, o_fox, p, tm_p)

    ms = nbd * dseq
    tq_d = 16
    lk = past + LANE
    xs1, nsa_rows_s, win_rows_s, fox_rows_s, hm_s, small_s = _token_stage_in(x_sample.reshape(ms, d_model), p, ms)
    nsa_dec, fox_dec, lf_dec = _regroup(
        page_table,
        cache_nsa_kv[0].reshape(cache_nsa_kv.shape[1], PAGE_SIZE, -1),
        cache_fox_kv[0].reshape(cache_fox_kv.shape[1], PAGE_SIZE, -1),
        cache_fox_logf[0], lk)
    hm_s4 = hm_s.reshape(N_HEAD_COLS, nbd, dseq, HEAD_DIM)
    nsa_dec = lax.dynamic_update_slice(nsa_dec, hm_s4[HM_K_CMP:HM_K_WIN], (0, 0, past, 0))
    fox_dec = lax.dynamic_update_slice(fox_dec, hm_s4[HM_K_FOX:HM_V_FOX + FOX_HEADS], (0, 0, past, 0))
    logf_s = small_s[:, N_GATE_COLS:N_GATE_COLS + FOX_HEADS].reshape(nbd, dseq, FOX_HEADS)
    lf_dec = lax.dynamic_update_slice(lf_dec, logf_s, (0, past, 0))

    pad_q = ((0, 0), (0, 0), (0, tq_d - dseq), (0, 0))
    q_dec = jnp.pad(hm_s4, pad_q).reshape(N_HEAD_COLS, nbd * tq_d, HEAD_DIM)
    gates_dec = jnp.pad(small_s.reshape(nbd, dseq, LANE), pad_q[1:]).reshape(nbd * tq_d, LANE)

    csum_d = _cumsum(lf_dec.transpose(0, 2, 1))
    tk_d = _largest_tile(past, (2048, 1024, 512, 256, 128))
    o_fox_s = _fox_attention(q_dec, fox_dec, csum_d[:, :, None, :], csum_d[:, :, past:past + tq_d, None],
                             q_head0=HM_Q_FOX, k_head0=0, v_head0=FOX_HEADS,
                             nb=nbd, nq=1, tq=tq_d, tk=tk_d, td=LANE, qoff=past)

    n_chunk_d = past // CMP_STRIDE
    n_cmp_d = (past + dseq - CMP_LEN) // CMP_STRIDE + 1
    n_sel_d = -(-(past + dseq) // SEL_BLOCK)
    assert n_cmp_d + CMP_LEN // CMP_STRIDE - 1 <= n_chunk_d, "compressed blocks must lie in the cached rows"
    kcv_d = _compress(nsa_dec.reshape(nsa_dec.shape[0], nbd, lk // CMP_STRIDE, half), 0,
                      w1cat, w2cat, pecat, k_norm_row, n_chunk_d)
    msel_d = _cmp_to_sel(n_cmp_d, n_sel_d, n_chunk_d, -(-n_sel_d // LANE) * LANE)
    win_old = state_win_kv[0].transpose(2, 3, 0, 1, 4).reshape(2 * NSA_KV_HEADS, nbd, WINDOW, HEAD_DIM).astype(BF16)
    win_dec = jnp.concatenate([win_old, hm_s4[HM_K_WIN:HM_V_WIN + NSA_KV_HEADS],
                               jnp.zeros((2 * NSA_KV_HEADS, nbd, LANE - dseq, HEAD_DIM), BF16)], axis=2)
    o_nsa_s = _nsa_attention(rel_table, q_dec, kcv_d, msel_d, nsa_dec, win_dec, gates_dec,
                             ks_head0=2 * NSA_KV_HEADS, vs_head0=3 * NSA_KV_HEADS, kw_head0=0, vw_head0=NSA_KV_HEADS,
                             nb=nbd, nq=1, tq=tq_d, tk=tk_d, td=LANE, qoff=past, wpos0=past - WINDOW,
                             n_sel=n_sel_d)
    unpad = lambda o: o.reshape(nbd, tq_d, -1)[:, :dseq].reshape(ms, -1)
    y_s = _token_stage_out(xs1, unpad(o_nsa_s), unpad(o_fox_s), p, ms)

    kvh = (NSA_KV_HEADS, HEAD_DIM)
    win_keep = min(WINDOW, seq)
    win_p = win_rows.reshape(nbp, seq, 2, *kvh)[:, seq - win_keep:]
    win_s = jnp.concatenate([state_win_kv[0], win_rows_s.reshape(nbd, dseq, 2, *kvh)], axis=1)[:, dseq:]
    return (
        y_p.reshape(nbp, seq, d_model),
        y_s.reshape(nbd, dseq, d_model),
        nsa_rows.reshape(1, nbp, seq, 4, *kvh),
        fox_rows.reshape(1, nbp, seq, 2, FOX_HEADS, HEAD_DIM),
        logf.reshape(1, nbp, seq, FOX_HEADS),
        win_p[None],
        nsa_rows_s.reshape(1, nbd, dseq, 4, *kvh),
        fox_rows_s.reshape(1, nbd, dseq, 2, FOX_HEADS, HEAD_DIM),
        logf_s[None],
        win_s[None],
    )
```

```python
import functools
import math

import numpy as np
import jax
import jax.numpy as jnp
from jax import lax
from jax.experimental import pallas as pl
from jax.experimental.pallas import tpu as pltpu

HEAD_DIM = 128
NSA_HEADS = 8
FOX_HEADS = 8
NSA_KV_HEADS = 4
NSA_GROUP = NSA_HEADS // NSA_KV_HEADS
CMP_LEN = 32
CMP_STRIDE = 16
CMP_HIDDEN = 512
SEL_BLOCK = 64
SEL_TOPK = 16
WINDOW = 512
REL_BUCKETS = 32
REL_MAX_DIST = 128
RMS_EPS = 1e-6
PAGE_SIZE = 128

LANE = 128
NEG = -1e30
FORCED_SCORE = 1e30
VMEM_LIMIT = 56 * 1024 * 1024

BF16 = jnp.bfloat16
F32 = jnp.float32
NT_DIMS = (((1,), (1,)), ((), ()))


def _bucket_thresholds():
    n = np.arange(0, 4 * REL_MAX_DIST)
    max_exact = REL_BUCKETS // 2
    nf = np.maximum(n, 1).astype(np.float32)
    large = max_exact + (np.log(nf / max_exact) / math.log(REL_MAX_DIST / max_exact)
                         * (REL_BUCKETS - max_exact)).astype(np.int32)
    bucket = np.where(n < max_exact, n, np.minimum(large, REL_BUCKETS - 1))
    return [int(np.min(n[bucket >= k])) for k in range(1, REL_BUCKETS)]


BUCKET_THR = _bucket_thresholds()
FAR_DIST = BUCKET_THR[-1]


def _cparams(sem):
    return pltpu.CompilerParams(dimension_semantics=sem, vmem_limit_bytes=VMEM_LIMIT)


def _rms_rows(x, gain):
    ms = jnp.mean(x * x, axis=-1, keepdims=True)
    return x * lax.rsqrt(ms + RMS_EPS) * gain


def _ffn_kernel(x_ref, g_ref, wg_ref, wu_ref, wd_ref, o_ref, xn_ref):
    @pl.when(pl.program_id(1) == 0)
    def _():
        x = x_ref[...]
        xn_ref[...] = _rms_rows(x, g_ref[...]).astype(BF16)
        o_ref[...] = x

    xn = xn_ref[...]
    a = jnp.dot(xn, wg_ref[...], preferred_element_type=F32)
    u = jnp.dot(xn, wu_ref[...], preferred_element_type=F32)
    h = (a / (1.0 + jnp.exp(-a))) * u * 0.5
    o_ref[...] += jnp.dot(h.astype(BF16), wd_ref[...], preferred_element_type=F32)


def _ffn(x, gain, wg, wu, wd, tm, tf):
    m, d = x.shape
    f = wg.shape[1]
    return pl.pallas_call(
        _ffn_kernel,
        out_shape=jax.ShapeDtypeStruct((m, d), F32),
        grid=(m // tm, f // tf),
        in_specs=[
            pl.BlockSpec((tm, d), lambda i, j: (i, 0)),
            pl.BlockSpec((1, d), lambda i, j: (0, 0)),
            pl.BlockSpec((d, tf), lambda i, j: (0, j)),
            pl.BlockSpec((d, tf), lambda i, j: (0, j)),
            pl.BlockSpec((tf, d), lambda i, j: (j, 0)),
        ],
        out_specs=pl.BlockSpec((tm, d), lambda i, j: (i, 0)),
        scratch_shapes=[pltpu.VMEM((tm, d), BF16)],
        compiler_params=_cparams(("parallel", "arbitrary")),
        name="ffn",
    )(x, gain, wg, wu, wd)


IN_TN = 4 * HEAD_DIM
J_NSA = (2, 6)
J_WIN = (6, 8)
J_FOX = (10, 14)
N_HEAD_COLS = 56
HM_Q_NSA, HM_K_CMP, HM_K_SLC, HM_V_SLC, HM_K_WIN, HM_V_WIN = 0, 8, 16, 20, 24, 28
HM_Q_FOX, HM_K_FOX, HM_V_FOX = 32, 40, 48


def _inproj_kernel(x_ref, g_ref, w_ref, cg_ref, cf_ref, nsa_ref, win_ref, fox_ref, hm_ref, xn_ref):
    j = pl.program_id(1)

    @pl.when(j == 0)
    def _():
        xn_ref[...] = _rms_rows(x_ref[...], g_ref[...]).astype(BF16)

    res = jnp.dot(xn_ref[...], w_ref[...], preferred_element_type=F32)
    tm = res.shape[0]
    heads = IN_TN // HEAD_DIM
    pieces = []
    for hh in range(heads):
        sl = slice(hh * HEAD_DIM, (hh + 1) * HEAD_DIM)
        r = res[:, sl]
        flag = cf_ref[:, sl]
        ms = jnp.mean(r * r, axis=-1, keepdims=True)
        r = r * (flag * lax.rsqrt(ms + RMS_EPS) + (1.0 - flag)) * cg_ref[:, sl]
        hm_ref[hh] = r.astype(BF16)
        pieces.append(r)

    def scatter_rows(ref, jlo, jhi):
        n_cols = (jhi - jlo) * heads

        @pl.when((j >= jlo) & (j < jhi))
        def _():
            for hh in range(heads):
                ref[pl.ds((j - jlo) * heads + hh, tm, stride=n_cols), :] = pieces[hh]

    scatter_rows(nsa_ref, *J_NSA)
    scatter_rows(win_ref, *J_WIN)
    scatter_rows(fox_ref, *J_FOX)


def _inproj(x, gain, w_main, colgain, colflag, tm):
    m, d = x.shape
    ncol = w_main.shape[1]
    nj = ncol // IN_TN

    heads = IN_TN // HEAD_DIM
    n_nsa, n_win, n_fox = [(hi - lo) * heads for lo, hi in (J_NSA, J_WIN, J_FOX)]
    return pl.pallas_call(
        _inproj_kernel,
        out_shape=(
            jax.ShapeDtypeStruct((m * n_nsa, HEAD_DIM), F32),
            jax.ShapeDtypeStruct((m * n_win, HEAD_DIM), F32),
            jax.ShapeDtypeStruct((m * n_fox, HEAD_DIM), F32),
            jax.ShapeDtypeStruct((N_HEAD_COLS, m, HEAD_DIM), BF16),
        ),
        grid=(m // tm, nj),
        in_specs=[
            pl.BlockSpec((tm, d), lambda i, j: (i, 0)),
            pl.BlockSpec((1, d), lambda i, j: (0, 0)),
            pl.BlockSpec((d, IN_TN), lambda i, j: (0, j)),
            pl.BlockSpec((1, IN_TN), lambda i, j: (0, j)),
            pl.BlockSpec((1, IN_TN), lambda i, j: (0, j)),
        ],
        out_specs=(
            pl.BlockSpec((tm * n_nsa, HEAD_DIM), lambda i, j: (i, 0)),
            pl.BlockSpec((tm * n_win, HEAD_DIM), lambda i, j: (i, 0)),
            pl.BlockSpec((tm * n_fox, HEAD_DIM), lambda i, j: (i, 0)),
            pl.BlockSpec((heads, tm, HEAD_DIM), lambda i, j: (j, i, 0)),
        ),
        scratch_shapes=[pltpu.VMEM((tm, d), BF16)],
        compiler_params=_cparams(("parallel", "arbitrary")),
        name="inproj",
    )(x, gain, w_main, colgain, colflag)


N_GATE_COLS = 3 * NSA_HEADS


def _small_kernel(x_ref, g_ref, w_ref, b_ref, o_ref):
    xn = _rms_rows(x_ref[...], g_ref[...]).astype(BF16)
    z = jnp.dot(xn, w_ref[...], preferred_element_type=F32) + b_ref[...]
    lane = lax.broadcasted_iota(jnp.int32, z.shape, 1)
    sig = 1.0 / (1.0 + jnp.exp(-z))
    logsig = jnp.minimum(z, 0.0) - jnp.log(1.0 + jnp.exp(-jnp.abs(z)))
    o_ref[...] = jnp.where(lane < N_GATE_COLS, sig,
                           jnp.where(lane < N_GATE_COLS + FOX_HEADS, logsig, 0.0))


def _small(x, gain, w_small, b_small, tm):
    m, d = x.shape
    return pl.pallas_call(
        _small_kernel,
        out_shape=jax.ShapeDtypeStruct((m, LANE), F32),
        grid=(m // tm,),
        in_specs=[
            pl.BlockSpec((tm, d), lambda i: (i, 0)),
            pl.BlockSpec((1, d), lambda i: (0, 0)),
            pl.BlockSpec((d, LANE), lambda i: (0, 0)),
            pl.BlockSpec((1, LANE), lambda i: (0, 0)),
        ],
        out_specs=pl.BlockSpec((tm, LANE), lambda i: (i, 0)),
        compiler_params=_cparams(("parallel",)),
        name="gates",
    )(x, gain, w_small, b_small)


def _outproj_kernel(on_ref, of_ref, gn_ref, gf_ref, w_ref, x_ref, y_ref):
    a = _rms_rows(on_ref[...], gn_ref[...]).astype(BF16)
    b = _rms_rows(of_ref[...], gf_ref[...]).astype(BF16)
    half = a.shape[1]
    y = jnp.dot(a, w_ref[:half, :], preferred_element_type=F32)
    y = y + jnp.dot(b, w_ref[half:, :], preferred_element_type=F32)
    y_ref[...] = x_ref[...] + y


def _outproj(o_nsa, o_fox, g_nsa, g_fox, w_out, x, tm):
    m, d = x.shape
    wn = o_nsa.shape[1]
    wf = o_fox.shape[1]
    return pl.pallas_call(
        _outproj_kernel,
        out_shape=jax.ShapeDtypeStruct((m, d), F32),
        grid=(m // tm,),
        in_specs=[
            pl.BlockSpec((tm, wn), lambda i: (i, 0)),
            pl.BlockSpec((tm, wf), lambda i: (i, 0)),
            pl.BlockSpec((1, wn), lambda i: (0, 0)),
            pl.BlockSpec((1, wf), lambda i: (0, 0)),
            pl.BlockSpec((wn + wf, d), lambda i: (0, 0)),
            pl.BlockSpec((tm, d), lambda i: (i, 0)),
        ],
        out_specs=pl.BlockSpec((tm, d), lambda i: (i, 0)),
        compiler_params=_cparams(("parallel",)),
        name="outproj",
    )(o_nsa, o_fox, g_nsa, g_fox, w_out, x)


CUMSUM_CHUNK = 512


def _cumsum_kernel(x_ref, o_ref):
    rows, length = x_ref.shape[1], x_ref.shape[2]
    n_big = length // CUMSUM_CHUNK

    def sweep(width, first, count, offset, carry):
        r = lax.broadcasted_iota(jnp.int32, (width, width), 0)
        c = lax.broadcasted_iota(jnp.int32, (width, width), 1)
        upper = (r <= c).astype(F32)

        def body(k, carry):
            st = pl.multiple_of(offset + k * width, LANE)
            x = x_ref[0, :, pl.ds(st, width)]
            cs = jnp.dot(x, upper, preferred_element_type=F32, precision=lax.Precision.HIGHEST) + carry
            o_ref[0, :, pl.ds(st, width)] = cs
            return cs[:, width - 1:width]

        return lax.fori_loop(first, count, body, carry)

    carry = sweep(CUMSUM_CHUNK, 0, n_big, 0, jnp.zeros((rows, 1), F32))
    sweep(LANE, 0, (length - n_big * CUMSUM_CHUNK) // LANE, n_big * CUMSUM_CHUNK, carry)


def _cumsum(x):
    b, h, length = x.shape
    return pl.pallas_call(
        _cumsum_kernel,
        out_shape=jax.ShapeDtypeStruct(x.shape, F32),
        grid=(b,),
        in_specs=[pl.BlockSpec((1, h, length), lambda i: (i, 0, 0))],
        out_specs=pl.BlockSpec((1, h, length), lambda i: (i, 0, 0)),
        compiler_params=_cparams(("parallel",)),
        name="logf_cumsum",
    )(x)


PE_ROWS = 16


def _compress_kernel(x_ref, w1_ref, w2_ref, pe_ref, kn_ref, o_ref):
    kind = pl.program_id(0) // NSA_KV_HEADS
    n = x_ref.shape[2]
    w1 = w1_ref[0]
    h = jnp.dot(x_ref[0, 0], w1, preferred_element_type=F32)
    pw = jnp.dot(pe_ref[0], w1, preferred_element_type=F32)
    const = pw[0:1, :CMP_HIDDEN] + pw[1:2, CMP_HIDDEN:]
    hid = h[:, :CMP_HIDDEN] + pltpu.roll(h[:, CMP_HIDDEN:], n - 1, 0) + const
    act = hid / (1.0 + jnp.exp(-hid))
    out = jnp.dot(act.astype(BF16), w2_ref[0], preferred_element_type=F32)
    normed = _rms_rows(out, kn_ref[...])
    o_ref[0, 0] = jnp.where(kind == 0, normed, out).astype(BF16)


def _compress(xc, c_off, w1cat, w2, pe, k_norm, n_rows):
    nb = xc.shape[1]
    return pl.pallas_call(
        _compress_kernel,
        out_shape=jax.ShapeDtypeStruct((2 * NSA_KV_HEADS, nb, n_rows, HEAD_DIM), BF16),
        grid=(2 * NSA_KV_HEADS, nb),
        in_specs=[
            pl.BlockSpec((1, 1, n_rows, CMP_STRIDE * HEAD_DIM), lambda c, b: (c_off + c, b, 0, 0)),
            pl.BlockSpec((1, CMP_STRIDE * HEAD_DIM, 2 * CMP_HIDDEN), lambda c, b: (c // NSA_KV_HEADS, 0, 0)),
            pl.BlockSpec((1, CMP_HIDDEN, HEAD_DIM), lambda c, b: (c // NSA_KV_HEADS, 0, 0)),
            pl.BlockSpec((1, PE_ROWS, CMP_STRIDE * HEAD_DIM), lambda c, b: (c // NSA_KV_HEADS, 0, 0)),
            pl.BlockSpec((1, HEAD_DIM), lambda c, b: (0, 0)),
        ],
        out_specs=pl.BlockSpec((1, 1, n_rows, HEAD_DIM), lambda c, b: (c, b, 0, 0)),
        compiler_params=_cparams(("parallel", "parallel")),
        name="compress",
    )(xc, w1cat, w2, pe, k_norm)


def _online_update(state, s, v):
    m, l, acc = state
    m_new = jnp.maximum(m, jnp.max(s, axis=-1, keepdims=True))
    alpha = jnp.exp(m - m_new)
    p = jnp.exp(s - m_new)
    l = alpha * l + jnp.sum(p, axis=-1, keepdims=True)
    acc = alpha * acc + jnp.dot(p.astype(BF16), v, preferred_element_type=F32)
    return m_new, l, acc


def _init_state(rows):
    return (jnp.full((rows, 1), NEG, F32), jnp.zeros((rows, 1), F32), jnp.zeros((rows, HEAD_DIM), F32))


def _rel_bias(dist, tbl_ref, head):
    out = jnp.full(dist.shape, tbl_ref[0, head], F32)
    for k, thr in enumerate(BUCKET_THR, start=1):
        out = jnp.where(dist >= thr, tbl_ref[k, head], out)
    return out


def _fox_kernel(q_ref, k_ref, v_ref, crow_ref, ccol_ref, o_ref, *, tq, tk, td, qoff):
    q0 = qoff + pl.program_id(2) * tq
    q = q_ref[0]
    cq = ccol_ref[0, 0]

    def logits(start, size):
        k = k_ref[0, 0, pl.ds(start, size), :]
        s = lax.dot_general(q, k, NT_DIMS, preferred_element_type=F32)
        return s + (cq - crow_ref[0, 0, :, pl.ds(start, size)])

    def body(kt, state):
        start = pl.multiple_of(kt * tk, tk)
        return _online_update(state, logits(start, tk), v_ref[0, 0, pl.ds(start, tk), :])

    state = lax.fori_loop(0, q0 // tk, body, _init_state(tq))
    start = pl.multiple_of(q0, LANE)
    s = logits(start, td)
    row = lax.broadcasted_iota(jnp.int32, (tq, td), 0)
    col = lax.broadcasted_iota(jnp.int32, (tq, td), 1)
    s = jnp.where(col <= row, s, NEG)
    _, l, acc = _online_update(state, s, v_ref[0, 0, pl.ds(start, td), :])
    o_ref[...] = acc / l


def _fox_attention(qarr, kv, crow, ccol, *, q_head0, k_head0, v_head0, nb, nq, tq, tk, td, qoff):
    lk = kv.shape[2]
    kern = functools.partial(_fox_kernel, tq=tq, tk=tk, td=td, qoff=qoff)
    return pl.pallas_call(
        kern,
        out_shape=jax.ShapeDtypeStruct((nb * nq * tq, FOX_HEADS * HEAD_DIM), F32),
        grid=(nb, FOX_HEADS, nq),
        in_specs=[
            pl.BlockSpec((1, tq, HEAD_DIM), lambda b, h, t: (q_head0 + h, b * nq + t, 0)),
            pl.BlockSpec((1, 1, lk, HEAD_DIM), lambda b, h, t: (k_head0 + h, b, 0, 0)),
            pl.BlockSpec((1, 1, lk, HEAD_DIM), lambda b, h, t: (v_head0 + h, b, 0, 0)),
            pl.BlockSpec((1, 1, 1, lk), lambda b, h, t: (b, h, 0, 0)),
            pl.BlockSpec((1, 1, tq, 1), lambda b, h, t: (b, h, t, 0)),
        ],
        out_specs=pl.BlockSpec((tq, HEAD_DIM), lambda b, h, t: (b * nq + t, h)),
        compiler_params=_cparams(("parallel", "parallel", "arbitrary")),
        name="fox_attention",
    )(qarr, kv, kv, crow, ccol)


def _nsa_kernel(tbl_ref, q_ref, kc_ref, vc_ref, msel_ref, ks_ref, vs_ref, kw_ref, vw_ref, gate_ref, o_ref,
                *, tq, tk, td, qoff, wpos0, n_sel):
    g = pl.program_id(1)
    q0 = qoff + pl.program_id(2) * tq
    rows = NSA_GROUP * tq
    q2 = q_ref[...].reshape(rows, HEAD_DIM)
    heads = [NSA_GROUP * g + hh for hh in range(NSA_GROUP)]
    pos_col = q0 + lax.broadcasted_iota(jnp.int32, (tq, 1), 0)

    def per_head(fn):
        return jnp.concatenate([fn(hh) for hh in range(NSA_GROUP)], axis=0)

    ncp = kc_ref.shape[2]
    blk_end = lax.broadcasted_iota(jnp.int32, (tq, ncp), 1) * CMP_STRIDE + (CMP_LEN - 1)
    d_cmp = pos_col - blk_end
    ok_cmp = d_cmp >= 0
    n_cmp = jnp.maximum(d_cmp, 0)
    s = lax.dot_general(q2, kc_ref[0, 0], NT_DIMS, preferred_element_type=F32)
    p_heads = []
    for hh in range(NSA_GROUP):
        sh = s[hh * tq:(hh + 1) * tq] + _rel_bias(n_cmp, tbl_ref, heads[hh])
        sh = jnp.where(ok_cmp, sh, NEG)
        e = jnp.where(ok_cmp, jnp.exp(sh - jnp.max(sh, axis=-1, keepdims=True)), 0.0)
        p_heads.append(e / jnp.maximum(jnp.sum(e, axis=-1, keepdims=True), 1e-30))
    o_cmp = jnp.dot(jnp.concatenate(p_heads, axis=0).astype(BF16), vc_ref[0, 0], preferred_element_type=F32)
    imp = jnp.dot(sum(p_heads), msel_ref[...], preferred_element_type=F32,
                  precision=lax.Precision.HIGHEST)

    nj = imp.shape[1]
    jj = lax.broadcasted_iota(jnp.int32, (tq, nj), 1)
    blk_q = pos_col // SEL_BLOCK
    forced = (jj == 0) | (jj == blk_q) | (jj == blk_q - 1)
    score = jnp.where(forced, FORCED_SCORE, jnp.where(jj <= blk_q, imp, -1.0))
    score = jnp.where(jj < n_sel, score, -2.0)
    rank = jnp.zeros((tq, nj), F32)
    for other in range(n_sel):
        col = jnp.broadcast_to(score[:, other:other + 1], score.shape)
        tie = jnp.where(jj > other, 1.0, 0.0)
        rank = rank + jnp.where(col > score, 1.0, jnp.where(col == score, tie, 0.0))
    sel = jnp.where((rank < SEL_TOPK) & (jj < n_sel), 1.0, 0.0).astype(BF16)

    def bias_and_mask(start, size, kpos0, near, window):
        def near_fn():
            kpos = kpos0 + start + lax.broadcasted_iota(jnp.int32, (tq, size), 1)
            d = pos_col - kpos
            ok = (d >= 0) & (d < WINDOW) if window else d >= 0
            n = jnp.maximum(d, 0)
            return per_head(lambda hh: jnp.where(ok, _rel_bias(n, tbl_ref, heads[hh]), NEG))

        def far_fn():
            return per_head(lambda hh: jnp.full((tq, size), tbl_ref[REL_BUCKETS - 1, heads[hh]], F32))

        if near is True:
            return near_fn()
        return lax.cond(near, near_fn, far_fn)

    def slc_tile(state, start, size, near):
        blk_of_key = (start + lax.broadcasted_iota(jnp.int32, (nj, size), 1)) // SEL_BLOCK
        expand = jnp.where(blk_of_key == lax.broadcasted_iota(jnp.int32, (nj, size), 0), 1.0, 0.0).astype(BF16)
        picked = jnp.dot(sel, expand, preferred_element_type=F32)
        drop = (picked - 1.0) * (-NEG)
        sc = lax.dot_general(q2, ks_ref[0, 0, pl.ds(start, size), :], NT_DIMS, preferred_element_type=F32)
        sc = sc + bias_and_mask(start, size, 0, near, False) + jnp.concatenate([drop] * NSA_GROUP, axis=0)
        return _online_update(state, sc, vs_ref[0, 0, pl.ds(start, size), :])

    def slc_body(kt, state):
        start = pl.multiple_of(kt * tk, tk)
        return slc_tile(state, start, tk, q0 - (start + tk - 1) < FAR_DIST)

    state = lax.fori_loop(0, q0 // tk, slc_body, _init_state(rows))
    _, l, acc = slc_tile(state, pl.multiple_of(q0, LANE), td, True)
    o_slc = acc / l

    def win_body(wt, state):
        start = pl.multiple_of(wt * LANE, LANE)
        near = q0 - (wpos0 + start + LANE - 1) < FAR_DIST
        sc = lax.dot_general(q2, kw_ref[0, 0, pl.ds(start, LANE), :], NT_DIMS, preferred_element_type=F32)
        kpos = wpos0 + start + lax.broadcasted_iota(jnp.int32, (tq, LANE), 1)
        d = pos_col - kpos
        edge = jnp.where((d >= 0) & (d < WINDOW), 0.0, NEG)
        sc = sc + bias_and_mask(start, LANE, wpos0, near, True) + jnp.concatenate([edge] * NSA_GROUP, axis=0)
        return _online_update(state, sc, vw_ref[0, 0, pl.ds(start, LANE), :])

    w_lo = jnp.maximum(q0 - wpos0 - WINDOW, 0) // LANE
    w_hi = (q0 + tq - 1 - wpos0) // LANE
    _, l, acc = lax.fori_loop(w_lo, w_hi + 1, win_body, _init_state(rows))
    o_win = acc / l

    gates = gate_ref[...]
    lane = lax.broadcasted_iota(jnp.int32, gates.shape, 1)

    def gate(hh, branch):
        return jnp.sum(jnp.where(lane == heads[hh] * 3 + branch, gates, 0.0), axis=-1, keepdims=True)

    for hh in range(NSA_GROUP):
        sl = slice(hh * tq, (hh + 1) * tq)
        o_ref[:, hh * HEAD_DIM:(hh + 1) * HEAD_DIM] = (
            gate(hh, 0) * o_cmp[sl] + gate(hh, 1) * o_slc[sl] + gate(hh, 2) * o_win[sl])


def _nsa_attention(rel_table, qarr, kcv, msel, kv, kwin, gates, *, ks_head0, vs_head0, kw_head0, vw_head0,
                   nb, nq, tq, tk, td, qoff, wpos0, n_sel):
    lk = kv.shape[2]
    lw = kwin.shape[2]
    ncp = kcv.shape[2]
    nj = msel.shape[1]
    kern = functools.partial(_nsa_kernel, tq=tq, tk=tk, td=td, qoff=qoff, wpos0=wpos0, n_sel=n_sel)
    g_w = NSA_GROUP * HEAD_DIM
    return pl.pallas_call(
        kern,
        out_shape=jax.ShapeDtypeStruct((nb * nq * tq, NSA_HEADS * HEAD_DIM), F32),
        grid=(nb, NSA_KV_HEADS, nq),
        in_specs=[
            pl.BlockSpec(memory_space=pltpu.SMEM),
            pl.BlockSpec((NSA_GROUP, tq, HEAD_DIM), lambda b, g, t: (g, b * nq + t, 0)),
            pl.BlockSpec((1, 1, ncp, HEAD_DIM), lambda b, g, t: (g, b, 0, 0)),
            pl.BlockSpec((1, 1, ncp, HEAD_DIM), lambda b, g, t: (NSA_KV_HEADS + g, b, 0, 0)),
            pl.BlockSpec((ncp, nj), lambda b, g, t: (0, 0)),
            pl.BlockSpec((1, 1, lk, HEAD_DIM), lambda b, g, t: (ks_head0 + g, b, 0, 0)),
            pl.BlockSpec((1, 1, lk, HEAD_DIM), lambda b, g, t: (vs_head0 + g, b, 0, 0)),
            pl.BlockSpec((1, 1, lw, HEAD_DIM), lambda b, g, t: (kw_head0 + g, b, 0, 0)),
            pl.BlockSpec((1, 1, lw, HEAD_DIM), lambda b, g, t: (vw_head0 + g, b, 0, 0)),
            pl.BlockSpec((tq, LANE), lambda b, g, t: (b * nq + t, 0)),
        ],
        out_specs=pl.BlockSpec((tq, g_w), lambda b, g, t: (b * nq + t, g)),
        compiler_params=_cparams(("parallel", "parallel", "arbitrary")),
        name="nsa_attention",
    )(rel_table, qarr, kcv, kcv, msel, kv, kv, kwin, kwin, gates)


def _cmp_to_sel(n_cmp, n_sel, rows, cols):
    c0 = np.arange(n_cmp)[:, None] * CMP_STRIDE
    s0 = np.arange(n_sel)[None, :] * SEL_BLOCK
    inter = np.clip(np.minimum(c0 + CMP_LEN, s0 + SEL_BLOCK) - np.maximum(c0, s0), 0, None)
    m = np.zeros((rows, cols), np.float32)
    m[:n_cmp, :n_sel] = inter / CMP_LEN
    return jnp.asarray(m)


def _transpose_tiles(x):
    n = x.shape[0] // LANE
    xf = x.astype(F32)
    return jnp.concatenate([xf[i * LANE:(i + 1) * LANE].T for i in range(n)], axis=1)


def _untranspose_tiles(xt):
    n = xt.shape[1] // LANE
    return jnp.concatenate([xt[:, i * LANE:(i + 1) * LANE].T for i in range(n)], axis=0)


def _fill_transposed(dst_ref, src_ref):
    def body(k, carry):
        st = pl.multiple_of(k * LANE, LANE)
        dst_ref[:, pl.ds(st, LANE)] = src_ref[0, 0, pl.ds(st, LANE), :].astype(F32).T.astype(BF16)
        return carry

    lax.fori_loop(0, src_ref.shape[2] // LANE, body, 0)


def _online_update_t(state, s_t, v_t):
    m, l, acc = state
    m_new = jnp.maximum(m, jnp.max(s_t, axis=0, keepdims=True))
    alpha = jnp.exp(m - m_new)
    p = jnp.exp(s_t - m_new)
    l = alpha * l + jnp.sum(p, axis=0, keepdims=True)
    acc = alpha * acc + jnp.dot(v_t, p.astype(BF16), preferred_element_type=F32)
    return m_new, l, acc


def _init_state_t(cols):
    return (jnp.full((1, cols), NEG, F32), jnp.zeros((1, cols), F32), jnp.zeros((HEAD_DIM, cols), F32))


FOX_TQ = 256
FOX_KEY_BLOCK = 1024


def _fox_prompt_kernel(q_ref, k_ref, v_ref, crow_ref, o_ref, vt_ref, ccol_ref):
    qt = pl.program_id(2)
    tq = FOX_TQ

    @pl.when(qt == 0)
    def _():
        _fill_transposed(vt_ref, v_ref)

        def body(k, carry):
            st = pl.multiple_of(k * LANE, LANE)
            row = crow_ref[0, 0, :, pl.ds(st, LANE)]
            ccol_ref[pl.ds(st, LANE), :] = jnp.broadcast_to(row, (LANE, LANE)).T
            return carry

        lax.fori_loop(0, k_ref.shape[2] // LANE, body, 0)

    q0 = pl.multiple_of(qt * tq, tq)
    q_t = _transpose_tiles(q_ref[0]).astype(BF16)
    c_q = crow_ref[0, 0, :, pl.ds(q0, tq)]
    blk = FOX_KEY_BLOCK

    def logits(start):
        s_t = jnp.dot(k_ref[0, 0, pl.ds(start, blk), :], q_t, preferred_element_type=F32)
        c_k = ccol_ref[pl.ds(start, blk), :]
        return s_t + (c_q - jnp.concatenate([c_k] * (tq // LANE), axis=1))

    def body(kb, state):
        start = pl.multiple_of(kb * blk, blk)
        return _online_update_t(state, logits(start), vt_ref[:, pl.ds(start, blk)])

    n_full = q0 // blk
    state = lax.fori_loop(0, n_full, body, _init_state_t(tq))
    start = pl.multiple_of(n_full * blk, blk)
    key = start + lax.broadcasted_iota(jnp.int32, (blk, tq), 0)
    qry = q0 + lax.broadcasted_iota(jnp.int32, (blk, tq), 1)
    s_t = jnp.where(key <= qry, logits(start), NEG)
    _, l, acc = _online_update_t(state, s_t, vt_ref[:, pl.ds(start, blk)])
    o_ref[...] = _untranspose_tiles(acc * (1.0 / l))


def _fox_prompt(hm, hm4, crow, nb, seq):
    nq = seq // FOX_TQ
    return pl.pallas_call(
        _fox_prompt_kernel,
        out_shape=jax.ShapeDtypeStruct((nb * seq, FOX_HEADS * HEAD_DIM), F32),
        grid=(nb, FOX_HEADS, nq),
        in_specs=[
            pl.BlockSpec((1, FOX_TQ, HEAD_DIM), lambda b, h, t: (HM_Q_FOX + h, b * nq + t, 0)),
            pl.BlockSpec((1, 1, seq, HEAD_DIM), lambda b, h, t: (HM_K_FOX + h, b, 0, 0)),
            pl.BlockSpec((1, 1, seq, HEAD_DIM), lambda b, h, t: (HM_V_FOX + h, b, 0, 0)),
            pl.BlockSpec((1, 1, 1, seq), lambda b, h, t: (b, h, 0, 0)),
        ],
        out_specs=pl.BlockSpec((FOX_TQ, HEAD_DIM), lambda b, h, t: (b * nq + t, h)),
        scratch_shapes=[pltpu.VMEM((HEAD_DIM, seq), BF16), pltpu.VMEM((seq, LANE), F32)],
        compiler_params=_cparams(("parallel", "parallel", "arbitrary")),
        name="fox_prompt",
    )(hm, hm4, hm4, crow)


NSA_TQ = LANE
NSA_COLS = NSA_GROUP * NSA_TQ
WIN_TILES = WINDOW // NSA_TQ + 1
SLC_BLOCK_TILES = 4


def _nsa_prompt_kernel(tbl_ref, q_ref, kc_ref, vc_ref, mselt_ref, ks_ref, vs_ref, kw_ref, vw_ref, gate_ref, o_ref,
                       vst_ref, vwt_ref, vct_ref, wb_ref, pc_ref, drop_ref, *, n_sel, cmp_back):
    g = pl.program_id(1)
    qt = pl.program_id(2)
    tq = NSA_TQ
    cols = NSA_COLS
    heads = [NSA_GROUP * g + hh for hh in range(NSA_GROUP)]
    ncp = kc_ref.shape[2]
    nj = mselt_ref.shape[0]

    @pl.when(qt == 0)
    def _():
        _fill_transposed(vst_ref, vs_ref)
        _fill_transposed(vwt_ref, vw_ref)
        vct_ref[...] = _transpose_tiles(vc_ref[0, 0]).astype(BF16)
        key = lax.broadcasted_iota(jnp.int32, (tq, tq), 0)
        qry = lax.broadcasted_iota(jnp.int32, (tq, tq), 1)
        blk = lax.broadcasted_iota(jnp.int32, (pc_ref.shape[0], tq), 0) - cmp_back
        d_cmp = lax.broadcasted_iota(jnp.int32, (pc_ref.shape[0], tq), 1) - (blk * CMP_STRIDE + (CMP_LEN - 1))
        for hh in range(NSA_GROUP):
            sl = slice(hh * tq, (hh + 1) * tq)
            far = tbl_ref[REL_BUCKETS - 1, heads[hh]]
            for delta in range(2):
                d = delta * tq + qry - key
                wb_ref[delta, :, sl] = jnp.where(d >= 0, _rel_bias(jnp.maximum(d, 0), tbl_ref, heads[hh]), NEG)
            for delta in range(2, WIN_TILES - 1):
                wb_ref[delta, :, sl] = jnp.full((tq, tq), far, F32)
            wb_ref[WIN_TILES - 1, :, sl] = jnp.where(qry < key, far, NEG)
            wb_ref[WIN_TILES, :, sl] = jnp.full((tq, tq), NEG, F32)
            pc_ref[:, sl] = jnp.where(d_cmp >= 0, _rel_bias(jnp.maximum(d_cmp, 0), tbl_ref, heads[hh]), NEG)

    q_t = jnp.concatenate([q_ref[hh].astype(F32).T for hh in range(NSA_GROUP)], axis=1).astype(BF16)

    off = pl.multiple_of(cmp_back - qt * (tq // CMP_STRIDE), 8)
    s_t = jnp.dot(kc_ref[0, 0], q_t, preferred_element_type=F32) + pc_ref[pl.ds(off, ncp), :]
    m = jnp.max(s_t, axis=0, keepdims=True)
    e = jnp.exp(s_t - m)
    inv = jnp.where(m > 0.5 * NEG, 1.0 / jnp.sum(e, axis=0, keepdims=True), 0.0)
    p_t = e * inv
    o_cmp = jnp.dot(vct_ref[...], p_t.astype(BF16), preferred_element_type=F32)
    p_sum = p_t[:, :tq]
    for hh in range(1, NSA_GROUP):
        p_sum = p_sum + p_t[:, hh * tq:(hh + 1) * tq]
    imp = jnp.dot(mselt_ref[...], p_sum, preferred_element_type=F32, precision=lax.Precision.HIGHEST)

    jrow = lax.broadcasted_iota(jnp.int32, (nj, tq), 0)
    pos = qt * tq + lax.broadcasted_iota(jnp.int32, (nj, tq), 1)
    blk_q = jnp.right_shift(pos, int(math.log2(SEL_BLOCK)))
    forced = (jrow == 0) | (jrow == blk_q) | (jrow == blk_q - 1)
    score = jnp.where(forced, FORCED_SCORE, jnp.where(jrow <= blk_q, imp, -1.0))
    score = jnp.where(jrow < n_sel, score, -2.0)
    ranks = []
    for r in range(nj // 8):
        mine = score[r * 8:(r + 1) * 8]
        jmine = jrow[r * 8:(r + 1) * 8]
        rank = jnp.zeros((8, tq), F32)
        for other in range(n_sel):
            row = jnp.broadcast_to(score[other:other + 1], (8, tq))
            if other < r * 8:
                beats = row >= mine
            elif other >= (r + 1) * 8:
                beats = row > mine
            else:
                beats = (row > mine) | ((row == mine) & (jmine > other))
            rank = rank + jnp.where(beats, 1.0, 0.0)
        ranks.append(rank)
    rank = jnp.concatenate(ranks, axis=0)
    drop = jnp.where((rank < SEL_TOPK) & (jrow < n_sel), 0.0, NEG)
    drop = jnp.concatenate([drop] * NSA_GROUP, axis=1)
    for j in range(nj):
        drop_ref[j] = drop[j:j + 1]

    def tile_bias(kt, far_index):
        delta = qt - kt
        return wb_ref[jnp.where(delta < 0, WIN_TILES, jnp.minimum(delta, far_index))]

    per_tile = tq // SEL_BLOCK
    blk = SLC_BLOCK_TILES * tq

    def slc_body(kb, state):
        start = pl.multiple_of(kb * blk, blk)
        terms = []
        for i in range(SLC_BLOCK_TILES):
            kt = kb * SLC_BLOCK_TILES + i
            masks = [jnp.broadcast_to(drop_ref[kt * per_tile + h], (SEL_BLOCK, cols)) for h in range(per_tile)]
            terms.append(tile_bias(kt, 2) + jnp.concatenate(masks, axis=0))
        s = jnp.dot(ks_ref[0, 0, pl.ds(start, blk), :], q_t, preferred_element_type=F32)
        return _online_update_t(state, s + jnp.concatenate(terms, axis=0), vst_ref[:, pl.ds(start, blk)])

    _, l, acc = lax.fori_loop(0, qt // SLC_BLOCK_TILES + 1, slc_body, _init_state_t(cols))
    o_slc = acc * (1.0 / l)

    w0 = jnp.maximum(qt - (WIN_TILES - 1), 0)
    start = pl.multiple_of(w0 * tq, tq)
    span = WIN_TILES * tq
    s = jnp.dot(kw_ref[0, 0, pl.ds(start, span), :], q_t, preferred_element_type=F32)
    s = s + jnp.concatenate([tile_bias(w0 + i, WIN_TILES - 1) for i in range(WIN_TILES)], axis=0)
    p = jnp.exp(s - jnp.max(s, axis=0, keepdims=True))
    o_win = jnp.dot(vwt_ref[:, pl.ds(start, span)], p.astype(BF16), preferred_element_type=F32)
    o_win = o_win * (1.0 / jnp.sum(p, axis=0, keepdims=True))

    gate = gate_ref[0, 0, 0]
    o_t = gate[0:1] * o_cmp + gate[1:2] * o_slc + gate[2:3] * o_win
    for hh in range(NSA_GROUP):
        o_ref[:, hh * HEAD_DIM:(hh + 1) * HEAD_DIM] = o_t[:, hh * tq:(hh + 1) * tq].T


def _nsa_prompt(rel_table, hm, hm4, kcv, mselt, gates_t, nb, seq, n_sel):
    tq = NSA_TQ
    nq = seq // tq
    ncp = kcv.shape[2]
    nj = mselt.shape[0]
    cmp_back = (nq - 1) * (tq // CMP_STRIDE)
    kern = functools.partial(_nsa_prompt_kernel, n_sel=n_sel, cmp_back=cmp_back)
    kv_spec = lambda head0: pl.BlockSpec((1, 1, seq, HEAD_DIM), lambda b, g, t: (head0 + g, b, 0, 0))
    return pl.pallas_call(
        kern,
        out_shape=jax.ShapeDtypeStruct((nb * seq, NSA_HEADS * HEAD_DIM), F32),
        grid=(nb, NSA_KV_HEADS, nq),
        in_specs=[
            pl.BlockSpec(memory_space=pltpu.SMEM),
            pl.BlockSpec((NSA_GROUP, tq, HEAD_DIM), lambda b, g, t: (g, b * nq + t, 0)),
            pl.BlockSpec((1, 1, ncp, HEAD_DIM), lambda b, g, t: (g, b, 0, 0)),
            pl.BlockSpec((1, 1, ncp, HEAD_DIM), lambda b, g, t: (NSA_KV_HEADS + g, b, 0, 0)),
            pl.BlockSpec((nj, ncp), lambda b, g, t: (0, 0)),
            kv_spec(HM_K_SLC), kv_spec(HM_V_SLC), kv_spec(HM_K_WIN), kv_spec(HM_V_WIN),
            pl.BlockSpec((1, 1, 1, 8, NSA_COLS), lambda b, g, t: (b, g, t, 0, 0)),
        ],
        out_specs=pl.BlockSpec((tq, NSA_GROUP * HEAD_DIM), lambda b, g, t: (b * nq + t, g)),
        scratch_shapes=[
            pltpu.VMEM((HEAD_DIM, seq), BF16),
            pltpu.VMEM((HEAD_DIM, seq), BF16),
            pltpu.VMEM((HEAD_DIM, ncp), BF16),
            pltpu.VMEM((WIN_TILES + 1, tq, NSA_COLS), F32),
            pltpu.VMEM((cmp_back + ncp, NSA_COLS), F32),
            pltpu.VMEM((nj, 1, NSA_COLS), F32),
        ],
        compiler_params=_cparams(("parallel", "parallel", "arbitrary")),
        name="nsa_prompt",
    )(rel_table, hm, kcv, kcv, mselt, hm4, hm4, hm4, hm4, gates_t)


CHUNK_ROWS = 512


def _chunkify_kernel(x_ref, o_ref, *, n_cols):
    n = o_ref.shape[2]
    for c in range(o_ref.shape[0]):
        for s in range(CMP_STRIDE):
            o_ref[c, 0, :, s * HEAD_DIM:(s + 1) * HEAD_DIM] = (
                x_ref[0, pl.ds(s * n_cols + c, n, stride=CMP_STRIDE * n_cols), :].astype(BF16))


def _chunkify(rows3, n_cols, n_heads, seq):
    nb = rows3.shape[0]
    tr = _largest_tile(seq, (CHUNK_ROWS, 256))
    return pl.pallas_call(
        functools.partial(_chunkify_kernel, n_cols=n_cols),
        out_shape=jax.ShapeDtypeStruct((n_heads, nb, seq // CMP_STRIDE, CMP_STRIDE * HEAD_DIM), BF16),
        grid=(nb, seq // tr),
        in_specs=[pl.BlockSpec((1, tr * n_cols, HEAD_DIM), lambda b, i: (b, i, 0))],
        out_specs=pl.BlockSpec((n_heads, 1, tr // CMP_STRIDE, CMP_STRIDE * HEAD_DIM), lambda b, i: (0, b, i, 0)),
        compiler_params=_cparams(("parallel", "parallel")),
        name="chunkify",
    )(rows3)


REGROUP_PAGES = 4


N_CACHE_COLS = 16
N_CMP_COLS = 2 * NSA_KV_HEADS


def _regroup_kernel(pt_ref, *refs):
    del pt_ref
    npg = REGROUP_PAGES
    nsa_in, fox_in, lf_in = refs[:npg], refs[npg:2 * npg], refs[2 * npg:3 * npg]
    xc_out, slc_out, fox_out, lf_out = refs[3 * npg:]
    is_tail = pl.program_id(1) == pl.num_programs(1) - 1
    chunks = PAGE_SIZE // CMP_STRIDE
    chunk_stride = CMP_STRIDE * N_CACHE_COLS

    @pl.when(is_tail)
    def _():
        slc_out[...] = jnp.zeros(slc_out.shape, slc_out.dtype)
        fox_out[...] = jnp.zeros(fox_out.shape, fox_out.dtype)
        lf_out[...] = jnp.zeros(lf_out.shape, lf_out.dtype)

    @pl.when(jnp.logical_not(is_tail))
    def _():
        for p in range(npg):
            rows = slice(p * PAGE_SIZE, (p + 1) * PAGE_SIZE)
            for c in range(N_CACHE_COLS - N_CMP_COLS):
                slc_out[c, 0, rows, :] = nsa_in[p][0, pl.ds(N_CMP_COLS + c, PAGE_SIZE, stride=N_CACHE_COLS), :].astype(BF16)
            for c in range(N_CACHE_COLS):
                fox_out[c, 0, rows, :] = fox_in[p][0, pl.ds(c, PAGE_SIZE, stride=N_CACHE_COLS), :].astype(BF16)
            lf_out[0, rows, :] = lf_in[p][0]
        for pair in range(npg // 2):
            for c in range(N_CMP_COLS):
                for s in range(CMP_STRIDE):
                    first = s * N_CACHE_COLS + c
                    both = [nsa_in[2 * pair + i][0, pl.ds(first, chunks, stride=chunk_stride), :] for i in range(2)]
                    xc_out[c, 0, pair * 2 * chunks:(pair + 1) * 2 * chunks, s * HEAD_DIM:(s + 1) * HEAD_DIM] = (
                        jnp.concatenate(both, axis=0).astype(BF16))


def _regroup(page_table, cache_nsa, cache_fox, cache_logf, lk):
    nb, n_pages = page_table.shape
    npg = REGROUP_PAGES
    n_steps = n_pages // npg
    rows = npg * PAGE_SIZE
    last = n_steps - 1
    n_slc = N_CACHE_COLS - N_CMP_COLS
    chunk_w = CMP_STRIDE * HEAD_DIM

    def page_map(p):
        return lambda b, s, pt: (pt[b, jnp.minimum(s, last) * npg + p], 0, 0)

    def specs(arr):
        return [pl.BlockSpec((1,) + arr.shape[1:], page_map(p)) for p in range(npg)]

    grid_spec = pltpu.PrefetchScalarGridSpec(
        num_scalar_prefetch=1,
        grid=(nb, n_steps + 1),
        in_specs=specs(cache_nsa) + specs(cache_fox) + specs(cache_logf),
        out_specs=(
            pl.BlockSpec((N_CMP_COLS, 1, rows // CMP_STRIDE, chunk_w), lambda b, s, pt: (0, b, jnp.minimum(s, last), 0)),
            pl.BlockSpec((n_slc, 1, rows, HEAD_DIM), lambda b, s, pt: (0, b, s, 0)),
            pl.BlockSpec((N_CACHE_COLS, 1, rows, HEAD_DIM), lambda b, s, pt: (0, b, s, 0)),
            pl.BlockSpec((1, rows, cache_logf.shape[2]), lambda b, s, pt: (b, s, 0)),
        ),
    )
    return pl.pallas_call(
        _regroup_kernel,
        out_shape=(
            jax.ShapeDtypeStruct((N_CMP_COLS, nb, n_pages * PAGE_SIZE // CMP_STRIDE, chunk_w), BF16),
            jax.ShapeDtypeStruct((n_slc, nb, lk, HEAD_DIM), BF16),
            jax.ShapeDtypeStruct((N_CACHE_COLS, nb, lk, HEAD_DIM), BF16),
            jax.ShapeDtypeStruct((nb, lk, cache_logf.shape[2]), F32),
        ),
        grid_spec=grid_spec,
        compiler_params=_cparams(("parallel", "arbitrary")),
        name="cache_regroup",
    )(page_table, *([cache_nsa] * npg), *([cache_fox] * npg), *([cache_logf] * npg))


def _largest_tile(n, candidates):
    for c in candidates:
        if n % c == 0:
            return c
    raise ValueError(f"no tile in {candidates} divides {n}")


def _token_stage_in(x, p, tm):
    x1 = _ffn(x, p["norm_ffn1"], p["wg1"], p["wu1"], p["wd1"], tm, p["tf"])
    nsa_rows, win_rows, fox_rows, hm = _inproj(x1, p["norm_mix"], p["w_main"], p["colgain"], p["colflag"], tm)
    small = _small(x1, p["norm_mix"], p["w_small"], p["b_small"], tm)
    return x1, nsa_rows, win_rows, fox_rows, hm, small


def _token_stage_out(x1, o_nsa, o_fox, p, tm):
    x2 = _outproj(o_nsa, o_fox, p["out_norm_nsa"], p["out_norm_fox"], p["w_out"], x1, min(tm, 256))
    return _ffn(x2, p["norm_ffn2"], p["wg2"], p["wu2"], p["wd2"], tm, p["tf"])


def kernel(x_prompt, x_sample, cache_nsa_kv, cache_fox_kv, cache_fox_logf, state_win_kv, page_table, rel_table, norm_ffn1, ffn1_gate, ffn1_up, ffn1_down, norm_mix, w_in, nsa_gate_bias, fox_forget_bias, q_norm_nsa, k_norm_nsa, q_norm_fox, k_norm_fox, cmp_pos_k, cmp_w1_k, cmp_w2_k, cmp_pos_v, cmp_w1_v, cmp_w2_v, out_norm_nsa, out_norm_fox, w_out, norm_ffn2, ffn2_gate, ffn2_up, ffn2_down):
    depth = w_in.shape[0]
    assert depth == 1, "single-layer trunk"
    nbp, seq, d_model = x_prompt.shape
    nbd, dseq, _ = x_sample.shape
    n_pages = page_table.shape[1]
    past = n_pages * PAGE_SIZE
    d_ff = ffn1_gate.shape[2]
    nsa_w = NSA_HEADS * HEAD_DIM
    kv6_w = 6 * NSA_KV_HEADS * HEAD_DIM
    fox_w = 3 * FOX_HEADS * HEAD_DIM
    off_gate = nsa_w + kv6_w
    off_fox = off_gate + N_GATE_COLS
    off_forget = off_fox + fox_w
    assert w_in.shape[2] == off_forget + FOX_HEADS and d_model == nsa_w + FOX_HEADS * HEAD_DIM
    assert seq % LANE == 0 and seq >= WINDOW and past % LANE == 0 and n_pages % REGROUP_PAGES == 0
    assert dseq <= 16 and state_win_kv.shape[2] == WINDOW
    assert seq % FOX_KEY_BLOCK == 0 and seq % (SLC_BLOCK_TILES * NSA_TQ) == 0 and seq >= WIN_TILES * NSA_TQ

    w0 = w_in[0]
    ones = lambda n: jnp.ones((n,), F32)
    zeros = lambda n: jnp.zeros((n,), F32)
    kn, kvw = NSA_KV_HEADS, NSA_KV_HEADS * HEAD_DIM
    qk_scale = HEAD_DIM ** -0.5
    p = {
        "tf": _largest_tile(d_ff, (512, 256, 128)),
        "norm_ffn1": norm_ffn1[0][None], "norm_mix": norm_mix[0][None], "norm_ffn2": norm_ffn2[0][None],
        "wg1": ffn1_gate[0].astype(BF16), "wu1": ffn1_up[0].astype(BF16), "wd1": ffn1_down[0].astype(BF16),
        "wg2": ffn2_gate[0].astype(BF16), "wu2": ffn2_up[0].astype(BF16), "wd2": ffn2_down[0].astype(BF16),
        "w_main": jnp.concatenate([w0[:, :off_gate], w0[:, off_fox:off_forget]], axis=1).astype(BF16),
        "w_small": jnp.concatenate([w0[:, off_gate:off_fox], w0[:, off_forget:],
                                    jnp.zeros((d_model, LANE - N_GATE_COLS - FOX_HEADS), F32)], axis=1).astype(BF16),
        "b_small": jnp.concatenate([nsa_gate_bias[0].reshape(-1), fox_forget_bias[0],
                                    zeros(LANE - N_GATE_COLS - FOX_HEADS)])[None],
        "colgain": jnp.concatenate([
            jnp.tile(q_norm_nsa[0] * qk_scale, NSA_HEADS), ones(2 * kvw), jnp.tile(k_norm_nsa[0], kn), ones(kvw),
            jnp.tile(k_norm_nsa[0], kn), ones(kvw), jnp.tile(q_norm_fox[0] * qk_scale, FOX_HEADS),
            jnp.tile(k_norm_fox[0], FOX_HEADS), ones(FOX_HEADS * HEAD_DIM)])[None],
        "colflag": jnp.concatenate([
            ones(nsa_w), zeros(2 * kvw), ones(kvw), zeros(kvw), ones(kvw), zeros(kvw),
            ones(2 * FOX_HEADS * HEAD_DIM), zeros(FOX_HEADS * HEAD_DIM)])[None],
        "out_norm_nsa": out_norm_nsa[0][None], "out_norm_fox": out_norm_fox[0][None],
        "w_out": w_out[0].astype(BF16),
    }
    half = CMP_STRIDE * HEAD_DIM

    def cmp_w1(w):
        return jnp.concatenate([w[0, :half], w[0, half:]], axis=1)

    def cmp_pe(pe):
        return jnp.concatenate([pe[0].reshape(CMP_LEN // CMP_STRIDE, half), jnp.zeros((PE_ROWS - CMP_LEN // CMP_STRIDE, half), F32)], axis=0)

    w1cat = jnp.stack([cmp_w1(cmp_w1_k), cmp_w1(cmp_w1_v)]).astype(BF16)
    w2cat = jnp.stack([cmp_w2_k[0], cmp_w2_v[0]]).astype(BF16)
    pecat = jnp.stack([cmp_pe(cmp_pos_k), cmp_pe(cmp_pos_v)]).astype(BF16)
    k_norm_row = k_norm_nsa[0][None]

    mp = nbp * seq
    tm_p = _largest_tile(mp, (512, 256, 128))
    x1, nsa_rows, win_rows, fox_rows, hm, small = _token_stage_in(x_prompt.reshape(mp, d_model), p, tm_p)
    hm4 = hm.reshape(N_HEAD_COLS, nbp, seq, HEAD_DIM)

    logf = small[:, N_GATE_COLS:N_GATE_COLS + FOX_HEADS]
    csum = _cumsum(logf.reshape(nbp, seq, FOX_HEADS).transpose(0, 2, 1))
    o_fox = _fox_prompt(hm, hm4, csum[:, :, None, :], nbp, seq)

    n_chunk = seq // CMP_STRIDE
    n_cmp = (seq - CMP_LEN) // CMP_STRIDE + 1
    n_sel = -(-seq // SEL_BLOCK)
    xc = _chunkify(nsa_rows.reshape(nbp, seq * N_CACHE_COLS, HEAD_DIM), N_CACHE_COLS, N_CMP_COLS, seq)
    kcv = _compress(xc, 0, w1cat, w2cat, pecat, k_norm_row, n_chunk)
    mselt = _cmp_to_sel(n_cmp, n_sel, n_chunk, -(-n_sel // 8) * 8).T
    nq = seq // NSA_TQ
    gates_t = small[:, :N_GATE_COLS].reshape(nbp, nq, NSA_TQ, NSA_KV_HEADS, NSA_GROUP, 3)
    gates_t = gates_t.transpose(0, 3, 1, 5, 4, 2).reshape(nbp, NSA_KV_HEADS, nq, 3, NSA_COLS)
    gates_t = jnp.pad(gates_t, ((0, 0), (0, 0), (0, 0), (0, 8 - 3), (0, 0)))
    o_nsa = _nsa_prompt(rel_table, hm, hm4, kcv, mselt, gates_t, nbp, seq, n_sel)
    y_p = _token_stage_out(x1, o_nsa, o_fox, p, tm_p)

    ms = nbd * dseq
    tq_d = 16
    lk = past + LANE
    xs1, nsa_rows_s, win_rows_s, fox_rows_s, hm_s, small_s = _token_stage_in(x_sample.reshape(ms, d_model), p, ms)
    xc_d, nsa_dec, fox_dec, lf_dec = _regroup(
        page_table,
        cache_nsa_kv.reshape(cache_nsa_kv.shape[1], PAGE_SIZE * N_CACHE_COLS, HEAD_DIM),
        cache_fox_kv.reshape(cache_fox_kv.shape[1], PAGE_SIZE * N_CACHE_COLS, HEAD_DIM),
        cache_fox_logf[0], lk)
    hm_s4 = hm_s.reshape(N_HEAD_COLS, nbd, dseq, HEAD_DIM)
    nsa_dec = lax.dynamic_update_slice(nsa_dec, hm_s4[HM_K_SLC:HM_K_WIN], (0, 0, past, 0))
    fox_dec = lax.dynamic_update_slice(fox_dec, hm_s4[HM_K_FOX:HM_V_FOX + FOX_HEADS], (0, 0, past, 0))
    logf_s = small_s[:, N_GATE_COLS:N_GATE_COLS + FOX_HEADS].reshape(nbd, dseq, FOX_HEADS)
    lf_dec = lax.dynamic_update_slice(lf_dec, logf_s, (0, past, 0))

    pad_q = ((0, 0), (0, 0), (0, tq_d - dseq), (0, 0))
    q_dec = jnp.pad(hm_s4, pad_q).reshape(N_HEAD_COLS, nbd * tq_d, HEAD_DIM)
    gates_dec = jnp.pad(small_s.reshape(nbd, dseq, LANE), pad_q[1:]).reshape(nbd * tq_d, LANE)

    csum_d = _cumsum(lf_dec.transpose(0, 2, 1))
    tk_d = _largest_tile(past, (2048, 1024, 512, 256, 128))
    o_fox_s = _fox_attention(q_dec, fox_dec, csum_d[:, :, None, :], csum_d[:, :, past:past + tq_d, None],
                             q_head0=HM_Q_FOX, k_head0=0, v_head0=FOX_HEADS,
                             nb=nbd, nq=1, tq=tq_d, tk=tk_d, td=LANE, qoff=past)

    n_chunk_d = past // CMP_STRIDE
    n_cmp_d = (past + dseq - CMP_LEN) // CMP_STRIDE + 1
    n_sel_d = -(-(past + dseq) // SEL_BLOCK)
    assert n_cmp_d + CMP_LEN // CMP_STRIDE - 1 <= n_chunk_d, "compressed blocks must lie in the cached rows"
    kcv_d = _compress(xc_d, 0, w1cat, w2cat, pecat, k_norm_row, n_chunk_d)
    msel_d = _cmp_to_sel(n_cmp_d, n_sel_d, n_chunk_d, -(-n_sel_d // LANE) * LANE)
    win_old = state_win_kv[0].transpose(2, 3, 0, 1, 4).reshape(2 * NSA_KV_HEADS, nbd, WINDOW, HEAD_DIM).astype(BF16)
    win_dec = jnp.concatenate([win_old, hm_s4[HM_K_WIN:HM_V_WIN + NSA_KV_HEADS],
                               jnp.zeros((2 * NSA_KV_HEADS, nbd, LANE - dseq, HEAD_DIM), BF16)], axis=2)
    o_nsa_s = _nsa_attention(rel_table, q_dec, kcv_d, msel_d, nsa_dec, win_dec, gates_dec,
                             ks_head0=0, vs_head0=NSA_KV_HEADS, kw_head0=0, vw_head0=NSA_KV_HEADS,
                             nb=nbd, nq=1, tq=tq_d, tk=tk_d, td=LANE, qoff=past, wpos0=past - WINDOW,
                             n_sel=n_sel_d)
    unpad = lambda o: o.reshape(nbd, tq_d, -1)[:, :dseq].reshape(ms, -1)
    y_s = _token_stage_out(xs1, unpad(o_nsa_s), unpad(o_fox_s), p, ms)

    kvh = (NSA_KV_HEADS, HEAD_DIM)
    win_keep = min(WINDOW, seq)
    win_p = win_rows.reshape(nbp, seq, 2, *kvh)[:, seq - win_keep:]
    win_s = jnp.concatenate([state_win_kv[0], win_rows_s.reshape(nbd, dseq, 2, *kvh)], axis=1)[:, dseq:]
    return (
        y_p.reshape(nbp, seq, d_model),
        y_s.reshape(nbd, dseq, d_model),
        nsa_rows.reshape(1, nbp, seq, 4, *kvh),
        fox_rows.reshape(1, nbp, seq, 2, FOX_HEADS, HEAD_DIM),
        logf.reshape(1, nbp, seq, FOX_HEADS),
        win_p[None],
        nsa_rows_s.reshape(1, nbd, dseq, 4, *kvh),
        fox_rows_s.reshape(1, nbd, dseq, 2, FOX_HEADS, HEAD_DIM),
        logf_s[None],
        win_s[None],
    )
```

```python
import functools
import math

import numpy as np
import jax
import jax.numpy as jnp
from jax import lax
from jax.experimental import pallas as pl
from jax.experimental.pallas import tpu as pltpu

HEAD_DIM = 128
NSA_HEADS = 8
FOX_HEADS = 8
NSA_KV_HEADS = 4
NSA_GROUP = NSA_HEADS // NSA_KV_HEADS
CMP_LEN = 32
CMP_STRIDE = 16
CMP_HIDDEN = 512
SEL_BLOCK = 64
SEL_TOPK = 16
WINDOW = 512
REL_BUCKETS = 32
REL_MAX_DIST = 128
RMS_EPS = 1e-6
PAGE_SIZE = 128

LANE = 128
NEG = -1e30
FORCED_SCORE = 1e30
VMEM_LIMIT = 56 * 1024 * 1024

BF16 = jnp.bfloat16
F32 = jnp.float32
NT_DIMS = (((1,), (1,)), ((), ()))


def _bucket_thresholds():
    n = np.arange(0, 4 * REL_MAX_DIST)
    max_exact = REL_BUCKETS // 2
    nf = np.maximum(n, 1).astype(np.float32)
    large = max_exact + (np.log(nf / max_exact) / math.log(REL_MAX_DIST / max_exact)
                         * (REL_BUCKETS - max_exact)).astype(np.int32)
    bucket = np.where(n < max_exact, n, np.minimum(large, REL_BUCKETS - 1))
    return [int(np.min(n[bucket >= k])) for k in range(1, REL_BUCKETS)]


BUCKET_THR = _bucket_thresholds()
FAR_DIST = BUCKET_THR[-1]


def _cparams(sem):
    return pltpu.CompilerParams(dimension_semantics=sem, vmem_limit_bytes=VMEM_LIMIT)


def _rms_rows(x, gain):
    ms = jnp.mean(x * x, axis=-1, keepdims=True)
    return x * lax.rsqrt(ms + RMS_EPS) * gain


def _ffn_kernel(x_ref, g_ref, wg_ref, wu_ref, wd_ref, o_ref, xn_ref):
    @pl.when(pl.program_id(1) == 0)
    def _():
        x = x_ref[...]
        xn_ref[...] = _rms_rows(x, g_ref[...]).astype(BF16)
        o_ref[...] = x

    xn = xn_ref[...]
    a = jnp.dot(xn, wg_ref[...], preferred_element_type=F32)
    u = jnp.dot(xn, wu_ref[...], preferred_element_type=F32)
    h = (a / (1.0 + jnp.exp(-a))) * u * 0.5
    o_ref[...] += jnp.dot(h.astype(BF16), wd_ref[...], preferred_element_type=F32)


def _ffn(x, gain, wg, wu, wd, tm, tf):
    m, d = x.shape
    f = wg.shape[1]
    return pl.pallas_call(
        _ffn_kernel,
        out_shape=jax.ShapeDtypeStruct((m, d), F32),
        grid=(m // tm, f // tf),
        in_specs=[
            pl.BlockSpec((tm, d), lambda i, j: (i, 0)),
            pl.BlockSpec((1, d), lambda i, j: (0, 0)),
            pl.BlockSpec((d, tf), lambda i, j: (0, j)),
            pl.BlockSpec((d, tf), lambda i, j: (0, j)),
            pl.BlockSpec((tf, d), lambda i, j: (j, 0)),
        ],
        out_specs=pl.BlockSpec((tm, d), lambda i, j: (i, 0)),
        scratch_shapes=[pltpu.VMEM((tm, d), BF16)],
        compiler_params=_cparams(("parallel", "arbitrary")),
        name="ffn",
    )(x, gain, wg, wu, wd)


IN_TN = 4 * HEAD_DIM
J_NSA = (2, 6)
J_WIN = (6, 8)
J_FOX = (10, 14)
N_HEAD_COLS = 56
HM_Q_NSA, HM_K_CMP, HM_K_SLC, HM_V_SLC, HM_K_WIN, HM_V_WIN = 0, 8, 16, 20, 24, 28
HM_Q_FOX, HM_K_FOX, HM_V_FOX = 32, 40, 48


IN_NORMED_TILES = (0, 1, 4, 6, 8, 9, 10, 11)


def _inproj_kernel(x_ref, g_ref, w_ref, cg_ref, ones_ref, nsa_ref, win_ref, fox_ref, hm_ref, xn_ref):
    j = pl.program_id(1)

    @pl.when(j == 0)
    def _():
        xn_ref[...] = _rms_rows(x_ref[...], g_ref[...]).astype(BF16)

    res = jnp.dot(xn_ref[...], w_ref[...], preferred_element_type=F32)
    tm = res.shape[0]
    heads = IN_TN // HEAD_DIM

    def emit(vals):
        for hh in range(heads):
            hm_ref[hh] = vals[:, hh * HEAD_DIM:(hh + 1) * HEAD_DIM].astype(BF16)
        for ref, (jlo, jhi) in ((nsa_ref, J_NSA), (win_ref, J_WIN), (fox_ref, J_FOX)):
            n_cols = (jhi - jlo) * heads

            @pl.when((j >= jlo) & (j < jhi))
            def _(ref=ref, jlo=jlo, n_cols=n_cols):
                for hh in range(heads):
                    ref[pl.ds((j - jlo) * heads + hh, tm, stride=n_cols), :] = vals[:, hh * HEAD_DIM:(hh + 1) * HEAD_DIM]

    is_normed = functools.reduce(jnp.logical_or, [j == t for t in IN_NORMED_TILES])

    @pl.when(is_normed)
    def _():
        sumsq = jnp.dot((res * res).astype(BF16), ones_ref[...], preferred_element_type=F32)
        emit(res * lax.rsqrt(sumsq * (1.0 / HEAD_DIM) + RMS_EPS) * cg_ref[...])

    @pl.when(jnp.logical_not(is_normed))
    def _():
        emit(res)


def _inproj(x, gain, w_main, colgain, tm):
    m, d = x.shape
    ncol = w_main.shape[1]
    nj = ncol // IN_TN

    heads = IN_TN // HEAD_DIM
    n_nsa, n_win, n_fox = [(hi - lo) * heads for lo, hi in (J_NSA, J_WIN, J_FOX)]
    head_ones = jnp.asarray(np.kron(np.eye(heads, dtype=np.float32), np.ones((HEAD_DIM, HEAD_DIM), np.float32)), BF16)
    return pl.pallas_call(
        _inproj_kernel,
        out_shape=(
            jax.ShapeDtypeStruct((m * n_nsa, HEAD_DIM), F32),
            jax.ShapeDtypeStruct((m * n_win, HEAD_DIM), F32),
            jax.ShapeDtypeStruct((m * n_fox, HEAD_DIM), F32),
            jax.ShapeDtypeStruct((N_HEAD_COLS, m, HEAD_DIM), BF16),
        ),
        grid=(m // tm, nj),
        in_specs=[
            pl.BlockSpec((tm, d), lambda i, j: (i, 0)),
            pl.BlockSpec((1, d), lambda i, j: (0, 0)),
            pl.BlockSpec((d, IN_TN), lambda i, j: (0, j)),
            pl.BlockSpec((1, IN_TN), lambda i, j: (0, j)),
            pl.BlockSpec((IN_TN, IN_TN), lambda i, j: (0, 0)),
        ],
        out_specs=(
            pl.BlockSpec((tm * n_nsa, HEAD_DIM), lambda i, j: (i, 0)),
            pl.BlockSpec((tm * n_win, HEAD_DIM), lambda i, j: (i, 0)),
            pl.BlockSpec((tm * n_fox, HEAD_DIM), lambda i, j: (i, 0)),
            pl.BlockSpec((heads, tm, HEAD_DIM), lambda i, j: (j, i, 0)),
        ),
        scratch_shapes=[pltpu.VMEM((tm, d), BF16)],
        compiler_params=_cparams(("parallel", "arbitrary")),
        name="inproj",
    )(x, gain, w_main, colgain, head_ones)


N_GATE_COLS = 3 * NSA_HEADS


def _small_kernel(x_ref, g_ref, w_ref, b_ref, o_ref):
    xn = _rms_rows(x_ref[...], g_ref[...]).astype(BF16)
    z = jnp.dot(xn, w_ref[...], preferred_element_type=F32) + b_ref[...]
    lane = lax.broadcasted_iota(jnp.int32, z.shape, 1)
    sig = 1.0 / (1.0 + jnp.exp(-z))
    logsig = jnp.minimum(z, 0.0) - jnp.log(1.0 + jnp.exp(-jnp.abs(z)))
    o_ref[...] = jnp.where(lane < N_GATE_COLS, sig,
                           jnp.where(lane < N_GATE_COLS + FOX_HEADS, logsig, 0.0))


def _small(x, gain, w_small, b_small, tm):
    m, d = x.shape
    return pl.pallas_call(
        _small_kernel,
        out_shape=jax.ShapeDtypeStruct((m, LANE), F32),
        grid=(m // tm,),
        in_specs=[
            pl.BlockSpec((tm, d), lambda i: (i, 0)),
            pl.BlockSpec((1, d), lambda i: (0, 0)),
            pl.BlockSpec((d, LANE), lambda i: (0, 0)),
            pl.BlockSpec((1, LANE), lambda i: (0, 0)),
        ],
        out_specs=pl.BlockSpec((tm, LANE), lambda i: (i, 0)),
        compiler_params=_cparams(("parallel",)),
        name="gates",
    )(x, gain, w_small, b_small)


def _outproj_kernel(on_ref, of_ref, gn_ref, gf_ref, w_ref, x_ref, y_ref):
    a = _rms_rows(on_ref[...], gn_ref[...]).astype(BF16)
    b = _rms_rows(of_ref[...], gf_ref[...]).astype(BF16)
    half = a.shape[1]
    y = jnp.dot(a, w_ref[:half, :], preferred_element_type=F32)
    y = y + jnp.dot(b, w_ref[half:, :], preferred_element_type=F32)
    y_ref[...] = x_ref[...] + y


def _outproj(o_nsa, o_fox, g_nsa, g_fox, w_out, x, tm):
    m, d = x.shape
    wn = o_nsa.shape[1]
    wf = o_fox.shape[1]
    return pl.pallas_call(
        _outproj_kernel,
        out_shape=jax.ShapeDtypeStruct((m, d), F32),
        grid=(m // tm,),
        in_specs=[
            pl.BlockSpec((tm, wn), lambda i: (i, 0)),
            pl.BlockSpec((tm, wf), lambda i: (i, 0)),
            pl.BlockSpec((1, wn), lambda i: (0, 0)),
            pl.BlockSpec((1, wf), lambda i: (0, 0)),
            pl.BlockSpec((wn + wf, d), lambda i: (0, 0)),
            pl.BlockSpec((tm, d), lambda i: (i, 0)),
        ],
        out_specs=pl.BlockSpec((tm, d), lambda i: (i, 0)),
        compiler_params=_cparams(("parallel",)),
        name="outproj",
    )(o_nsa, o_fox, g_nsa, g_fox, w_out, x)


CUMSUM_CHUNK = 512


def _cumsum_kernel(x_ref, o_ref):
    rows, length = x_ref.shape[1], x_ref.shape[2]
    n_big = length // CUMSUM_CHUNK

    def sweep(width, first, count, offset, carry):
        r = lax.broadcasted_iota(jnp.int32, (width, width), 0)
        c = lax.broadcasted_iota(jnp.int32, (width, width), 1)
        upper = (r <= c).astype(F32)

        def body(k, carry):
            st = pl.multiple_of(offset + k * width, LANE)
            x = x_ref[0, :, pl.ds(st, width)]
            cs = jnp.dot(x, upper, preferred_element_type=F32, precision=lax.Precision.HIGHEST) + carry
            o_ref[0, :, pl.ds(st, width)] = cs
            return cs[:, width - 1:width]

        return lax.fori_loop(first, count, body, carry)

    carry = sweep(CUMSUM_CHUNK, 0, n_big, 0, jnp.zeros((rows, 1), F32))
    sweep(LANE, 0, (length - n_big * CUMSUM_CHUNK) // LANE, n_big * CUMSUM_CHUNK, carry)


def _cumsum(x):
    b, h, length = x.shape
    return pl.pallas_call(
        _cumsum_kernel,
        out_shape=jax.ShapeDtypeStruct(x.shape, F32),
        grid=(b,),
        in_specs=[pl.BlockSpec((1, h, length), lambda i: (i, 0, 0))],
        out_specs=pl.BlockSpec((1, h, length), lambda i: (i, 0, 0)),
        compiler_params=_cparams(("parallel",)),
        name="logf_cumsum",
    )(x)


PE_ROWS = 16


def _compress_kernel(x_ref, w1_ref, w2_ref, pe_ref, kn_ref, o_ref):
    kind = pl.program_id(0) // NSA_KV_HEADS
    n = x_ref.shape[2]
    w1 = w1_ref[0]
    h = jnp.dot(x_ref[0, 0], w1, preferred_element_type=F32)
    pw = jnp.dot(pe_ref[0], w1, preferred_element_type=F32)
    const = pw[0:1, :CMP_HIDDEN] + pw[1:2, CMP_HIDDEN:]
    hid = h[:, :CMP_HIDDEN] + pltpu.roll(h[:, CMP_HIDDEN:], n - 1, 0) + const
    act = hid / (1.0 + jnp.exp(-hid))
    out = jnp.dot(act.astype(BF16), w2_ref[0], preferred_element_type=F32)
    normed = _rms_rows(out, kn_ref[...])
    o_ref[0, 0] = jnp.where(kind == 0, normed, out).astype(BF16)


def _compress(xc, c_off, w1cat, w2, pe, k_norm, n_rows):
    nb = xc.shape[1]
    return pl.pallas_call(
        _compress_kernel,
        out_shape=jax.ShapeDtypeStruct((2 * NSA_KV_HEADS, nb, n_rows, HEAD_DIM), BF16),
        grid=(2 * NSA_KV_HEADS, nb),
        in_specs=[
            pl.BlockSpec((1, 1, n_rows, CMP_STRIDE * HEAD_DIM), lambda c, b: (c_off + c, b, 0, 0)),
            pl.BlockSpec((1, CMP_STRIDE * HEAD_DIM, 2 * CMP_HIDDEN), lambda c, b: (c // NSA_KV_HEADS, 0, 0)),
            pl.BlockSpec((1, CMP_HIDDEN, HEAD_DIM), lambda c, b: (c // NSA_KV_HEADS, 0, 0)),
            pl.BlockSpec((1, PE_ROWS, CMP_STRIDE * HEAD_DIM), lambda c, b: (c // NSA_KV_HEADS, 0, 0)),
            pl.BlockSpec((1, HEAD_DIM), lambda c, b: (0, 0)),
        ],
        out_specs=pl.BlockSpec((1, 1, n_rows, HEAD_DIM), lambda c, b: (c, b, 0, 0)),
        compiler_params=_cparams(("parallel", "parallel")),
        name="compress",
    )(xc, w1cat, w2, pe, k_norm)


def _online_update(state, s, v):
    m, l, acc = state
    m_new = jnp.maximum(m, jnp.max(s, axis=-1, keepdims=True))
    alpha = jnp.exp(m - m_new)
    p = jnp.exp(s - m_new)
    l = alpha * l + jnp.sum(p, axis=-1, keepdims=True)
    acc = alpha * acc + jnp.dot(p.astype(BF16), v, preferred_element_type=F32)
    return m_new, l, acc


def _init_state(rows):
    return (jnp.full((rows, 1), NEG, F32), jnp.zeros((rows, 1), F32), jnp.zeros((rows, HEAD_DIM), F32))


def _rel_bias(dist, tbl_ref, head):
    out = jnp.full(dist.shape, tbl_ref[0, head], F32)
    for k, thr in enumerate(BUCKET_THR, start=1):
        out = jnp.where(dist >= thr, tbl_ref[k, head], out)
    return out


def _fox_kernel(q_ref, k_ref, v_ref, crow_ref, ccol_ref, o_ref, *, tq, tk, td, qoff):
    q0 = qoff + pl.program_id(2) * tq
    q = q_ref[0]
    cq = ccol_ref[0, 0]

    def logits(start, size):
        k = k_ref[0, 0, pl.ds(start, size), :]
        s = lax.dot_general(q, k, NT_DIMS, preferred_element_type=F32)
        return s + (cq - crow_ref[0, 0, :, pl.ds(start, size)])

    def body(kt, state):
        start = pl.multiple_of(kt * tk, tk)
        return _online_update(state, logits(start, tk), v_ref[0, 0, pl.ds(start, tk), :])

    state = lax.fori_loop(0, q0 // tk, body, _init_state(tq))
    start = pl.multiple_of(q0, LANE)
    s = logits(start, td)
    row = lax.broadcasted_iota(jnp.int32, (tq, td), 0)
    col = lax.broadcasted_iota(jnp.int32, (tq, td), 1)
    s = jnp.where(col <= row, s, NEG)
    _, l, acc = _online_update(state, s, v_ref[0, 0, pl.ds(start, td), :])
    o_ref[...] = acc / l


def _fox_attention(qarr, kv, crow, ccol, *, q_head0, k_head0, v_head0, nb, nq, tq, tk, td, qoff):
    lk = kv.shape[2]
    kern = functools.partial(_fox_kernel, tq=tq, tk=tk, td=td, qoff=qoff)
    return pl.pallas_call(
        kern,
        out_shape=jax.ShapeDtypeStruct((nb * nq * tq, FOX_HEADS * HEAD_DIM), F32),
        grid=(nb, FOX_HEADS, nq),
        in_specs=[
            pl.BlockSpec((1, tq, HEAD_DIM), lambda b, h, t: (q_head0 + h, b * nq + t, 0)),
            pl.BlockSpec((1, 1, lk, HEAD_DIM), lambda b, h, t: (k_head0 + h, b, 0, 0)),
            pl.BlockSpec((1, 1, lk, HEAD_DIM), lambda b, h, t: (v_head0 + h, b, 0, 0)),
            pl.BlockSpec((1, 1, 1, lk), lambda b, h, t: (b, h, 0, 0)),
            pl.BlockSpec((1, 1, tq, 1), lambda b, h, t: (b, h, t, 0)),
        ],
        out_specs=pl.BlockSpec((tq, HEAD_DIM), lambda b, h, t: (b * nq + t, h)),
        compiler_params=_cparams(("parallel", "parallel", "arbitrary")),
        name="fox_attention",
    )(qarr, kv, kv, crow, ccol)


def _nsa_kernel(tbl_ref, q_ref, kc_ref, vc_ref, msel_ref, ks_ref, vs_ref, kw_ref, vw_ref, gate_ref, o_ref,
                *, tq, tk, td, qoff, wpos0, n_sel):
    g = pl.program_id(1)
    q0 = qoff + pl.program_id(2) * tq
    rows = NSA_GROUP * tq
    q2 = q_ref[...].reshape(rows, HEAD_DIM)
    heads = [NSA_GROUP * g + hh for hh in range(NSA_GROUP)]
    pos_col = q0 + lax.broadcasted_iota(jnp.int32, (tq, 1), 0)

    def per_head(fn):
        return jnp.concatenate([fn(hh) for hh in range(NSA_GROUP)], axis=0)

    ncp = kc_ref.shape[2]
    blk_end = lax.broadcasted_iota(jnp.int32, (tq, ncp), 1) * CMP_STRIDE + (CMP_LEN - 1)
    d_cmp = pos_col - blk_end
    ok_cmp = d_cmp >= 0
    n_cmp = jnp.maximum(d_cmp, 0)
    s = lax.dot_general(q2, kc_ref[0, 0], NT_DIMS, preferred_element_type=F32)
    p_heads = []
    for hh in range(NSA_GROUP):
        sh = s[hh * tq:(hh + 1) * tq] + _rel_bias(n_cmp, tbl_ref, heads[hh])
        sh = jnp.where(ok_cmp, sh, NEG)
        e = jnp.where(ok_cmp, jnp.exp(sh - jnp.max(sh, axis=-1, keepdims=True)), 0.0)
        p_heads.append(e / jnp.maximum(jnp.sum(e, axis=-1, keepdims=True), 1e-30))
    o_cmp = jnp.dot(jnp.concatenate(p_heads, axis=0).astype(BF16), vc_ref[0, 0], preferred_element_type=F32)
    imp = jnp.dot(sum(p_heads), msel_ref[...], preferred_element_type=F32,
                  precision=lax.Precision.HIGHEST)

    nj = imp.shape[1]
    jj = lax.broadcasted_iota(jnp.int32, (tq, nj), 1)
    blk_q = pos_col // SEL_BLOCK
    forced = (jj == 0) | (jj == blk_q) | (jj == blk_q - 1)
    score = jnp.where(forced, FORCED_SCORE, jnp.where(jj <= blk_q, imp, -1.0))
    score = jnp.where(jj < n_sel, score, -2.0)
    rank = jnp.zeros((tq, nj), F32)
    for other in range(n_sel):
        col = jnp.broadcast_to(score[:, other:other + 1], score.shape)
        tie = jnp.where(jj > other, 1.0, 0.0)
        rank = rank + jnp.where(col > score, 1.0, jnp.where(col == score, tie, 0.0))
    sel = jnp.where((rank < SEL_TOPK) & (jj < n_sel), 1.0, 0.0).astype(BF16)

    def bias_and_mask(start, size, kpos0, near, window):
        def near_fn():
            kpos = kpos0 + start + lax.broadcasted_iota(jnp.int32, (tq, size), 1)
            d = pos_col - kpos
            ok = (d >= 0) & (d < WINDOW) if window else d >= 0
            n = jnp.maximum(d, 0)
            return per_head(lambda hh: jnp.where(ok, _rel_bias(n, tbl_ref, heads[hh]), NEG))

        def far_fn():
            return per_head(lambda hh: jnp.full((tq, size), tbl_ref[REL_BUCKETS - 1, heads[hh]], F32))

        if near is True:
            return near_fn()
        return lax.cond(near, near_fn, far_fn)

    def slc_tile(state, start, size, near):
        blk_of_key = (start + lax.broadcasted_iota(jnp.int32, (nj, size), 1)) // SEL_BLOCK
        expand = jnp.where(blk_of_key == lax.broadcasted_iota(jnp.int32, (nj, size), 0), 1.0, 0.0).astype(BF16)
        picked = jnp.dot(sel, expand, preferred_element_type=F32)
        drop = (picked - 1.0) * (-NEG)
        sc = lax.dot_general(q2, ks_ref[0, 0, pl.ds(start, size), :], NT_DIMS, preferred_element_type=F32)
        sc = sc + bias_and_mask(start, size, 0, near, False) + jnp.concatenate([drop] * NSA_GROUP, axis=0)
        return _online_update(state, sc, vs_ref[0, 0, pl.ds(start, size), :])

    def slc_body(kt, state):
        start = pl.multiple_of(kt * tk, tk)
        return slc_tile(state, start, tk, q0 - (start + tk - 1) < FAR_DIST)

    state = lax.fori_loop(0, q0 // tk, slc_body, _init_state(rows))
    _, l, acc = slc_tile(state, pl.multiple_of(q0, LANE), td, True)
    o_slc = acc / l

    def win_body(wt, state):
        start = pl.multiple_of(wt * LANE, LANE)
        near = q0 - (wpos0 + start + LANE - 1) < FAR_DIST
        sc = lax.dot_general(q2, kw_ref[0, 0, pl.ds(start, LANE), :], NT_DIMS, preferred_element_type=F32)
        kpos = wpos0 + start + lax.broadcasted_iota(jnp.int32, (tq, LANE), 1)
        d = pos_col - kpos
        edge = jnp.where((d >= 0) & (d < WINDOW), 0.0, NEG)
        sc = sc + bias_and_mask(start, LANE, wpos0, near, True) + jnp.concatenate([edge] * NSA_GROUP, axis=0)
        return _online_update(state, sc, vw_ref[0, 0, pl.ds(start, LANE), :])

    w_lo = jnp.maximum(q0 - wpos0 - WINDOW, 0) // LANE
    w_hi = (q0 + tq - 1 - wpos0) // LANE
    _, l, acc = lax.fori_loop(w_lo, w_hi + 1, win_body, _init_state(rows))
    o_win = acc / l

    gates = gate_ref[...]
    lane = lax.broadcasted_iota(jnp.int32, gates.shape, 1)

    def gate(hh, branch):
        return jnp.sum(jnp.where(lane == heads[hh] * 3 + branch, gates, 0.0), axis=-1, keepdims=True)

    for hh in range(NSA_GROUP):
        sl = slice(hh * tq, (hh + 1) * tq)
        o_ref[:, hh * HEAD_DIM:(hh + 1) * HEAD_DIM] = (
            gate(hh, 0) * o_cmp[sl] + gate(hh, 1) * o_slc[sl] + gate(hh, 2) * o_win[sl])


def _nsa_attention(rel_table, qarr, kcv, msel, kv, kwin, gates, *, ks_head0, vs_head0, kw_head0, vw_head0,
                   nb, nq, tq, tk, td, qoff, wpos0, n_sel):
    lk = kv.shape[2]
    lw = kwin.shape[2]
    ncp = kcv.shape[2]
    nj = msel.shape[1]
    kern = functools.partial(_nsa_kernel, tq=tq, tk=tk, td=td, qoff=qoff, wpos0=wpos0, n_sel=n_sel)
    g_w = NSA_GROUP * HEAD_DIM
    return pl.pallas_call(
        kern,
        out_shape=jax.ShapeDtypeStruct((nb * nq * tq, NSA_HEADS * HEAD_DIM), F32),
        grid=(nb, NSA_KV_HEADS, nq),
        in_specs=[
            pl.BlockSpec(memory_space=pltpu.SMEM),
            pl.BlockSpec((NSA_GROUP, tq, HEAD_DIM), lambda b, g, t: (g, b * nq + t, 0)),
            pl.BlockSpec((1, 1, ncp, HEAD_DIM), lambda b, g, t: (g, b, 0, 0)),
            pl.BlockSpec((1, 1, ncp, HEAD_DIM), lambda b, g, t: (NSA_KV_HEADS + g, b, 0, 0)),
            pl.BlockSpec((ncp, nj), lambda b, g, t: (0, 0)),
            pl.BlockSpec((1, 1, lk, HEAD_DIM), lambda b, g, t: (ks_head0 + g, b, 0, 0)),
            pl.BlockSpec((1, 1, lk, HEAD_DIM), lambda b, g, t: (vs_head0 + g, b, 0, 0)),
            pl.BlockSpec((1, 1, lw, HEAD_DIM), lambda b, g, t: (kw_head0 + g, b, 0, 0)),
            pl.BlockSpec((1, 1, lw, HEAD_DIM), lambda b, g, t: (vw_head0 + g, b, 0, 0)),
            pl.BlockSpec((tq, LANE), lambda b, g, t: (b * nq + t, 0)),
        ],
        out_specs=pl.BlockSpec((tq, g_w), lambda b, g, t: (b * nq + t, g)),
        compiler_params=_cparams(("parallel", "parallel", "arbitrary")),
        name="nsa_attention",
    )(rel_table, qarr, kcv, kcv, msel, kv, kv, kwin, kwin, gates)


def _cmp_to_sel(n_cmp, n_sel, rows, cols):
    c0 = np.arange(n_cmp)[:, None] * CMP_STRIDE
    s0 = np.arange(n_sel)[None, :] * SEL_BLOCK
    inter = np.clip(np.minimum(c0 + CMP_LEN, s0 + SEL_BLOCK) - np.maximum(c0, s0), 0, None)
    m = np.zeros((rows, cols), np.float32)
    m[:n_cmp, :n_sel] = inter / CMP_LEN
    return jnp.asarray(m)


def _transpose_tiles(x):
    n = x.shape[0] // LANE
    xf = x.astype(F32)
    return jnp.concatenate([xf[i * LANE:(i + 1) * LANE].T for i in range(n)], axis=1)


def _untranspose_tiles(xt):
    n = xt.shape[1] // LANE
    return jnp.concatenate([xt[:, i * LANE:(i + 1) * LANE].T for i in range(n)], axis=0)


AUX_ROWS = 16


def _fill_transposed(dst_ref, src_ref):
    def body(k, carry):
        st = pl.multiple_of(k * LANE, LANE)
        dst_ref[:HEAD_DIM, pl.ds(st, LANE)] = src_ref[0, 0, pl.ds(st, LANE), :].astype(F32).T.astype(BF16)
        return carry

    lax.fori_loop(0, src_ref.shape[2] // LANE, body, 0)


def _online_update_t(state, s_t, v_t):
    m, l, acc = state
    m_new = jnp.maximum(m, jnp.max(s_t, axis=0, keepdims=True))
    alpha = jnp.exp(m - m_new)
    p = jnp.exp(s_t - m_new)
    l = alpha * l + jnp.sum(p, axis=0, keepdims=True)
    acc = alpha * acc + jnp.dot(v_t, p.astype(BF16), preferred_element_type=F32)
    return m_new, l, acc


def _init_state_t(cols):
    return (jnp.full((1, cols), NEG, F32), jnp.zeros((1, cols), F32), jnp.zeros((HEAD_DIM, cols), F32))


FOX_TQ = 256
FOX_KEY_BLOCK = 512


def _split3(x):
    hi = x.astype(BF16)
    r1 = x - hi.astype(F32)
    mid = r1.astype(BF16)
    lo = (r1 - mid.astype(F32)).astype(BF16)
    return hi, mid, lo


def _lane_select3(parts, shape):
    lane = lax.broadcasted_iota(jnp.int32, shape, 1)
    hi, mid, lo = [part.astype(F32) for part in parts]
    return jnp.where(lane == 0, hi, jnp.where(lane == 1, mid, jnp.where(lane == 2, lo, 0.0))).astype(BF16)


def _ones_rows(width):
    row = lax.broadcasted_iota(jnp.int32, (AUX_ROWS, width), 0)
    return jnp.where(row == 0, 1.0, 0.0).astype(BF16)


def _online_update_aug(state, s_t, v_aug, shift=None):
    m, acc = state
    top = jnp.max(s_t, axis=0, keepdims=True)
    m_new = jnp.maximum(m, top if shift is None else top + shift)
    alpha = jnp.exp(m - m_new)
    p = jnp.exp(s_t - (m_new if shift is None else m_new - shift))
    acc = alpha * acc + jnp.dot(v_aug, p.astype(BF16), preferred_element_type=F32)
    return m_new, acc


def _init_state_aug(cols):
    return jnp.full((1, cols), NEG, F32), jnp.zeros((HEAD_DIM + AUX_ROWS, cols), F32)


def _finish_aug(acc):
    return acc[:HEAD_DIM] * (1.0 / acc[HEAD_DIM:HEAD_DIM + 1])


def _fox_prompt_kernel(q_ref, k_ref, v_ref, crow_ref, o_ref, vt_ref, ka_ref):
    qt = pl.program_id(2)
    tq = FOX_TQ

    @pl.when(qt == 0)
    def _():
        _fill_transposed(vt_ref, v_ref)
        vt_ref[HEAD_DIM:, :] = _ones_rows(vt_ref.shape[1])

        def body(k, carry):
            st = pl.multiple_of(k * LANE, LANE)
            c_col = jnp.broadcast_to(crow_ref[0, 0, :, pl.ds(st, LANE)], (LANE, LANE)).T
            ka_ref[pl.ds(st, LANE), :HEAD_DIM] = k_ref[0, 0, pl.ds(st, LANE), :]
            ka_ref[pl.ds(st, LANE), HEAD_DIM:] = _lane_select3(_split3(c_col), (LANE, LANE))
            return carry

        lax.fori_loop(0, k_ref.shape[2] // LANE, body, 0)

    q0 = pl.multiple_of(qt * tq, tq)
    q_t = _transpose_tiles(q_ref[0]).astype(BF16)
    row = lax.broadcasted_iota(jnp.int32, (LANE, tq), 0)
    q_aug = jnp.concatenate([q_t, jnp.where(row < 3, -1.0, 0.0).astype(BF16)], axis=0)
    c_q = crow_ref[0, 0, :, pl.ds(q0, tq)]
    blk = FOX_KEY_BLOCK

    last = q0 // blk
    rel = (lax.broadcasted_iota(jnp.int32, (blk, tq), 0) - lax.broadcasted_iota(jnp.int32, (blk, tq), 1))

    def logits(kb):
        start = pl.multiple_of(kb * blk, blk)
        s = jnp.dot(ka_ref[pl.ds(start, blk), :], q_aug, preferred_element_type=F32)
        return jnp.where(rel <= q0 - start, s, NEG)

    def body(kb, carry):
        m, acc, s_cur = carry
        s_next = logits(jnp.minimum(kb + 1, last))
        start = pl.multiple_of(kb * blk, blk)
        m, acc = _online_update_aug((m, acc), s_cur, vt_ref[:, pl.ds(start, blk)], c_q)
        return m, acc, s_next

    _, acc, _ = lax.fori_loop(0, last + 1, body, _init_state_aug(tq) + (logits(0),))
    o_ref[...] = _untranspose_tiles(_finish_aug(acc))


def _fox_prompt(hm, hm4, crow, nb, seq):
    nq = seq // FOX_TQ
    return pl.pallas_call(
        _fox_prompt_kernel,
        out_shape=jax.ShapeDtypeStruct((nb * seq, FOX_HEADS * HEAD_DIM), F32),
        grid=(nb, FOX_HEADS, nq),
        in_specs=[
            pl.BlockSpec((1, FOX_TQ, HEAD_DIM), lambda b, h, t: (HM_Q_FOX + h, b * nq + t, 0)),
            pl.BlockSpec((1, 1, seq, HEAD_DIM), lambda b, h, t: (HM_K_FOX + h, b, 0, 0)),
            pl.BlockSpec((1, 1, seq, HEAD_DIM), lambda b, h, t: (HM_V_FOX + h, b, 0, 0)),
            pl.BlockSpec((1, 1, 1, seq), lambda b, h, t: (b, h, 0, 0)),
        ],
        out_specs=pl.BlockSpec((FOX_TQ, HEAD_DIM), lambda b, h, t: (b * nq + t, h)),
        scratch_shapes=[pltpu.VMEM((HEAD_DIM + AUX_ROWS, seq), BF16), pltpu.VMEM((seq, 2 * HEAD_DIM), BF16)],
        compiler_params=_cparams(("parallel", "parallel", "arbitrary")),
        name="fox_prompt",
    )(hm, hm4, hm4, crow)


NSA_TQ = LANE
NSA_COLS = NSA_GROUP * NSA_TQ
WIN_TILES = WINDOW // NSA_TQ + 1
SLC_BLOCK_TILES = 4
SLC_MASK_ROWS = SLC_BLOCK_TILES * NSA_TQ // SEL_BLOCK


def _nsa_prompt_kernel(tbl_ref, q_ref, kc_ref, vc_ref, mselt_ref, ks_ref, vs_ref, kw_ref, vw_ref, gate_ref, o_ref,
                       vst_ref, vwt_ref, vct_ref, wb_ref, pc_ref, drop_ref, ksa_ref, wbd_ref, *, n_sel, cmp_back):
    g = pl.program_id(1)
    qt = pl.program_id(2)
    tq = NSA_TQ
    cols = NSA_COLS
    heads = [NSA_GROUP * g + hh for hh in range(NSA_GROUP)]
    ncp = kc_ref.shape[2]
    nj = mselt_ref.shape[0]

    @pl.when(qt == 0)
    def _():
        _fill_transposed(vst_ref, vs_ref)
        _fill_transposed(vwt_ref, vw_ref)
        vst_ref[HEAD_DIM:, :] = _ones_rows(vst_ref.shape[1])
        vwt_ref[HEAD_DIM:, :] = _ones_rows(vwt_ref.shape[1])
        vct_ref[...] = _transpose_tiles(vc_ref[0, 0]).astype(BF16)

        def fill_keys(k, carry):
            st = pl.multiple_of(k * tq, tq)
            lane = lax.broadcasted_iota(jnp.int32, (tq, tq), 1)
            blk_in_step = (k % SLC_BLOCK_TILES) * (tq // SEL_BLOCK) + lax.broadcasted_iota(jnp.int32, (tq, tq), 0) // SEL_BLOCK
            extra = (lane == blk_in_step) | ((lane >= SLC_MASK_ROWS) & (lane < SLC_MASK_ROWS + 3))
            ksa_ref[pl.ds(st, tq), :HEAD_DIM] = ks_ref[0, 0, pl.ds(st, tq), :]
            ksa_ref[pl.ds(st, tq), HEAD_DIM:] = jnp.where(extra, 1.0, 0.0).astype(BF16)
            return carry

        lax.fori_loop(0, ks_ref.shape[2] // tq, fill_keys, 0)
        key = lax.broadcasted_iota(jnp.int32, (tq, tq), 0)
        qry = lax.broadcasted_iota(jnp.int32, (tq, tq), 1)
        blk = lax.broadcasted_iota(jnp.int32, (pc_ref.shape[0], tq), 0) - cmp_back
        d_cmp = lax.broadcasted_iota(jnp.int32, (pc_ref.shape[0], tq), 1) - (blk * CMP_STRIDE + (CMP_LEN - 1))
        for hh in range(NSA_GROUP):
            sl = slice(hh * tq, (hh + 1) * tq)
            far = tbl_ref[REL_BUCKETS - 1, heads[hh]]
            for delta in range(2):
                d = delta * tq + qry - key
                wb_ref[delta, :, sl] = jnp.where(d >= 0, _rel_bias(jnp.maximum(d, 0), tbl_ref, heads[hh]), NEG)
            for delta in range(2, WIN_TILES - 1):
                wb_ref[delta, :, sl] = jnp.full((tq, tq), far, F32)
            wb_ref[WIN_TILES - 1, :, sl] = jnp.where(qry < key, far, NEG)
            wb_ref[WIN_TILES, :, sl] = jnp.full((tq, tq), NEG, F32)
            for delta in range(2):
                wbd_ref[delta, :, sl] = wb_ref[delta, :, sl] - far
            wbd_ref[2, :, sl] = jnp.zeros((tq, tq), F32)
            wbd_ref[3, :, sl] = jnp.full((tq, tq), NEG, F32)
            pc_ref[:, sl] = jnp.where(d_cmp >= 0, _rel_bias(jnp.maximum(d_cmp, 0), tbl_ref, heads[hh]), NEG)

    q_t = jnp.concatenate([q_ref[hh].astype(F32).T for hh in range(NSA_GROUP)], axis=1).astype(BF16)

    off = pl.multiple_of(cmp_back - qt * (tq // CMP_STRIDE), 8)
    s_t = jnp.dot(kc_ref[0, 0], q_t, preferred_element_type=F32) + pc_ref[pl.ds(off, ncp), :]
    m = jnp.max(s_t, axis=0, keepdims=True)
    e = jnp.exp(s_t - m)
    inv = jnp.where(m > 0.5 * NEG, 1.0 / jnp.sum(e, axis=0, keepdims=True), 0.0)
    p_t = e * inv
    o_cmp = jnp.dot(vct_ref[...], p_t.astype(BF16), preferred_element_type=F32)
    p_sum = p_t[:, :tq]
    for hh in range(1, NSA_GROUP):
        p_sum = p_sum + p_t[:, hh * tq:(hh + 1) * tq]
    imp = jnp.dot(mselt_ref[...], p_sum, preferred_element_type=F32, precision=lax.Precision.HIGHEST)

    jrow = lax.broadcasted_iota(jnp.int32, (nj, tq), 0)
    pos = qt * tq + lax.broadcasted_iota(jnp.int32, (nj, tq), 1)
    blk_q = jnp.right_shift(pos, int(math.log2(SEL_BLOCK)))
    forced = (jrow == 0) | (jrow == blk_q) | (jrow == blk_q - 1)
    score = jnp.where(forced, FORCED_SCORE, jnp.where(jrow <= blk_q, imp, -1.0))
    score = jnp.where(jrow < n_sel, score, -2.0)
    ranks = []
    for r in range(nj // 8):
        mine = score[r * 8:(r + 1) * 8]
        jmine = jrow[r * 8:(r + 1) * 8]
        rank = jnp.zeros((8, tq), F32)
        for other in range(n_sel):
            row = jnp.broadcast_to(score[other:other + 1], (8, tq))
            if other < r * 8:
                beats = row >= mine
            elif other >= (r + 1) * 8:
                beats = row > mine
            else:
                beats = (row > mine) | ((row == mine) & (jmine > other))
            rank = rank + jnp.where(beats, 1.0, 0.0)
        ranks.append(rank)
    rank = jnp.concatenate(ranks, axis=0)
    drop = jnp.where((rank < SEL_TOPK) & (jrow < n_sel), 0.0, NEG)
    drop_ref[...] = jnp.concatenate([drop] * NSA_GROUP, axis=1)

    blk = SLC_BLOCK_TILES * tq
    col = lax.broadcasted_iota(jnp.int32, (8, cols), 1)
    far_row = jnp.full((8, cols), tbl_ref[REL_BUCKETS - 1, heads[NSA_GROUP - 1]], F32)
    for hh in range(NSA_GROUP - 1):
        far_row = jnp.where(col // tq == hh, tbl_ref[REL_BUCKETS - 1, heads[hh]], far_row)
    row8 = lax.broadcasted_iota(jnp.int32, (8, cols), 0)
    parts = [part.astype(F32) for part in _split3(far_row)]
    far_rows = jnp.where(row8 == 0, parts[0], jnp.where(row8 == 1, parts[1], jnp.where(row8 == 2, parts[2], 0.0)))
    pad_rows = jnp.zeros((HEAD_DIM - SLC_MASK_ROWS - 8, cols), BF16)

    last = qt // SLC_BLOCK_TILES

    def slc_logits(kb):
        start = pl.multiple_of(kb * blk, blk)
        masks = drop_ref[pl.ds(pl.multiple_of(kb * SLC_MASK_ROWS, SLC_MASK_ROWS), SLC_MASK_ROWS), :]
        extra = jnp.concatenate([masks, far_rows], axis=0).astype(BF16)
        q_aug = jnp.concatenate([q_t, extra, pad_rows], axis=0)
        s = jnp.dot(ksa_ref[pl.ds(start, blk), :], q_aug, preferred_element_type=F32)
        terms = []
        for i in range(SLC_BLOCK_TILES):
            delta = qt - (kb * SLC_BLOCK_TILES + i)
            terms.append(wbd_ref[jnp.where(delta < 0, 3, jnp.minimum(delta, 2))])
        return s + jnp.concatenate(terms, axis=0)

    def slc_body(kb, carry):
        m, acc, s_cur = carry
        s_next = slc_logits(jnp.minimum(kb + 1, last))
        start = pl.multiple_of(kb * blk, blk)
        m, acc = _online_update_aug((m, acc), s_cur, vst_ref[:, pl.ds(start, blk)])
        return m, acc, s_next

    _, acc, _ = lax.fori_loop(0, last + 1, slc_body, _init_state_aug(cols) + (slc_logits(0),))
    o_slc = _finish_aug(acc)

    w0 = jnp.maximum(qt - (WIN_TILES - 1), 0)
    start = pl.multiple_of(w0 * tq, tq)
    span = WIN_TILES * tq
    s = jnp.dot(kw_ref[0, 0, pl.ds(start, span), :], q_t, preferred_element_type=F32)
    terms = []
    for i in range(WIN_TILES):
        delta = qt - (w0 + i)
        terms.append(wb_ref[jnp.where(delta < 0, WIN_TILES, delta)])
    s = s + jnp.concatenate(terms, axis=0)
    p = jnp.exp(s - jnp.max(s, axis=0, keepdims=True))
    o_win = _finish_aug(jnp.dot(vwt_ref[:, pl.ds(start, span)], p.astype(BF16), preferred_element_type=F32))

    gate = gate_ref[0, 0, 0]
    o_t = gate[0:1] * o_cmp + gate[1:2] * o_slc + gate[2:3] * o_win
    for hh in range(NSA_GROUP):
        o_ref[:, hh * HEAD_DIM:(hh + 1) * HEAD_DIM] = o_t[:, hh * tq:(hh + 1) * tq].T


def _nsa_prompt(rel_table, hm, hm4, kcv, mselt, gates_t, nb, seq, n_sel):
    tq = NSA_TQ
    nq = seq // tq
    ncp = kcv.shape[2]
    nj = mselt.shape[0]
    cmp_back = (nq - 1) * (tq // CMP_STRIDE)
    kern = functools.partial(_nsa_prompt_kernel, n_sel=n_sel, cmp_back=cmp_back)
    kv_spec = lambda head0: pl.BlockSpec((1, 1, seq, HEAD_DIM), lambda b, g, t: (head0 + g, b, 0, 0))
    return pl.pallas_call(
        kern,
        out_shape=jax.ShapeDtypeStruct((nb * seq, NSA_HEADS * HEAD_DIM), F32),
        grid=(nb, NSA_KV_HEADS, nq),
        in_specs=[
            pl.BlockSpec(memory_space=pltpu.SMEM),
            pl.BlockSpec((NSA_GROUP, tq, HEAD_DIM), lambda b, g, t: (g, b * nq + t, 0)),
            pl.BlockSpec((1, 1, ncp, HEAD_DIM), lambda b, g, t: (g, b, 0, 0)),
            pl.BlockSpec((1, 1, ncp, HEAD_DIM), lambda b, g, t: (NSA_KV_HEADS + g, b, 0, 0)),
            pl.BlockSpec((nj, ncp), lambda b, g, t: (0, 0)),
            kv_spec(HM_K_SLC), kv_spec(HM_V_SLC), kv_spec(HM_K_WIN), kv_spec(HM_V_WIN),
            pl.BlockSpec((1, 1, 1, 8, NSA_COLS), lambda b, g, t: (b, g, t, 0, 0)),
        ],
        out_specs=pl.BlockSpec((tq, NSA_GROUP * HEAD_DIM), lambda b, g, t: (b * nq + t, g)),
        scratch_shapes=[
            pltpu.VMEM((HEAD_DIM + AUX_ROWS, seq), BF16),
            pltpu.VMEM((HEAD_DIM + AUX_ROWS, seq), BF16),
            pltpu.VMEM((HEAD_DIM, ncp), BF16),
            pltpu.VMEM((WIN_TILES + 1, tq, NSA_COLS), F32),
            pltpu.VMEM((cmp_back + ncp, NSA_COLS), F32),
            pltpu.VMEM((nj, NSA_COLS), F32),
            pltpu.VMEM((seq, 2 * HEAD_DIM), BF16),
            pltpu.VMEM((4, tq, NSA_COLS), F32),
        ],
        compiler_params=_cparams(("parallel", "parallel", "arbitrary")),
        name="nsa_prompt",
    )(rel_table, hm, kcv, kcv, mselt, hm4, hm4, hm4, hm4, gates_t)


CHUNK_ROWS = 512


def _chunkify_kernel(x_ref, o_ref, *, n_cols):
    n = o_ref.shape[2]
    for c in range(o_ref.shape[0]):
        for s in range(CMP_STRIDE):
            o_ref[c, 0, :, s * HEAD_DIM:(s + 1) * HEAD_DIM] = (
                x_ref[0, pl.ds(s * n_cols + c, n, stride=CMP_STRIDE * n_cols), :].astype(BF16))


def _chunkify(rows3, n_cols, n_heads, seq):
    nb = rows3.shape[0]
    tr = _largest_tile(seq, (CHUNK_ROWS, 256))
    return pl.pallas_call(
        functools.partial(_chunkify_kernel, n_cols=n_cols),
        out_shape=jax.ShapeDtypeStruct((n_heads, nb, seq // CMP_STRIDE, CMP_STRIDE * HEAD_DIM), BF16),
        grid=(nb, seq // tr),
        in_specs=[pl.BlockSpec((1, tr * n_cols, HEAD_DIM), lambda b, i: (b, i, 0))],
        out_specs=pl.BlockSpec((n_heads, 1, tr // CMP_STRIDE, CMP_STRIDE * HEAD_DIM), lambda b, i: (0, b, i, 0)),
        compiler_params=_cparams(("parallel", "parallel")),
        name="chunkify",
    )(rows3)


REGROUP_PAGES = 4


N_CACHE_COLS = 16
N_CMP_COLS = 2 * NSA_KV_HEADS


def _regroup_kernel(pt_ref, *refs):
    del pt_ref
    npg = REGROUP_PAGES
    nsa_in, fox_in, lf_in = refs[:npg], refs[npg:2 * npg], refs[2 * npg:3 * npg]
    xc_out, slc_out, fox_out, lf_out = refs[3 * npg:]
    is_tail = pl.program_id(1) == pl.num_programs(1) - 1
    chunks = PAGE_SIZE // CMP_STRIDE
    chunk_stride = CMP_STRIDE * N_CACHE_COLS

    @pl.when(is_tail)
    def _():
        slc_out[...] = jnp.zeros(slc_out.shape, slc_out.dtype)
        fox_out[...] = jnp.zeros(fox_out.shape, fox_out.dtype)
        lf_out[...] = jnp.zeros(lf_out.shape, lf_out.dtype)

    @pl.when(jnp.logical_not(is_tail))
    def _():
        for p in range(npg):
            rows = slice(p * PAGE_SIZE, (p + 1) * PAGE_SIZE)
            for c in range(N_CACHE_COLS - N_CMP_COLS):
                slc_out[c, 0, rows, :] = nsa_in[p][0, pl.ds(N_CMP_COLS + c, PAGE_SIZE, stride=N_CACHE_COLS), :].astype(BF16)
            for c in range(N_CACHE_COLS):
                fox_out[c, 0, rows, :] = fox_in[p][0, pl.ds(c, PAGE_SIZE, stride=N_CACHE_COLS), :].astype(BF16)
            lf_out[0, rows, :] = lf_in[p][0]
        for pair in range(npg // 2):
            for c in range(N_CMP_COLS):
                for s in range(CMP_STRIDE):
                    first = s * N_CACHE_COLS + c
                    both = [nsa_in[2 * pair + i][0, pl.ds(first, chunks, stride=chunk_stride), :] for i in range(2)]
                    xc_out[c, 0, pair * 2 * chunks:(pair + 1) * 2 * chunks, s * HEAD_DIM:(s + 1) * HEAD_DIM] = (
                        jnp.concatenate(both, axis=0).astype(BF16))


def _regroup(page_table, cache_nsa, cache_fox, cache_logf, lk):
    nb, n_pages = page_table.shape
    npg = REGROUP_PAGES
    n_steps = n_pages // npg
    rows = npg * PAGE_SIZE
    last = n_steps - 1
    n_slc = N_CACHE_COLS - N_CMP_COLS
    chunk_w = CMP_STRIDE * HEAD_DIM

    def page_map(p):
        return lambda b, s, pt: (pt[b, jnp.minimum(s, last) * npg + p], 0, 0)

    def specs(arr):
        return [pl.BlockSpec((1,) + arr.shape[1:], page_map(p)) for p in range(npg)]

    grid_spec = pltpu.PrefetchScalarGridSpec(
        num_scalar_prefetch=1,
        grid=(nb, n_steps + 1),
        in_specs=specs(cache_nsa) + specs(cache_fox) + specs(cache_logf),
        out_specs=(
            pl.BlockSpec((N_CMP_COLS, 1, rows // CMP_STRIDE, chunk_w), lambda b, s, pt: (0, b, jnp.minimum(s, last), 0)),
            pl.BlockSpec((n_slc, 1, rows, HEAD_DIM), lambda b, s, pt: (0, b, s, 0)),
            pl.BlockSpec((N_CACHE_COLS, 1, rows, HEAD_DIM), lambda b, s, pt: (0, b, s, 0)),
            pl.BlockSpec((1, rows, cache_logf.shape[2]), lambda b, s, pt: (b, s, 0)),
        ),
    )
    return pl.pallas_call(
        _regroup_kernel,
        out_shape=(
            jax.ShapeDtypeStruct((N_CMP_COLS, nb, n_pages * PAGE_SIZE // CMP_STRIDE, chunk_w), BF16),
            jax.ShapeDtypeStruct((n_slc, nb, lk, HEAD_DIM), BF16),
            jax.ShapeDtypeStruct((N_CACHE_COLS, nb, lk, HEAD_DIM), BF16),
            jax.ShapeDtypeStruct((nb, lk, cache_logf.shape[2]), F32),
        ),
        grid_spec=grid_spec,
        compiler_params=_cparams(("parallel", "arbitrary")),
        name="cache_regroup",
    )(page_table, *([cache_nsa] * npg), *([cache_fox] * npg), *([cache_logf] * npg))


def _largest_tile(n, candidates):
    for c in candidates:
        if n % c == 0:
            return c
    raise ValueError(f"no tile in {candidates} divides {n}")


FFN_ROW_TILES = (1024, 512, 256, 128)
ROW_TILES = (512, 256, 128)


def _token_stage_in(x, p):
    m = x.shape[0]
    tm = m if m < ROW_TILES[-1] else _largest_tile(m, ROW_TILES)
    tm_ffn = m if m < FFN_ROW_TILES[-1] else _largest_tile(m, FFN_ROW_TILES)
    x1 = _ffn(x, p["norm_ffn1"], p["wg1"], p["wu1"], p["wd1"], tm_ffn, p["tf"])
    nsa_rows, win_rows, fox_rows, hm = _inproj(x1, p["norm_mix"], p["w_main"], p["colgain"], tm)
    small = _small(x1, p["norm_mix"], p["w_small"], p["b_small"], tm)
    return x1, nsa_rows, win_rows, fox_rows, hm, small


def _token_stage_out(x1, o_nsa, o_fox, p):
    m = x1.shape[0]
    tm = m if m < ROW_TILES[-1] else _largest_tile(m, ROW_TILES[1:])
    tm_ffn = m if m < FFN_ROW_TILES[-1] else _largest_tile(m, FFN_ROW_TILES)
    x2 = _outproj(o_nsa, o_fox, p["out_norm_nsa"], p["out_norm_fox"], p["w_out"], x1, tm)
    return _ffn(x2, p["norm_ffn2"], p["wg2"], p["wu2"], p["wd2"], tm_ffn, p["tf"])


def kernel(x_prompt, x_sample, cache_nsa_kv, cache_fox_kv, cache_fox_logf, state_win_kv, page_table, rel_table, norm_ffn1, ffn1_gate, ffn1_up, ffn1_down, norm_mix, w_in, nsa_gate_bias, fox_forget_bias, q_norm_nsa, k_norm_nsa, q_norm_fox, k_norm_fox, cmp_pos_k, cmp_w1_k, cmp_w2_k, cmp_pos_v, cmp_w1_v, cmp_w2_v, out_norm_nsa, out_norm_fox, w_out, norm_ffn2, ffn2_gate, ffn2_up, ffn2_down):
    depth = w_in.shape[0]
    assert depth == 1, "single-layer trunk"
    nbp, seq, d_model = x_prompt.shape
    nbd, dseq, _ = x_sample.shape
    n_pages = page_table.shape[1]
    past = n_pages * PAGE_SIZE
    d_ff = ffn1_gate.shape[2]
    nsa_w = NSA_HEADS * HEAD_DIM
    kv6_w = 6 * NSA_KV_HEADS * HEAD_DIM
    fox_w = 3 * FOX_HEADS * HEAD_DIM
    off_gate = nsa_w + kv6_w
    off_fox = off_gate + N_GATE_COLS
    off_forget = off_fox + fox_w
    assert w_in.shape[2] == off_forget + FOX_HEADS and d_model == nsa_w + FOX_HEADS * HEAD_DIM
    assert seq % LANE == 0 and seq >= WINDOW and past % LANE == 0 and n_pages % REGROUP_PAGES == 0
    assert dseq <= 16 and state_win_kv.shape[2] == WINDOW
    assert seq % FOX_KEY_BLOCK == 0 and seq % (SLC_BLOCK_TILES * NSA_TQ) == 0 and seq >= WIN_TILES * NSA_TQ

    w0 = w_in[0]
    ones = lambda n: jnp.ones((n,), F32)
    zeros = lambda n: jnp.zeros((n,), F32)
    kn, kvw = NSA_KV_HEADS, NSA_KV_HEADS * HEAD_DIM
    qk_scale = HEAD_DIM ** -0.5
    p = {
        "tf": _largest_tile(d_ff, (512, 256, 128)),
        "norm_ffn1": norm_ffn1[0][None], "norm_mix": norm_mix[0][None], "norm_ffn2": norm_ffn2[0][None],
        "wg1": ffn1_gate[0].astype(BF16), "wu1": ffn1_up[0].astype(BF16), "wd1": ffn1_down[0].astype(BF16),
        "wg2": ffn2_gate[0].astype(BF16), "wu2": ffn2_up[0].astype(BF16), "wd2": ffn2_down[0].astype(BF16),
        "w_main": jnp.concatenate([w0[:, :off_gate], w0[:, off_fox:off_forget]], axis=1).astype(BF16),
        "w_small": jnp.concatenate([w0[:, off_gate:off_fox], w0[:, off_forget:],
                                    jnp.zeros((d_model, LANE - N_GATE_COLS - FOX_HEADS), F32)], axis=1).astype(BF16),
        "b_small": jnp.concatenate([nsa_gate_bias[0].reshape(-1), fox_forget_bias[0],
                                    zeros(LANE - N_GATE_COLS - FOX_HEADS)])[None],
        "colgain": jnp.concatenate([
            jnp.tile(q_norm_nsa[0] * qk_scale, NSA_HEADS), ones(2 * kvw), jnp.tile(k_norm_nsa[0], kn), ones(kvw),
            jnp.tile(k_norm_nsa[0], kn), ones(kvw), jnp.tile(q_norm_fox[0] * qk_scale, FOX_HEADS),
            jnp.tile(k_norm_fox[0], FOX_HEADS), ones(FOX_HEADS * HEAD_DIM)])[None],
        "out_norm_nsa": out_norm_nsa[0][None], "out_norm_fox": out_norm_fox[0][None],
        "w_out": w_out[0].astype(BF16),
    }
    half = CMP_STRIDE * HEAD_DIM

    def cmp_w1(w):
        return jnp.concatenate([w[0, :half], w[0, half:]], axis=1)

    def cmp_pe(pe):
        return jnp.concatenate([pe[0].reshape(CMP_LEN // CMP_STRIDE, half), jnp.zeros((PE_ROWS - CMP_LEN // CMP_STRIDE, half), F32)], axis=0)

    w1cat = jnp.stack([cmp_w1(cmp_w1_k), cmp_w1(cmp_w1_v)]).astype(BF16)
    w2cat = jnp.stack([cmp_w2_k[0], cmp_w2_v[0]]).astype(BF16)
    pecat = jnp.stack([cmp_pe(cmp_pos_k), cmp_pe(cmp_pos_v)]).astype(BF16)
    k_norm_row = k_norm_nsa[0][None]

    mp = nbp * seq
    x1, nsa_rows, win_rows, fox_rows, hm, small = _token_stage_in(x_prompt.reshape(mp, d_model), p)
    hm4 = hm.reshape(N_HEAD_COLS, nbp, seq, HEAD_DIM)

    logf = small[:, N_GATE_COLS:N_GATE_COLS + FOX_HEADS]
    csum = _cumsum(logf.reshape(nbp, seq, FOX_HEADS).transpose(0, 2, 1))
    o_fox = _fox_prompt(hm, hm4, csum[:, :, None, :], nbp, seq)

    n_chunk = seq // CMP_STRIDE
    n_cmp = (seq - CMP_LEN) // CMP_STRIDE + 1
    n_sel = -(-seq // SEL_BLOCK)
    xc = _chunkify(nsa_rows.reshape(nbp, seq * N_CACHE_COLS, HEAD_DIM), N_CACHE_COLS, N_CMP_COLS, seq)
    kcv = _compress(xc, 0, w1cat, w2cat, pecat, k_norm_row, n_chunk)
    mselt = _cmp_to_sel(n_cmp, n_sel, n_chunk, -(-n_sel // 8) * 8).T
    nq = seq // NSA_TQ
    gates_t = small[:, :N_GATE_COLS].reshape(nbp, nq, NSA_TQ, NSA_KV_HEADS, NSA_GROUP, 3)
    gates_t = gates_t.transpose(0, 3, 1, 5, 4, 2).reshape(nbp, NSA_KV_HEADS, nq, 3, NSA_COLS)
    gates_t = jnp.pad(gates_t, ((0, 0), (0, 0), (0, 0), (0, 8 - 3), (0, 0)))
    o_nsa = _nsa_prompt(rel_table, hm, hm4, kcv, mselt, gates_t, nbp, seq, n_sel)
    y_p = _token_stage_out(x1, o_nsa, o_fox, p)

    ms = nbd * dseq
    tq_d = 16
    lk = past + LANE
    xs1, nsa_rows_s, win_rows_s, fox_rows_s, hm_s, small_s = _token_stage_in(x_sample.reshape(ms, d_model), p)
    xc_d, nsa_dec, fox_dec, lf_dec = _regroup(
        page_table,
        cache_nsa_kv.reshape(cache_nsa_kv.shape[1], PAGE_SIZE * N_CACHE_COLS, HEAD_DIM),
        cache_fox_kv.reshape(cache_fox_kv.shape[1], PAGE_SIZE * N_CACHE_COLS, HEAD_DIM),
        cache_fox_logf[0], lk)
    hm_s4 = hm_s.reshape(N_HEAD_COLS, nbd, dseq, HEAD_DIM)
    nsa_dec = lax.dynamic_update_slice(nsa_dec, hm_s4[HM_K_SLC:HM_K_WIN], (0, 0, past, 0))
    fox_dec = lax.dynamic_update_slice(fox_dec, hm_s4[HM_K_FOX:HM_V_FOX + FOX_HEADS], (0, 0, past, 0))
    logf_s = small_s[:, N_GATE_COLS:N_GATE_COLS + FOX_HEADS].reshape(nbd, dseq, FOX_HEADS)
    lf_dec = lax.dynamic_update_slice(lf_dec, logf_s, (0, past, 0))

    pad_q = ((0, 0), (0, 0), (0, tq_d - dseq), (0, 0))
    q_dec = jnp.pad(hm_s4, pad_q).reshape(N_HEAD_COLS, nbd * tq_d, HEAD_DIM)
    gates_dec = jnp.pad(small_s.reshape(nbd, dseq, LANE), pad_q[1:]).reshape(nbd * tq_d, LANE)

    csum_d = _cumsum(lf_dec.transpose(0, 2, 1))
    tk_d = _largest_tile(past, (2048, 1024, 512, 256, 128))
    o_fox_s = _fox_attention(q_dec, fox_dec, csum_d[:, :, None, :], csum_d[:, :, past:past + tq_d, None],
                             q_head0=HM_Q_FOX, k_head0=0, v_head0=FOX_HEADS,
                             nb=nbd, nq=1, tq=tq_d, tk=tk_d, td=LANE, qoff=past)

    n_chunk_d = past // CMP_STRIDE
    n_cmp_d = (past + dseq - CMP_LEN) // CMP_STRIDE + 1
    n_sel_d = -(-(past + dseq) // SEL_BLOCK)
    assert n_cmp_d + CMP_LEN // CMP_STRIDE - 1 <= n_chunk_d, "compressed blocks must lie in the cached rows"
    kcv_d = _compress(xc_d, 0, w1cat, w2cat, pecat, k_norm_row, n_chunk_d)
    msel_d = _cmp_to_sel(n_cmp_d, n_sel_d, n_chunk_d, -(-n_sel_d // LANE) * LANE)
    win_old = state_win_kv[0].transpose(2, 3, 0, 1, 4).reshape(2 * NSA_KV_HEADS, nbd, WINDOW, HEAD_DIM).astype(BF16)
    win_dec = jnp.concatenate([win_old, hm_s4[HM_K_WIN:HM_V_WIN + NSA_KV_HEADS],
                               jnp.zeros((2 * NSA_KV_HEADS, nbd, LANE - dseq, HEAD_DIM), BF16)], axis=2)
    o_nsa_s = _nsa_attention(rel_table, q_dec, kcv_d, msel_d, nsa_dec, win_dec, gates_dec,
                             ks_head0=0, vs_head0=NSA_KV_HEADS, kw_head0=0, vw_head0=NSA_KV_HEADS,
                             nb=nbd, nq=1, tq=tq_d, tk=tk_d, td=LANE, qoff=past, wpos0=past - WINDOW,
                             n_sel=n_sel_d)
    unpad = lambda o: o.reshape(nbd, tq_d, -1)[:, :dseq].reshape(ms, -1)
    y_s = _token_stage_out(xs1, unpad(o_nsa_s), unpad(o_fox_s), p)

    kvh = (NSA_KV_HEADS, HEAD_DIM)
    win_keep = min(WINDOW, seq)
    win_p = win_rows.reshape(nbp, seq, 2, *kvh)[:, seq - win_keep:]
    win_s = jnp.concatenate([state_win_kv[0], win_rows_s.reshape(nbd, dseq, 2, *kvh)], axis=1)[:, dseq:]
    return (
        y_p.reshape(nbp, seq, d_model),
        y_s.reshape(nbd, dseq, d_model),
        nsa_rows.reshape(1, nbp, seq, 4, *kvh),
        fox_rows.reshape(1, nbp, seq, 2, FOX_HEADS, HEAD_DIM),
        logf.reshape(1, nbp, seq, FOX_HEADS),
        win_p[None],
        nsa_rows_s.reshape(1, nbd, dseq, 4, *kvh),
        fox_rows_s.reshape(1, nbd, dseq, 2, FOX_HEADS, HEAD_DIM),
        logf_s[None],
        win_s[None],
    )
```

```python
import functools
import math

import numpy as np
import jax
import jax.numpy as jnp
from jax import lax
from jax.experimental import pallas as pl
from jax.experimental.pallas import tpu as pltpu

HEAD_DIM = 128
NSA_HEADS = 8
FOX_HEADS = 8
NSA_KV_HEADS = 4
NSA_GROUP = NSA_HEADS // NSA_KV_HEADS
CMP_LEN = 32
CMP_STRIDE = 16
CMP_HIDDEN = 512
SEL_BLOCK = 64
SEL_TOPK = 16
WINDOW = 512
REL_BUCKETS = 32
REL_MAX_DIST = 128
RMS_EPS = 1e-6
PAGE_SIZE = 128

LANE = 128
NEG = -1e30
FORCED_SCORE = 1e30
VMEM_LIMIT = 56 * 1024 * 1024

BF16 = jnp.bfloat16
F32 = jnp.float32
NT_DIMS = (((1,), (1,)), ((), ()))


def _bucket_thresholds():
    n = np.arange(0, 4 * REL_MAX_DIST)
    max_exact = REL_BUCKETS // 2
    nf = np.maximum(n, 1).astype(np.float32)
    large = max_exact + (np.log(nf / max_exact) / math.log(REL_MAX_DIST / max_exact)
                         * (REL_BUCKETS - max_exact)).astype(np.int32)
    bucket = np.where(n < max_exact, n, np.minimum(large, REL_BUCKETS - 1))
    return [int(np.min(n[bucket >= k])) for k in range(1, REL_BUCKETS)]


BUCKET_THR = _bucket_thresholds()
FAR_DIST = BUCKET_THR[-1]


def _cparams(sem):
    return pltpu.CompilerParams(dimension_semantics=sem, vmem_limit_bytes=VMEM_LIMIT)


def _rms_rows(x, gain):
    ms = jnp.mean(x * x, axis=-1, keepdims=True)
    return x * lax.rsqrt(ms + RMS_EPS) * gain


def _ffn_kernel(x_ref, g_ref, wg_ref, wu_ref, wd_ref, o_ref, xn_ref):
    @pl.when(pl.program_id(1) == 0)
    def _():
        x = x_ref[...]
        xn_ref[...] = _rms_rows(x, g_ref[...]).astype(BF16)
        o_ref[...] = x

    xn = xn_ref[...]
    a = jnp.dot(xn, wg_ref[...], preferred_element_type=F32)
    u = jnp.dot(xn, wu_ref[...], preferred_element_type=F32)
    h = (a / (1.0 + jnp.exp(-a))) * u * 0.5
    o_ref[...] += jnp.dot(h.astype(BF16), wd_ref[...], preferred_element_type=F32)


def _ffn(x, gain, wg, wu, wd, tm, tf):
    m, d = x.shape
    f = wg.shape[1]
    return pl.pallas_call(
        _ffn_kernel,
        out_shape=jax.ShapeDtypeStruct((m, d), F32),
        grid=(m // tm, f // tf),
        in_specs=[
            pl.BlockSpec((tm, d), lambda i, j: (i, 0)),
            pl.BlockSpec((1, d), lambda i, j: (0, 0)),
            pl.BlockSpec((d, tf), lambda i, j: (0, j)),
            pl.BlockSpec((d, tf), lambda i, j: (0, j)),
            pl.BlockSpec((tf, d), lambda i, j: (j, 0)),
        ],
        out_specs=pl.BlockSpec((tm, d), lambda i, j: (i, 0)),
        scratch_shapes=[pltpu.VMEM((tm, d), BF16)],
        compiler_params=_cparams(("parallel", "arbitrary")),
        name="ffn",
    )(x, gain, wg, wu, wd)


IN_TN = 4 * HEAD_DIM
J_NSA = (2, 6)
J_WIN = (6, 8)
J_FOX = (10, 14)
N_HEAD_COLS = 56
HM_Q_NSA, HM_K_CMP, HM_K_SLC, HM_V_SLC, HM_K_WIN, HM_V_WIN = 0, 8, 16, 20, 24, 28
HM_Q_FOX, HM_K_FOX, HM_V_FOX = 32, 40, 48


IN_NORMED_TILES = (0, 1, 4, 6, 8, 9, 10, 11)


def _inproj_kernel(x_ref, g_ref, w_ref, cg_ref, ones_ref, nsa_ref, win_ref, fox_ref, hm_ref, xn_ref):
    j = pl.program_id(1)

    @pl.when(j == 0)
    def _():
        xn_ref[...] = _rms_rows(x_ref[...], g_ref[...]).astype(BF16)

    res = jnp.dot(xn_ref[...], w_ref[...], preferred_element_type=F32)
    tm = res.shape[0]
    heads = IN_TN // HEAD_DIM

    def emit(vals):
        for hh in range(heads):
            hm_ref[hh] = vals[:, hh * HEAD_DIM:(hh + 1) * HEAD_DIM].astype(BF16)
        for ref, (jlo, jhi) in ((nsa_ref, J_NSA), (win_ref, J_WIN), (fox_ref, J_FOX)):
            n_cols = (jhi - jlo) * heads

            @pl.when((j >= jlo) & (j < jhi))
            def _(ref=ref, jlo=jlo, n_cols=n_cols):
                for hh in range(heads):
                    ref[pl.ds((j - jlo) * heads + hh, tm, stride=n_cols), :] = vals[:, hh * HEAD_DIM:(hh + 1) * HEAD_DIM]

    is_normed = functools.reduce(jnp.logical_or, [j == t for t in IN_NORMED_TILES])

    @pl.when(is_normed)
    def _():
        sumsq = jnp.dot((res * res).astype(BF16), ones_ref[...], preferred_element_type=F32)
        emit(res * lax.rsqrt(sumsq * (1.0 / HEAD_DIM) + RMS_EPS) * cg_ref[...])

    @pl.when(jnp.logical_not(is_normed))
    def _():
        emit(res)


def _inproj(x, gain, w_main, colgain, tm):
    m, d = x.shape
    ncol = w_main.shape[1]
    nj = ncol // IN_TN

    heads = IN_TN // HEAD_DIM
    n_nsa, n_win, n_fox = [(hi - lo) * heads for lo, hi in (J_NSA, J_WIN, J_FOX)]
    head_ones = jnp.asarray(np.kron(np.eye(heads, dtype=np.float32), np.ones((HEAD_DIM, HEAD_DIM), np.float32)), BF16)
    return pl.pallas_call(
        _inproj_kernel,
        out_shape=(
            jax.ShapeDtypeStruct((m * n_nsa, HEAD_DIM), F32),
            jax.ShapeDtypeStruct((m * n_win, HEAD_DIM), F32),
            jax.ShapeDtypeStruct((m * n_fox, HEAD_DIM), F32),
            jax.ShapeDtypeStruct((N_HEAD_COLS, m, HEAD_DIM), BF16),
        ),
        grid=(m // tm, nj),
        in_specs=[
            pl.BlockSpec((tm, d), lambda i, j: (i, 0)),
            pl.BlockSpec((1, d), lambda i, j: (0, 0)),
            pl.BlockSpec((d, IN_TN), lambda i, j: (0, j)),
            pl.BlockSpec((1, IN_TN), lambda i, j: (0, j)),
            pl.BlockSpec((IN_TN, IN_TN), lambda i, j: (0, 0)),
        ],
        out_specs=(
            pl.BlockSpec((tm * n_nsa, HEAD_DIM), lambda i, j: (i, 0)),
            pl.BlockSpec((tm * n_win, HEAD_DIM), lambda i, j: (i, 0)),
            pl.BlockSpec((tm * n_fox, HEAD_DIM), lambda i, j: (i, 0)),
            pl.BlockSpec((heads, tm, HEAD_DIM), lambda i, j: (j, i, 0)),
        ),
        scratch_shapes=[pltpu.VMEM((tm, d), BF16)],
        compiler_params=_cparams(("parallel", "arbitrary")),
        name="inproj",
    )(x, gain, w_main, colgain, head_ones)


N_GATE_COLS = 3 * NSA_HEADS


def _small_kernel(x_ref, g_ref, w_ref, b_ref, o_ref):
    xn = _rms_rows(x_ref[...], g_ref[...]).astype(BF16)
    z = jnp.dot(xn, w_ref[...], preferred_element_type=F32) + b_ref[...]
    lane = lax.broadcasted_iota(jnp.int32, z.shape, 1)
    sig = 1.0 / (1.0 + jnp.exp(-z))
    logsig = jnp.minimum(z, 0.0) - jnp.log(1.0 + jnp.exp(-jnp.abs(z)))
    o_ref[...] = jnp.where(lane < N_GATE_COLS, sig,
                           jnp.where(lane < N_GATE_COLS + FOX_HEADS, logsig, 0.0))


def _small(x, gain, w_small, b_small, tm):
    m, d = x.shape
    return pl.pallas_call(
        _small_kernel,
        out_shape=jax.ShapeDtypeStruct((m, LANE), F32),
        grid=(m // tm,),
        in_specs=[
            pl.BlockSpec((tm, d), lambda i: (i, 0)),
            pl.BlockSpec((1, d), lambda i: (0, 0)),
            pl.BlockSpec((d, LANE), lambda i: (0, 0)),
            pl.BlockSpec((1, LANE), lambda i: (0, 0)),
        ],
        out_specs=pl.BlockSpec((tm, LANE), lambda i: (i, 0)),
        compiler_params=_cparams(("parallel",)),
        name="gates",
    )(x, gain, w_small, b_small)


def _outproj_kernel(on_ref, of_ref, gn_ref, gf_ref, w_ref, x_ref, y_ref):
    a = _rms_rows(on_ref[...], gn_ref[...]).astype(BF16)
    b = _rms_rows(of_ref[...], gf_ref[...]).astype(BF16)
    half = a.shape[1]
    y = jnp.dot(a, w_ref[:half, :], preferred_element_type=F32)
    y = y + jnp.dot(b, w_ref[half:, :], preferred_element_type=F32)
    y_ref[...] = x_ref[...] + y


def _outproj(o_nsa, o_fox, g_nsa, g_fox, w_out, x, tm):
    m, d = x.shape
    wn = o_nsa.shape[1]
    wf = o_fox.shape[1]
    return pl.pallas_call(
        _outproj_kernel,
        out_shape=jax.ShapeDtypeStruct((m, d), F32),
        grid=(m // tm,),
        in_specs=[
            pl.BlockSpec((tm, wn), lambda i: (i, 0)),
            pl.BlockSpec((tm, wf), lambda i: (i, 0)),
            pl.BlockSpec((1, wn), lambda i: (0, 0)),
            pl.BlockSpec((1, wf), lambda i: (0, 0)),
            pl.BlockSpec((wn + wf, d), lambda i: (0, 0)),
            pl.BlockSpec((tm, d), lambda i: (i, 0)),
        ],
        out_specs=pl.BlockSpec((tm, d), lambda i: (i, 0)),
        compiler_params=_cparams(("parallel",)),
        name="outproj",
    )(o_nsa, o_fox, g_nsa, g_fox, w_out, x)


CUMSUM_CHUNK = 512


def _cumsum_kernel(x_ref, o_ref):
    rows, length = x_ref.shape[1], x_ref.shape[2]
    n_big = length // CUMSUM_CHUNK

    def sweep(width, first, count, offset, carry):
        r = lax.broadcasted_iota(jnp.int32, (width, width), 0)
        c = lax.broadcasted_iota(jnp.int32, (width, width), 1)
        upper = (r <= c).astype(F32)

        def body(k, carry):
            st = pl.multiple_of(offset + k * width, LANE)
            x = x_ref[0, :, pl.ds(st, width)]
            cs = jnp.dot(x, upper, preferred_element_type=F32, precision=lax.Precision.HIGHEST) + carry
            o_ref[0, :, pl.ds(st, width)] = cs
            return cs[:, width - 1:width]

        return lax.fori_loop(first, count, body, carry)

    carry = sweep(CUMSUM_CHUNK, 0, n_big, 0, jnp.zeros((rows, 1), F32))
    sweep(LANE, 0, (length - n_big * CUMSUM_CHUNK) // LANE, n_big * CUMSUM_CHUNK, carry)


def _cumsum(x):
    b, h, length = x.shape
    return pl.pallas_call(
        _cumsum_kernel,
        out_shape=jax.ShapeDtypeStruct(x.shape, F32),
        grid=(b,),
        in_specs=[pl.BlockSpec((1, h, length), lambda i: (i, 0, 0))],
        out_specs=pl.BlockSpec((1, h, length), lambda i: (i, 0, 0)),
        compiler_params=_cparams(("parallel",)),
        name="logf_cumsum",
    )(x)


PE_ROWS = 16


def _compress_kernel(x_ref, w1_ref, w2_ref, pe_ref, kn_ref, o_ref):
    kind = pl.program_id(0) // NSA_KV_HEADS
    n = x_ref.shape[2]
    w1 = w1_ref[0]
    h = jnp.dot(x_ref[0, 0], w1, preferred_element_type=F32)
    pw = jnp.dot(pe_ref[0], w1, preferred_element_type=F32)
    const = pw[0:1, :CMP_HIDDEN] + pw[1:2, CMP_HIDDEN:]
    hid = h[:, :CMP_HIDDEN] + pltpu.roll(h[:, CMP_HIDDEN:], n - 1, 0) + const
    act = hid / (1.0 + jnp.exp(-hid))
    out = jnp.dot(act.astype(BF16), w2_ref[0], preferred_element_type=F32)
    normed = _rms_rows(out, kn_ref[...])
    o_ref[0, 0] = jnp.where(kind == 0, normed, out).astype(BF16)


def _compress(xc, c_off, w1cat, w2, pe, k_norm, n_rows):
    nb = xc.shape[1]
    return pl.pallas_call(
        _compress_kernel,
        out_shape=jax.ShapeDtypeStruct((2 * NSA_KV_HEADS, nb, n_rows, HEAD_DIM), BF16),
        grid=(2 * NSA_KV_HEADS, nb),
        in_specs=[
            pl.BlockSpec((1, 1, n_rows, CMP_STRIDE * HEAD_DIM), lambda c, b: (c_off + c, b, 0, 0)),
            pl.BlockSpec((1, CMP_STRIDE * HEAD_DIM, 2 * CMP_HIDDEN), lambda c, b: (c // NSA_KV_HEADS, 0, 0)),
            pl.BlockSpec((1, CMP_HIDDEN, HEAD_DIM), lambda c, b: (c // NSA_KV_HEADS, 0, 0)),
            pl.BlockSpec((1, PE_ROWS, CMP_STRIDE * HEAD_DIM), lambda c, b: (c // NSA_KV_HEADS, 0, 0)),
            pl.BlockSpec((1, HEAD_DIM), lambda c, b: (0, 0)),
        ],
        out_specs=pl.BlockSpec((1, 1, n_rows, HEAD_DIM), lambda c, b: (c, b, 0, 0)),
        compiler_params=_cparams(("parallel", "parallel")),
        name="compress",
    )(xc, w1cat, w2, pe, k_norm)


def _online_update(state, s, v):
    m, l, acc = state
    m_new = jnp.maximum(m, jnp.max(s, axis=-1, keepdims=True))
    alpha = jnp.exp(m - m_new)
    p = jnp.exp(s - m_new)
    l = alpha * l + jnp.sum(p, axis=-1, keepdims=True)
    acc = alpha * acc + jnp.dot(p.astype(BF16), v, preferred_element_type=F32)
    return m_new, l, acc


def _init_state(rows):
    return (jnp.full((rows, 1), NEG, F32), jnp.zeros((rows, 1), F32), jnp.zeros((rows, HEAD_DIM), F32))


def _rel_bias(dist, tbl_ref, head):
    out = jnp.full(dist.shape, tbl_ref[0, head], F32)
    for k, thr in enumerate(BUCKET_THR, start=1):
        out = jnp.where(dist >= thr, tbl_ref[k, head], out)
    return out


def _fox_kernel(q_ref, k_ref, v_ref, crow_ref, ccol_ref, o_ref, *, tq, tk, td, qoff):
    q0 = qoff + pl.program_id(2) * tq
    q = q_ref[0]
    cq = ccol_ref[0, 0]

    def logits(start, size):
        k = k_ref[0, 0, pl.ds(start, size), :]
        s = lax.dot_general(q, k, NT_DIMS, preferred_element_type=F32)
        return s + (cq - crow_ref[0, 0, :, pl.ds(start, size)])

    def body(kt, state):
        start = pl.multiple_of(kt * tk, tk)
        return _online_update(state, logits(start, tk), v_ref[0, 0, pl.ds(start, tk), :])

    state = lax.fori_loop(0, q0 // tk, body, _init_state(tq))
    start = pl.multiple_of(q0, LANE)
    s = logits(start, td)
    row = lax.broadcasted_iota(jnp.int32, (tq, td), 0)
    col = lax.broadcasted_iota(jnp.int32, (tq, td), 1)
    s = jnp.where(col <= row, s, NEG)
    _, l, acc = _online_update(state, s, v_ref[0, 0, pl.ds(start, td), :])
    o_ref[...] = acc / l


def _fox_attention(qarr, kv, crow, ccol, *, q_head0, k_head0, v_head0, nb, nq, tq, tk, td, qoff):
    lk = kv.shape[2]
    kern = functools.partial(_fox_kernel, tq=tq, tk=tk, td=td, qoff=qoff)
    return pl.pallas_call(
        kern,
        out_shape=jax.ShapeDtypeStruct((nb * nq * tq, FOX_HEADS * HEAD_DIM), F32),
        grid=(nb, FOX_HEADS, nq),
        in_specs=[
            pl.BlockSpec((1, tq, HEAD_DIM), lambda b, h, t: (q_head0 + h, b * nq + t, 0)),
            pl.BlockSpec((1, 1, lk, HEAD_DIM), lambda b, h, t: (k_head0 + h, b, 0, 0)),
            pl.BlockSpec((1, 1, lk, HEAD_DIM), lambda b, h, t: (v_head0 + h, b, 0, 0)),
            pl.BlockSpec((1, 1, 1, lk), lambda b, h, t: (b, h, 0, 0)),
            pl.BlockSpec((1, 1, tq, 1), lambda b, h, t: (b, h, t, 0)),
        ],
        out_specs=pl.BlockSpec((tq, HEAD_DIM), lambda b, h, t: (b * nq + t, h)),
        compiler_params=_cparams(("parallel", "parallel", "arbitrary")),
        name="fox_attention",
    )(qarr, kv, kv, crow, ccol)


def _nsa_kernel(tbl_ref, q_ref, kc_ref, vc_ref, msel_ref, ks_ref, vs_ref, kw_ref, vw_ref, gate_ref, o_ref,
                *, tq, tk, td, qoff, wpos0, n_sel):
    g = pl.program_id(1)
    q0 = qoff + pl.program_id(2) * tq
    rows = NSA_GROUP * tq
    q2 = q_ref[...].reshape(rows, HEAD_DIM)
    heads = [NSA_GROUP * g + hh for hh in range(NSA_GROUP)]
    pos_col = q0 + lax.broadcasted_iota(jnp.int32, (tq, 1), 0)

    def per_head(fn):
        return jnp.concatenate([fn(hh) for hh in range(NSA_GROUP)], axis=0)

    ncp = kc_ref.shape[2]
    blk_end = lax.broadcasted_iota(jnp.int32, (tq, ncp), 1) * CMP_STRIDE + (CMP_LEN - 1)
    d_cmp = pos_col - blk_end
    ok_cmp = d_cmp >= 0
    n_cmp = jnp.maximum(d_cmp, 0)
    s = lax.dot_general(q2, kc_ref[0, 0], NT_DIMS, preferred_element_type=F32)
    p_heads = []
    for hh in range(NSA_GROUP):
        sh = s[hh * tq:(hh + 1) * tq] + _rel_bias(n_cmp, tbl_ref, heads[hh])
        sh = jnp.where(ok_cmp, sh, NEG)
        e = jnp.where(ok_cmp, jnp.exp(sh - jnp.max(sh, axis=-1, keepdims=True)), 0.0)
        p_heads.append(e / jnp.maximum(jnp.sum(e, axis=-1, keepdims=True), 1e-30))
    o_cmp = jnp.dot(jnp.concatenate(p_heads, axis=0).astype(BF16), vc_ref[0, 0], preferred_element_type=F32)
    imp = jnp.dot(sum(p_heads), msel_ref[...], preferred_element_type=F32,
                  precision=lax.Precision.HIGHEST)

    nj = imp.shape[1]
    jj = lax.broadcasted_iota(jnp.int32, (tq, nj), 1)
    blk_q = pos_col // SEL_BLOCK
    forced = (jj == 0) | (jj == blk_q) | (jj == blk_q - 1)
    score = jnp.where(forced, FORCED_SCORE, jnp.where(jj <= blk_q, imp, -1.0))
    score = jnp.where(jj < n_sel, score, -2.0)
    rank = jnp.zeros((tq, nj), F32)
    for other in range(n_sel):
        col = jnp.broadcast_to(score[:, other:other + 1], score.shape)
        tie = jnp.where(jj > other, 1.0, 0.0)
        rank = rank + jnp.where(col > score, 1.0, jnp.where(col == score, tie, 0.0))
    sel = jnp.where((rank < SEL_TOPK) & (jj < n_sel), 1.0, 0.0).astype(BF16)

    def bias_and_mask(start, size, kpos0, near, window):
        def near_fn():
            kpos = kpos0 + start + lax.broadcasted_iota(jnp.int32, (tq, size), 1)
            d = pos_col - kpos
            ok = (d >= 0) & (d < WINDOW) if window else d >= 0
            n = jnp.maximum(d, 0)
            return per_head(lambda hh: jnp.where(ok, _rel_bias(n, tbl_ref, heads[hh]), NEG))

        def far_fn():
            return per_head(lambda hh: jnp.full((tq, size), tbl_ref[REL_BUCKETS - 1, heads[hh]], F32))

        if near is True:
            return near_fn()
        return lax.cond(near, near_fn, far_fn)

    def slc_tile(state, start, size, near):
        blk_of_key = (start + lax.broadcasted_iota(jnp.int32, (nj, size), 1)) // SEL_BLOCK
        expand = jnp.where(blk_of_key == lax.broadcasted_iota(jnp.int32, (nj, size), 0), 1.0, 0.0).astype(BF16)
        picked = jnp.dot(sel, expand, preferred_element_type=F32)
        drop = (picked - 1.0) * (-NEG)
        sc = lax.dot_general(q2, ks_ref[0, 0, pl.ds(start, size), :], NT_DIMS, preferred_element_type=F32)
        sc = sc + bias_and_mask(start, size, 0, near, False) + jnp.concatenate([drop] * NSA_GROUP, axis=0)
        return _online_update(state, sc, vs_ref[0, 0, pl.ds(start, size), :])

    def slc_body(kt, state):
        start = pl.multiple_of(kt * tk, tk)
        return slc_tile(state, start, tk, q0 - (start + tk - 1) < FAR_DIST)

    state = lax.fori_loop(0, q0 // tk, slc_body, _init_state(rows))
    _, l, acc = slc_tile(state, pl.multiple_of(q0, LANE), td, True)
    o_slc = acc / l

    def win_body(wt, state):
        start = pl.multiple_of(wt * LANE, LANE)
        near = q0 - (wpos0 + start + LANE - 1) < FAR_DIST
        sc = lax.dot_general(q2, kw_ref[0, 0, pl.ds(start, LANE), :], NT_DIMS, preferred_element_type=F32)
        kpos = wpos0 + start + lax.broadcasted_iota(jnp.int32, (tq, LANE), 1)
        d = pos_col - kpos
        edge = jnp.where((d >= 0) & (d < WINDOW), 0.0, NEG)
        sc = sc + bias_and_mask(start, LANE, wpos0, near, True) + jnp.concatenate([edge] * NSA_GROUP, axis=0)
        return _online_update(state, sc, vw_ref[0, 0, pl.ds(start, LANE), :])

    w_lo = jnp.maximum(q0 - wpos0 - WINDOW, 0) // LANE
    w_hi = (q0 + tq - 1 - wpos0) // LANE
    _, l, acc = lax.fori_loop(w_lo, w_hi + 1, win_body, _init_state(rows))
    o_win = acc / l

    gates = gate_ref[...]
    lane = lax.broadcasted_iota(jnp.int32, gates.shape, 1)

    def gate(hh, branch):
        return jnp.sum(jnp.where(lane == heads[hh] * 3 + branch, gates, 0.0), axis=-1, keepdims=True)

    for hh in range(NSA_GROUP):
        sl = slice(hh * tq, (hh + 1) * tq)
        o_ref[:, hh * HEAD_DIM:(hh + 1) * HEAD_DIM] = (
            gate(hh, 0) * o_cmp[sl] + gate(hh, 1) * o_slc[sl] + gate(hh, 2) * o_win[sl])


def _nsa_attention(rel_table, qarr, kcv, msel, kv, kwin, gates, *, ks_head0, vs_head0, kw_head0, vw_head0,
                   nb, nq, tq, tk, td, qoff, wpos0, n_sel):
    lk = kv.shape[2]
    lw = kwin.shape[2]
    ncp = kcv.shape[2]
    nj = msel.shape[1]
    kern = functools.partial(_nsa_kernel, tq=tq, tk=tk, td=td, qoff=qoff, wpos0=wpos0, n_sel=n_sel)
    g_w = NSA_GROUP * HEAD_DIM
    return pl.pallas_call(
        kern,
        out_shape=jax.ShapeDtypeStruct((nb * nq * tq, NSA_HEADS * HEAD_DIM), F32),
        grid=(nb, NSA_KV_HEADS, nq),
        in_specs=[
            pl.BlockSpec(memory_space=pltpu.SMEM),
            pl.BlockSpec((NSA_GROUP, tq, HEAD_DIM), lambda b, g, t: (g, b * nq + t, 0)),
            pl.BlockSpec((1, 1, ncp, HEAD_DIM), lambda b, g, t: (g, b, 0, 0)),
            pl.BlockSpec((1, 1, ncp, HEAD_DIM), lambda b, g, t: (NSA_KV_HEADS + g, b, 0, 0)),
            pl.BlockSpec((ncp, nj), lambda b, g, t: (0, 0)),
            pl.BlockSpec((1, 1, lk, HEAD_DIM), lambda b, g, t: (ks_head0 + g, b, 0, 0)),
            pl.BlockSpec((1, 1, lk, HEAD_DIM), lambda b, g, t: (vs_head0 + g, b, 0, 0)),
            pl.BlockSpec((1, 1, lw, HEAD_DIM), lambda b, g, t: (kw_head0 + g, b, 0, 0)),
            pl.BlockSpec((1, 1, lw, HEAD_DIM), lambda b, g, t: (vw_head0 + g, b, 0, 0)),
            pl.BlockSpec((tq, LANE), lambda b, g, t: (b * nq + t, 0)),
        ],
        out_specs=pl.BlockSpec((tq, g_w), lambda b, g, t: (b * nq + t, g)),
        compiler_params=_cparams(("parallel", "parallel", "arbitrary")),
        name="nsa_attention",
    )(rel_table, qarr, kcv, kcv, msel, kv, kv, kwin, kwin, gates)


def _cmp_to_sel(n_cmp, n_sel, rows, cols):
    c0 = np.arange(n_cmp)[:, None] * CMP_STRIDE
    s0 = np.arange(n_sel)[None, :] * SEL_BLOCK
    inter = np.clip(np.minimum(c0 + CMP_LEN, s0 + SEL_BLOCK) - np.maximum(c0, s0), 0, None)
    m = np.zeros((rows, cols), np.float32)
    m[:n_cmp, :n_sel] = inter / CMP_LEN
    return jnp.asarray(m)


def _transpose_tiles(x):
    n = x.shape[0] // LANE
    xf = x.astype(F32)
    return jnp.concatenate([xf[i * LANE:(i + 1) * LANE].T for i in range(n)], axis=1)


def _untranspose_tiles(xt):
    n = xt.shape[1] // LANE
    return jnp.concatenate([xt[:, i * LANE:(i + 1) * LANE].T for i in range(n)], axis=0)


AUX_ROWS = 16


def _fill_transposed(dst_ref, src_ref):
    def body(k, carry):
        st = pl.multiple_of(k * LANE, LANE)
        dst_ref[:HEAD_DIM, pl.ds(st, LANE)] = src_ref[0, 0, pl.ds(st, LANE), :].astype(F32).T.astype(BF16)
        return carry

    lax.fori_loop(0, src_ref.shape[2] // LANE, body, 0)


def _online_update_t(state, s_t, v_t):
    m, l, acc = state
    m_new = jnp.maximum(m, jnp.max(s_t, axis=0, keepdims=True))
    alpha = jnp.exp(m - m_new)
    p = jnp.exp(s_t - m_new)
    l = alpha * l + jnp.sum(p, axis=0, keepdims=True)
    acc = alpha * acc + jnp.dot(v_t, p.astype(BF16), preferred_element_type=F32)
    return m_new, l, acc


def _init_state_t(cols):
    return (jnp.full((1, cols), NEG, F32), jnp.zeros((1, cols), F32), jnp.zeros((HEAD_DIM, cols), F32))


FOX_TQ = 256
FOX_KEY_BLOCK = 512


def _split3(x):
    hi = x.astype(BF16)
    r1 = x - hi.astype(F32)
    mid = r1.astype(BF16)
    lo = (r1 - mid.astype(F32)).astype(BF16)
    return hi, mid, lo


def _lane_select3(parts, shape):
    lane = lax.broadcasted_iota(jnp.int32, shape, 1)
    hi, mid, lo = [part.astype(F32) for part in parts]
    return jnp.where(lane == 0, hi, jnp.where(lane == 1, mid, jnp.where(lane == 2, lo, 0.0))).astype(BF16)


def _ones_rows(width):
    row = lax.broadcasted_iota(jnp.int32, (AUX_ROWS, width), 0)
    return jnp.where(row == 0, 1.0, 0.0).astype(BF16)


def _online_update_aug(state, s_t, v_aug, shift=None):
    m, acc = state
    top = jnp.max(s_t, axis=0, keepdims=True)
    m_new = jnp.maximum(m, top if shift is None else top + shift)
    alpha = jnp.exp(m - m_new)
    p = jnp.exp(s_t - (m_new if shift is None else m_new - shift))
    acc = alpha * acc + jnp.dot(v_aug, p.astype(BF16), preferred_element_type=F32)
    return m_new, acc


def _init_state_aug(cols):
    return jnp.full((1, cols), NEG, F32), jnp.zeros((HEAD_DIM + AUX_ROWS, cols), F32)


def _finish_aug(acc):
    return acc[:HEAD_DIM] * (1.0 / acc[HEAD_DIM:HEAD_DIM + 1])


def _fox_prompt_kernel(q_ref, k_ref, v_ref, crow_ref, o_ref, vt_ref, ka_ref):
    qt = pl.program_id(2)
    tq = FOX_TQ

    @pl.when(qt == 0)
    def _():
        _fill_transposed(vt_ref, v_ref)
        vt_ref[HEAD_DIM:, :] = _ones_rows(vt_ref.shape[1])

        def body(k, carry):
            st = pl.multiple_of(k * LANE, LANE)
            c_col = jnp.broadcast_to(crow_ref[0, 0, :, pl.ds(st, LANE)], (LANE, LANE)).T
            ka_ref[pl.ds(st, LANE), :HEAD_DIM] = k_ref[0, 0, pl.ds(st, LANE), :]
            ka_ref[pl.ds(st, LANE), HEAD_DIM:] = _lane_select3(_split3(c_col), (LANE, LANE))
            return carry

        lax.fori_loop(0, k_ref.shape[2] // LANE, body, 0)

    q0 = pl.multiple_of(qt * tq, tq)
    q_t = _transpose_tiles(q_ref[0]).astype(BF16)
    row = lax.broadcasted_iota(jnp.int32, (LANE, tq), 0)
    q_aug = jnp.concatenate([q_t, jnp.where(row < 3, -1.0, 0.0).astype(BF16)], axis=0)
    c_q = crow_ref[0, 0, :, pl.ds(q0, tq)]
    blk = FOX_KEY_BLOCK

    last = q0 // blk
    rel = (lax.broadcasted_iota(jnp.int32, (blk, tq), 0) - lax.broadcasted_iota(jnp.int32, (blk, tq), 1))

    def logits(kb):
        start = pl.multiple_of(kb * blk, blk)
        s = jnp.dot(ka_ref[pl.ds(start, blk), :], q_aug, preferred_element_type=F32)
        return jnp.where(rel <= q0 - start, s, NEG)

    def body(kb, carry):
        m, acc, s_cur = carry
        s_next = logits(jnp.minimum(kb + 1, last))
        start = pl.multiple_of(kb * blk, blk)
        m, acc = _online_update_aug((m, acc), s_cur, vt_ref[:, pl.ds(start, blk)], c_q)
        return m, acc, s_next

    _, acc, _ = lax.fori_loop(0, last + 1, body, _init_state_aug(tq) + (logits(0),))
    o_ref[...] = _untranspose_tiles(_finish_aug(acc))


def _fox_prompt(hm, hm4, crow, nb, seq):
    nq = seq // FOX_TQ
    return pl.pallas_call(
        _fox_prompt_kernel,
        out_shape=jax.ShapeDtypeStruct((nb * seq, FOX_HEADS * HEAD_DIM), F32),
        grid=(nb, FOX_HEADS, nq),
        in_specs=[
            pl.BlockSpec((1, FOX_TQ, HEAD_DIM), lambda b, h, t: (HM_Q_FOX + h, b * nq + t, 0)),
            pl.BlockSpec((1, 1, seq, HEAD_DIM), lambda b, h, t: (HM_K_FOX + h, b, 0, 0)),
            pl.BlockSpec((1, 1, seq, HEAD_DIM), lambda b, h, t: (HM_V_FOX + h, b, 0, 0)),
            pl.BlockSpec((1, 1, 1, seq), lambda b, h, t: (b, h, 0, 0)),
        ],
        out_specs=pl.BlockSpec((FOX_TQ, HEAD_DIM), lambda b, h, t: (b * nq + t, h)),
        scratch_shapes=[pltpu.VMEM((HEAD_DIM + AUX_ROWS, seq), BF16), pltpu.VMEM((seq, 2 * HEAD_DIM), BF16)],
        compiler_params=_cparams(("parallel", "parallel", "arbitrary")),
        name="fox_prompt",
    )(hm, hm4, hm4, crow)


NSA_TQ = LANE
NSA_COLS = NSA_GROUP * NSA_TQ
WIN_TILES = WINDOW // NSA_TQ + 1
SLC_BLOCK_TILES = 4
SLC_MASK_ROWS = SLC_BLOCK_TILES * NSA_TQ // SEL_BLOCK


def _nsa_prompt_kernel(tbl_ref, q_ref, kc_ref, vc_ref, mselt_ref, ks_ref, vs_ref, kw_ref, vw_ref, gate_ref, o_ref,
                       vst_ref, vwt_ref, vct_ref, wb_ref, pc_ref, drop_ref, ksa_ref, wbd_ref, *, n_sel, cmp_back):
    g = pl.program_id(1)
    qt = pl.program_id(2)
    tq = NSA_TQ
    cols = NSA_COLS
    heads = [NSA_GROUP * g + hh for hh in range(NSA_GROUP)]
    ncp = kc_ref.shape[2]
    nj = mselt_ref.shape[0]

    @pl.when(qt == 0)
    def _():
        _fill_transposed(vst_ref, vs_ref)
        _fill_transposed(vwt_ref, vw_ref)
        vst_ref[HEAD_DIM:, :] = _ones_rows(vst_ref.shape[1])
        vwt_ref[HEAD_DIM:, :] = _ones_rows(vwt_ref.shape[1])
        vct_ref[...] = _transpose_tiles(vc_ref[0, 0]).astype(BF16)

        def fill_keys(k, carry):
            st = pl.multiple_of(k * tq, tq)
            lane = lax.broadcasted_iota(jnp.int32, (tq, tq), 1)
            blk_in_step = (k % SLC_BLOCK_TILES) * (tq // SEL_BLOCK) + lax.broadcasted_iota(jnp.int32, (tq, tq), 0) // SEL_BLOCK
            extra = (lane == blk_in_step) | ((lane >= SLC_MASK_ROWS) & (lane < SLC_MASK_ROWS + 3))
            ksa_ref[pl.ds(st, tq), :HEAD_DIM] = ks_ref[0, 0, pl.ds(st, tq), :]
            ksa_ref[pl.ds(st, tq), HEAD_DIM:] = jnp.where(extra, 1.0, 0.0).astype(BF16)
            return carry

        lax.fori_loop(0, ks_ref.shape[2] // tq, fill_keys, 0)
        key = lax.broadcasted_iota(jnp.int32, (tq, tq), 0)
        qry = lax.broadcasted_iota(jnp.int32, (tq, tq), 1)
        blk = lax.broadcasted_iota(jnp.int32, (pc_ref.shape[0], tq), 0) - cmp_back
        d_cmp = lax.broadcasted_iota(jnp.int32, (pc_ref.shape[0], tq), 1) - (blk * CMP_STRIDE + (CMP_LEN - 1))
        for hh in range(NSA_GROUP):
            sl = slice(hh * tq, (hh + 1) * tq)
            far = tbl_ref[REL_BUCKETS - 1, heads[hh]]
            for delta in range(2):
                d = delta * tq + qry - key
                wb_ref[delta, :, sl] = jnp.where(d >= 0, _rel_bias(jnp.maximum(d, 0), tbl_ref, heads[hh]), NEG)
            for delta in range(2, WIN_TILES - 1):
                wb_ref[delta, :, sl] = jnp.full((tq, tq), far, F32)
            wb_ref[WIN_TILES - 1, :, sl] = jnp.where(qry < key, far, NEG)
            wb_ref[WIN_TILES, :, sl] = jnp.full((tq, tq), NEG, F32)
            for delta in range(2):
                wbd_ref[delta, :, sl] = wb_ref[delta, :, sl] - far
            wbd_ref[2, :, sl] = jnp.zeros((tq, tq), F32)
            wbd_ref[3, :, sl] = jnp.full((tq, tq), NEG, F32)
            pc_ref[:, sl] = jnp.where(d_cmp >= 0, _rel_bias(jnp.maximum(d_cmp, 0), tbl_ref, heads[hh]), NEG)

    q_t = jnp.concatenate([q_ref[hh].astype(F32).T for hh in range(NSA_GROUP)], axis=1).astype(BF16)

    off = pl.multiple_of(cmp_back - qt * (tq // CMP_STRIDE), 8)
    s_t = jnp.dot(kc_ref[0, 0], q_t, preferred_element_type=F32) + pc_ref[pl.ds(off, ncp), :]
    m = jnp.max(s_t, axis=0, keepdims=True)
    e = jnp.exp(s_t - m)
    inv = jnp.where(m > 0.5 * NEG, 1.0 / jnp.sum(e, axis=0, keepdims=True), 0.0)
    p_t = e * inv
    o_cmp = jnp.dot(vct_ref[...], p_t.astype(BF16), preferred_element_type=F32)
    p_sum = p_t[:, :tq]
    for hh in range(1, NSA_GROUP):
        p_sum = p_sum + p_t[:, hh * tq:(hh + 1) * tq]
    imp = jnp.dot(mselt_ref[...], p_sum, preferred_element_type=F32, precision=lax.Precision.HIGHEST)

    jrow = lax.broadcasted_iota(jnp.int32, (nj, tq), 0)
    pos = qt * tq + lax.broadcasted_iota(jnp.int32, (nj, tq), 1)
    blk_q = jnp.right_shift(pos, int(math.log2(SEL_BLOCK)))
    forced = (jrow == 0) | (jrow == blk_q) | (jrow == blk_q - 1)
    score = jnp.where(forced, FORCED_SCORE, jnp.where(jrow <= blk_q, imp, -1.0))
    score = jnp.where(jrow < n_sel, score, -2.0)
    ranks = []
    for r in range(nj // 8):
        mine = score[r * 8:(r + 1) * 8]
        jmine = jrow[r * 8:(r + 1) * 8]
        rank = jnp.zeros((8, tq), F32)
        for other in range(n_sel):
            row = jnp.broadcast_to(score[other:other + 1], (8, tq))
            if other < r * 8:
                beats = row >= mine
            elif other >= (r + 1) * 8:
                beats = row > mine
            else:
                beats = (row > mine) | ((row == mine) & (jmine > other))
            rank = rank + jnp.where(beats, 1.0, 0.0)
        ranks.append(rank)
    rank = jnp.concatenate(ranks, axis=0)
    drop = jnp.where((rank < SEL_TOPK) & (jrow < n_sel), 0.0, NEG)
    drop_ref[...] = jnp.concatenate([drop] * NSA_GROUP, axis=1)

    blk = SLC_BLOCK_TILES * tq
    col = lax.broadcasted_iota(jnp.int32, (8, cols), 1)
    far_row = jnp.full((8, cols), tbl_ref[REL_BUCKETS - 1, heads[NSA_GROUP - 1]], F32)
    for hh in range(NSA_GROUP - 1):
        far_row = jnp.where(col // tq == hh, tbl_ref[REL_BUCKETS - 1, heads[hh]], far_row)
    row8 = lax.broadcasted_iota(jnp.int32, (8, cols), 0)
    parts = [part.astype(F32) for part in _split3(far_row)]
    far_rows = jnp.where(row8 == 0, parts[0], jnp.where(row8 == 1, parts[1], jnp.where(row8 == 2, parts[2], 0.0)))
    pad_rows = jnp.zeros((HEAD_DIM - SLC_MASK_ROWS - 8, cols), BF16)

    last = qt // SLC_BLOCK_TILES

    def slc_logits(kb):
        start = pl.multiple_of(kb * blk, blk)
        masks = drop_ref[pl.ds(pl.multiple_of(kb * SLC_MASK_ROWS, SLC_MASK_ROWS), SLC_MASK_ROWS), :]
        extra = jnp.concatenate([masks, far_rows], axis=0).astype(BF16)
        q_aug = jnp.concatenate([q_t, extra, pad_rows], axis=0)
        s = jnp.dot(ksa_ref[pl.ds(start, blk), :], q_aug, preferred_element_type=F32)
        terms = []
        for i in range(SLC_BLOCK_TILES):
            delta = qt - (kb * SLC_BLOCK_TILES + i)
            terms.append(wbd_ref[jnp.where(delta < 0, 3, jnp.minimum(delta, 2))])
        return s + jnp.concatenate(terms, axis=0)

    def slc_body(kb, carry):
        m, acc, s_cur = carry
        s_next = slc_logits(jnp.minimum(kb + 1, last))
        start = pl.multiple_of(kb * blk, blk)
        m, acc = _online_update_aug((m, acc), s_cur, vst_ref[:, pl.ds(start, blk)])
        return m, acc, s_next

    _, acc, _ = lax.fori_loop(0, last + 1, slc_body, _init_state_aug(cols) + (slc_logits(0),))
    o_slc = _finish_aug(acc)

    w0 = jnp.maximum(qt - (WIN_TILES - 1), 0)
    start = pl.multiple_of(w0 * tq, tq)
    span = WIN_TILES * tq
    s = jnp.dot(kw_ref[0, 0, pl.ds(start, span), :], q_t, preferred_element_type=F32)
    terms = []
    for i in range(WIN_TILES):
        delta = qt - (w0 + i)
        terms.append(wb_ref[jnp.where(delta < 0, WIN_TILES, delta)])
    s = s + jnp.concatenate(terms, axis=0)
    p = jnp.exp(s - jnp.max(s, axis=0, keepdims=True))
    o_win = _finish_aug(jnp.dot(vwt_ref[:, pl.ds(start, span)], p.astype(BF16), preferred_element_type=F32))

    gate = gate_ref[0, 0, 0]
    o_t = gate[0:1] * o_cmp + gate[1:2] * o_slc + gate[2:3] * o_win
    for hh in range(NSA_GROUP):
        o_ref[:, hh * HEAD_DIM:(hh + 1) * HEAD_DIM] = o_t[:, hh * tq:(hh + 1) * tq].T


def _nsa_prompt(rel_table, hm, hm4, kcv, mselt, gates_t, nb, seq, n_sel):
    tq = NSA_TQ
    nq = seq // tq
    ncp = kcv.shape[2]
    nj = mselt.shape[0]
    cmp_back = (nq - 1) * (tq // CMP_STRIDE)
    kern = functools.partial(_nsa_prompt_kernel, n_sel=n_sel, cmp_back=cmp_back)
    kv_spec = lambda head0: pl.BlockSpec((1, 1, seq, HEAD_DIM), lambda b, g, t: (head0 + g, b, 0, 0))
    return pl.pallas_call(
        kern,
        out_shape=jax.ShapeDtypeStruct((nb * seq, NSA_HEADS * HEAD_DIM), F32),
        grid=(nb, NSA_KV_HEADS, nq),
        in_specs=[
            pl.BlockSpec(memory_space=pltpu.SMEM),
            pl.BlockSpec((NSA_GROUP, tq, HEAD_DIM), lambda b, g, t: (g, b * nq + t, 0)),
            pl.BlockSpec((1, 1, ncp, HEAD_DIM), lambda b, g, t: (g, b, 0, 0)),
            pl.BlockSpec((1, 1, ncp, HEAD_DIM), lambda b, g, t: (NSA_KV_HEADS + g, b, 0, 0)),
            pl.BlockSpec((nj, ncp), lambda b, g, t: (0, 0)),
            kv_spec(HM_K_SLC), kv_spec(HM_V_SLC), kv_spec(HM_K_WIN), kv_spec(HM_V_WIN),
            pl.BlockSpec((1, 1, 1, 8, NSA_COLS), lambda b, g, t: (b, g, t, 0, 0)),
        ],
        out_specs=pl.BlockSpec((tq, NSA_GROUP * HEAD_DIM), lambda b, g, t: (b * nq + t, g)),
        scratch_shapes=[
            pltpu.VMEM((HEAD_DIM + AUX_ROWS, seq), BF16),
            pltpu.VMEM((HEAD_DIM + AUX_ROWS, seq), BF16),
            pltpu.VMEM((HEAD_DIM, ncp), BF16),
            pltpu.VMEM((WIN_TILES + 1, tq, NSA_COLS), F32),
            pltpu.VMEM((cmp_back + ncp, NSA_COLS), F32),
            pltpu.VMEM((nj, NSA_COLS), F32),
            pltpu.VMEM((seq, 2 * HEAD_DIM), BF16),
            pltpu.VMEM((4, tq, NSA_COLS), F32),
        ],
        compiler_params=_cparams(("parallel", "parallel", "arbitrary")),
        name="nsa_prompt",
    )(rel_table, hm, kcv, kcv, mselt, hm4, hm4, hm4, hm4, gates_t)


CHUNK_ROWS = 512


def _chunkify_kernel(x_ref, o_ref, *, n_cols):
    n = o_ref.shape[2]
    for c in range(o_ref.shape[0]):
        for s in range(CMP_STRIDE):
            o_ref[c, 0, :, s * HEAD_DIM:(s + 1) * HEAD_DIM] = (
                x_ref[0, pl.ds(s * n_cols + c, n, stride=CMP_STRIDE * n_cols), :].astype(BF16))


def _chunkify(rows3, n_cols, n_heads, seq):
    nb = rows3.shape[0]
    tr = _largest_tile(seq, (CHUNK_ROWS, 256))
    return pl.pallas_call(
        functools.partial(_chunkify_kernel, n_cols=n_cols),
        out_shape=jax.ShapeDtypeStruct((n_heads, nb, seq // CMP_STRIDE, CMP_STRIDE * HEAD_DIM), BF16),
        grid=(nb, seq // tr),
        in_specs=[pl.BlockSpec((1, tr * n_cols, HEAD_DIM), lambda b, i: (b, i, 0))],
        out_specs=pl.BlockSpec((n_heads, 1, tr // CMP_STRIDE, CMP_STRIDE * HEAD_DIM), lambda b, i: (0, b, i, 0)),
        compiler_params=_cparams(("parallel", "parallel")),
        name="chunkify",
    )(rows3)


REGROUP_PAGES = 4


N_CACHE_COLS = 16
N_CMP_COLS = 2 * NSA_KV_HEADS


def _regroup_kernel(pt_ref, *refs, n_steps):
    del pt_ref
    npg = REGROUP_PAGES
    nsa_in, lf_in = refs[:npg], refs[npg:2 * npg]
    xc_out, slc_out, lf_out = refs[2 * npg:]
    is_tail = pl.program_id(1) >= n_steps
    chunks = PAGE_SIZE // CMP_STRIDE
    chunk_stride = CMP_STRIDE * N_CACHE_COLS

    @pl.when(is_tail)
    def _():
        slc_out[...] = jnp.zeros(slc_out.shape, slc_out.dtype)
        lf_out[...] = jnp.zeros(lf_out.shape, lf_out.dtype)

    @pl.when(jnp.logical_not(is_tail))
    def _():
        for p in range(npg):
            rows = slice(p * PAGE_SIZE, (p + 1) * PAGE_SIZE)
            for c in range(N_CACHE_COLS - N_CMP_COLS):
                slc_out[c, 0, rows, :] = nsa_in[p][0, pl.ds(N_CMP_COLS + c, PAGE_SIZE, stride=N_CACHE_COLS), :].astype(BF16)
            lf_out[0, rows, :] = lf_in[p][0]
        for pair in range(npg // 2):
            for c in range(N_CMP_COLS):
                for s in range(CMP_STRIDE):
                    first = s * N_CACHE_COLS + c
                    both = [nsa_in[2 * pair + i][0, pl.ds(first, chunks, stride=chunk_stride), :] for i in range(2)]
                    xc_out[c, 0, pair * 2 * chunks:(pair + 1) * 2 * chunks, s * HEAD_DIM:(s + 1) * HEAD_DIM] = (
                        jnp.concatenate(both, axis=0).astype(BF16))


def _regroup(page_table, cache_nsa, cache_logf, lk):
    nb, n_pages = page_table.shape
    npg = REGROUP_PAGES
    n_steps = n_pages // npg
    rows = npg * PAGE_SIZE
    last = n_steps - 1
    n_slc = N_CACHE_COLS - N_CMP_COLS
    chunk_w = CMP_STRIDE * HEAD_DIM

    def page_map(p):
        return lambda b, s, pt: (pt[b, jnp.minimum(s, last) * npg + p], 0, 0)

    def specs(arr):
        return [pl.BlockSpec((1,) + arr.shape[1:], page_map(p)) for p in range(npg)]

    assert (lk - n_pages * PAGE_SIZE) % rows == 0
    n_tail = (lk - n_pages * PAGE_SIZE) // rows
    grid_spec = pltpu.PrefetchScalarGridSpec(
        num_scalar_prefetch=1,
        grid=(nb, n_steps + n_tail),
        in_specs=specs(cache_nsa) + specs(cache_logf),
        out_specs=(
            pl.BlockSpec((N_CMP_COLS, 1, rows // CMP_STRIDE, chunk_w), lambda b, s, pt: (0, b, jnp.minimum(s, last), 0)),
            pl.BlockSpec((n_slc, 1, rows, HEAD_DIM), lambda b, s, pt: (0, b, s, 0)),
            pl.BlockSpec((1, rows, cache_logf.shape[2]), lambda b, s, pt: (b, s, 0)),
        ),
    )
    return pl.pallas_call(
        functools.partial(_regroup_kernel, n_steps=n_steps),
        out_shape=(
            jax.ShapeDtypeStruct((N_CMP_COLS, nb, n_pages * PAGE_SIZE // CMP_STRIDE, chunk_w), BF16),
            jax.ShapeDtypeStruct((n_slc, nb, lk, HEAD_DIM), BF16),
            jax.ShapeDtypeStruct((nb, lk, cache_logf.shape[2]), F32),
        ),
        grid_spec=grid_spec,
        compiler_params=_cparams(("parallel", "arbitrary")),
        name="cache_regroup",
    )(page_table, *([cache_nsa] * npg), *([cache_logf] * npg))


FOX_DEC_PAGES = 8


def _fox_decode_kernel(pt_ref, *refs, dseq):
    del pt_ref
    npg = FOX_DEC_PAGES
    pages = refs[:npg]
    qbd_ref, ccol_ref, cq_ref, knew_ref, vnew_ref, o_ref, m_ref, acc_ref = refs[npg:]
    step = pl.program_id(1)
    is_tail = step == pl.num_programs(1) - 1
    width = FOX_HEADS * HEAD_DIM

    @pl.when(step == 0)
    def _():
        m_ref[...] = jnp.full(m_ref.shape, NEG, F32)
        acc_ref[...] = jnp.zeros(acc_ref.shape, F32)

    def fold(k_blk, v_blk, bias):
        n = k_blk.shape[0]
        s_t = jnp.dot(k_blk, qbd_ref[0], preferred_element_type=F32) + bias
        m_new = jnp.maximum(m_ref[...], jnp.max(s_t, axis=0, keepdims=True))
        alpha = jnp.exp(m_ref[...] - m_new)
        p = jnp.exp(s_t - m_new)
        p_t = jnp.concatenate([p[i * LANE:(i + 1) * LANE].T for i in range(n // LANE)], axis=1).astype(BF16)
        v_aug = jnp.concatenate([v_blk, jnp.ones((n, LANE), BF16)], axis=1)
        upd = jnp.dot(p_t, v_aug, preferred_element_type=F32)
        alpha_col = jnp.broadcast_to(alpha, (LANE, LANE)).T
        acc_ref[...] = acc_ref[...] * jnp.concatenate([alpha_col] * (width // LANE + 1), axis=1) + upd
        m_ref[...] = m_new

    @pl.when(jnp.logical_not(is_tail))
    def _():
        def heads_of(p, first):
            cols = [pages[p][0, pl.ds(first + h, PAGE_SIZE, stride=N_CACHE_COLS), :] for h in range(FOX_HEADS)]
            return jnp.concatenate(cols, axis=1).astype(BF16)

        k_blk = jnp.concatenate([heads_of(p, 0) for p in range(npg)], axis=0)
        v_blk = jnp.concatenate([heads_of(p, FOX_HEADS) for p in range(npg)], axis=0)
        fold(k_blk, v_blk, cq_ref[0] - ccol_ref[0])

    @pl.when(is_tail)
    def _():
        row = lax.broadcasted_iota(jnp.int32, (LANE, LANE), 0)
        t_of_col = lax.broadcasted_iota(jnp.int32, (LANE, LANE), 1) % dseq
        bias = jnp.where(row <= t_of_col, cq_ref[0] - ccol_ref[0, :LANE, :], NEG)
        fold(knew_ref[0], vnew_ref[0], bias)
        acc = acc_ref[...]
        inv = 1.0 / acc[:, width:]
        o_ref[0] = acc[:, :width] * jnp.concatenate([inv] * (width // LANE), axis=1)


def _fox_decode(page_table, cache_fox, qbd, ccols, cq, knew, vnew, dseq):
    nb, n_pages = page_table.shape
    npg = FOX_DEC_PAGES
    n_steps = n_pages // npg
    last = n_steps - 1
    width = FOX_HEADS * HEAD_DIM

    def page_map(p):
        return lambda b, s, pt: (pt[b, jnp.minimum(s, last) * npg + p], 0, 0)

    per_b = lambda shape: pl.BlockSpec((1,) + shape, lambda b, s, pt: (b, 0, 0))
    grid_spec = pltpu.PrefetchScalarGridSpec(
        num_scalar_prefetch=1,
        grid=(nb, n_steps + 1),
        in_specs=[pl.BlockSpec((1,) + cache_fox.shape[1:], page_map(p)) for p in range(npg)] + [
            per_b((width, LANE)),
            pl.BlockSpec((1, npg * PAGE_SIZE, LANE), lambda b, s, pt: (b, s, 0)),
            per_b((1, LANE)),
            per_b((LANE, width)),
            per_b((LANE, width)),
        ],
        out_specs=per_b((LANE, width)),
        scratch_shapes=[pltpu.VMEM((1, LANE), F32), pltpu.VMEM((LANE, width + LANE), F32)],
    )
    return pl.pallas_call(
        functools.partial(_fox_decode_kernel, dseq=dseq),
        out_shape=jax.ShapeDtypeStruct((nb, LANE, width), F32),
        grid_spec=grid_spec,
        compiler_params=_cparams(("parallel", "arbitrary")),
        name="fox_decode",
    )(page_table, *([cache_fox] * npg), qbd, ccols, cq, knew, vnew)


NSA_DEC_KEYS = 2048


def _lane_transpose(p):
    return jnp.concatenate([p[i * LANE:(i + 1) * LANE].T for i in range(p.shape[0] // LANE)], axis=1).astype(BF16)


def _nsa_decode_kernel(qbd_ref, kc_ref, vc_ref, mselt_ref, pair_ref, tcol_ref, ks_ref, vs_ref, kw_ref, vw_ref,
                       gate_ref, o_ref, drop_ref, m_ref, acc_ref, ocmp_ref, *, dseq, past, n_sel):
    step = pl.program_id(1)
    n_tiles = pl.num_programs(1)
    width = NSA_KV_HEADS * HEAD_DIM
    qbd = qbd_ref[0]
    col = lax.broadcasted_iota(jnp.int32, (1, LANE), 1)
    q_pos = past + col % dseq

    def side_by_side(ref, rows=None):
        parts = [ref[g, 0] if rows is None else ref[g, 0, rows, :] for g in range(NSA_KV_HEADS)]
        return jnp.concatenate(parts, axis=1)

    def rel_bias_cols(dist):
        out = jnp.broadcast_to(tcol_ref[0:1, :], dist.shape)
        for k, thr in enumerate(BUCKET_THR, start=1):
            out = jnp.where(dist >= thr, tcol_ref[k:k + 1, :], out)
        return out

    def softmax_pv(s_t, v_all):
        m = jnp.max(s_t, axis=0, keepdims=True)
        e = jnp.exp(s_t - m)
        inv = jnp.where(m > 0.5 * NEG, 1.0 / jnp.sum(e, axis=0, keepdims=True), 0.0)
        p = e * inv
        return p, jnp.dot(_lane_transpose(p), v_all, preferred_element_type=F32)

    @pl.when(step == 0)
    def _():
        m_ref[...] = jnp.full(m_ref.shape, NEG, F32)
        acc_ref[...] = jnp.zeros(acc_ref.shape, F32)
        ncp = kc_ref.shape[2]
        blk_end = lax.broadcasted_iota(jnp.int32, (ncp, LANE), 0) * CMP_STRIDE + (CMP_LEN - 1)
        d = q_pos - blk_end
        s_t = jnp.dot(side_by_side(kc_ref), qbd, preferred_element_type=F32)
        s_t = jnp.where(d >= 0, s_t + rel_bias_cols(jnp.maximum(d, 0)), NEG)
        p, ocmp_ref[...] = softmax_pv(s_t, side_by_side(vc_ref))
        imp = jnp.dot(mselt_ref[...], p, preferred_element_type=F32, precision=lax.Precision.HIGHEST)
        imp = jnp.dot(imp, pair_ref[...], preferred_element_type=F32, precision=lax.Precision.HIGHEST)
        nj = imp.shape[0]
        jrow = lax.broadcasted_iota(jnp.int32, (nj, LANE), 0)
        blk_q = jnp.right_shift(q_pos, int(math.log2(SEL_BLOCK)))
        forced = (jrow == 0) | (jrow == blk_q) | (jrow == blk_q - 1)
        score = jnp.where(forced, FORCED_SCORE, jnp.where(jrow <= blk_q, imp, -1.0))
        score = jnp.where(jrow < n_sel, score, -2.0)
        drop_ref[...] = score
        rank = jnp.zeros((nj, LANE), F32)

        def count(other, rank):
            row = jnp.broadcast_to(drop_ref[pl.ds(other, 1), :], (nj, LANE))
            beats = (row > score) | ((row == score) & (jrow > other))
            return rank + jnp.where(beats, 1.0, 0.0)

        rank = lax.fori_loop(0, n_sel, count, rank)
        far = tcol_ref[REL_BUCKETS - 1:REL_BUCKETS, :]
        drop_ref[...] = jnp.where((rank < SEL_TOPK) & (jrow <= blk_q), far, NEG)

    tile = ks_ref.shape[2]
    per_tile = tile // SEL_BLOCK
    start = step * tile
    first_blk = pl.multiple_of(step * per_tile, 8)
    rows = [jnp.broadcast_to(drop_ref[pl.ds(first_blk + i, 1), :], (SEL_BLOCK, LANE)) for i in range(per_tile)]
    s_t = jnp.dot(side_by_side(ks_ref), qbd, preferred_element_type=F32) + jnp.concatenate(rows, axis=0)

    def near_fix(s_t):
        def fix(rows_at, s_rows):
            key_pos = start + rows_at + lax.broadcasted_iota(jnp.int32, (LANE, LANE), 0)
            d = q_pos - key_pos
            far = tcol_ref[REL_BUCKETS - 1:REL_BUCKETS, :]
            return jnp.where(d >= 0, s_rows + (rel_bias_cols(jnp.maximum(d, 0)) - far), NEG)

        head = fix(0, s_t[:LANE])
        tail = fix(tile - LANE, s_t[tile - LANE:])
        return jnp.concatenate([head, s_t[LANE:tile - LANE], tail], axis=0)

    is_near = (start + tile > past - FAR_DIST)
    s_t = lax.cond(is_near, near_fix, lambda s: s, s_t)
    m_new = jnp.maximum(m_ref[...], jnp.max(s_t, axis=0, keepdims=True))
    alpha = jnp.exp(m_ref[...] - m_new)
    p = jnp.exp(s_t - m_new)
    v_aug = jnp.concatenate([side_by_side(vs_ref), jnp.ones((tile, LANE), BF16)], axis=1)
    upd = jnp.dot(_lane_transpose(p), v_aug, preferred_element_type=F32)
    alpha_col = jnp.broadcast_to(alpha, (LANE, LANE)).T
    acc_ref[...] = acc_ref[...] * jnp.concatenate([alpha_col] * (width // LANE + 1), axis=1) + upd
    m_ref[...] = m_new

    @pl.when(step == n_tiles - 1)
    def _():
        acc = acc_ref[...]
        o_slc = acc[:, :width] * jnp.concatenate([1.0 / acc[:, width:]] * (width // LANE), axis=1)
        span = kw_ref.shape[2]
        key_pos = (past - WINDOW) + lax.broadcasted_iota(jnp.int32, (span, LANE), 0)
        d = q_pos - key_pos
        s_w = jnp.dot(side_by_side(kw_ref), qbd, preferred_element_type=F32)
        s_w = jnp.where((d >= 0) & (d < WINDOW), s_w + rel_bias_cols(jnp.maximum(d, 0)), NEG)
        _, o_win = softmax_pv(s_w, side_by_side(vw_ref))
        g = gate_ref[0]
        tile4 = lambda a: jnp.concatenate([a] * (width // LANE), axis=1)
        o_ref[0] = tile4(g[0]) * ocmp_ref[...] + tile4(g[1]) * o_slc + tile4(g[2]) * o_win


def _nsa_decode(qbd, kcv, mselt, pair, tcols, nsa_dec, win_dec, gcols, *, dseq, past, n_sel):
    nb = qbd.shape[0]
    ncp = kcv.shape[2]
    nj = mselt.shape[0]
    span = win_dec.shape[2]
    tile = NSA_DEC_KEYS
    n_tiles = -(-nsa_dec.shape[2] // tile)
    width = NSA_KV_HEADS * HEAD_DIM
    kern = functools.partial(_nsa_decode_kernel, dseq=dseq, past=past, n_sel=n_sel)
    grp = lambda rows, half, tiled: pl.BlockSpec(
        (NSA_KV_HEADS, 1, rows, HEAD_DIM), (lambda b, s: (half, b, s, 0)) if tiled else (lambda b, s: (half, b, 0, 0)))
    const = lambda shape: pl.BlockSpec(shape, lambda b, s: (0,) * len(shape))
    return pl.pallas_call(
        kern,
        out_shape=jax.ShapeDtypeStruct((nb, LANE, width), F32),
        grid=(nb, n_tiles),
        in_specs=[
            pl.BlockSpec((1, width, LANE), lambda b, s: (b, 0, 0)),
            grp(ncp, 0, False), grp(ncp, 1, False),
            const((nj, ncp)), const((LANE, LANE)), const((REL_BUCKETS, LANE)),
            grp(tile, 0, True), grp(tile, 1, True),
            grp(span, 0, False), grp(span, 1, False),
            pl.BlockSpec((1, 3, LANE, LANE), lambda b, s: (b, 0, 0, 0)),
        ],
        out_specs=pl.BlockSpec((1, LANE, width), lambda b, s: (b, 0, 0)),
        scratch_shapes=[
            pltpu.VMEM((nj, LANE), F32),
            pltpu.VMEM((1, LANE), F32),
            pltpu.VMEM((LANE, width + LANE), F32),
            pltpu.VMEM((LANE, width), F32),
        ],
        compiler_params=_cparams(("parallel", "arbitrary")),
        name="nsa_decode",
    )(qbd, kcv, kcv, mselt, pair, tcols, nsa_dec, nsa_dec, win_dec, win_dec, gcols)


def _largest_tile(n, candidates):
    for c in candidates:
        if n % c == 0:
            return c
    raise ValueError(f"no tile in {candidates} divides {n}")


FFN_ROW_TILES = (1024, 512, 256, 128)
ROW_TILES = (512, 256, 128)


def _token_stage_in(x, p):
    m = x.shape[0]
    tm = m if m < ROW_TILES[-1] else _largest_tile(m, ROW_TILES)
    tm_ffn = m if m < FFN_ROW_TILES[-1] else _largest_tile(m, FFN_ROW_TILES)
    x1 = _ffn(x, p["norm_ffn1"], p["wg1"], p["wu1"], p["wd1"], tm_ffn, p["tf"])
    nsa_rows, win_rows, fox_rows, hm = _inproj(x1, p["norm_mix"], p["w_main"], p["colgain"], tm)
    small = _small(x1, p["norm_mix"], p["w_small"], p["b_small"], tm)
    return x1, nsa_rows, win_rows, fox_rows, hm, small


def _token_stage_out(x1, o_nsa, o_fox, p):
    m = x1.shape[0]
    tm = m if m < ROW_TILES[-1] else _largest_tile(m, ROW_TILES[1:])
    tm_ffn = m if m < FFN_ROW_TILES[-1] else _largest_tile(m, FFN_ROW_TILES)
    x2 = _outproj(o_nsa, o_fox, p["out_norm_nsa"], p["out_norm_fox"], p["w_out"], x1, tm)
    return _ffn(x2, p["norm_ffn2"], p["wg2"], p["wu2"], p["wd2"], tm_ffn, p["tf"])


def kernel(x_prompt, x_sample, cache_nsa_kv, cache_fox_kv, cache_fox_logf, state_win_kv, page_table, rel_table, norm_ffn1, ffn1_gate, ffn1_up, ffn1_down, norm_mix, w_in, nsa_gate_bias, fox_forget_bias, q_norm_nsa, k_norm_nsa, q_norm_fox, k_norm_fox, cmp_pos_k, cmp_w1_k, cmp_w2_k, cmp_pos_v, cmp_w1_v, cmp_w2_v, out_norm_nsa, out_norm_fox, w_out, norm_ffn2, ffn2_gate, ffn2_up, ffn2_down):
    depth = w_in.shape[0]
    assert depth == 1, "single-layer trunk"
    nbp, seq, d_model = x_prompt.shape
    nbd, dseq, _ = x_sample.shape
    n_pages = page_table.shape[1]
    past = n_pages * PAGE_SIZE
    d_ff = ffn1_gate.shape[2]
    nsa_w = NSA_HEADS * HEAD_DIM
    kv6_w = 6 * NSA_KV_HEADS * HEAD_DIM
    fox_w = 3 * FOX_HEADS * HEAD_DIM
    off_gate = nsa_w + kv6_w
    off_fox = off_gate + N_GATE_COLS
    off_forget = off_fox + fox_w
    assert w_in.shape[2] == off_forget + FOX_HEADS and d_model == nsa_w + FOX_HEADS * HEAD_DIM
    assert seq % LANE == 0 and seq >= WINDOW and past % LANE == 0 and n_pages % REGROUP_PAGES == 0
    assert dseq <= 16 and state_win_kv.shape[2] == WINDOW
    assert seq % FOX_KEY_BLOCK == 0 and seq % (SLC_BLOCK_TILES * NSA_TQ) == 0 and seq >= WIN_TILES * NSA_TQ

    w0 = w_in[0]
    ones = lambda n: jnp.ones((n,), F32)
    zeros = lambda n: jnp.zeros((n,), F32)
    kn, kvw = NSA_KV_HEADS, NSA_KV_HEADS * HEAD_DIM
    qk_scale = HEAD_DIM ** -0.5
    p = {
        "tf": _largest_tile(d_ff, (512, 256, 128)),
        "norm_ffn1": norm_ffn1[0][None], "norm_mix": norm_mix[0][None], "norm_ffn2": norm_ffn2[0][None],
        "wg1": ffn1_gate[0].astype(BF16), "wu1": ffn1_up[0].astype(BF16), "wd1": ffn1_down[0].astype(BF16),
        "wg2": ffn2_gate[0].astype(BF16), "wu2": ffn2_up[0].astype(BF16), "wd2": ffn2_down[0].astype(BF16),
        "w_main": jnp.concatenate([w0[:, :off_gate], w0[:, off_fox:off_forget]], axis=1).astype(BF16),
        "w_small": jnp.concatenate([w0[:, off_gate:off_fox], w0[:, off_forget:],
                                    jnp.zeros((d_model, LANE - N_GATE_COLS - FOX_HEADS), F32)], axis=1).astype(BF16),
        "b_small": jnp.concatenate([nsa_gate_bias[0].reshape(-1), fox_forget_bias[0],
                                    zeros(LANE - N_GATE_COLS - FOX_HEADS)])[None],
        "colgain": jnp.concatenate([
            jnp.tile(q_norm_nsa[0] * qk_scale, NSA_HEADS), ones(2 * kvw), jnp.tile(k_norm_nsa[0], kn), ones(kvw),
            jnp.tile(k_norm_nsa[0], kn), ones(kvw), jnp.tile(q_norm_fox[0] * qk_scale, FOX_HEADS),
            jnp.tile(k_norm_fox[0], FOX_HEADS), ones(FOX_HEADS * HEAD_DIM)])[None],
        "out_norm_nsa": out_norm_nsa[0][None], "out_norm_fox": out_norm_fox[0][None],
        "w_out": w_out[0].astype(BF16),
    }
    half = CMP_STRIDE * HEAD_DIM

    def cmp_w1(w):
        return jnp.concatenate([w[0, :half], w[0, half:]], axis=1)

    def cmp_pe(pe):
        return jnp.concatenate([pe[0].reshape(CMP_LEN // CMP_STRIDE, half), jnp.zeros((PE_ROWS - CMP_LEN // CMP_STRIDE, half), F32)], axis=0)

    w1cat = jnp.stack([cmp_w1(cmp_w1_k), cmp_w1(cmp_w1_v)]).astype(BF16)
    w2cat = jnp.stack([cmp_w2_k[0], cmp_w2_v[0]]).astype(BF16)
    pecat = jnp.stack([cmp_pe(cmp_pos_k), cmp_pe(cmp_pos_v)]).astype(BF16)
    k_norm_row = k_norm_nsa[0][None]

    mp = nbp * seq
    x1, nsa_rows, win_rows, fox_rows, hm, small = _token_stage_in(x_prompt.reshape(mp, d_model), p)
    hm4 = hm.reshape(N_HEAD_COLS, nbp, seq, HEAD_DIM)

    logf = small[:, N_GATE_COLS:N_GATE_COLS + FOX_HEADS]
    csum = _cumsum(logf.reshape(nbp, seq, FOX_HEADS).transpose(0, 2, 1))
    o_fox = _fox_prompt(hm, hm4, csum[:, :, None, :], nbp, seq)

    n_chunk = seq // CMP_STRIDE
    n_cmp = (seq - CMP_LEN) // CMP_STRIDE + 1
    n_sel = -(-seq // SEL_BLOCK)
    xc = _chunkify(nsa_rows.reshape(nbp, seq * N_CACHE_COLS, HEAD_DIM), N_CACHE_COLS, N_CMP_COLS, seq)
    kcv = _compress(xc, 0, w1cat, w2cat, pecat, k_norm_row, n_chunk)
    mselt = _cmp_to_sel(n_cmp, n_sel, n_chunk, -(-n_sel // 8) * 8).T
    nq = seq // NSA_TQ
    gates_t = small[:, :N_GATE_COLS].reshape(nbp, nq, NSA_TQ, NSA_KV_HEADS, NSA_GROUP, 3)
    gates_t = gates_t.transpose(0, 3, 1, 5, 4, 2).reshape(nbp, NSA_KV_HEADS, nq, 3, NSA_COLS)
    gates_t = jnp.pad(gates_t, ((0, 0), (0, 0), (0, 0), (0, 8 - 3), (0, 0)))
    o_nsa = _nsa_prompt(rel_table, hm, hm4, kcv, mselt, gates_t, nbp, seq, n_sel)
    y_p = _token_stage_out(x1, o_nsa, o_fox, p)

    ms = nbd * dseq
    lk = past + NSA_DEC_KEYS
    assert n_pages % FOX_DEC_PAGES == 0 and past % NSA_DEC_KEYS == 0 and FOX_HEADS * dseq <= LANE
    xs1, nsa_rows_s, win_rows_s, fox_rows_s, hm_s, small_s = _token_stage_in(x_sample.reshape(ms, d_model), p)
    xc_d, nsa_dec, lf_dec = _regroup(
        page_table,
        cache_nsa_kv.reshape(cache_nsa_kv.shape[1], PAGE_SIZE * N_CACHE_COLS, HEAD_DIM),
        cache_fox_logf[0], lk)
    hm_s4 = hm_s.reshape(N_HEAD_COLS, nbd, dseq, HEAD_DIM)
    nsa_dec = lax.dynamic_update_slice(nsa_dec, hm_s4[HM_K_SLC:HM_K_WIN], (0, 0, past, 0))
    logf_s = small_s[:, N_GATE_COLS:N_GATE_COLS + FOX_HEADS].reshape(nbd, dseq, FOX_HEADS)
    lf_dec = lax.dynamic_update_slice(lf_dec, logf_s, (0, past, 0))

    csum_d = _cumsum(lf_dec.transpose(0, 2, 1))
    n_cols = FOX_HEADS * dseq
    lane_pad = lambda a: jnp.pad(a, [(0, 0)] * (a.ndim - 1) + [(0, LANE - n_cols)])
    head_eye = jnp.eye(FOX_HEADS, dtype=BF16)
    qbd = jnp.einsum("hbtd,hg->bhdgt", hm_s4[HM_Q_FOX:HM_Q_FOX + FOX_HEADS], head_eye)
    qbd = lane_pad(qbd.reshape(nbd, FOX_HEADS * HEAD_DIM, n_cols))
    ccols = lane_pad(jnp.repeat(csum_d.transpose(0, 2, 1), dseq, axis=2))
    cq = lane_pad(csum_d[:, :, past:past + dseq].reshape(nbd, 1, n_cols))

    def new_rows(head0):
        rows = hm_s4[head0:head0 + FOX_HEADS].transpose(1, 2, 0, 3).reshape(nbd, dseq, FOX_HEADS * HEAD_DIM)
        return jnp.pad(rows, ((0, 0), (0, LANE - dseq), (0, 0)))

    o_full = _fox_decode(page_table, cache_fox_kv.reshape(cache_fox_kv.shape[1], PAGE_SIZE * N_CACHE_COLS, HEAD_DIM),
                         qbd, ccols, cq, new_rows(HM_K_FOX), new_rows(HM_V_FOX), dseq)
    o_fox_s = jnp.concatenate([o_full[:, h * dseq:(h + 1) * dseq, h * HEAD_DIM:(h + 1) * HEAD_DIM]
                               for h in range(FOX_HEADS)], axis=2).reshape(ms, FOX_HEADS * HEAD_DIM)

    n_chunk_d = past // CMP_STRIDE
    n_cmp_d = (past + dseq - CMP_LEN) // CMP_STRIDE + 1
    n_sel_d = -(-(past + dseq) // SEL_BLOCK)
    assert n_cmp_d + CMP_LEN // CMP_STRIDE - 1 <= n_chunk_d, "compressed blocks must lie in the cached rows"
    kcv_d = _compress(xc_d, 0, w1cat, w2cat, pecat, k_norm_row, n_chunk_d)
    mselt_d = _cmp_to_sel(n_cmp_d, n_sel_d, n_chunk_d, lk // SEL_BLOCK).T
    win_old = state_win_kv[0].transpose(2, 3, 0, 1, 4).reshape(2 * NSA_KV_HEADS, nbd, WINDOW, HEAD_DIM).astype(BF16)
    win_dec = jnp.concatenate([win_old, hm_s4[HM_K_WIN:HM_V_WIN + NSA_KV_HEADS],
                               jnp.zeros((2 * NSA_KV_HEADS, nbd, LANE - dseq, HEAD_DIM), BF16)], axis=2)
    grp_eye = jnp.eye(NSA_KV_HEADS, dtype=BF16)
    q_grp = hm_s4[HM_Q_NSA:HM_Q_NSA + NSA_HEADS].reshape(NSA_KV_HEADS, NSA_GROUP, nbd, dseq, HEAD_DIM)
    qbd_n = jnp.einsum("gjbtd,gk->bgdkjt", q_grp, grp_eye)
    qbd_n = lane_pad(qbd_n.reshape(nbd, NSA_KV_HEADS * HEAD_DIM, n_cols))
    tcols = lane_pad(jnp.repeat(rel_table, dseq, axis=1))
    col_id = np.arange(n_cols)
    same = ((col_id[:, None] // (NSA_GROUP * dseq) == col_id[None, :] // (NSA_GROUP * dseq))
            & (col_id[:, None] % dseq == col_id[None, :] % dseq))
    pair = jnp.asarray(np.pad(same.astype(np.float32), ((0, LANE - n_cols), (0, LANE - n_cols))))
    gcols = small_s[:, :N_GATE_COLS].reshape(nbd, dseq, NSA_HEADS, 3).transpose(0, 3, 2, 1).reshape(nbd, 3, n_cols)
    gcols = jnp.broadcast_to(lane_pad(gcols)[..., None], (nbd, 3, LANE, LANE))
    o_full_n = _nsa_decode(qbd_n, kcv_d, mselt_d, pair, tcols, nsa_dec, win_dec, gcols,
                           dseq=dseq, past=past, n_sel=n_sel_d)
    o_nsa_s = jnp.concatenate(
        [o_full_n[:, h * dseq:(h + 1) * dseq, (h // NSA_GROUP) * HEAD_DIM:(h // NSA_GROUP + 1) * HEAD_DIM]
         for h in range(NSA_HEADS)], axis=2).reshape(ms, NSA_HEADS * HEAD_DIM)
    y_s = _token_stage_out(xs1, o_nsa_s, o_fox_s, p)

    kvh = (NSA_KV_HEADS, HEAD_DIM)
    win_keep = min(WINDOW, seq)
    win_p = win_rows.reshape(nbp, seq, 2, *kvh)[:, seq - win_keep:]
    win_s = jnp.concatenate([state_win_kv[0], win_rows_s.reshape(nbd, dseq, 2, *kvh)], axis=1)[:, dseq:]
    return (
        y_p.reshape(nbp, seq, d_model),
        y_s.reshape(nbd, dseq, d_model),
        nsa_rows.reshape(1, nbp, seq, 4, *kvh),
        fox_rows.reshape(1, nbp, seq, 2, FOX_HEADS, HEAD_DIM),
        logf.reshape(1, nbp, seq, FOX_HEADS),
        win_p[None],
        nsa_rows_s.reshape(1, nbd, dseq, 4, *kvh),
        fox_rows_s.reshape(1, nbd, dseq, 2, FOX_HEADS, HEAD_DIM),
        logf_s[None],
        win_s[None],
    )
```

```python
import functools
import math

import numpy as np
import jax
import jax.numpy as jnp
from jax import lax
from jax.experimental import pallas as pl
from jax.experimental.pallas import tpu as pltpu

HEAD_DIM = 128
NSA_HEADS = 8
FOX_HEADS = 8
NSA_KV_HEADS = 4
NSA_GROUP = NSA_HEADS // NSA_KV_HEADS
CMP_LEN = 32
CMP_STRIDE = 16
CMP_HIDDEN = 512
SEL_BLOCK = 64
SEL_TOPK = 16
WINDOW = 512
REL_BUCKETS = 32
REL_MAX_DIST = 128
RMS_EPS = 1e-6
PAGE_SIZE = 128

LANE = 128
NEG = -1e30
FORCED_SCORE = 1e30
VMEM_LIMIT = 56 * 1024 * 1024

BF16 = jnp.bfloat16
F32 = jnp.float32
NT_DIMS = (((1,), (1,)), ((), ()))


def _bucket_thresholds():
    n = np.arange(0, 4 * REL_MAX_DIST)
    max_exact = REL_BUCKETS // 2
    nf = np.maximum(n, 1).astype(np.float32)
    large = max_exact + (np.log(nf / max_exact) / math.log(REL_MAX_DIST / max_exact)
                         * (REL_BUCKETS - max_exact)).astype(np.int32)
    bucket = np.where(n < max_exact, n, np.minimum(large, REL_BUCKETS - 1))
    return [int(np.min(n[bucket >= k])) for k in range(1, REL_BUCKETS)]


BUCKET_THR = _bucket_thresholds()
FAR_DIST = BUCKET_THR[-1]


def _cparams(sem):
    return pltpu.CompilerParams(dimension_semantics=sem, vmem_limit_bytes=VMEM_LIMIT)


def _rms_rows(x, gain):
    ms = jnp.mean(x * x, axis=-1, keepdims=True)
    return x * lax.rsqrt(ms + RMS_EPS) * gain


def _ffn_kernel(x_ref, g_ref, wg_ref, wu_ref, wd_ref, o_ref, xn_ref):
    @pl.when(pl.program_id(1) == 0)
    def _():
        x = x_ref[...]
        xn_ref[...] = _rms_rows(x, g_ref[...]).astype(BF16)
        o_ref[...] = x

    xn = xn_ref[...]
    a = jnp.dot(xn, wg_ref[...], preferred_element_type=F32)
    u = jnp.dot(xn, wu_ref[...], preferred_element_type=F32)
    h = (a / (1.0 + jnp.exp(-a))) * u * 0.5
    o_ref[...] += jnp.dot(h.astype(BF16), wd_ref[...], preferred_element_type=F32)


def _ffn(x, gain, wg, wu, wd, tm, tf):
    m, d = x.shape
    f = wg.shape[1]
    return pl.pallas_call(
        _ffn_kernel,
        out_shape=jax.ShapeDtypeStruct((m, d), F32),
        grid=(m // tm, f // tf),
        in_specs=[
            pl.BlockSpec((tm, d), lambda i, j: (i, 0)),
            pl.BlockSpec((1, d), lambda i, j: (0, 0)),
            pl.BlockSpec((d, tf), lambda i, j: (0, j)),
            pl.BlockSpec((d, tf), lambda i, j: (0, j)),
            pl.BlockSpec((tf, d), lambda i, j: (j, 0)),
        ],
        out_specs=pl.BlockSpec((tm, d), lambda i, j: (i, 0)),
        scratch_shapes=[pltpu.VMEM((tm, d), BF16)],
        compiler_params=_cparams(("parallel", "arbitrary")),
        name="ffn",
    )(x, gain, wg, wu, wd)


IN_TN = 4 * HEAD_DIM
J_NSA = (2, 6)
J_WIN = (6, 8)
J_FOX = (10, 14)
N_HEAD_COLS = 56
HM_Q_NSA, HM_K_CMP, HM_K_SLC, HM_V_SLC, HM_K_WIN, HM_V_WIN = 0, 8, 16, 20, 24, 28
HM_Q_FOX, HM_K_FOX, HM_V_FOX = 32, 40, 48


IN_NORMED_TILES = (0, 1, 4, 6, 8, 9, 10, 11)


def _inproj_kernel(x_ref, g_ref, w_ref, cg_ref, ones_ref, nsa_ref, win_ref, fox_ref, hm_ref, xn_ref):
    j = pl.program_id(1)

    @pl.when(j == 0)
    def _():
        xn_ref[...] = _rms_rows(x_ref[...], g_ref[...]).astype(BF16)

    res = jnp.dot(xn_ref[...], w_ref[...], preferred_element_type=F32)
    tm = res.shape[0]
    heads = IN_TN // HEAD_DIM

    def emit(vals):
        for hh in range(heads):
            hm_ref[hh] = vals[:, hh * HEAD_DIM:(hh + 1) * HEAD_DIM].astype(BF16)
        for ref, (jlo, jhi) in ((nsa_ref, J_NSA), (win_ref, J_WIN), (fox_ref, J_FOX)):
            n_cols = (jhi - jlo) * heads

            @pl.when((j >= jlo) & (j < jhi))
            def _(ref=ref, jlo=jlo, n_cols=n_cols):
                for hh in range(heads):
                    ref[pl.ds((j - jlo) * heads + hh, tm, stride=n_cols), :] = vals[:, hh * HEAD_DIM:(hh + 1) * HEAD_DIM]

    is_normed = functools.reduce(jnp.logical_or, [j == t for t in IN_NORMED_TILES])

    @pl.when(is_normed)
    def _():
        sumsq = jnp.dot((res * res).astype(BF16), ones_ref[...], preferred_element_type=F32)
        emit(res * lax.rsqrt(sumsq * (1.0 / HEAD_DIM) + RMS_EPS) * cg_ref[...])

    @pl.when(jnp.logical_not(is_normed))
    def _():
        emit(res)


def _inproj(x, gain, w_main, colgain, tm):
    m, d = x.shape
    ncol = w_main.shape[1]
    nj = ncol // IN_TN

    heads = IN_TN // HEAD_DIM
    n_nsa, n_win, n_fox = [(hi - lo) * heads for lo, hi in (J_NSA, J_WIN, J_FOX)]
    head_ones = jnp.asarray(np.kron(np.eye(heads, dtype=np.float32), np.ones((HEAD_DIM, HEAD_DIM), np.float32)), BF16)
    return pl.pallas_call(
        _inproj_kernel,
        out_shape=(
            jax.ShapeDtypeStruct((m * n_nsa, HEAD_DIM), F32),
            jax.ShapeDtypeStruct((m * n_win, HEAD_DIM), F32),
            jax.ShapeDtypeStruct((m * n_fox, HEAD_DIM), F32),
            jax.ShapeDtypeStruct((N_HEAD_COLS, m, HEAD_DIM), BF16),
        ),
        grid=(m // tm, nj),
        in_specs=[
            pl.BlockSpec((tm, d), lambda i, j: (i, 0)),
            pl.BlockSpec((1, d), lambda i, j: (0, 0)),
            pl.BlockSpec((d, IN_TN), lambda i, j: (0, j)),
            pl.BlockSpec((1, IN_TN), lambda i, j: (0, j)),
            pl.BlockSpec((IN_TN, IN_TN), lambda i, j: (0, 0)),
        ],
        out_specs=(
            pl.BlockSpec((tm * n_nsa, HEAD_DIM), lambda i, j: (i, 0)),
            pl.BlockSpec((tm * n_win, HEAD_DIM), lambda i, j: (i, 0)),
            pl.BlockSpec((tm * n_fox, HEAD_DIM), lambda i, j: (i, 0)),
            pl.BlockSpec((heads, tm, HEAD_DIM), lambda i, j: (j, i, 0)),
        ),
        scratch_shapes=[pltpu.VMEM((tm, d), BF16)],
        compiler_params=_cparams(("parallel", "arbitrary")),
        name="inproj",
    )(x, gain, w_main, colgain, head_ones)


N_GATE_COLS = 3 * NSA_HEADS


def _small_kernel(x_ref, g_ref, w_ref, b_ref, o_ref):
    xn = _rms_rows(x_ref[...], g_ref[...]).astype(BF16)
    z = jnp.dot(xn, w_ref[...], preferred_element_type=F32) + b_ref[...]
    lane = lax.broadcasted_iota(jnp.int32, z.shape, 1)
    sig = 1.0 / (1.0 + jnp.exp(-z))
    logsig = jnp.minimum(z, 0.0) - jnp.log(1.0 + jnp.exp(-jnp.abs(z)))
    o_ref[...] = jnp.where(lane < N_GATE_COLS, sig,
                           jnp.where(lane < N_GATE_COLS + FOX_HEADS, logsig, 0.0))


def _small(x, gain, w_small, b_small, tm):
    m, d = x.shape
    return pl.pallas_call(
        _small_kernel,
        out_shape=jax.ShapeDtypeStruct((m, LANE), F32),
        grid=(m // tm,),
        in_specs=[
            pl.BlockSpec((tm, d), lambda i: (i, 0)),
            pl.BlockSpec((1, d), lambda i: (0, 0)),
            pl.BlockSpec((d, LANE), lambda i: (0, 0)),
            pl.BlockSpec((1, LANE), lambda i: (0, 0)),
        ],
        out_specs=pl.BlockSpec((tm, LANE), lambda i: (i, 0)),
        compiler_params=_cparams(("parallel",)),
        name="gates",
    )(x, gain, w_small, b_small)


def _outproj_kernel(on_ref, of_ref, gn_ref, gf_ref, w_ref, x_ref, y_ref):
    a = _rms_rows(on_ref[...], gn_ref[...]).astype(BF16)
    b = _rms_rows(of_ref[...], gf_ref[...]).astype(BF16)
    half = a.shape[1]
    y = jnp.dot(a, w_ref[:half, :], preferred_element_type=F32)
    y = y + jnp.dot(b, w_ref[half:, :], preferred_element_type=F32)
    y_ref[...] = x_ref[...] + y


def _outproj(o_nsa, o_fox, g_nsa, g_fox, w_out, x, tm):
    m, d = x.shape
    wn = o_nsa.shape[1]
    wf = o_fox.shape[1]
    return pl.pallas_call(
        _outproj_kernel,
        out_shape=jax.ShapeDtypeStruct((m, d), F32),
        grid=(m // tm,),
        in_specs=[
            pl.BlockSpec((tm, wn), lambda i: (i, 0)),
            pl.BlockSpec((tm, wf), lambda i: (i, 0)),
            pl.BlockSpec((1, wn), lambda i: (0, 0)),
            pl.BlockSpec((1, wf), lambda i: (0, 0)),
            pl.BlockSpec((wn + wf, d), lambda i: (0, 0)),
            pl.BlockSpec((tm, d), lambda i: (i, 0)),
        ],
        out_specs=pl.BlockSpec((tm, d), lambda i: (i, 0)),
        compiler_params=_cparams(("parallel",)),
        name="outproj",
    )(o_nsa, o_fox, g_nsa, g_fox, w_out, x)


CUMSUM_CHUNK = 512


def _cumsum_kernel(x_ref, o_ref):
    rows, length = x_ref.shape[1], x_ref.shape[2]
    n_big = length // CUMSUM_CHUNK

    def sweep(width, first, count, offset, carry):
        r = lax.broadcasted_iota(jnp.int32, (width, width), 0)
        c = lax.broadcasted_iota(jnp.int32, (width, width), 1)
        upper = (r <= c).astype(F32)

        def body(k, carry):
            st = pl.multiple_of(offset + k * width, LANE)
            x = x_ref[0, :, pl.ds(st, width)]
            cs = jnp.dot(x, upper, preferred_element_type=F32, precision=lax.Precision.HIGHEST) + carry
            o_ref[0, :, pl.ds(st, width)] = cs
            return cs[:, width - 1:width]

        return lax.fori_loop(first, count, body, carry)

    carry = sweep(CUMSUM_CHUNK, 0, n_big, 0, jnp.zeros((rows, 1), F32))
    sweep(LANE, 0, (length - n_big * CUMSUM_CHUNK) // LANE, n_big * CUMSUM_CHUNK, carry)


def _cumsum(x):
    b, h, length = x.shape
    return pl.pallas_call(
        _cumsum_kernel,
        out_shape=jax.ShapeDtypeStruct(x.shape, F32),
        grid=(b,),
        in_specs=[pl.BlockSpec((1, h, length), lambda i: (i, 0, 0))],
        out_specs=pl.BlockSpec((1, h, length), lambda i: (i, 0, 0)),
        compiler_params=_cparams(("parallel",)),
        name="logf_cumsum",
    )(x)


PE_ROWS = 16


def _compress_kernel(x_ref, w1_ref, w2_ref, pe_ref, kn_ref, o_ref):
    kind = pl.program_id(0) // NSA_KV_HEADS
    n = x_ref.shape[2]
    w1 = w1_ref[0]
    h = jnp.dot(x_ref[0, 0], w1, preferred_element_type=F32)
    pw = jnp.dot(pe_ref[0], w1, preferred_element_type=F32)
    const = pw[0:1, :CMP_HIDDEN] + pw[1:2, CMP_HIDDEN:]
    hid = h[:, :CMP_HIDDEN] + pltpu.roll(h[:, CMP_HIDDEN:], n - 1, 0) + const
    act = hid / (1.0 + jnp.exp(-hid))
    out = jnp.dot(act.astype(BF16), w2_ref[0], preferred_element_type=F32)
    normed = _rms_rows(out, kn_ref[...])
    o_ref[0, 0] = jnp.where(kind == 0, normed, out).astype(BF16)


def _compress(xc, c_off, w1cat, w2, pe, k_norm, n_rows):
    nb = xc.shape[1]
    return pl.pallas_call(
        _compress_kernel,
        out_shape=jax.ShapeDtypeStruct((2 * NSA_KV_HEADS, nb, n_rows, HEAD_DIM), BF16),
        grid=(2 * NSA_KV_HEADS, nb),
        in_specs=[
            pl.BlockSpec((1, 1, n_rows, CMP_STRIDE * HEAD_DIM), lambda c, b: (c_off + c, b, 0, 0)),
            pl.BlockSpec((1, CMP_STRIDE * HEAD_DIM, 2 * CMP_HIDDEN), lambda c, b: (c // NSA_KV_HEADS, 0, 0)),
            pl.BlockSpec((1, CMP_HIDDEN, HEAD_DIM), lambda c, b: (c // NSA_KV_HEADS, 0, 0)),
            pl.BlockSpec((1, PE_ROWS, CMP_STRIDE * HEAD_DIM), lambda c, b: (c // NSA_KV_HEADS, 0, 0)),
            pl.BlockSpec((1, HEAD_DIM), lambda c, b: (0, 0)),
        ],
        out_specs=pl.BlockSpec((1, 1, n_rows, HEAD_DIM), lambda c, b: (c, b, 0, 0)),
        compiler_params=_cparams(("parallel", "parallel")),
        name="compress",
    )(xc, w1cat, w2, pe, k_norm)


def _online_update(state, s, v):
    m, l, acc = state
    m_new = jnp.maximum(m, jnp.max(s, axis=-1, keepdims=True))
    alpha = jnp.exp(m - m_new)
    p = jnp.exp(s - m_new)
    l = alpha * l + jnp.sum(p, axis=-1, keepdims=True)
    acc = alpha * acc + jnp.dot(p.astype(BF16), v, preferred_element_type=F32)
    return m_new, l, acc


def _init_state(rows):
    return (jnp.full((rows, 1), NEG, F32), jnp.zeros((rows, 1), F32), jnp.zeros((rows, HEAD_DIM), F32))


def _rel_bias(dist, tbl_ref, head):
    out = jnp.full(dist.shape, tbl_ref[0, head], F32)
    for k, thr in enumerate(BUCKET_THR, start=1):
        out = jnp.where(dist >= thr, tbl_ref[k, head], out)
    return out


def _fox_kernel(q_ref, k_ref, v_ref, crow_ref, ccol_ref, o_ref, *, tq, tk, td, qoff):
    q0 = qoff + pl.program_id(2) * tq
    q = q_ref[0]
    cq = ccol_ref[0, 0]

    def logits(start, size):
        k = k_ref[0, 0, pl.ds(start, size), :]
        s = lax.dot_general(q, k, NT_DIMS, preferred_element_type=F32)
        return s + (cq - crow_ref[0, 0, :, pl.ds(start, size)])

    def body(kt, state):
        start = pl.multiple_of(kt * tk, tk)
        return _online_update(state, logits(start, tk), v_ref[0, 0, pl.ds(start, tk), :])

    state = lax.fori_loop(0, q0 // tk, body, _init_state(tq))
    start = pl.multiple_of(q0, LANE)
    s = logits(start, td)
    row = lax.broadcasted_iota(jnp.int32, (tq, td), 0)
    col = lax.broadcasted_iota(jnp.int32, (tq, td), 1)
    s = jnp.where(col <= row, s, NEG)
    _, l, acc = _online_update(state, s, v_ref[0, 0, pl.ds(start, td), :])
    o_ref[...] = acc / l


def _fox_attention(qarr, kv, crow, ccol, *, q_head0, k_head0, v_head0, nb, nq, tq, tk, td, qoff):
    lk = kv.shape[2]
    kern = functools.partial(_fox_kernel, tq=tq, tk=tk, td=td, qoff=qoff)
    return pl.pallas_call(
        kern,
        out_shape=jax.ShapeDtypeStruct((nb * nq * tq, FOX_HEADS * HEAD_DIM), F32),
        grid=(nb, FOX_HEADS, nq),
        in_specs=[
            pl.BlockSpec((1, tq, HEAD_DIM), lambda b, h, t: (q_head0 + h, b * nq + t, 0)),
            pl.BlockSpec((1, 1, lk, HEAD_DIM), lambda b, h, t: (k_head0 + h, b, 0, 0)),
            pl.BlockSpec((1, 1, lk, HEAD_DIM), lambda b, h, t: (v_head0 + h, b, 0, 0)),
            pl.BlockSpec((1, 1, 1, lk), lambda b, h, t: (b, h, 0, 0)),
            pl.BlockSpec((1, 1, tq, 1), lambda b, h, t: (b, h, t, 0)),
        ],
        out_specs=pl.BlockSpec((tq, HEAD_DIM), lambda b, h, t: (b * nq + t, h)),
        compiler_params=_cparams(("parallel", "parallel", "arbitrary")),
        name="fox_attention",
    )(qarr, kv, kv, crow, ccol)


def _nsa_kernel(tbl_ref, q_ref, kc_ref, vc_ref, msel_ref, ks_ref, vs_ref, kw_ref, vw_ref, gate_ref, o_ref,
                *, tq, tk, td, qoff, wpos0, n_sel):
    g = pl.program_id(1)
    q0 = qoff + pl.program_id(2) * tq
    rows = NSA_GROUP * tq
    q2 = q_ref[...].reshape(rows, HEAD_DIM)
    heads = [NSA_GROUP * g + hh for hh in range(NSA_GROUP)]
    pos_col = q0 + lax.broadcasted_iota(jnp.int32, (tq, 1), 0)

    def per_head(fn):
        return jnp.concatenate([fn(hh) for hh in range(NSA_GROUP)], axis=0)

    ncp = kc_ref.shape[2]
    blk_end = lax.broadcasted_iota(jnp.int32, (tq, ncp), 1) * CMP_STRIDE + (CMP_LEN - 1)
    d_cmp = pos_col - blk_end
    ok_cmp = d_cmp >= 0
    n_cmp = jnp.maximum(d_cmp, 0)
    s = lax.dot_general(q2, kc_ref[0, 0], NT_DIMS, preferred_element_type=F32)
    p_heads = []
    for hh in range(NSA_GROUP):
        sh = s[hh * tq:(hh + 1) * tq] + _rel_bias(n_cmp, tbl_ref, heads[hh])
        sh = jnp.where(ok_cmp, sh, NEG)
        e = jnp.where(ok_cmp, jnp.exp(sh - jnp.max(sh, axis=-1, keepdims=True)), 0.0)
        p_heads.append(e / jnp.maximum(jnp.sum(e, axis=-1, keepdims=True), 1e-30))
    o_cmp = jnp.dot(jnp.concatenate(p_heads, axis=0).astype(BF16), vc_ref[0, 0], preferred_element_type=F32)
    imp = jnp.dot(sum(p_heads), msel_ref[...], preferred_element_type=F32,
                  precision=lax.Precision.HIGHEST)

    nj = imp.shape[1]
    jj = lax.broadcasted_iota(jnp.int32, (tq, nj), 1)
    blk_q = pos_col // SEL_BLOCK
    forced = (jj == 0) | (jj == blk_q) | (jj == blk_q - 1)
    score = jnp.where(forced, FORCED_SCORE, jnp.where(jj <= blk_q, imp, -1.0))
    score = jnp.where(jj < n_sel, score, -2.0)
    rank = jnp.zeros((tq, nj), F32)
    for other in range(n_sel):
        col = jnp.broadcast_to(score[:, other:other + 1], score.shape)
        tie = jnp.where(jj > other, 1.0, 0.0)
        rank = rank + jnp.where(col > score, 1.0, jnp.where(col == score, tie, 0.0))
    sel = jnp.where((rank < SEL_TOPK) & (jj < n_sel), 1.0, 0.0).astype(BF16)

    def bias_and_mask(start, size, kpos0, near, window):
        def near_fn():
            kpos = kpos0 + start + lax.broadcasted_iota(jnp.int32, (tq, size), 1)
            d = pos_col - kpos
            ok = (d >= 0) & (d < WINDOW) if window else d >= 0
            n = jnp.maximum(d, 0)
            return per_head(lambda hh: jnp.where(ok, _rel_bias(n, tbl_ref, heads[hh]), NEG))

        def far_fn():
            return per_head(lambda hh: jnp.full((tq, size), tbl_ref[REL_BUCKETS - 1, heads[hh]], F32))

        if near is True:
            return near_fn()
        return lax.cond(near, near_fn, far_fn)

    def slc_tile(state, start, size, near):
        blk_of_key = (start + lax.broadcasted_iota(jnp.int32, (nj, size), 1)) // SEL_BLOCK
        expand = jnp.where(blk_of_key == lax.broadcasted_iota(jnp.int32, (nj, size), 0), 1.0, 0.0).astype(BF16)
        picked = jnp.dot(sel, expand, preferred_element_type=F32)
        drop = (picked - 1.0) * (-NEG)
        sc = lax.dot_general(q2, ks_ref[0, 0, pl.ds(start, size), :], NT_DIMS, preferred_element_type=F32)
        sc = sc + bias_and_mask(start, size, 0, near, False) + jnp.concatenate([drop] * NSA_GROUP, axis=0)
        return _online_update(state, sc, vs_ref[0, 0, pl.ds(start, size), :])

    def slc_body(kt, state):
        start = pl.multiple_of(kt * tk, tk)
        return slc_tile(state, start, tk, q0 - (start + tk - 1) < FAR_DIST)

    state = lax.fori_loop(0, q0 // tk, slc_body, _init_state(rows))
    _, l, acc = slc_tile(state, pl.multiple_of(q0, LANE), td, True)
    o_slc = acc / l

    def win_body(wt, state):
        start = pl.multiple_of(wt * LANE, LANE)
        near = q0 - (wpos0 + start + LANE - 1) < FAR_DIST
        sc = lax.dot_general(q2, kw_ref[0, 0, pl.ds(start, LANE), :], NT_DIMS, preferred_element_type=F32)
        kpos = wpos0 + start + lax.broadcasted_iota(jnp.int32, (tq, LANE), 1)
        d = pos_col - kpos
        edge = jnp.where((d >= 0) & (d < WINDOW), 0.0, NEG)
        sc = sc + bias_and_mask(start, LANE, wpos0, near, True) + jnp.concatenate([edge] * NSA_GROUP, axis=0)
        return _online_update(state, sc, vw_ref[0, 0, pl.ds(start, LANE), :])

    w_lo = jnp.maximum(q0 - wpos0 - WINDOW, 0) // LANE
    w_hi = (q0 + tq - 1 - wpos0) // LANE
    _, l, acc = lax.fori_loop(w_lo, w_hi + 1, win_body, _init_state(rows))
    o_win = acc / l

    gates = gate_ref[...]
    lane = lax.broadcasted_iota(jnp.int32, gates.shape, 1)

    def gate(hh, branch):
        return jnp.sum(jnp.where(lane == heads[hh] * 3 + branch, gates, 0.0), axis=-1, keepdims=True)

    for hh in range(NSA_GROUP):
        sl = slice(hh * tq, (hh + 1) * tq)
        o_ref[:, hh * HEAD_DIM:(hh + 1) * HEAD_DIM] = (
            gate(hh, 0) * o_cmp[sl] + gate(hh, 1) * o_slc[sl] + gate(hh, 2) * o_win[sl])


def _nsa_attention(rel_table, qarr, kcv, msel, kv, kwin, gates, *, ks_head0, vs_head0, kw_head0, vw_head0,
                   nb, nq, tq, tk, td, qoff, wpos0, n_sel):
    lk = kv.shape[2]
    lw = kwin.shape[2]
    ncp = kcv.shape[2]
    nj = msel.shape[1]
    kern = functools.partial(_nsa_kernel, tq=tq, tk=tk, td=td, qoff=qoff, wpos0=wpos0, n_sel=n_sel)
    g_w = NSA_GROUP * HEAD_DIM
    return pl.pallas_call(
        kern,
        out_shape=jax.ShapeDtypeStruct((nb * nq * tq, NSA_HEADS * HEAD_DIM), F32),
        grid=(nb, NSA_KV_HEADS, nq),
        in_specs=[
            pl.BlockSpec(memory_space=pltpu.SMEM),
            pl.BlockSpec((NSA_GROUP, tq, HEAD_DIM), lambda b, g, t: (g, b * nq + t, 0)),
            pl.BlockSpec((1, 1, ncp, HEAD_DIM), lambda b, g, t: (g, b, 0, 0)),
            pl.BlockSpec((1, 1, ncp, HEAD_DIM), lambda b, g, t: (NSA_KV_HEADS + g, b, 0, 0)),
            pl.BlockSpec((ncp, nj), lambda b, g, t: (0, 0)),
            pl.BlockSpec((1, 1, lk, HEAD_DIM), lambda b, g, t: (ks_head0 + g, b, 0, 0)),
            pl.BlockSpec((1, 1, lk, HEAD_DIM), lambda b, g, t: (vs_head0 + g, b, 0, 0)),
            pl.BlockSpec((1, 1, lw, HEAD_DIM), lambda b, g, t: (kw_head0 + g, b, 0, 0)),
            pl.BlockSpec((1, 1, lw, HEAD_DIM), lambda b, g, t: (vw_head0 + g, b, 0, 0)),
            pl.BlockSpec((tq, LANE), lambda b, g, t: (b * nq + t, 0)),
        ],
        out_specs=pl.BlockSpec((tq, g_w), lambda b, g, t: (b * nq + t, g)),
        compiler_params=_cparams(("parallel", "parallel", "arbitrary")),
        name="nsa_attention",
    )(rel_table, qarr, kcv, kcv, msel, kv, kv, kwin, kwin, gates)


def _cmp_to_sel(n_cmp, n_sel, rows, cols):
    c0 = np.arange(n_cmp)[:, None] * CMP_STRIDE
    s0 = np.arange(n_sel)[None, :] * SEL_BLOCK
    inter = np.clip(np.minimum(c0 + CMP_LEN, s0 + SEL_BLOCK) - np.maximum(c0, s0), 0, None)
    m = np.zeros((rows, cols), np.float32)
    m[:n_cmp, :n_sel] = inter / CMP_LEN
    return jnp.asarray(m)


def _transpose_tiles(x):
    n = x.shape[0] // LANE
    xf = x.astype(F32)
    return jnp.concatenate([xf[i * LANE:(i + 1) * LANE].T for i in range(n)], axis=1)


def _untranspose_tiles(xt):
    n = xt.shape[1] // LANE
    return jnp.concatenate([xt[:, i * LANE:(i + 1) * LANE].T for i in range(n)], axis=0)


AUX_ROWS = 16


def _fill_transposed(dst_ref, src_ref):
    def body(k, carry):
        st = pl.multiple_of(k * LANE, LANE)
        dst_ref[:HEAD_DIM, pl.ds(st, LANE)] = src_ref[0, 0, pl.ds(st, LANE), :].astype(F32).T.astype(BF16)
        return carry

    lax.fori_loop(0, src_ref.shape[2] // LANE, body, 0)


def _online_update_t(state, s_t, v_t):
    m, l, acc = state
    m_new = jnp.maximum(m, jnp.max(s_t, axis=0, keepdims=True))
    alpha = jnp.exp(m - m_new)
    p = jnp.exp(s_t - m_new)
    l = alpha * l + jnp.sum(p, axis=0, keepdims=True)
    acc = alpha * acc + jnp.dot(v_t, p.astype(BF16), preferred_element_type=F32)
    return m_new, l, acc


def _init_state_t(cols):
    return (jnp.full((1, cols), NEG, F32), jnp.zeros((1, cols), F32), jnp.zeros((HEAD_DIM, cols), F32))


FOX_TQ = 256
FOX_KEY_BLOCK = 1024


def _split3(x):
    hi = x.astype(BF16)
    r1 = x - hi.astype(F32)
    mid = r1.astype(BF16)
    lo = (r1 - mid.astype(F32)).astype(BF16)
    return hi, mid, lo


def _lane_select3(parts, shape):
    lane = lax.broadcasted_iota(jnp.int32, shape, 1)
    hi, mid, lo = [part.astype(F32) for part in parts]
    return jnp.where(lane == 0, hi, jnp.where(lane == 1, mid, jnp.where(lane == 2, lo, 0.0))).astype(BF16)


def _ones_rows(width):
    row = lax.broadcasted_iota(jnp.int32, (AUX_ROWS, width), 0)
    return jnp.where(row == 0, 1.0, 0.0).astype(BF16)


def _online_update_aug(state, s_t, v_aug, shift=None):
    m, acc = state
    top = jnp.max(s_t, axis=0, keepdims=True)
    m_new = jnp.maximum(m, top if shift is None else top + shift)
    alpha = jnp.exp(m - m_new)
    p = jnp.exp(s_t - (m_new if shift is None else m_new - shift))
    acc = alpha * acc + jnp.dot(v_aug, p.astype(BF16), preferred_element_type=F32)
    return m_new, acc


def _init_state_aug(cols):
    return jnp.full((1, cols), NEG, F32), jnp.zeros((HEAD_DIM + AUX_ROWS, cols), F32)


def _finish_aug(acc):
    return acc[:HEAD_DIM] * (1.0 / acc[HEAD_DIM:HEAD_DIM + 1])


def _fox_prompt_kernel(q_ref, k_ref, v_ref, crow_ref, o_ref, vt_ref, ka_ref, s_ref):
    qt = pl.program_id(2)
    tq = FOX_TQ

    @pl.when(qt == 0)
    def _():
        _fill_transposed(vt_ref, v_ref)
        vt_ref[HEAD_DIM:, :] = _ones_rows(vt_ref.shape[1])

        def body(k, carry):
            st = pl.multiple_of(k * LANE, LANE)
            c_col = jnp.broadcast_to(crow_ref[0, 0, :, pl.ds(st, LANE)], (LANE, LANE)).T
            ka_ref[pl.ds(st, LANE), :HEAD_DIM] = k_ref[0, 0, pl.ds(st, LANE), :]
            ka_ref[pl.ds(st, LANE), HEAD_DIM:] = _lane_select3(_split3(c_col), (LANE, LANE))
            return carry

        lax.fori_loop(0, k_ref.shape[2] // LANE, body, 0)

    q0 = pl.multiple_of(qt * tq, tq)
    q_t = _transpose_tiles(q_ref[0]).astype(BF16)
    row = lax.broadcasted_iota(jnp.int32, (LANE, tq), 0)
    q_aug = jnp.concatenate([q_t, jnp.where(row < 3, -1.0, 0.0).astype(BF16)], axis=0)
    c_q = crow_ref[0, 0, :, pl.ds(q0, tq)]
    blk = FOX_KEY_BLOCK

    last = q0 // blk
    rel = (lax.broadcasted_iota(jnp.int32, (blk, tq), 0) - lax.broadcasted_iota(jnp.int32, (blk, tq), 1))

    def run(n_blocks):
        top = jnp.full((8, tq), NEG, F32)
        for kb in range(n_blocks):
            start = kb * blk
            s = jnp.dot(ka_ref[start:start + blk, :], q_aug, preferred_element_type=F32)
            if kb == n_blocks - 1:
                s = jnp.where(rel <= q0 - start, s, NEG)
            s_ref[start:start + blk, :] = s
            top = jnp.maximum(top, jnp.max(s.reshape(blk // 8, 8, tq), axis=0))
        m = jnp.max(top, axis=0, keepdims=True) + c_q
        shift = m - c_q
        acc = jnp.zeros((HEAD_DIM + AUX_ROWS, tq), F32)
        for kb in range(n_blocks):
            start = kb * blk
            p = jnp.exp(s_ref[start:start + blk, :] - shift).astype(BF16)
            acc = acc + jnp.dot(vt_ref[:, start:start + blk], p, preferred_element_type=F32)
        return acc

    n_max = k_ref.shape[2] // blk
    acc = lax.switch(last, [functools.partial(run, n) for n in range(1, n_max + 1)])
    o_ref[...] = _untranspose_tiles(_finish_aug(acc))


def _fox_prompt(hm, hm4, crow, nb, seq):
    nq = seq // FOX_TQ
    return pl.pallas_call(
        _fox_prompt_kernel,
        out_shape=jax.ShapeDtypeStruct((nb * seq, FOX_HEADS * HEAD_DIM), F32),
        grid=(nb, FOX_HEADS, nq),
        in_specs=[
            pl.BlockSpec((1, FOX_TQ, HEAD_DIM), lambda b, h, t: (HM_Q_FOX + h, b * nq + t, 0)),
            pl.BlockSpec((1, 1, seq, HEAD_DIM), lambda b, h, t: (HM_K_FOX + h, b, 0, 0)),
            pl.BlockSpec((1, 1, seq, HEAD_DIM), lambda b, h, t: (HM_V_FOX + h, b, 0, 0)),
            pl.BlockSpec((1, 1, 1, seq), lambda b, h, t: (b, h, 0, 0)),
        ],
        out_specs=pl.BlockSpec((FOX_TQ, HEAD_DIM), lambda b, h, t: (b * nq + t, h)),
        scratch_shapes=[pltpu.VMEM((HEAD_DIM + AUX_ROWS, seq), BF16), pltpu.VMEM((seq, 2 * HEAD_DIM), BF16),
                        pltpu.VMEM((seq, FOX_TQ), F32)],
        compiler_params=_cparams(("parallel", "parallel", "arbitrary")),
        name="fox_prompt",
    )(hm, hm4, hm4, crow)


NSA_TQ = LANE
NSA_COLS = NSA_GROUP * NSA_TQ
WIN_TILES = WINDOW // NSA_TQ + 1
SLC_BLOCK_TILES = 4
SLC_MASK_ROWS = SLC_BLOCK_TILES * NSA_TQ // SEL_BLOCK


def _nsa_prompt_kernel(tbl_ref, q_ref, kc_ref, vc_ref, mselt_ref, ks_ref, vs_ref, kw_ref, vw_ref, gate_ref, o_ref,
                       vst_ref, vwt_ref, vct_ref, wb_ref, pc_ref, drop_ref, ksa_ref, wbd_ref, s_ref, *, n_sel, cmp_back):
    g = pl.program_id(1)
    qt = pl.program_id(2)
    tq = NSA_TQ
    cols = NSA_COLS
    heads = [NSA_GROUP * g + hh for hh in range(NSA_GROUP)]
    ncp = kc_ref.shape[2]
    nj = mselt_ref.shape[0]

    @pl.when(qt == 0)
    def _():
        _fill_transposed(vst_ref, vs_ref)
        _fill_transposed(vwt_ref, vw_ref)
        vst_ref[HEAD_DIM:, :] = _ones_rows(vst_ref.shape[1])
        vwt_ref[HEAD_DIM:, :] = _ones_rows(vwt_ref.shape[1])
        vct_ref[...] = _transpose_tiles(vc_ref[0, 0]).astype(BF16)

        def fill_keys(k, carry):
            st = pl.multiple_of(k * tq, tq)
            lane = lax.broadcasted_iota(jnp.int32, (tq, tq), 1)
            blk_in_step = (k % SLC_BLOCK_TILES) * (tq // SEL_BLOCK) + lax.broadcasted_iota(jnp.int32, (tq, tq), 0) // SEL_BLOCK
            extra = (lane == blk_in_step) | ((lane >= SLC_MASK_ROWS) & (lane < SLC_MASK_ROWS + 3))
            ksa_ref[pl.ds(st, tq), :HEAD_DIM] = ks_ref[0, 0, pl.ds(st, tq), :]
            ksa_ref[pl.ds(st, tq), HEAD_DIM:] = jnp.where(extra, 1.0, 0.0).astype(BF16)
            return carry

        lax.fori_loop(0, ks_ref.shape[2] // tq, fill_keys, 0)
        key = lax.broadcasted_iota(jnp.int32, (tq, tq), 0)
        qry = lax.broadcasted_iota(jnp.int32, (tq, tq), 1)
        blk = lax.broadcasted_iota(jnp.int32, (pc_ref.shape[0], tq), 0) - cmp_back
        d_cmp = lax.broadcasted_iota(jnp.int32, (pc_ref.shape[0], tq), 1) - (blk * CMP_STRIDE + (CMP_LEN - 1))
        for hh in range(NSA_GROUP):
            sl = slice(hh * tq, (hh + 1) * tq)
            far = tbl_ref[REL_BUCKETS - 1, heads[hh]]
            for delta in range(2):
                d = delta * tq + qry - key
                wb_ref[delta, :, sl] = jnp.where(d >= 0, _rel_bias(jnp.maximum(d, 0), tbl_ref, heads[hh]), NEG)
            for delta in range(2, WIN_TILES - 1):
                wb_ref[delta, :, sl] = jnp.full((tq, tq), far, F32)
            wb_ref[WIN_TILES - 1, :, sl] = jnp.where(qry < key, far, NEG)
            wb_ref[WIN_TILES, :, sl] = jnp.full((tq, tq), NEG, F32)
            for delta in range(2):
                wbd_ref[delta, :, sl] = wb_ref[delta, :, sl] - far
            wbd_ref[2, :, sl] = jnp.zeros((tq, tq), F32)
            wbd_ref[3, :, sl] = jnp.full((tq, tq), NEG, F32)
            pc_ref[:, sl] = jnp.where(d_cmp >= 0, _rel_bias(jnp.maximum(d_cmp, 0), tbl_ref, heads[hh]), NEG)

    q_t = jnp.concatenate([q_ref[hh].astype(F32).T for hh in range(NSA_GROUP)], axis=1).astype(BF16)

    off = pl.multiple_of(cmp_back - qt * (tq // CMP_STRIDE), 8)
    s_t = jnp.dot(kc_ref[0, 0], q_t, preferred_element_type=F32) + pc_ref[pl.ds(off, ncp), :]
    m = jnp.max(s_t, axis=0, keepdims=True)
    e = jnp.exp(s_t - m)
    inv = jnp.where(m > 0.5 * NEG, 1.0 / jnp.sum(e, axis=0, keepdims=True), 0.0)
    p_t = e * inv
    o_cmp = jnp.dot(vct_ref[...], p_t.astype(BF16), preferred_element_type=F32)
    p_sum = p_t[:, :tq]
    for hh in range(1, NSA_GROUP):
        p_sum = p_sum + p_t[:, hh * tq:(hh + 1) * tq]
    imp = jnp.dot(mselt_ref[...], p_sum, preferred_element_type=F32, precision=lax.Precision.HIGHEST)

    jrow = lax.broadcasted_iota(jnp.int32, (nj, tq), 0)
    pos = qt * tq + lax.broadcasted_iota(jnp.int32, (nj, tq), 1)
    blk_q = jnp.right_shift(pos, int(math.log2(SEL_BLOCK)))
    forced = (jrow == 0) | (jrow == blk_q) | (jrow == blk_q - 1)
    score = jnp.where(forced, FORCED_SCORE, jnp.where(jrow <= blk_q, imp, -1.0))
    score = jnp.where(jrow < n_sel, score, -2.0)
    ranks = []
    for r in range(nj // 8):
        mine = score[r * 8:(r + 1) * 8]
        jmine = jrow[r * 8:(r + 1) * 8]
        rank = jnp.zeros((8, tq), F32)
        for other in range(n_sel):
            row = jnp.broadcast_to(score[other:other + 1], (8, tq))
            if other < r * 8:
                beats = row >= mine
            elif other >= (r + 1) * 8:
                beats = row > mine
            else:
                beats = (row > mine) | ((row == mine) & (jmine > other))
            rank = rank + jnp.where(beats, 1.0, 0.0)
        ranks.append(rank)
    rank = jnp.concatenate(ranks, axis=0)
    drop = jnp.where((rank < SEL_TOPK) & (jrow < n_sel), 0.0, NEG)
    drop_ref[...] = jnp.concatenate([drop] * NSA_GROUP, axis=1)

    blk = SLC_BLOCK_TILES * tq
    col = lax.broadcasted_iota(jnp.int32, (8, cols), 1)
    far_row = jnp.full((8, cols), tbl_ref[REL_BUCKETS - 1, heads[NSA_GROUP - 1]], F32)
    for hh in range(NSA_GROUP - 1):
        far_row = jnp.where(col // tq == hh, tbl_ref[REL_BUCKETS - 1, heads[hh]], far_row)
    row8 = lax.broadcasted_iota(jnp.int32, (8, cols), 0)
    parts = [part.astype(F32) for part in _split3(far_row)]
    far_rows = jnp.where(row8 == 0, parts[0], jnp.where(row8 == 1, parts[1], jnp.where(row8 == 2, parts[2], 0.0)))
    pad_rows = jnp.zeros((HEAD_DIM - SLC_MASK_ROWS - 8, cols), BF16)

    last = qt // SLC_BLOCK_TILES

    def slc_run(n_blocks):
        top = jnp.full((8, cols), NEG, F32)
        for kb in range(n_blocks):
            start = kb * blk
            masks = drop_ref[kb * SLC_MASK_ROWS:(kb + 1) * SLC_MASK_ROWS, :]
            extra = jnp.concatenate([masks, far_rows], axis=0).astype(BF16)
            q_aug = jnp.concatenate([q_t, extra, pad_rows], axis=0)
            s = jnp.dot(ksa_ref[start:start + blk, :], q_aug, preferred_element_type=F32)
            if kb >= n_blocks - 2:
                terms = []
                for i in range(SLC_BLOCK_TILES):
                    delta = qt - (kb * SLC_BLOCK_TILES + i)
                    terms.append(wbd_ref[jnp.where(delta < 0, 3, jnp.minimum(delta, 2))])
                s = s + jnp.concatenate(terms, axis=0)
            s_ref[start:start + blk, :] = s
            top = jnp.maximum(top, jnp.max(s.reshape(blk // 8, 8, cols), axis=0))
        m = jnp.max(top, axis=0, keepdims=True)
        acc = jnp.zeros((HEAD_DIM + AUX_ROWS, cols), F32)
        for kb in range(n_blocks):
            start = kb * blk
            p = jnp.exp(s_ref[start:start + blk, :] - m).astype(BF16)
            acc = acc + jnp.dot(vst_ref[:, start:start + blk], p, preferred_element_type=F32)
        return acc

    n_max = ks_ref.shape[2] // blk
    o_slc = _finish_aug(lax.switch(last, [functools.partial(slc_run, n) for n in range(1, n_max + 1)]))

    w0 = jnp.maximum(qt - (WIN_TILES - 1), 0)
    start = pl.multiple_of(w0 * tq, tq)
    span = WIN_TILES * tq
    s = jnp.dot(kw_ref[0, 0, pl.ds(start, span), :], q_t, preferred_element_type=F32)
    terms = []
    for i in range(WIN_TILES):
        delta = qt - (w0 + i)
        terms.append(wb_ref[jnp.where(delta < 0, WIN_TILES, delta)])
    s = s + jnp.concatenate(terms, axis=0)
    p = jnp.exp(s - jnp.max(s, axis=0, keepdims=True))
    o_win = _finish_aug(jnp.dot(vwt_ref[:, pl.ds(start, span)], p.astype(BF16), preferred_element_type=F32))

    gate = gate_ref[0, 0, 0]
    o_t = gate[0:1] * o_cmp + gate[1:2] * o_slc + gate[2:3] * o_win
    for hh in range(NSA_GROUP):
        o_ref[:, hh * HEAD_DIM:(hh + 1) * HEAD_DIM] = o_t[:, hh * tq:(hh + 1) * tq].T


def _nsa_prompt(rel_table, hm, hm4, kcv, mselt, gates_t, nb, seq, n_sel):
    tq = NSA_TQ
    nq = seq // tq
    ncp = kcv.shape[2]
    nj = mselt.shape[0]
    cmp_back = (nq - 1) * (tq // CMP_STRIDE)
    kern = functools.partial(_nsa_prompt_kernel, n_sel=n_sel, cmp_back=cmp_back)
    kv_spec = lambda head0: pl.BlockSpec((1, 1, seq, HEAD_DIM), lambda b, g, t: (head0 + g, b, 0, 0))
    return pl.pallas_call(
        kern,
        out_shape=jax.ShapeDtypeStruct((nb * seq, NSA_HEADS * HEAD_DIM), F32),
        grid=(nb, NSA_KV_HEADS, nq),
        in_specs=[
            pl.BlockSpec(memory_space=pltpu.SMEM),
            pl.BlockSpec((NSA_GROUP, tq, HEAD_DIM), lambda b, g, t: (g, b * nq + t, 0)),
            pl.BlockSpec((1, 1, ncp, HEAD_DIM), lambda b, g, t: (g, b, 0, 0)),
            pl.BlockSpec((1, 1, ncp, HEAD_DIM), lambda b, g, t: (NSA_KV_HEADS + g, b, 0, 0)),
            pl.BlockSpec((nj, ncp), lambda b, g, t: (0, 0)),
            kv_spec(HM_K_SLC), kv_spec(HM_V_SLC), kv_spec(HM_K_WIN), kv_spec(HM_V_WIN),
            pl.BlockSpec((1, 1, 1, 8, NSA_COLS), lambda b, g, t: (b, g, t, 0, 0)),
        ],
        out_specs=pl.BlockSpec((tq, NSA_GROUP * HEAD_DIM), lambda b, g, t: (b * nq + t, g)),
        scratch_shapes=[
            pltpu.VMEM((HEAD_DIM + AUX_ROWS, seq), BF16),
            pltpu.VMEM((HEAD_DIM + AUX_ROWS, seq), BF16),
            pltpu.VMEM((HEAD_DIM, ncp), BF16),
            pltpu.VMEM((WIN_TILES + 1, tq, NSA_COLS), F32),
            pltpu.VMEM((cmp_back + ncp, NSA_COLS), F32),
            pltpu.VMEM((nj, NSA_COLS), F32),
            pltpu.VMEM((seq, 2 * HEAD_DIM), BF16),
            pltpu.VMEM((4, tq, NSA_COLS), F32),
            pltpu.VMEM((seq, NSA_COLS), F32),
        ],
        compiler_params=_cparams(("parallel", "parallel", "arbitrary")),
        name="nsa_prompt",
    )(rel_table, hm, kcv, kcv, mselt, hm4, hm4, hm4, hm4, gates_t)


CHUNK_ROWS = 512


def _chunkify_kernel(x_ref, o_ref, *, n_cols):
    n = o_ref.shape[2]
    for c in range(o_ref.shape[0]):
        for s in range(CMP_STRIDE):
            o_ref[c, 0, :, s * HEAD_DIM:(s + 1) * HEAD_DIM] = (
                x_ref[0, pl.ds(s * n_cols + c, n, stride=CMP_STRIDE * n_cols), :].astype(BF16))


def _chunkify(rows3, n_cols, n_heads, seq):
    nb = rows3.shape[0]
    tr = _largest_tile(seq, (CHUNK_ROWS, 256))
    return pl.pallas_call(
        functools.partial(_chunkify_kernel, n_cols=n_cols),
        out_shape=jax.ShapeDtypeStruct((n_heads, nb, seq // CMP_STRIDE, CMP_STRIDE * HEAD_DIM), BF16),
        grid=(nb, seq // tr),
        in_specs=[pl.BlockSpec((1, tr * n_cols, HEAD_DIM), lambda b, i: (b, i, 0))],
        out_specs=pl.BlockSpec((n_heads, 1, tr // CMP_STRIDE, CMP_STRIDE * HEAD_DIM), lambda b, i: (0, b, i, 0)),
        compiler_params=_cparams(("parallel", "parallel")),
        name="chunkify",
    )(rows3)


REGROUP_PAGES = 4


N_CACHE_COLS = 16
N_CMP_COLS = 2 * NSA_KV_HEADS


def _regroup_kernel(pt_ref, *refs, n_steps):
    del pt_ref
    npg = REGROUP_PAGES
    nsa_in, lf_in = refs[:npg], refs[npg:2 * npg]
    xc_out, slc_out, lf_out = refs[2 * npg:]
    is_tail = pl.program_id(1) >= n_steps
    chunks = PAGE_SIZE // CMP_STRIDE
    chunk_stride = CMP_STRIDE * N_CACHE_COLS

    @pl.when(is_tail)
    def _():
        slc_out[...] = jnp.zeros(slc_out.shape, slc_out.dtype)
        lf_out[...] = jnp.zeros(lf_out.shape, lf_out.dtype)

    @pl.when(jnp.logical_not(is_tail))
    def _():
        for p in range(npg):
            rows = slice(p * PAGE_SIZE, (p + 1) * PAGE_SIZE)
            for c in range(N_CACHE_COLS - N_CMP_COLS):
                slc_out[c, 0, rows, :] = nsa_in[p][0, pl.ds(N_CMP_COLS + c, PAGE_SIZE, stride=N_CACHE_COLS), :].astype(BF16)
            lf_out[0, rows, :] = lf_in[p][0]
        for pair in range(npg // 2):
            for c in range(N_CMP_COLS):
                for s in range(CMP_STRIDE):
                    first = s * N_CACHE_COLS + c
                    both = [nsa_in[2 * pair + i][0, pl.ds(first, chunks, stride=chunk_stride), :] for i in range(2)]
                    xc_out[c, 0, pair * 2 * chunks:(pair + 1) * 2 * chunks, s * HEAD_DIM:(s + 1) * HEAD_DIM] = (
                        jnp.concatenate(both, axis=0).astype(BF16))


def _regroup(page_table, cache_nsa, cache_logf, lk):
    nb, n_pages = page_table.shape
    npg = REGROUP_PAGES
    n_steps = n_pages // npg
    rows = npg * PAGE_SIZE
    last = n_steps - 1
    n_slc = N_CACHE_COLS - N_CMP_COLS
    chunk_w = CMP_STRIDE * HEAD_DIM

    def page_map(p):
        return lambda b, s, pt: (pt[b, jnp.minimum(s, last) * npg + p], 0, 0)

    def specs(arr):
        return [pl.BlockSpec((1,) + arr.shape[1:], page_map(p)) for p in range(npg)]

    assert (lk - n_pages * PAGE_SIZE) % rows == 0
    n_tail = (lk - n_pages * PAGE_SIZE) // rows
    grid_spec = pltpu.PrefetchScalarGridSpec(
        num_scalar_prefetch=1,
        grid=(nb, n_steps + n_tail),
        in_specs=specs(cache_nsa) + specs(cache_logf),
        out_specs=(
            pl.BlockSpec((N_CMP_COLS, 1, rows // CMP_STRIDE, chunk_w), lambda b, s, pt: (0, b, jnp.minimum(s, last), 0)),
            pl.BlockSpec((n_slc, 1, rows, HEAD_DIM), lambda b, s, pt: (0, b, s, 0)),
            pl.BlockSpec((1, rows, cache_logf.shape[2]), lambda b, s, pt: (b, s, 0)),
        ),
    )
    return pl.pallas_call(
        functools.partial(_regroup_kernel, n_steps=n_steps),
        out_shape=(
            jax.ShapeDtypeStruct((N_CMP_COLS, nb, n_pages * PAGE_SIZE // CMP_STRIDE, chunk_w), BF16),
            jax.ShapeDtypeStruct((n_slc, nb, lk, HEAD_DIM), BF16),
            jax.ShapeDtypeStruct((nb, lk, cache_logf.shape[2]), F32),
        ),
        grid_spec=grid_spec,
        compiler_params=_cparams(("parallel", "arbitrary")),
        name="cache_regroup",
    )(page_table, *([cache_nsa] * npg), *([cache_logf] * npg))


FOX_DEC_PAGES = 8


def _fox_decode_kernel(pt_ref, *refs, dseq):
    del pt_ref
    npg = FOX_DEC_PAGES
    pages = refs[:npg]
    qbd_ref, ccol_ref, cq_ref, knew_ref, vnew_ref, o_ref, m_ref, acc_ref = refs[npg:]
    step = pl.program_id(1)
    is_tail = step == pl.num_programs(1) - 1
    width = FOX_HEADS * HEAD_DIM

    @pl.when(step == 0)
    def _():
        m_ref[...] = jnp.full(m_ref.shape, NEG, F32)
        acc_ref[...] = jnp.zeros(acc_ref.shape, F32)

    def fold(k_blk, v_blk, bias):
        n = k_blk.shape[0]
        s_t = jnp.dot(k_blk, qbd_ref[0], preferred_element_type=F32) + bias
        m_new = jnp.maximum(m_ref[...], jnp.max(s_t, axis=0, keepdims=True))
        alpha = jnp.exp(m_ref[...] - m_new)
        p = jnp.exp(s_t - m_new)
        p_t = jnp.concatenate([p[i * LANE:(i + 1) * LANE].T for i in range(n // LANE)], axis=1).astype(BF16)
        v_aug = jnp.concatenate([v_blk, jnp.ones((n, LANE), BF16)], axis=1)
        upd = jnp.dot(p_t, v_aug, preferred_element_type=F32)
        alpha_col = jnp.broadcast_to(alpha, (LANE, LANE)).T
        acc_ref[...] = acc_ref[...] * jnp.concatenate([alpha_col] * (width // LANE + 1), axis=1) + upd
        m_ref[...] = m_new

    @pl.when(jnp.logical_not(is_tail))
    def _():
        def heads_of(p, first):
            cols = [pages[p][0, pl.ds(first + h, PAGE_SIZE, stride=N_CACHE_COLS), :] for h in range(FOX_HEADS)]
            return jnp.concatenate(cols, axis=1).astype(BF16)

        k_blk = jnp.concatenate([heads_of(p, 0) for p in range(npg)], axis=0)
        v_blk = jnp.concatenate([heads_of(p, FOX_HEADS) for p in range(npg)], axis=0)
        fold(k_blk, v_blk, cq_ref[0] - ccol_ref[0])

    @pl.when(is_tail)
    def _():
        row = lax.broadcasted_iota(jnp.int32, (LANE, LANE), 0)
        t_of_col = lax.broadcasted_iota(jnp.int32, (LANE, LANE), 1) % dseq
        bias = jnp.where(row <= t_of_col, cq_ref[0] - ccol_ref[0, :LANE, :], NEG)
        fold(knew_ref[0], vnew_ref[0], bias)
        acc = acc_ref[...]
        inv = 1.0 / acc[:, width:]
        o_ref[0] = acc[:, :width] * jnp.concatenate([inv] * (width // LANE), axis=1)


def _fox_decode(page_table, cache_fox, qbd, ccols, cq, knew, vnew, dseq):
    nb, n_pages = page_table.shape
    npg = FOX_DEC_PAGES
    n_steps = n_pages // npg
    last = n_steps - 1
    width = FOX_HEADS * HEAD_DIM

    def page_map(p):
        return lambda b, s, pt: (pt[b, jnp.minimum(s, last) * npg + p], 0, 0)

    per_b = lambda shape: pl.BlockSpec((1,) + shape, lambda b, s, pt: (b, 0, 0))
    grid_spec = pltpu.PrefetchScalarGridSpec(
        num_scalar_prefetch=1,
        grid=(nb, n_steps + 1),
        in_specs=[pl.BlockSpec((1,) + cache_fox.shape[1:], page_map(p)) for p in range(npg)] + [
            per_b((width, LANE)),
            pl.BlockSpec((1, npg * PAGE_SIZE, LANE), lambda b, s, pt: (b, s, 0)),
            per_b((1, LANE)),
            per_b((LANE, width)),
            per_b((LANE, width)),
        ],
        out_specs=per_b((LANE, width)),
        scratch_shapes=[pltpu.VMEM((1, LANE), F32), pltpu.VMEM((LANE, width + LANE), F32)],
    )
    return pl.pallas_call(
        functools.partial(_fox_decode_kernel, dseq=dseq),
        out_shape=jax.ShapeDtypeStruct((nb, LANE, width), F32),
        grid_spec=grid_spec,
        compiler_params=_cparams(("parallel", "arbitrary")),
        name="fox_decode",
    )(page_table, *([cache_fox] * npg), qbd, ccols, cq, knew, vnew)


NSA_DEC_KEYS = 2048


def _lane_transpose(p):
    return jnp.concatenate([p[i * LANE:(i + 1) * LANE].T for i in range(p.shape[0] // LANE)], axis=1).astype(BF16)


def _nsa_decode_kernel(qbd_ref, kc_ref, vc_ref, mselt_ref, pair_ref, tcol_ref, ks_ref, vs_ref, kw_ref, vw_ref,
                       gate_ref, o_ref, drop_ref, m_ref, acc_ref, ocmp_ref, *, dseq, past, n_sel):
    step = pl.program_id(1)
    n_tiles = pl.num_programs(1)
    width = NSA_KV_HEADS * HEAD_DIM
    qbd = qbd_ref[0]
    col = lax.broadcasted_iota(jnp.int32, (1, LANE), 1)
    q_pos = past + col % dseq

    def side_by_side(ref, rows=None):
        parts = [ref[g, 0] if rows is None else ref[g, 0, rows, :] for g in range(NSA_KV_HEADS)]
        return jnp.concatenate(parts, axis=1)

    def rel_bias_cols(dist):
        out = jnp.broadcast_to(tcol_ref[0:1, :], dist.shape)
        for k, thr in enumerate(BUCKET_THR, start=1):
            out = jnp.where(dist >= thr, tcol_ref[k:k + 1, :], out)
        return out

    def softmax_pv(s_t, v_all):
        m = jnp.max(s_t, axis=0, keepdims=True)
        e = jnp.exp(s_t - m)
        inv = jnp.where(m > 0.5 * NEG, 1.0 / jnp.sum(e, axis=0, keepdims=True), 0.0)
        p = e * inv
        return p, jnp.dot(_lane_transpose(p), v_all, preferred_element_type=F32)

    @pl.when(step == 0)
    def _():
        m_ref[...] = jnp.full(m_ref.shape, NEG, F32)
        acc_ref[...] = jnp.zeros(acc_ref.shape, F32)
        ncp = kc_ref.shape[2]
        blk_end = lax.broadcasted_iota(jnp.int32, (ncp, LANE), 0) * CMP_STRIDE + (CMP_LEN - 1)
        d = q_pos - blk_end
        s_t = jnp.dot(side_by_side(kc_ref), qbd, preferred_element_type=F32)
        s_t = jnp.where(d >= 0, s_t + rel_bias_cols(jnp.maximum(d, 0)), NEG)
        p, ocmp_ref[...] = softmax_pv(s_t, side_by_side(vc_ref))
        imp = jnp.dot(mselt_ref[...], p, preferred_element_type=F32, precision=lax.Precision.HIGHEST)
        imp = jnp.dot(imp, pair_ref[...], preferred_element_type=F32, precision=lax.Precision.HIGHEST)
        nj = imp.shape[0]
        jrow = lax.broadcasted_iota(jnp.int32, (nj, LANE), 0)
        blk_q = jnp.right_shift(q_pos, int(math.log2(SEL_BLOCK)))
        forced = (jrow == 0) | (jrow == blk_q) | (jrow == blk_q - 1)
        score = jnp.where(forced, FORCED_SCORE, jnp.where(jrow <= blk_q, imp, -1.0))
        score = jnp.where(jrow < n_sel, score, -2.0)
        drop_ref[...] = score
        rank = jnp.zeros((nj, LANE), F32)

        def count(other, rank):
            row = jnp.broadcast_to(drop_ref[pl.ds(other, 1), :], (nj, LANE))
            beats = (row > score) | ((row == score) & (jrow > other))
            return rank + jnp.where(beats, 1.0, 0.0)

        rank = lax.fori_loop(0, n_sel, count, rank)
        far = tcol_ref[REL_BUCKETS - 1:REL_BUCKETS, :]
        drop_ref[...] = jnp.where((rank < SEL_TOPK) & (jrow <= blk_q), far, NEG)

    tile = ks_ref.shape[2]
    per_tile = tile // SEL_BLOCK
    start = step * tile
    first_blk = pl.multiple_of(step * per_tile, 8)
    rows = [jnp.broadcast_to(drop_ref[pl.ds(first_blk + i, 1), :], (SEL_BLOCK, LANE)) for i in range(per_tile)]
    s_t = jnp.dot(side_by_side(ks_ref), qbd, preferred_element_type=F32) + jnp.concatenate(rows, axis=0)

    def near_fix(s_t):
        def fix(rows_at, s_rows):
            key_pos = start + rows_at + lax.broadcasted_iota(jnp.int32, (LANE, LANE), 0)
            d = q_pos - key_pos
            far = tcol_ref[REL_BUCKETS - 1:REL_BUCKETS, :]
            return jnp.where(d >= 0, s_rows + (rel_bias_cols(jnp.maximum(d, 0)) - far), NEG)

        head = fix(0, s_t[:LANE])
        tail = fix(tile - LANE, s_t[tile - LANE:])
        return jnp.concatenate([head, s_t[LANE:tile - LANE], tail], axis=0)

    is_near = (start + tile > past - FAR_DIST)
    s_t = lax.cond(is_near, near_fix, lambda s: s, s_t)
    m_new = jnp.maximum(m_ref[...], jnp.max(s_t, axis=0, keepdims=True))
    alpha = jnp.exp(m_ref[...] - m_new)
    p = jnp.exp(s_t - m_new)
    v_aug = jnp.concatenate([side_by_side(vs_ref), jnp.ones((tile, LANE), BF16)], axis=1)
    upd = jnp.dot(_lane_transpose(p), v_aug, preferred_element_type=F32)
    alpha_col = jnp.broadcast_to(alpha, (LANE, LANE)).T
    acc_ref[...] = acc_ref[...] * jnp.concatenate([alpha_col] * (width // LANE + 1), axis=1) + upd
    m_ref[...] = m_new

    @pl.when(step == n_tiles - 1)
    def _():
        acc = acc_ref[...]
        o_slc = acc[:, :width] * jnp.concatenate([1.0 / acc[:, width:]] * (width // LANE), axis=1)
        span = kw_ref.shape[2]
        key_pos = (past - WINDOW) + lax.broadcasted_iota(jnp.int32, (span, LANE), 0)
        d = q_pos - key_pos
        s_w = jnp.dot(side_by_side(kw_ref), qbd, preferred_element_type=F32)
        s_w = jnp.where((d >= 0) & (d < WINDOW), s_w + rel_bias_cols(jnp.maximum(d, 0)), NEG)
        _, o_win = softmax_pv(s_w, side_by_side(vw_ref))
        g = gate_ref[0]
        tile4 = lambda a: jnp.concatenate([a] * (width // LANE), axis=1)
        o_ref[0] = tile4(g[0]) * ocmp_ref[...] + tile4(g[1]) * o_slc + tile4(g[2]) * o_win


def _nsa_decode(qbd, kcv, mselt, pair, tcols, nsa_dec, win_dec, gcols, *, dseq, past, n_sel):
    nb = qbd.shape[0]
    ncp = kcv.shape[2]
    nj = mselt.shape[0]
    span = win_dec.shape[2]
    tile = NSA_DEC_KEYS
    n_tiles = -(-nsa_dec.shape[2] // tile)
    width = NSA_KV_HEADS * HEAD_DIM
    kern = functools.partial(_nsa_decode_kernel, dseq=dseq, past=past, n_sel=n_sel)
    grp = lambda rows, half, tiled: pl.BlockSpec(
        (NSA_KV_HEADS, 1, rows, HEAD_DIM), (lambda b, s: (half, b, s, 0)) if tiled else (lambda b, s: (half, b, 0, 0)))
    const = lambda shape: pl.BlockSpec(shape, lambda b, s: (0,) * len(shape))
    return pl.pallas_call(
        kern,
        out_shape=jax.ShapeDtypeStruct((nb, LANE, width), F32),
        grid=(nb, n_tiles),
        in_specs=[
            pl.BlockSpec((1, width, LANE), lambda b, s: (b, 0, 0)),
            grp(ncp, 0, False), grp(ncp, 1, False),
            const((nj, ncp)), const((LANE, LANE)), const((REL_BUCKETS, LANE)),
            grp(tile, 0, True), grp(tile, 1, True),
            grp(span, 0, False), grp(span, 1, False),
            pl.BlockSpec((1, 3, LANE, LANE), lambda b, s: (b, 0, 0, 0)),
        ],
        out_specs=pl.BlockSpec((1, LANE, width), lambda b, s: (b, 0, 0)),
        scratch_shapes=[
            pltpu.VMEM((nj, LANE), F32),
            pltpu.VMEM((1, LANE), F32),
            pltpu.VMEM((LANE, width + LANE), F32),
            pltpu.VMEM((LANE, width), F32),
        ],
        compiler_params=_cparams(("parallel", "arbitrary")),
        name="nsa_decode",
    )(qbd, kcv, kcv, mselt, pair, tcols, nsa_dec, nsa_dec, win_dec, win_dec, gcols)


def _largest_tile(n, candidates):
    for c in candidates:
        if n % c == 0:
            return c
    raise ValueError(f"no tile in {candidates} divides {n}")


FFN_ROW_TILES = (1024, 512, 256, 128)
ROW_TILES = (512, 256, 128)


def _token_stage_in(x, p):
    m = x.shape[0]
    tm = m if m < ROW_TILES[-1] else _largest_tile(m, ROW_TILES)
    tm_ffn = m if m < FFN_ROW_TILES[-1] else _largest_tile(m, FFN_ROW_TILES)
    x1 = _ffn(x, p["norm_ffn1"], p["wg1"], p["wu1"], p["wd1"], tm_ffn, p["tf"])
    nsa_rows, win_rows, fox_rows, hm = _inproj(x1, p["norm_mix"], p["w_main"], p["colgain"], tm)
    small = _small(x1, p["norm_mix"], p["w_small"], p["b_small"], tm)
    return x1, nsa_rows, win_rows, fox_rows, hm, small


def _token_stage_out(x1, o_nsa, o_fox, p):
    m = x1.shape[0]
    tm = m if m < ROW_TILES[-1] else _largest_tile(m, ROW_TILES[1:])
    tm_ffn = m if m < FFN_ROW_TILES[-1] else _largest_tile(m, FFN_ROW_TILES)
    x2 = _outproj(o_nsa, o_fox, p["out_norm_nsa"], p["out_norm_fox"], p["w_out"], x1, tm)
    return _ffn(x2, p["norm_ffn2"], p["wg2"], p["wu2"], p["wd2"], tm_ffn, p["tf"])


def kernel(x_prompt, x_sample, cache_nsa_kv, cache_fox_kv, cache_fox_logf, state_win_kv, page_table, rel_table, norm_ffn1, ffn1_gate, ffn1_up, ffn1_down, norm_mix, w_in, nsa_gate_bias, fox_forget_bias, q_norm_nsa, k_norm_nsa, q_norm_fox, k_norm_fox, cmp_pos_k, cmp_w1_k, cmp_w2_k, cmp_pos_v, cmp_w1_v, cmp_w2_v, out_norm_nsa, out_norm_fox, w_out, norm_ffn2, ffn2_gate, ffn2_up, ffn2_down):
    depth = w_in.shape[0]
    assert depth == 1, "single-layer trunk"
    nbp, seq, d_model = x_prompt.shape
    nbd, dseq, _ = x_sample.shape
    n_pages = page_table.shape[1]
    past = n_pages * PAGE_SIZE
    d_ff = ffn1_gate.shape[2]
    nsa_w = NSA_HEADS * HEAD_DIM
    kv6_w = 6 * NSA_KV_HEADS * HEAD_DIM
    fox_w = 3 * FOX_HEADS * HEAD_DIM
    off_gate = nsa_w + kv6_w
    off_fox = off_gate + N_GATE_COLS
    off_forget = off_fox + fox_w
    assert w_in.shape[2] == off_forget + FOX_HEADS and d_model == nsa_w + FOX_HEADS * HEAD_DIM
    assert seq % LANE == 0 and seq >= WINDOW and past % LANE == 0 and n_pages % REGROUP_PAGES == 0
    assert dseq <= 16 and state_win_kv.shape[2] == WINDOW
    assert seq % FOX_KEY_BLOCK == 0 and seq % (SLC_BLOCK_TILES * NSA_TQ) == 0 and seq >= WIN_TILES * NSA_TQ

    w0 = w_in[0]
    ones = lambda n: jnp.ones((n,), F32)
    zeros = lambda n: jnp.zeros((n,), F32)
    kn, kvw = NSA_KV_HEADS, NSA_KV_HEADS * HEAD_DIM
    qk_scale = HEAD_DIM ** -0.5
    p = {
        "tf": _largest_tile(d_ff, (512, 256, 128)),
        "norm_ffn1": norm_ffn1[0][None], "norm_mix": norm_mix[0][None], "norm_ffn2": norm_ffn2[0][None],
        "wg1": ffn1_gate[0].astype(BF16), "wu1": ffn1_up[0].astype(BF16), "wd1": ffn1_down[0].astype(BF16),
        "wg2": ffn2_gate[0].astype(BF16), "wu2": ffn2_up[0].astype(BF16), "wd2": ffn2_down[0].astype(BF16),
        "w_main": jnp.concatenate([w0[:, :off_gate], w0[:, off_fox:off_forget]], axis=1).astype(BF16),
        "w_small": jnp.concatenate([w0[:, off_gate:off_fox], w0[:, off_forget:],
                                    jnp.zeros((d_model, LANE - N_GATE_COLS - FOX_HEADS), F32)], axis=1).astype(BF16),
        "b_small": jnp.concatenate([nsa_gate_bias[0].reshape(-1), fox_forget_bias[0],
                                    zeros(LANE - N_GATE_COLS - FOX_HEADS)])[None],
        "colgain": jnp.concatenate([
            jnp.tile(q_norm_nsa[0] * qk_scale, NSA_HEADS), ones(2 * kvw), jnp.tile(k_norm_nsa[0], kn), ones(kvw),
            jnp.tile(k_norm_nsa[0], kn), ones(kvw), jnp.tile(q_norm_fox[0] * qk_scale, FOX_HEADS),
            jnp.tile(k_norm_fox[0], FOX_HEADS), ones(FOX_HEADS * HEAD_DIM)])[None],
        "out_norm_nsa": out_norm_nsa[0][None], "out_norm_fox": out_norm_fox[0][None],
        "w_out": w_out[0].astype(BF16),
    }
    half = CMP_STRIDE * HEAD_DIM

    def cmp_w1(w):
        return jnp.concatenate([w[0, :half], w[0, half:]], axis=1)

    def cmp_pe(pe):
        return jnp.concatenate([pe[0].reshape(CMP_LEN // CMP_STRIDE, half), jnp.zeros((PE_ROWS - CMP_LEN // CMP_STRIDE, half), F32)], axis=0)

    w1cat = jnp.stack([cmp_w1(cmp_w1_k), cmp_w1(cmp_w1_v)]).astype(BF16)
    w2cat = jnp.stack([cmp_w2_k[0], cmp_w2_v[0]]).astype(BF16)
    pecat = jnp.stack([cmp_pe(cmp_pos_k), cmp_pe(cmp_pos_v)]).astype(BF16)
    k_norm_row = k_norm_nsa[0][None]

    mp = nbp * seq
    x1, nsa_rows, win_rows, fox_rows, hm, small = _token_stage_in(x_prompt.reshape(mp, d_model), p)
    hm4 = hm.reshape(N_HEAD_COLS, nbp, seq, HEAD_DIM)

    logf = small[:, N_GATE_COLS:N_GATE_COLS + FOX_HEADS]
    csum = _cumsum(logf.reshape(nbp, seq, FOX_HEADS).transpose(0, 2, 1))
    o_fox = _fox_prompt(hm, hm4, csum[:, :, None, :], nbp, seq)

    n_chunk = seq // CMP_STRIDE
    n_cmp = (seq - CMP_LEN) // CMP_STRIDE + 1
    n_sel = -(-seq // SEL_BLOCK)
    xc = _chunkify(nsa_rows.reshape(nbp, seq * N_CACHE_COLS, HEAD_DIM), N_CACHE_COLS, N_CMP_COLS, seq)
    kcv = _compress(xc, 0, w1cat, w2cat, pecat, k_norm_row, n_chunk)
    mselt = _cmp_to_sel(n_cmp, n_sel, n_chunk, -(-n_sel // 8) * 8).T
    nq = seq // NSA_TQ
    gates_t = small[:, :N_GATE_COLS].reshape(nbp, nq, NSA_TQ, NSA_KV_HEADS, NSA_GROUP, 3)
    gates_t = gates_t.transpose(0, 3, 1, 5, 4, 2).reshape(nbp, NSA_KV_HEADS, nq, 3, NSA_COLS)
    gates_t = jnp.pad(gates_t, ((0, 0), (0, 0), (0, 0), (0, 8 - 3), (0, 0)))
    o_nsa = _nsa_prompt(rel_table, hm, hm4, kcv, mselt, gates_t, nbp, seq, n_sel)
    y_p = _token_stage_out(x1, o_nsa, o_fox, p)

    ms = nbd * dseq
    lk = past + NSA_DEC_KEYS
    assert n_pages % FOX_DEC_PAGES == 0 and past % NSA_DEC_KEYS == 0 and FOX_HEADS * dseq <= LANE
    xs1, nsa_rows_s, win_rows_s, fox_rows_s, hm_s, small_s = _token_stage_in(x_sample.reshape(ms, d_model), p)
    xc_d, nsa_dec, lf_dec = _regroup(
        page_table,
        cache_nsa_kv.reshape(cache_nsa_kv.shape[1], PAGE_SIZE * N_CACHE_COLS, HEAD_DIM),
        cache_fox_logf[0], lk)
    hm_s4 = hm_s.reshape(N_HEAD_COLS, nbd, dseq, HEAD_DIM)
    nsa_dec = lax.dynamic_update_slice(nsa_dec, hm_s4[HM_K_SLC:HM_K_WIN], (0, 0, past, 0))
    logf_s = small_s[:, N_GATE_COLS:N_GATE_COLS + FOX_HEADS].reshape(nbd, dseq, FOX_HEADS)
    lf_dec = lax.dynamic_update_slice(lf_dec, logf_s, (0, past, 0))

    csum_d = _cumsum(lf_dec.transpose(0, 2, 1))
    n_cols = FOX_HEADS * dseq
    lane_pad = lambda a: jnp.pad(a, [(0, 0)] * (a.ndim - 1) + [(0, LANE - n_cols)])
    head_eye = jnp.eye(FOX_HEADS, dtype=BF16)
    qbd = jnp.einsum("hbtd,hg->bhdgt", hm_s4[HM_Q_FOX:HM_Q_FOX + FOX_HEADS], head_eye)
    qbd = lane_pad(qbd.reshape(nbd, FOX_HEADS * HEAD_DIM, n_cols))
    ccols = lane_pad(jnp.repeat(csum_d.transpose(0, 2, 1), dseq, axis=2))
    cq = lane_pad(csum_d[:, :, past:past + dseq].reshape(nbd, 1, n_cols))

    def new_rows(head0):
        rows = hm_s4[head0:head0 + FOX_HEADS].transpose(1, 2, 0, 3).reshape(nbd, dseq, FOX_HEADS * HEAD_DIM)
        return jnp.pad(rows, ((0, 0), (0, LANE - dseq), (0, 0)))

    o_full = _fox_decode(page_table, cache_fox_kv.reshape(cache_fox_kv.shape[1], PAGE_SIZE * N_CACHE_COLS, HEAD_DIM),
                         qbd, ccols, cq, new_rows(HM_K_FOX), new_rows(HM_V_FOX), dseq)
    o_fox_s = jnp.concatenate([o_full[:, h * dseq:(h + 1) * dseq, h * HEAD_DIM:(h + 1) * HEAD_DIM]
                               for h in range(FOX_HEADS)], axis=2).reshape(ms, FOX_HEADS * HEAD_DIM)

    n_chunk_d = past // CMP_STRIDE
    n_cmp_d = (past + dseq - CMP_LEN) // CMP_STRIDE + 1
    n_sel_d = -(-(past + dseq) // SEL_BLOCK)
    assert n_cmp_d + CMP_LEN // CMP_STRIDE - 1 <= n_chunk_d, "compressed blocks must lie in the cached rows"
    kcv_d = _compress(xc_d, 0, w1cat, w2cat, pecat, k_norm_row, n_chunk_d)
    mselt_d = _cmp_to_sel(n_cmp_d, n_sel_d, n_chunk_d, lk // SEL_BLOCK).T
    win_old = state_win_kv[0].transpose(2, 3, 0, 1, 4).reshape(2 * NSA_KV_HEADS, nbd, WINDOW, HEAD_DIM).astype(BF16)
    win_dec = jnp.concatenate([win_old, hm_s4[HM_K_WIN:HM_V_WIN + NSA_KV_HEADS],
                               jnp.zeros((2 * NSA_KV_HEADS, nbd, LANE - dseq, HEAD_DIM), BF16)], axis=2)
    grp_eye = jnp.eye(NSA_KV_HEADS, dtype=BF16)
    q_grp = hm_s4[HM_Q_NSA:HM_Q_NSA + NSA_HEADS].reshape(NSA_KV_HEADS, NSA_GROUP, nbd, dseq, HEAD_DIM)
    qbd_n = jnp.einsum("gjbtd,gk->bgdkjt", q_grp, grp_eye)
    qbd_n = lane_pad(qbd_n.reshape(nbd, NSA_KV_HEADS * HEAD_DIM, n_cols))
    tcols = lane_pad(jnp.repeat(rel_table, dseq, axis=1))
    col_id = np.arange(n_cols)
    same = ((col_id[:, None] // (NSA_GROUP * dseq) == col_id[None, :] // (NSA_GROUP * dseq))
            & (col_id[:, None] % dseq == col_id[None, :] % dseq))
    pair = jnp.asarray(np.pad(same.astype(np.float32), ((0, LANE - n_cols), (0, LANE - n_cols))))
    gcols = small_s[:, :N_GATE_COLS].reshape(nbd, dseq, NSA_HEADS, 3).transpose(0, 3, 2, 1).reshape(nbd, 3, n_cols)
    gcols = jnp.broadcast_to(lane_pad(gcols)[..., None], (nbd, 3, LANE, LANE))
    o_full_n = _nsa_decode(qbd_n, kcv_d, mselt_d, pair, tcols, nsa_dec, win_dec, gcols,
                           dseq=dseq, past=past, n_sel=n_sel_d)
    o_nsa_s = jnp.concatenate(
        [o_full_n[:, h * dseq:(h + 1) * dseq, (h // NSA_GROUP) * HEAD_DIM:(h // NSA_GROUP + 1) * HEAD_DIM]
         for h in range(NSA_HEADS)], axis=2).reshape(ms, NSA_HEADS * HEAD_DIM)
    y_s = _token_stage_out(xs1, o_nsa_s, o_fox_s, p)

    kvh = (NSA_KV_HEADS, HEAD_DIM)
    win_keep = min(WINDOW, seq)
    win_p = win_rows.reshape(nbp, seq, 2, *kvh)[:, seq - win_keep:]
    win_s = jnp.concatenate([state_win_kv[0], win_rows_s.reshape(nbd, dseq, 2, *kvh)], axis=1)[:, dseq:]
    return (
        y_p.reshape(nbp, seq, d_model),
        y_s.reshape(nbd, dseq, d_model),
        nsa_rows.reshape(1, nbp, seq, 4, *kvh),
        fox_rows.reshape(1, nbp, seq, 2, FOX_HEADS, HEAD_DIM),
        logf.reshape(1, nbp, seq, FOX_HEADS),
        win_p[None],
        nsa_rows_s.reshape(1, nbd, dseq, 4, *kvh),
        fox_rows_s.reshape(1, nbd, dseq, 2, FOX_HEADS, HEAD_DIM),
        logf_s[None],
        win_s[None],
    )
```

```python
import functools
import math

import numpy as np
import jax
import jax.numpy as jnp
from jax import lax
from jax.experimental import pallas as pl
from jax.experimental.pallas import tpu as pltpu

HEAD_DIM = 128
NSA_HEADS = 8
FOX_HEADS = 8
NSA_KV_HEADS = 4
NSA_GROUP = NSA_HEADS // NSA_KV_HEADS
CMP_LEN = 32
CMP_STRIDE = 16
CMP_HIDDEN = 512
SEL_BLOCK = 64
SEL_TOPK = 16
WINDOW = 512
REL_BUCKETS = 32
REL_MAX_DIST = 128
RMS_EPS = 1e-6
PAGE_SIZE = 128

LANE = 128
NEG = -1e30
FORCED_SCORE = 1e30
VMEM_LIMIT = 56 * 1024 * 1024

BF16 = jnp.bfloat16
F32 = jnp.float32
NT_DIMS = (((1,), (1,)), ((), ()))


def _bucket_thresholds():
    n = np.arange(0, 4 * REL_MAX_DIST)
    max_exact = REL_BUCKETS // 2
    nf = np.maximum(n, 1).astype(np.float32)
    large = max_exact + (np.log(nf / max_exact) / math.log(REL_MAX_DIST / max_exact)
                         * (REL_BUCKETS - max_exact)).astype(np.int32)
    bucket = np.where(n < max_exact, n, np.minimum(large, REL_BUCKETS - 1))
    return [int(np.min(n[bucket >= k])) for k in range(1, REL_BUCKETS)]


BUCKET_THR = _bucket_thresholds()
FAR_DIST = BUCKET_THR[-1]


def _cparams(sem):
    return pltpu.CompilerParams(dimension_semantics=sem, vmem_limit_bytes=VMEM_LIMIT)


def _rms_rows(x, gain):
    ms = jnp.mean(x * x, axis=-1, keepdims=True)
    return x * lax.rsqrt(ms + RMS_EPS) * gain


def _ffn_kernel(x_ref, g_ref, wg_ref, wu_ref, wd_ref, o_ref, xn_ref):
    @pl.when(pl.program_id(1) == 0)
    def _():
        x = x_ref[...]
        xn_ref[...] = _rms_rows(x, g_ref[...]).astype(BF16)
        o_ref[...] = x

    xn = xn_ref[...]
    a = jnp.dot(xn, wg_ref[...], preferred_element_type=F32)
    u = jnp.dot(xn, wu_ref[...], preferred_element_type=F32)
    h = (a / (1.0 + jnp.exp(-a))) * u * 0.5
    o_ref[...] += jnp.dot(h.astype(BF16), wd_ref[...], preferred_element_type=F32)


def _ffn(x, gain, wg, wu, wd, tm, tf):
    m, d = x.shape
    f = wg.shape[1]
    return pl.pallas_call(
        _ffn_kernel,
        out_shape=jax.ShapeDtypeStruct((m, d), F32),
        grid=(m // tm, f // tf),
        in_specs=[
            pl.BlockSpec((tm, d), lambda i, j: (i, 0)),
            pl.BlockSpec((1, d), lambda i, j: (0, 0)),
            pl.BlockSpec((d, tf), lambda i, j: (0, j)),
            pl.BlockSpec((d, tf), lambda i, j: (0, j)),
            pl.BlockSpec((tf, d), lambda i, j: (j, 0)),
        ],
        out_specs=pl.BlockSpec((tm, d), lambda i, j: (i, 0)),
        scratch_shapes=[pltpu.VMEM((tm, d), BF16)],
        compiler_params=_cparams(("parallel", "arbitrary")),
        name="ffn",
    )(x, gain, wg, wu, wd)


IN_TN = 4 * HEAD_DIM
J_NSA = (2, 6)
J_WIN = (6, 8)
J_FOX = (10, 14)
N_HEAD_COLS = 56
HM_Q_NSA, HM_K_CMP, HM_K_SLC, HM_V_SLC, HM_K_WIN, HM_V_WIN = 0, 8, 16, 20, 24, 28
HM_Q_FOX, HM_K_FOX, HM_V_FOX = 32, 40, 48


IN_NORMED_TILES = (0, 1, 4, 6, 8, 9, 10, 11)


def _inproj_kernel(x_ref, g_ref, w_ref, cg_ref, ones_ref, nsa_ref, win_ref, fox_ref, hm_ref):
    xn = _rms_rows(x_ref[...], g_ref[...]).astype(BF16)
    tm = xn.shape[0]
    heads = IN_TN // HEAD_DIM
    for j in range(w_ref.shape[1] // IN_TN):
        cols = slice(j * IN_TN, (j + 1) * IN_TN)
        vals = jnp.dot(xn, w_ref[:, cols], preferred_element_type=F32)
        if j in IN_NORMED_TILES:
            sumsq = jnp.dot((vals * vals).astype(BF16), ones_ref[...], preferred_element_type=F32)
            vals = vals * lax.rsqrt(sumsq * (1.0 / HEAD_DIM) + RMS_EPS) * cg_ref[:, cols]
        for hh in range(heads):
            hm_ref[j * heads + hh] = vals[:, hh * HEAD_DIM:(hh + 1) * HEAD_DIM].astype(BF16)
        for ref, (jlo, jhi) in ((nsa_ref, J_NSA), (win_ref, J_WIN), (fox_ref, J_FOX)):
            if jlo <= j < jhi:
                n_cols = (jhi - jlo) * heads
                for hh in range(heads):
                    ref[pl.ds((j - jlo) * heads + hh, tm, stride=n_cols), :] = vals[:, hh * HEAD_DIM:(hh + 1) * HEAD_DIM]


def _inproj(x, gain, w_main, colgain, tm):
    m, d = x.shape
    ncol = w_main.shape[1]
    heads = IN_TN // HEAD_DIM
    n_nsa, n_win, n_fox = [(hi - lo) * heads for lo, hi in (J_NSA, J_WIN, J_FOX)]
    head_ones = jnp.asarray(np.kron(np.eye(heads, dtype=np.float32), np.ones((HEAD_DIM, HEAD_DIM), np.float32)), BF16)
    resident = lambda shape: pl.BlockSpec(shape, lambda i: (0, 0), pipeline_mode=pl.Buffered(1))
    return pl.pallas_call(
        _inproj_kernel,
        out_shape=(
            jax.ShapeDtypeStruct((m * n_nsa, HEAD_DIM), F32),
            jax.ShapeDtypeStruct((m * n_win, HEAD_DIM), F32),
            jax.ShapeDtypeStruct((m * n_fox, HEAD_DIM), F32),
            jax.ShapeDtypeStruct((N_HEAD_COLS, m, HEAD_DIM), BF16),
        ),
        grid=(m // tm,),
        in_specs=[
            pl.BlockSpec((tm, d), lambda i: (i, 0)),
            resident((1, d)),
            resident((d, ncol)),
            resident((1, ncol)),
            resident((IN_TN, IN_TN)),
        ],
        out_specs=(
            pl.BlockSpec((tm * n_nsa, HEAD_DIM), lambda i: (i, 0)),
            pl.BlockSpec((tm * n_win, HEAD_DIM), lambda i: (i, 0)),
            pl.BlockSpec((tm * n_fox, HEAD_DIM), lambda i: (i, 0)),
            pl.BlockSpec((N_HEAD_COLS, tm, HEAD_DIM), lambda i: (0, i, 0)),
        ),
        compiler_params=_cparams(("parallel",)),
        name="inproj",
    )(x, gain, w_main, colgain, head_ones)


N_GATE_COLS = 3 * NSA_HEADS


def _small_kernel(x_ref, g_ref, w_ref, b_ref, o_ref):
    xn = _rms_rows(x_ref[...], g_ref[...]).astype(BF16)
    z = jnp.dot(xn, w_ref[...], preferred_element_type=F32) + b_ref[...]
    lane = lax.broadcasted_iota(jnp.int32, z.shape, 1)
    sig = 1.0 / (1.0 + jnp.exp(-z))
    logsig = jnp.minimum(z, 0.0) - jnp.log(1.0 + jnp.exp(-jnp.abs(z)))
    o_ref[...] = jnp.where(lane < N_GATE_COLS, sig,
                           jnp.where(lane < N_GATE_COLS + FOX_HEADS, logsig, 0.0))


def _small(x, gain, w_small, b_small, tm):
    m, d = x.shape
    return pl.pallas_call(
        _small_kernel,
        out_shape=jax.ShapeDtypeStruct((m, LANE), F32),
        grid=(m // tm,),
        in_specs=[
            pl.BlockSpec((tm, d), lambda i: (i, 0)),
            pl.BlockSpec((1, d), lambda i: (0, 0)),
            pl.BlockSpec((d, LANE), lambda i: (0, 0)),
            pl.BlockSpec((1, LANE), lambda i: (0, 0)),
        ],
        out_specs=pl.BlockSpec((tm, LANE), lambda i: (i, 0)),
        compiler_params=_cparams(("parallel",)),
        name="gates",
    )(x, gain, w_small, b_small)


def _outproj_kernel(on_ref, of_ref, gn_ref, gf_ref, w_ref, x_ref, y_ref):
    a = _rms_rows(on_ref[...], gn_ref[...]).astype(BF16)
    b = _rms_rows(of_ref[...], gf_ref[...]).astype(BF16)
    half = a.shape[1]
    y = jnp.dot(a, w_ref[:half, :], preferred_element_type=F32)
    y = y + jnp.dot(b, w_ref[half:, :], preferred_element_type=F32)
    y_ref[...] = x_ref[...] + y


def _outproj(o_nsa, o_fox, g_nsa, g_fox, w_out, x, tm):
    m, d = x.shape
    wn = o_nsa.shape[1]
    wf = o_fox.shape[1]
    return pl.pallas_call(
        _outproj_kernel,
        out_shape=jax.ShapeDtypeStruct((m, d), F32),
        grid=(m // tm,),
        in_specs=[
            pl.BlockSpec((tm, wn), lambda i: (i, 0)),
            pl.BlockSpec((tm, wf), lambda i: (i, 0)),
            pl.BlockSpec((1, wn), lambda i: (0, 0)),
            pl.BlockSpec((1, wf), lambda i: (0, 0)),
            pl.BlockSpec((wn + wf, d), lambda i: (0, 0)),
            pl.BlockSpec((tm, d), lambda i: (i, 0)),
        ],
        out_specs=pl.BlockSpec((tm, d), lambda i: (i, 0)),
        compiler_params=_cparams(("parallel",)),
        name="outproj",
    )(o_nsa, o_fox, g_nsa, g_fox, w_out, x)


CUMSUM_CHUNK = 512


def _cumsum_kernel(x_ref, before_ref, o_ref):
    width = x_ref.shape[2]
    r = lax.broadcasted_iota(jnp.int32, (width, width), 0)
    c = lax.broadcasted_iota(jnp.int32, (width, width), 1)
    upper = (r <= c).astype(F32)
    local = jnp.dot(x_ref[0], upper, preferred_element_type=F32, precision=lax.Precision.HIGHEST)
    totals = jnp.broadcast_to(local[:, width - 1:width], (local.shape[0], LANE))
    offset = jnp.dot(before_ref[...], totals, preferred_element_type=F32, precision=lax.Precision.HIGHEST)
    o_ref[0] = local + jnp.concatenate([offset] * (width // LANE), axis=1)


def _cumsum(x):
    b, h, length = x.shape
    assert length % CUMSUM_CHUNK == 0
    pieces = length // CUMSUM_CHUNK
    used = h * pieces
    rows = -(-used // LANE) * LANE
    idx = np.arange(rows)
    before = ((idx[:, None] // pieces == idx[None, :] // pieces) & (idx[None, :] < idx[:, None])
              & (idx[:, None] < used))
    x = jnp.pad(x.reshape(b, used, CUMSUM_CHUNK), ((0, 0), (0, rows - used), (0, 0)))
    out = pl.pallas_call(
        _cumsum_kernel,
        out_shape=jax.ShapeDtypeStruct((b, rows, CUMSUM_CHUNK), F32),
        grid=(b,),
        in_specs=[pl.BlockSpec((1, rows, CUMSUM_CHUNK), lambda i: (i, 0, 0)),
                  pl.BlockSpec((rows, rows), lambda i: (0, 0))],
        out_specs=pl.BlockSpec((1, rows, CUMSUM_CHUNK), lambda i: (i, 0, 0)),
        compiler_params=_cparams(("parallel",)),
        name="logf_cumsum",
    )(x, jnp.asarray(before, F32))
    return out[:, :used].reshape(b, h, length)


PE_ROWS = 16


def _compress_kernel(x_ref, w1_ref, w2_ref, pe_ref, kn_ref, o_ref):
    kind = pl.program_id(0) // NSA_KV_HEADS
    n = x_ref.shape[2]
    w1 = w1_ref[0]
    h = jnp.dot(x_ref[0, 0], w1, preferred_element_type=F32)
    pw = jnp.dot(pe_ref[0], w1, preferred_element_type=F32)
    const = pw[0:1, :CMP_HIDDEN] + pw[1:2, CMP_HIDDEN:]
    hid = h[:, :CMP_HIDDEN] + pltpu.roll(h[:, CMP_HIDDEN:], n - 1, 0) + const
    act = hid / (1.0 + jnp.exp(-hid))
    out = jnp.dot(act.astype(BF16), w2_ref[0], preferred_element_type=F32)
    normed = _rms_rows(out, kn_ref[...])
    o_ref[0, 0] = jnp.where(kind == 0, normed, out).astype(BF16)


def _compress(xc, c_off, w1cat, w2, pe, k_norm, n_rows):
    nb = xc.shape[1]
    return pl.pallas_call(
        _compress_kernel,
        out_shape=jax.ShapeDtypeStruct((2 * NSA_KV_HEADS, nb, n_rows, HEAD_DIM), BF16),
        grid=(2 * NSA_KV_HEADS, nb),
        in_specs=[
            pl.BlockSpec((1, 1, n_rows, CMP_STRIDE * HEAD_DIM), lambda c, b: (c_off + c, b, 0, 0)),
            pl.BlockSpec((1, CMP_STRIDE * HEAD_DIM, 2 * CMP_HIDDEN), lambda c, b: (c // NSA_KV_HEADS, 0, 0)),
            pl.BlockSpec((1, CMP_HIDDEN, HEAD_DIM), lambda c, b: (c // NSA_KV_HEADS, 0, 0)),
            pl.BlockSpec((1, PE_ROWS, CMP_STRIDE * HEAD_DIM), lambda c, b: (c // NSA_KV_HEADS, 0, 0)),
            pl.BlockSpec((1, HEAD_DIM), lambda c, b: (0, 0)),
        ],
        out_specs=pl.BlockSpec((1, 1, n_rows, HEAD_DIM), lambda c, b: (c, b, 0, 0)),
        compiler_params=_cparams(("parallel", "parallel")),
        name="compress",
    )(xc, w1cat, w2, pe, k_norm)


def _online_update(state, s, v):
    m, l, acc = state
    m_new = jnp.maximum(m, jnp.max(s, axis=-1, keepdims=True))
    alpha = jnp.exp(m - m_new)
    p = jnp.exp(s - m_new)
    l = alpha * l + jnp.sum(p, axis=-1, keepdims=True)
    acc = alpha * acc + jnp.dot(p.astype(BF16), v, preferred_element_type=F32)
    return m_new, l, acc


def _init_state(rows):
    return (jnp.full((rows, 1), NEG, F32), jnp.zeros((rows, 1), F32), jnp.zeros((rows, HEAD_DIM), F32))


def _rel_bias(dist, tbl_ref, head):
    out = jnp.full(dist.shape, tbl_ref[0, head], F32)
    for k, thr in enumerate(BUCKET_THR, start=1):
        out = jnp.where(dist >= thr, tbl_ref[k, head], out)
    return out


def _fox_kernel(q_ref, k_ref, v_ref, crow_ref, ccol_ref, o_ref, *, tq, tk, td, qoff):
    q0 = qoff + pl.program_id(2) * tq
    q = q_ref[0]
    cq = ccol_ref[0, 0]

    def logits(start, size):
        k = k_ref[0, 0, pl.ds(start, size), :]
        s = lax.dot_general(q, k, NT_DIMS, preferred_element_type=F32)
        return s + (cq - crow_ref[0, 0, :, pl.ds(start, size)])

    def body(kt, state):
        start = pl.multiple_of(kt * tk, tk)
        return _online_update(state, logits(start, tk), v_ref[0, 0, pl.ds(start, tk), :])

    state = lax.fori_loop(0, q0 // tk, body, _init_state(tq))
    start = pl.multiple_of(q0, LANE)
    s = logits(start, td)
    row = lax.broadcasted_iota(jnp.int32, (tq, td), 0)
    col = lax.broadcasted_iota(jnp.int32, (tq, td), 1)
    s = jnp.where(col <= row, s, NEG)
    _, l, acc = _online_update(state, s, v_ref[0, 0, pl.ds(start, td), :])
    o_ref[...] = acc / l


def _fox_attention(qarr, kv, crow, ccol, *, q_head0, k_head0, v_head0, nb, nq, tq, tk, td, qoff):
    lk = kv.shape[2]
    kern = functools.partial(_fox_kernel, tq=tq, tk=tk, td=td, qoff=qoff)
    return pl.pallas_call(
        kern,
        out_shape=jax.ShapeDtypeStruct((nb * nq * tq, FOX_HEADS * HEAD_DIM), F32),
        grid=(nb, FOX_HEADS, nq),
        in_specs=[
            pl.BlockSpec((1, tq, HEAD_DIM), lambda b, h, t: (q_head0 + h, b * nq + t, 0)),
            pl.BlockSpec((1, 1, lk, HEAD_DIM), lambda b, h, t: (k_head0 + h, b, 0, 0)),
            pl.BlockSpec((1, 1, lk, HEAD_DIM), lambda b, h, t: (v_head0 + h, b, 0, 0)),
            pl.BlockSpec((1, 1, 1, lk), lambda b, h, t: (b, h, 0, 0)),
            pl.BlockSpec((1, 1, tq, 1), lambda b, h, t: (b, h, t, 0)),
        ],
        out_specs=pl.BlockSpec((tq, HEAD_DIM), lambda b, h, t: (b * nq + t, h)),
        compiler_params=_cparams(("parallel", "parallel", "arbitrary")),
        name="fox_attention",
    )(qarr, kv, kv, crow, ccol)


def _nsa_kernel(tbl_ref, q_ref, kc_ref, vc_ref, msel_ref, ks_ref, vs_ref, kw_ref, vw_ref, gate_ref, o_ref,
                *, tq, tk, td, qoff, wpos0, n_sel):
    g = pl.program_id(1)
    q0 = qoff + pl.program_id(2) * tq
    rows = NSA_GROUP * tq
    q2 = q_ref[...].reshape(rows, HEAD_DIM)
    heads = [NSA_GROUP * g + hh for hh in range(NSA_GROUP)]
    pos_col = q0 + lax.broadcasted_iota(jnp.int32, (tq, 1), 0)

    def per_head(fn):
        return jnp.concatenate([fn(hh) for hh in range(NSA_GROUP)], axis=0)

    ncp = kc_ref.shape[2]
    blk_end = lax.broadcasted_iota(jnp.int32, (tq, ncp), 1) * CMP_STRIDE + (CMP_LEN - 1)
    d_cmp = pos_col - blk_end
    ok_cmp = d_cmp >= 0
    n_cmp = jnp.maximum(d_cmp, 0)
    s = lax.dot_general(q2, kc_ref[0, 0], NT_DIMS, preferred_element_type=F32)
    p_heads = []
    for hh in range(NSA_GROUP):
        sh = s[hh * tq:(hh + 1) * tq] + _rel_bias(n_cmp, tbl_ref, heads[hh])
        sh = jnp.where(ok_cmp, sh, NEG)
        e = jnp.where(ok_cmp, jnp.exp(sh - jnp.max(sh, axis=-1, keepdims=True)), 0.0)
        p_heads.append(e / jnp.maximum(jnp.sum(e, axis=-1, keepdims=True), 1e-30))
    o_cmp = jnp.dot(jnp.concatenate(p_heads, axis=0).astype(BF16), vc_ref[0, 0], preferred_element_type=F32)
    imp = jnp.dot(sum(p_heads), msel_ref[...], preferred_element_type=F32,
                  precision=lax.Precision.HIGHEST)

    nj = imp.shape[1]
    jj = lax.broadcasted_iota(jnp.int32, (tq, nj), 1)
    blk_q = pos_col // SEL_BLOCK
    forced = (jj == 0) | (jj == blk_q) | (jj == blk_q - 1)
    score = jnp.where(forced, FORCED_SCORE, jnp.where(jj <= blk_q, imp, -1.0))
    score = jnp.where(jj < n_sel, score, -2.0)
    rank = jnp.zeros((tq, nj), F32)
    for other in range(n_sel):
        col = jnp.broadcast_to(score[:, other:other + 1], score.shape)
        tie = jnp.where(jj > other, 1.0, 0.0)
        rank = rank + jnp.where(col > score, 1.0, jnp.where(col == score, tie, 0.0))
    sel = jnp.where((rank < SEL_TOPK) & (jj < n_sel), 1.0, 0.0).astype(BF16)

    def bias_and_mask(start, size, kpos0, near, window):
        def near_fn():
            kpos = kpos0 + start + lax.broadcasted_iota(jnp.int32, (tq, size), 1)
            d = pos_col - kpos
            ok = (d >= 0) & (d < WINDOW) if window else d >= 0
            n = jnp.maximum(d, 0)
            return per_head(lambda hh: jnp.where(ok, _rel_bias(n, tbl_ref, heads[hh]), NEG))

        def far_fn():
            return per_head(lambda hh: jnp.full((tq, size), tbl_ref[REL_BUCKETS - 1, heads[hh]], F32))

        if near is True:
            return near_fn()
        return lax.cond(near, near_fn, far_fn)

    def slc_tile(state, start, size, near):
        blk_of_key = (start + lax.broadcasted_iota(jnp.int32, (nj, size), 1)) // SEL_BLOCK
        expand = jnp.where(blk_of_key == lax.broadcasted_iota(jnp.int32, (nj, size), 0), 1.0, 0.0).astype(BF16)
        picked = jnp.dot(sel, expand, preferred_element_type=F32)
        drop = (picked - 1.0) * (-NEG)
        sc = lax.dot_general(q2, ks_ref[0, 0, pl.ds(start, size), :], NT_DIMS, preferred_element_type=F32)
        sc = sc + bias_and_mask(start, size, 0, near, False) + jnp.concatenate([drop] * NSA_GROUP, axis=0)
        return _online_update(state, sc, vs_ref[0, 0, pl.ds(start, size), :])

    def slc_body(kt, state):
        start = pl.multiple_of(kt * tk, tk)
        return slc_tile(state, start, tk, q0 - (start + tk - 1) < FAR_DIST)

    state = lax.fori_loop(0, q0 // tk, slc_body, _init_state(rows))
    _, l, acc = slc_tile(state, pl.multiple_of(q0, LANE), td, True)
    o_slc = acc / l

    def win_body(wt, state):
        start = pl.multiple_of(wt * LANE, LANE)
        near = q0 - (wpos0 + start + LANE - 1) < FAR_DIST
        sc = lax.dot_general(q2, kw_ref[0, 0, pl.ds(start, LANE), :], NT_DIMS, preferred_element_type=F32)
        kpos = wpos0 + start + lax.broadcasted_iota(jnp.int32, (tq, LANE), 1)
        d = pos_col - kpos
        edge = jnp.where((d >= 0) & (d < WINDOW), 0.0, NEG)
        sc = sc + bias_and_mask(start, LANE, wpos0, near, True) + jnp.concatenate([edge] * NSA_GROUP, axis=0)
        return _online_update(state, sc, vw_ref[0, 0, pl.ds(start, LANE), :])

    w_lo = jnp.maximum(q0 - wpos0 - WINDOW, 0) // LANE
    w_hi = (q0 + tq - 1 - wpos0) // LANE
    _, l, acc = lax.fori_loop(w_lo, w_hi + 1, win_body, _init_state(rows))
    o_win = acc / l

    gates = gate_ref[...]
    lane = lax.broadcasted_iota(jnp.int32, gates.shape, 1)

    def gate(hh, branch):
        return jnp.sum(jnp.where(lane == heads[hh] * 3 + branch, gates, 0.0), axis=-1, keepdims=True)

    for hh in range(NSA_GROUP):
        sl = slice(hh * tq, (hh + 1) * tq)
        o_ref[:, hh * HEAD_DIM:(hh + 1) * HEAD_DIM] = (
            gate(hh, 0) * o_cmp[sl] + gate(hh, 1) * o_slc[sl] + gate(hh, 2) * o_win[sl])


def _nsa_attention(rel_table, qarr, kcv, msel, kv, kwin, gates, *, ks_head0, vs_head0, kw_head0, vw_head0,
                   nb, nq, tq, tk, td, qoff, wpos0, n_sel):
    lk = kv.shape[2]
    lw = kwin.shape[2]
    ncp = kcv.shape[2]
    nj = msel.shape[1]
    kern = functools.partial(_nsa_kernel, tq=tq, tk=tk, td=td, qoff=qoff, wpos0=wpos0, n_sel=n_sel)
    g_w = NSA_GROUP * HEAD_DIM
    return pl.pallas_call(
        kern,
        out_shape=jax.ShapeDtypeStruct((nb * nq * tq, NSA_HEADS * HEAD_DIM), F32),
        grid=(nb, NSA_KV_HEADS, nq),
        in_specs=[
            pl.BlockSpec(memory_space=pltpu.SMEM),
            pl.BlockSpec((NSA_GROUP, tq, HEAD_DIM), lambda b, g, t: (g, b * nq + t, 0)),
            pl.BlockSpec((1, 1, ncp, HEAD_DIM), lambda b, g, t: (g, b, 0, 0)),
            pl.BlockSpec((1, 1, ncp, HEAD_DIM), lambda b, g, t: (NSA_KV_HEADS + g, b, 0, 0)),
            pl.BlockSpec((ncp, nj), lambda b, g, t: (0, 0)),
            pl.BlockSpec((1, 1, lk, HEAD_DIM), lambda b, g, t: (ks_head0 + g, b, 0, 0)),
            pl.BlockSpec((1, 1, lk, HEAD_DIM), lambda b, g, t: (vs_head0 + g, b, 0, 0)),
            pl.BlockSpec((1, 1, lw, HEAD_DIM), lambda b, g, t: (kw_head0 + g, b, 0, 0)),
            pl.BlockSpec((1, 1, lw, HEAD_DIM), lambda b, g, t: (vw_head0 + g, b, 0, 0)),
            pl.BlockSpec((tq, LANE), lambda b, g, t: (b * nq + t, 0)),
        ],
        out_specs=pl.BlockSpec((tq, g_w), lambda b, g, t: (b * nq + t, g)),
        compiler_params=_cparams(("parallel", "parallel", "arbitrary")),
        name="nsa_attention",
    )(rel_table, qarr, kcv, kcv, msel, kv, kv, kwin, kwin, gates)


def _cmp_to_sel(n_cmp, n_sel, rows, cols):
    c0 = np.arange(n_cmp)[:, None] * CMP_STRIDE
    s0 = np.arange(n_sel)[None, :] * SEL_BLOCK
    inter = np.clip(np.minimum(c0 + CMP_LEN, s0 + SEL_BLOCK) - np.maximum(c0, s0), 0, None)
    m = np.zeros((rows, cols), np.float32)
    m[:n_cmp, :n_sel] = inter / CMP_LEN
    return jnp.asarray(m)


def _transpose_tiles(x):
    n = x.shape[0] // LANE
    xf = x.astype(F32)
    return jnp.concatenate([xf[i * LANE:(i + 1) * LANE].T for i in range(n)], axis=1)


def _untranspose_tiles(xt):
    n = xt.shape[1] // LANE
    return jnp.concatenate([xt[:, i * LANE:(i + 1) * LANE].T for i in range(n)], axis=0)


AUX_ROWS = 16


def _fill_transposed(dst_ref, src_ref):
    def body(k, carry):
        st = pl.multiple_of(k * LANE, LANE)
        dst_ref[:HEAD_DIM, pl.ds(st, LANE)] = src_ref[0, 0, pl.ds(st, LANE), :].astype(F32).T.astype(BF16)
        return carry

    lax.fori_loop(0, src_ref.shape[2] // LANE, body, 0)


def _online_update_t(state, s_t, v_t):
    m, l, acc = state
    m_new = jnp.maximum(m, jnp.max(s_t, axis=0, keepdims=True))
    alpha = jnp.exp(m - m_new)
    p = jnp.exp(s_t - m_new)
    l = alpha * l + jnp.sum(p, axis=0, keepdims=True)
    acc = alpha * acc + jnp.dot(v_t, p.astype(BF16), preferred_element_type=F32)
    return m_new, l, acc


def _init_state_t(cols):
    return (jnp.full((1, cols), NEG, F32), jnp.zeros((1, cols), F32), jnp.zeros((HEAD_DIM, cols), F32))


FOX_TQ = 256
FOX_KEY_BLOCK = 1024


def _split3(x):
    hi = x.astype(BF16)
    r1 = x - hi.astype(F32)
    mid = r1.astype(BF16)
    lo = (r1 - mid.astype(F32)).astype(BF16)
    return hi, mid, lo


def _lane_select3(parts, shape):
    lane = lax.broadcasted_iota(jnp.int32, shape, 1)
    hi, mid, lo = [part.astype(F32) for part in parts]
    return jnp.where(lane == 0, hi, jnp.where(lane == 1, mid, jnp.where(lane == 2, lo, 0.0))).astype(BF16)


def _ones_rows(width):
    row = lax.broadcasted_iota(jnp.int32, (AUX_ROWS, width), 0)
    return jnp.where(row == 0, 1.0, 0.0).astype(BF16)


def _online_update_aug(state, s_t, v_aug, shift=None):
    m, acc = state
    top = jnp.max(s_t, axis=0, keepdims=True)
    m_new = jnp.maximum(m, top if shift is None else top + shift)
    alpha = jnp.exp(m - m_new)
    p = jnp.exp(s_t - (m_new if shift is None else m_new - shift))
    acc = alpha * acc + jnp.dot(v_aug, p.astype(BF16), preferred_element_type=F32)
    return m_new, acc


def _init_state_aug(cols):
    return jnp.full((1, cols), NEG, F32), jnp.zeros((HEAD_DIM + AUX_ROWS, cols), F32)


def _finish_aug(acc):
    return acc[:HEAD_DIM] * (1.0 / acc[HEAD_DIM:HEAD_DIM + 1])


def _fox_prompt_kernel(q_ref, k_ref, v_ref, crow_ref, o_ref, vt_ref, ka_ref, s_ref):
    qt = pl.program_id(2)
    tq = FOX_TQ

    @pl.when(qt == 0)
    def _():
        _fill_transposed(vt_ref, v_ref)
        vt_ref[HEAD_DIM:, :] = _ones_rows(vt_ref.shape[1])

        def body(k, carry):
            st = pl.multiple_of(k * LANE, LANE)
            c_col = jnp.broadcast_to(crow_ref[0, 0, :, pl.ds(st, LANE)], (LANE, LANE)).T
            ka_ref[pl.ds(st, LANE), :HEAD_DIM] = k_ref[0, 0, pl.ds(st, LANE), :]
            ka_ref[pl.ds(st, LANE), HEAD_DIM:] = _lane_select3(_split3(c_col), (LANE, LANE))
            return carry

        lax.fori_loop(0, k_ref.shape[2] // LANE, body, 0)

    q0 = pl.multiple_of(qt * tq, tq)
    q_t = _transpose_tiles(q_ref[0]).astype(BF16)
    row = lax.broadcasted_iota(jnp.int32, (LANE, tq), 0)
    q_aug = jnp.concatenate([q_t, jnp.where(row < 3, -1.0, 0.0).astype(BF16)], axis=0)
    c_q = crow_ref[0, 0, :, pl.ds(q0, tq)]
    blk = FOX_KEY_BLOCK

    last = q0 // blk
    rel = (lax.broadcasted_iota(jnp.int32, (blk, tq), 0) - lax.broadcasted_iota(jnp.int32, (blk, tq), 1))

    def run(n_blocks):
        top = jnp.full((8, tq), NEG, F32)
        for kb in range(n_blocks):
            start = kb * blk
            s = jnp.dot(ka_ref[start:start + blk, :], q_aug, preferred_element_type=F32)
            if kb == n_blocks - 1:
                s = jnp.where(rel <= q0 - start, s, NEG)
            s_ref[start:start + blk, :] = s
            top = jnp.maximum(top, jnp.max(s.reshape(blk // 8, 8, tq), axis=0))
        m = jnp.max(top, axis=0, keepdims=True) + c_q
        shift = m - c_q
        acc = jnp.zeros((HEAD_DIM + AUX_ROWS, tq), F32)
        for kb in range(n_blocks):
            start = kb * blk
            p = jnp.exp(s_ref[start:start + blk, :] - shift).astype(BF16)
            acc = acc + jnp.dot(vt_ref[:, start:start + blk], p, preferred_element_type=F32)
        return acc

    n_max = k_ref.shape[2] // blk
    acc = lax.switch(last, [functools.partial(run, n) for n in range(1, n_max + 1)])
    o_ref[...] = _untranspose_tiles(_finish_aug(acc))


def _fox_prompt(hm, hm4, crow, nb, seq):
    nq = seq // FOX_TQ
    return pl.pallas_call(
        _fox_prompt_kernel,
        out_shape=jax.ShapeDtypeStruct((nb * seq, FOX_HEADS * HEAD_DIM), F32),
        grid=(nb, FOX_HEADS, nq),
        in_specs=[
            pl.BlockSpec((1, FOX_TQ, HEAD_DIM), lambda b, h, t: (HM_Q_FOX + h, b * nq + t, 0)),
            pl.BlockSpec((1, 1, seq, HEAD_DIM), lambda b, h, t: (HM_K_FOX + h, b, 0, 0)),
            pl.BlockSpec((1, 1, seq, HEAD_DIM), lambda b, h, t: (HM_V_FOX + h, b, 0, 0)),
            pl.BlockSpec((1, 1, 1, seq), lambda b, h, t: (b, h, 0, 0)),
        ],
        out_specs=pl.BlockSpec((FOX_TQ, HEAD_DIM), lambda b, h, t: (b * nq + t, h)),
        scratch_shapes=[pltpu.VMEM((HEAD_DIM + AUX_ROWS, seq), BF16), pltpu.VMEM((seq, 2 * HEAD_DIM), BF16),
                        pltpu.VMEM((seq, FOX_TQ), F32)],
        compiler_params=_cparams(("parallel", "parallel", "arbitrary")),
        name="fox_prompt",
    )(hm, hm4, hm4, crow)


NSA_TQ = LANE
NSA_COLS = NSA_GROUP * NSA_TQ
WIN_TILES = WINDOW // NSA_TQ + 1
SLC_BLOCK_TILES = 4
SLC_MASK_ROWS = SLC_BLOCK_TILES * NSA_TQ // SEL_BLOCK


def _nsa_prompt_kernel(tbl_ref, q_ref, kc_ref, vc_ref, mselt_ref, ks_ref, vs_ref, kw_ref, vw_ref, gate_ref, o_ref,
                       vst_ref, vwt_ref, vct_ref, wb_ref, pc_ref, drop_ref, ksa_ref, wbd_ref, s_ref, *, n_sel, cmp_back):
    g = pl.program_id(1)
    qt = pl.program_id(2)
    tq = NSA_TQ
    cols = NSA_COLS
    heads = [NSA_GROUP * g + hh for hh in range(NSA_GROUP)]
    ncp = kc_ref.shape[2]
    nj = mselt_ref.shape[0]

    @pl.when(qt == 0)
    def _():
        _fill_transposed(vst_ref, vs_ref)
        _fill_transposed(vwt_ref, vw_ref)
        vst_ref[HEAD_DIM:, :] = _ones_rows(vst_ref.shape[1])
        vwt_ref[HEAD_DIM:, :] = _ones_rows(vwt_ref.shape[1])
        vct_ref[...] = _transpose_tiles(vc_ref[0, 0]).astype(BF16)

        def fill_keys(k, carry):
            st = pl.multiple_of(k * tq, tq)
            lane = lax.broadcasted_iota(jnp.int32, (tq, tq), 1)
            blk_in_step = (k % SLC_BLOCK_TILES) * (tq // SEL_BLOCK) + lax.broadcasted_iota(jnp.int32, (tq, tq), 0) // SEL_BLOCK
            extra = (lane == blk_in_step) | ((lane >= SLC_MASK_ROWS) & (lane < SLC_MASK_ROWS + 3))
            ksa_ref[pl.ds(st, tq), :HEAD_DIM] = ks_ref[0, 0, pl.ds(st, tq), :]
            ksa_ref[pl.ds(st, tq), HEAD_DIM:] = jnp.where(extra, 1.0, 0.0).astype(BF16)
            return carry

        lax.fori_loop(0, ks_ref.shape[2] // tq, fill_keys, 0)
        key = lax.broadcasted_iota(jnp.int32, (tq, tq), 0)
        qry = lax.broadcasted_iota(jnp.int32, (tq, tq), 1)
        blk = lax.broadcasted_iota(jnp.int32, (pc_ref.shape[0], tq), 0) - cmp_back
        d_cmp = lax.broadcasted_iota(jnp.int32, (pc_ref.shape[0], tq), 1) - (blk * CMP_STRIDE + (CMP_LEN - 1))
        for hh in range(NSA_GROUP):
            sl = slice(hh * tq, (hh + 1) * tq)
            far = tbl_ref[REL_BUCKETS - 1, heads[hh]]
            for delta in range(2):
                d = delta * tq + qry - key
                wb_ref[delta, :, sl] = jnp.where(d >= 0, _rel_bias(jnp.maximum(d, 0), tbl_ref, heads[hh]), NEG)
            for delta in range(2, WIN_TILES - 1):
                wb_ref[delta, :, sl] = jnp.full((tq, tq), far, F32)
            wb_ref[WIN_TILES - 1, :, sl] = jnp.where(qry < key, far, NEG)
            wb_ref[WIN_TILES, :, sl] = jnp.full((tq, tq), NEG, F32)
            for delta in range(2):
                wbd_ref[delta, :, sl] = wb_ref[delta, :, sl] - far
            wbd_ref[2, :, sl] = jnp.zeros((tq, tq), F32)
            wbd_ref[3, :, sl] = jnp.full((tq, tq), NEG, F32)
            pc_ref[:, sl] = jnp.where(d_cmp >= 0, _rel_bias(jnp.maximum(d_cmp, 0), tbl_ref, heads[hh]), NEG)

    q_t = jnp.concatenate([q_ref[hh].astype(F32).T for hh in range(NSA_GROUP)], axis=1).astype(BF16)

    off = pl.multiple_of(cmp_back - qt * (tq // CMP_STRIDE), 8)
    s_t = jnp.dot(kc_ref[0, 0], q_t, preferred_element_type=F32) + pc_ref[pl.ds(off, ncp), :]
    m = jnp.max(s_t, axis=0, keepdims=True)
    e = jnp.exp(s_t - m)
    inv = jnp.where(m > 0.5 * NEG, 1.0 / jnp.sum(e, axis=0, keepdims=True), 0.0)
    p_t = e * inv
    o_cmp = jnp.dot(vct_ref[...], p_t.astype(BF16), preferred_element_type=F32)
    p_sum = p_t[:, :tq]
    for hh in range(1, NSA_GROUP):
        p_sum = p_sum + p_t[:, hh * tq:(hh + 1) * tq]
    imp = jnp.dot(mselt_ref[...], p_sum, preferred_element_type=F32, precision=lax.Precision.HIGHEST)

    jrow = lax.broadcasted_iota(jnp.int32, (nj, tq), 0)
    pos = qt * tq + lax.broadcasted_iota(jnp.int32, (nj, tq), 1)
    blk_q = jnp.right_shift(pos, int(math.log2(SEL_BLOCK)))
    forced = (jrow == 0) | (jrow == blk_q) | (jrow == blk_q - 1)
    score = jnp.where(forced, FORCED_SCORE, jnp.where(jrow <= blk_q, imp, -1.0))
    score = jnp.where(jrow < n_sel, score, -2.0)
    ranks = []
    for r in range(nj // 8):
        mine = score[r * 8:(r + 1) * 8]
        jmine = jrow[r * 8:(r + 1) * 8]
        rank = jnp.zeros((8, tq), F32)
        for other in range(n_sel):
            row = jnp.broadcast_to(score[other:other + 1], (8, tq))
            if other < r * 8:
                beats = row >= mine
            elif other >= (r + 1) * 8:
                beats = row > mine
            else:
                beats = (row > mine) | ((row == mine) & (jmine > other))
            rank = rank + jnp.where(beats, 1.0, 0.0)
        ranks.append(rank)
    rank = jnp.concatenate(ranks, axis=0)
    drop = jnp.where((rank < SEL_TOPK) & (jrow < n_sel), 0.0, NEG)
    drop_ref[...] = jnp.concatenate([drop] * NSA_GROUP, axis=1)

    blk = SLC_BLOCK_TILES * tq
    col = lax.broadcasted_iota(jnp.int32, (8, cols), 1)
    far_row = jnp.full((8, cols), tbl_ref[REL_BUCKETS - 1, heads[NSA_GROUP - 1]], F32)
    for hh in range(NSA_GROUP - 1):
        far_row = jnp.where(col // tq == hh, tbl_ref[REL_BUCKETS - 1, heads[hh]], far_row)
    row8 = lax.broadcasted_iota(jnp.int32, (8, cols), 0)
    parts = [part.astype(F32) for part in _split3(far_row)]
    far_rows = jnp.where(row8 == 0, parts[0], jnp.where(row8 == 1, parts[1], jnp.where(row8 == 2, parts[2], 0.0)))
    pad_rows = jnp.zeros((HEAD_DIM - SLC_MASK_ROWS - 8, cols), BF16)

    last = qt // SLC_BLOCK_TILES

    def slc_run(n_blocks):
        top = jnp.full((8, cols), NEG, F32)
        for kb in range(n_blocks):
            start = kb * blk
            masks = drop_ref[kb * SLC_MASK_ROWS:(kb + 1) * SLC_MASK_ROWS, :]
            extra = jnp.concatenate([masks, far_rows], axis=0).astype(BF16)
            q_aug = jnp.concatenate([q_t, extra, pad_rows], axis=0)
            s = jnp.dot(ksa_ref[start:start + blk, :], q_aug, preferred_element_type=F32)
            if kb >= n_blocks - 2:
                terms = []
                for i in range(SLC_BLOCK_TILES):
                    delta = qt - (kb * SLC_BLOCK_TILES + i)
                    terms.append(wbd_ref[jnp.where(delta < 0, 3, jnp.minimum(delta, 2))])
                s = s + jnp.concatenate(terms, axis=0)
            s_ref[start:start + blk, :] = s
            top = jnp.maximum(top, jnp.max(s.reshape(blk // 8, 8, cols), axis=0))
        m = jnp.max(top, axis=0, keepdims=True)
        acc = jnp.zeros((HEAD_DIM + AUX_ROWS, cols), F32)
        for kb in range(n_blocks):
            start = kb * blk
            p = jnp.exp(s_ref[start:start + blk, :] - m).astype(BF16)
            acc = acc + jnp.dot(vst_ref[:, start:start + blk], p, preferred_element_type=F32)
        return acc

    n_max = ks_ref.shape[2] // blk
    o_slc = _finish_aug(lax.switch(last, [functools.partial(slc_run, n) for n in range(1, n_max + 1)]))

    w0 = jnp.maximum(qt - (WIN_TILES - 1), 0)
    start = pl.multiple_of(w0 * tq, tq)
    span = WIN_TILES * tq
    s = jnp.dot(kw_ref[0, 0, pl.ds(start, span), :], q_t, preferred_element_type=F32)
    terms = []
    for i in range(WIN_TILES):
        delta = qt - (w0 + i)
        terms.append(wb_ref[jnp.where(delta < 0, WIN_TILES, delta)])
    s = s + jnp.concatenate(terms, axis=0)
    p = jnp.exp(s - jnp.max(s, axis=0, keepdims=True))
    o_win = _finish_aug(jnp.dot(vwt_ref[:, pl.ds(start, span)], p.astype(BF16), preferred_element_type=F32))

    gate = gate_ref[0, 0, 0]
    o_t = gate[0:1] * o_cmp + gate[1:2] * o_slc + gate[2:3] * o_win
    for hh in range(NSA_GROUP):
        o_ref[:, hh * HEAD_DIM:(hh + 1) * HEAD_DIM] = o_t[:, hh * tq:(hh + 1) * tq].T


def _nsa_prompt(rel_table, hm, hm4, kcv, mselt, gates_t, nb, seq, n_sel):
    tq = NSA_TQ
    nq = seq // tq
    ncp = kcv.shape[2]
    nj = mselt.shape[0]
    cmp_back = (nq - 1) * (tq // CMP_STRIDE)
    kern = functools.partial(_nsa_prompt_kernel, n_sel=n_sel, cmp_back=cmp_back)
    kv_spec = lambda head0: pl.BlockSpec((1, 1, seq, HEAD_DIM), lambda b, g, t: (head0 + g, b, 0, 0))
    return pl.pallas_call(
        kern,
        out_shape=jax.ShapeDtypeStruct((nb * seq, NSA_HEADS * HEAD_DIM), F32),
        grid=(nb, NSA_KV_HEADS, nq),
        in_specs=[
            pl.BlockSpec(memory_space=pltpu.SMEM),
            pl.BlockSpec((NSA_GROUP, tq, HEAD_DIM), lambda b, g, t: (g, b * nq + t, 0)),
            pl.BlockSpec((1, 1, ncp, HEAD_DIM), lambda b, g, t: (g, b, 0, 0)),
            pl.BlockSpec((1, 1, ncp, HEAD_DIM), lambda b, g, t: (NSA_KV_HEADS + g, b, 0, 0)),
            pl.BlockSpec((nj, ncp), lambda b, g, t: (0, 0)),
            kv_spec(HM_K_SLC), kv_spec(HM_V_SLC), kv_spec(HM_K_WIN), kv_spec(HM_V_WIN),
            pl.BlockSpec((1, 1, 1, 8, NSA_COLS), lambda b, g, t: (b, g, t, 0, 0)),
        ],
        out_specs=pl.BlockSpec((tq, NSA_GROUP * HEAD_DIM), lambda b, g, t: (b * nq + t, g)),
        scratch_shapes=[
            pltpu.VMEM((HEAD_DIM + AUX_ROWS, seq), BF16),
            pltpu.VMEM((HEAD_DIM + AUX_ROWS, seq), BF16),
            pltpu.VMEM((HEAD_DIM, ncp), BF16),
            pltpu.VMEM((WIN_TILES + 1, tq, NSA_COLS), F32),
            pltpu.VMEM((cmp_back + ncp, NSA_COLS), F32),
            pltpu.VMEM((nj, NSA_COLS), F32),
            pltpu.VMEM((seq, 2 * HEAD_DIM), BF16),
            pltpu.VMEM((4, tq, NSA_COLS), F32),
            pltpu.VMEM((seq, NSA_COLS), F32),
        ],
        compiler_params=_cparams(("parallel", "parallel", "arbitrary")),
        name="nsa_prompt",
    )(rel_table, hm, kcv, kcv, mselt, hm4, hm4, hm4, hm4, gates_t)


CHUNK_ROWS = 512


def _chunkify_kernel(x_ref, o_ref, *, n_cols):
    n = o_ref.shape[2]
    for c in range(o_ref.shape[0]):
        for s in range(CMP_STRIDE):
            o_ref[c, 0, :, s * HEAD_DIM:(s + 1) * HEAD_DIM] = (
                x_ref[0, pl.ds(s * n_cols + c, n, stride=CMP_STRIDE * n_cols), :].astype(BF16))


def _chunkify(rows3, n_cols, n_heads, seq):
    nb = rows3.shape[0]
    tr = _largest_tile(seq, (CHUNK_ROWS, 256))
    return pl.pallas_call(
        functools.partial(_chunkify_kernel, n_cols=n_cols),
        out_shape=jax.ShapeDtypeStruct((n_heads, nb, seq // CMP_STRIDE, CMP_STRIDE * HEAD_DIM), BF16),
        grid=(nb, seq // tr),
        in_specs=[pl.BlockSpec((1, tr * n_cols, HEAD_DIM), lambda b, i: (b, i, 0))],
        out_specs=pl.BlockSpec((n_heads, 1, tr // CMP_STRIDE, CMP_STRIDE * HEAD_DIM), lambda b, i: (0, b, i, 0)),
        compiler_params=_cparams(("parallel", "parallel")),
        name="chunkify",
    )(rows3)


REGROUP_PAGES = 4


N_CACHE_COLS = 16
N_CMP_COLS = 2 * NSA_KV_HEADS


def _regroup_kernel(pt_ref, *refs, n_steps):
    del pt_ref
    npg = REGROUP_PAGES
    nsa_in, lf_in = refs[:npg], refs[npg:2 * npg]
    xc_out, slc_out, lf_out = refs[2 * npg:]
    is_tail = pl.program_id(1) >= n_steps
    chunks = PAGE_SIZE // CMP_STRIDE
    chunk_stride = CMP_STRIDE * N_CACHE_COLS

    @pl.when(is_tail)
    def _():
        slc_out[...] = jnp.zeros(slc_out.shape, slc_out.dtype)
        lf_out[...] = jnp.zeros(lf_out.shape, lf_out.dtype)

    @pl.when(jnp.logical_not(is_tail))
    def _():
        for p in range(npg):
            rows = slice(p * PAGE_SIZE, (p + 1) * PAGE_SIZE)
            for c in range(N_CACHE_COLS - N_CMP_COLS):
                slc_out[c, 0, rows, :] = nsa_in[p][0, pl.ds(N_CMP_COLS + c, PAGE_SIZE, stride=N_CACHE_COLS), :].astype(BF16)
            lf_out[0, rows, :] = lf_in[p][0]
        for pair in range(npg // 2):
            for c in range(N_CMP_COLS):
                for s in range(CMP_STRIDE):
                    first = s * N_CACHE_COLS + c
                    both = [nsa_in[2 * pair + i][0, pl.ds(first, chunks, stride=chunk_stride), :] for i in range(2)]
                    xc_out[c, 0, pair * 2 * chunks:(pair + 1) * 2 * chunks, s * HEAD_DIM:(s + 1) * HEAD_DIM] = (
                        jnp.concatenate(both, axis=0).astype(BF16))


def _regroup(page_table, cache_nsa, cache_logf, lk):
    nb, n_pages = page_table.shape
    npg = REGROUP_PAGES
    n_steps = n_pages // npg
    rows = npg * PAGE_SIZE
    last = n_steps - 1
    n_slc = N_CACHE_COLS - N_CMP_COLS
    chunk_w = CMP_STRIDE * HEAD_DIM

    def page_map(p):
        return lambda b, s, pt: (pt[b, jnp.minimum(s, last) * npg + p], 0, 0)

    def specs(arr):
        return [pl.BlockSpec((1,) + arr.shape[1:], page_map(p)) for p in range(npg)]

    assert (lk - n_pages * PAGE_SIZE) % rows == 0
    n_tail = (lk - n_pages * PAGE_SIZE) // rows
    grid_spec = pltpu.PrefetchScalarGridSpec(
        num_scalar_prefetch=1,
        grid=(nb, n_steps + n_tail),
        in_specs=specs(cache_nsa) + specs(cache_logf),
        out_specs=(
            pl.BlockSpec((N_CMP_COLS, 1, rows // CMP_STRIDE, chunk_w), lambda b, s, pt: (0, b, jnp.minimum(s, last), 0)),
            pl.BlockSpec((n_slc, 1, rows, HEAD_DIM), lambda b, s, pt: (0, b, s, 0)),
            pl.BlockSpec((1, rows, cache_logf.shape[2]), lambda b, s, pt: (b, s, 0)),
        ),
    )
    return pl.pallas_call(
        functools.partial(_regroup_kernel, n_steps=n_steps),
        out_shape=(
            jax.ShapeDtypeStruct((N_CMP_COLS, nb, n_pages * PAGE_SIZE // CMP_STRIDE, chunk_w), BF16),
            jax.ShapeDtypeStruct((n_slc, nb, lk, HEAD_DIM), BF16),
            jax.ShapeDtypeStruct((nb, lk, cache_logf.shape[2]), F32),
        ),
        grid_spec=grid_spec,
        compiler_params=_cparams(("parallel", "arbitrary")),
        name="cache_regroup",
    )(page_table, *([cache_nsa] * npg), *([cache_logf] * npg))


FOX_DEC_PAGES = 8


def _fox_decode_kernel(pt_ref, *refs, dseq):
    del pt_ref
    npg = FOX_DEC_PAGES
    pages = refs[:npg]
    qbd_ref, ccol_ref, cq_ref, knew_ref, vnew_ref, o_ref, m_ref, acc_ref = refs[npg:]
    step = pl.program_id(1)
    is_tail = step == pl.num_programs(1) - 1
    width = FOX_HEADS * HEAD_DIM

    @pl.when(step == 0)
    def _():
        m_ref[...] = jnp.full(m_ref.shape, NEG, F32)
        acc_ref[...] = jnp.zeros(acc_ref.shape, F32)

    def fold(k_blk, v_blk, bias):
        n = k_blk.shape[0]
        s_t = jnp.dot(k_blk, qbd_ref[0], preferred_element_type=F32) + bias
        m_new = jnp.maximum(m_ref[...], jnp.max(s_t, axis=0, keepdims=True))
        alpha = jnp.exp(m_ref[...] - m_new)
        p = jnp.exp(s_t - m_new)
        p_t = jnp.concatenate([p[i * LANE:(i + 1) * LANE].T for i in range(n // LANE)], axis=1).astype(BF16)
        v_aug = jnp.concatenate([v_blk, jnp.ones((n, LANE), BF16)], axis=1)
        upd = jnp.dot(p_t, v_aug, preferred_element_type=F32)
        alpha_col = jnp.broadcast_to(alpha, (LANE, LANE)).T
        acc_ref[...] = acc_ref[...] * jnp.concatenate([alpha_col] * (width // LANE + 1), axis=1) + upd
        m_ref[...] = m_new

    @pl.when(jnp.logical_not(is_tail))
    def _():
        def heads_of(p, first):
            cols = [pages[p][0, pl.ds(first + h, PAGE_SIZE, stride=N_CACHE_COLS), :] for h in range(FOX_HEADS)]
            return jnp.concatenate(cols, axis=1).astype(BF16)

        k_blk = jnp.concatenate([heads_of(p, 0) for p in range(npg)], axis=0)
        v_blk = jnp.concatenate([heads_of(p, FOX_HEADS) for p in range(npg)], axis=0)
        fold(k_blk, v_blk, cq_ref[0] - ccol_ref[0])

    @pl.when(is_tail)
    def _():
        row = lax.broadcasted_iota(jnp.int32, (LANE, LANE), 0)
        t_of_col = lax.broadcasted_iota(jnp.int32, (LANE, LANE), 1) % dseq
        bias = jnp.where(row <= t_of_col, cq_ref[0] - ccol_ref[0, :LANE, :], NEG)
        fold(knew_ref[0], vnew_ref[0], bias)
        acc = acc_ref[...]
        inv = 1.0 / acc[:, width:]
        o_ref[0] = acc[:, :width] * jnp.concatenate([inv] * (width // LANE), axis=1)


def _fox_decode(page_table, cache_fox, qbd, ccols, cq, knew, vnew, dseq):
    nb, n_pages = page_table.shape
    npg = FOX_DEC_PAGES
    n_steps = n_pages // npg
    last = n_steps - 1
    width = FOX_HEADS * HEAD_DIM

    def page_map(p):
        return lambda b, s, pt: (pt[b, jnp.minimum(s, last) * npg + p], 0, 0)

    per_b = lambda shape: pl.BlockSpec((1,) + shape, lambda b, s, pt: (b, 0, 0))
    grid_spec = pltpu.PrefetchScalarGridSpec(
        num_scalar_prefetch=1,
        grid=(nb, n_steps + 1),
        in_specs=[pl.BlockSpec((1,) + cache_fox.shape[1:], page_map(p)) for p in range(npg)] + [
            per_b((width, LANE)),
            pl.BlockSpec((1, npg * PAGE_SIZE, LANE), lambda b, s, pt: (b, s, 0)),
            per_b((1, LANE)),
            per_b((LANE, width)),
            per_b((LANE, width)),
        ],
        out_specs=per_b((LANE, width)),
        scratch_shapes=[pltpu.VMEM((1, LANE), F32), pltpu.VMEM((LANE, width + LANE), F32)],
    )
    return pl.pallas_call(
        functools.partial(_fox_decode_kernel, dseq=dseq),
        out_shape=jax.ShapeDtypeStruct((nb, LANE, width), F32),
        grid_spec=grid_spec,
        compiler_params=_cparams(("parallel", "arbitrary")),
        name="fox_decode",
    )(page_table, *([cache_fox] * npg), qbd, ccols, cq, knew, vnew)


NSA_DEC_KEYS = 2048


def _lane_transpose(p):
    return jnp.concatenate([p[i * LANE:(i + 1) * LANE].T for i in range(p.shape[0] // LANE)], axis=1).astype(BF16)


def _nsa_decode_kernel(qbd_ref, kc_ref, vc_ref, mselt_ref, pair_ref, tcol_ref, ks_ref, vs_ref, kw_ref, vw_ref,
                       gate_ref, o_ref, drop_ref, m_ref, acc_ref, ocmp_ref, *, dseq, past, n_sel):
    step = pl.program_id(1)
    n_tiles = pl.num_programs(1)
    width = NSA_KV_HEADS * HEAD_DIM
    qbd = qbd_ref[0]
    col = lax.broadcasted_iota(jnp.int32, (1, LANE), 1)
    q_pos = past + col % dseq

    def side_by_side(ref, rows=None):
        parts = [ref[g, 0] if rows is None else ref[g, 0, rows, :] for g in range(NSA_KV_HEADS)]
        return jnp.concatenate(parts, axis=1)

    def rel_bias_cols(dist):
        out = jnp.broadcast_to(tcol_ref[0:1, :], dist.shape)
        for k, thr in enumerate(BUCKET_THR, start=1):
            out = jnp.where(dist >= thr, tcol_ref[k:k + 1, :], out)
        return out

    def softmax_pv(s_t, v_all):
        m = jnp.max(s_t, axis=0, keepdims=True)
        e = jnp.exp(s_t - m)
        inv = jnp.where(m > 0.5 * NEG, 1.0 / jnp.sum(e, axis=0, keepdims=True), 0.0)
        p = e * inv
        return p, jnp.dot(_lane_transpose(p), v_all, preferred_element_type=F32)

    @pl.when(step == 0)
    def _():
        m_ref[...] = jnp.full(m_ref.shape, NEG, F32)
        acc_ref[...] = jnp.zeros(acc_ref.shape, F32)
        ncp = kc_ref.shape[2]
        blk_end = lax.broadcasted_iota(jnp.int32, (ncp, LANE), 0) * CMP_STRIDE + (CMP_LEN - 1)
        d = q_pos - blk_end
        s_t = jnp.dot(side_by_side(kc_ref), qbd, preferred_element_type=F32)
        s_t = jnp.where(d >= 0, s_t + rel_bias_cols(jnp.maximum(d, 0)), NEG)
        p, ocmp_ref[...] = softmax_pv(s_t, side_by_side(vc_ref))
        imp = jnp.dot(mselt_ref[...], p, preferred_element_type=F32, precision=lax.Precision.HIGHEST)
        imp = jnp.dot(imp, pair_ref[...], preferred_element_type=F32, precision=lax.Precision.HIGHEST)
        nj = imp.shape[0]
        jrow = lax.broadcasted_iota(jnp.int32, (nj, LANE), 0)
        blk_q = jnp.right_shift(q_pos, int(math.log2(SEL_BLOCK)))
        forced = (jrow == 0) | (jrow == blk_q) | (jrow == blk_q - 1)
        score = jnp.where(forced, FORCED_SCORE, jnp.where(jrow <= blk_q, imp, -1.0))
        score = jnp.where(jrow < n_sel, score, -2.0)
        drop_ref[...] = score
        rank = jnp.zeros((nj, LANE), F32)

        def count(other, rank):
            row = jnp.broadcast_to(drop_ref[pl.ds(other, 1), :], (nj, LANE))
            beats = (row > score) | ((row == score) & (jrow > other))
            return rank + jnp.where(beats, 1.0, 0.0)

        rank = lax.fori_loop(0, n_sel, count, rank, unroll=4)
        far = tcol_ref[REL_BUCKETS - 1:REL_BUCKETS, :]
        drop_ref[...] = jnp.where((rank < SEL_TOPK) & (jrow <= blk_q), far, NEG)

    tile = ks_ref.shape[2]
    per_tile = tile // SEL_BLOCK
    start = step * tile
    first_blk = pl.multiple_of(step * per_tile, 8)
    rows = [jnp.broadcast_to(drop_ref[pl.ds(first_blk + i, 1), :], (SEL_BLOCK, LANE)) for i in range(per_tile)]
    s_t = jnp.dot(side_by_side(ks_ref), qbd, preferred_element_type=F32) + jnp.concatenate(rows, axis=0)

    def near_fix(s_t):
        def fix(rows_at, s_rows):
            key_pos = start + rows_at + lax.broadcasted_iota(jnp.int32, (LANE, LANE), 0)
            d = q_pos - key_pos
            far = tcol_ref[REL_BUCKETS - 1:REL_BUCKETS, :]
            return jnp.where(d >= 0, s_rows + (rel_bias_cols(jnp.maximum(d, 0)) - far), NEG)

        head = fix(0, s_t[:LANE])
        tail = fix(tile - LANE, s_t[tile - LANE:])
        return jnp.concatenate([head, s_t[LANE:tile - LANE], tail], axis=0)

    is_near = (start + tile > past - FAR_DIST)
    s_t = lax.cond(is_near, near_fix, lambda s: s, s_t)
    m_new = jnp.maximum(m_ref[...], jnp.max(s_t, axis=0, keepdims=True))
    alpha = jnp.exp(m_ref[...] - m_new)
    p = jnp.exp(s_t - m_new)
    v_aug = jnp.concatenate([side_by_side(vs_ref), jnp.ones((tile, LANE), BF16)], axis=1)
    upd = jnp.dot(_lane_transpose(p), v_aug, preferred_element_type=F32)
    alpha_col = jnp.broadcast_to(alpha, (LANE, LANE)).T
    acc_ref[...] = acc_ref[...] * jnp.concatenate([alpha_col] * (width // LANE + 1), axis=1) + upd
    m_ref[...] = m_new

    @pl.when(step == n_tiles - 1)
    def _():
        acc = acc_ref[...]
        o_slc = acc[:, :width] * jnp.concatenate([1.0 / acc[:, width:]] * (width // LANE), axis=1)
        span = kw_ref.shape[2]
        key_pos = (past - WINDOW) + lax.broadcasted_iota(jnp.int32, (span, LANE), 0)
        d = q_pos - key_pos
        s_w = jnp.dot(side_by_side(kw_ref), qbd, preferred_element_type=F32)
        s_w = jnp.where((d >= 0) & (d < WINDOW), s_w + rel_bias_cols(jnp.maximum(d, 0)), NEG)
        _, o_win = softmax_pv(s_w, side_by_side(vw_ref))
        g = gate_ref[0]
        tile4 = lambda a: jnp.concatenate([a] * (width // LANE), axis=1)
        o_ref[0] = tile4(g[0]) * ocmp_ref[...] + tile4(g[1]) * o_slc + tile4(g[2]) * o_win


def _nsa_decode(qbd, kcv, mselt, pair, tcols, nsa_dec, win_dec, gcols, *, dseq, past, n_sel):
    nb = qbd.shape[0]
    ncp = kcv.shape[2]
    nj = mselt.shape[0]
    span = win_dec.shape[2]
    tile = NSA_DEC_KEYS
    n_tiles = -(-nsa_dec.shape[2] // tile)
    width = NSA_KV_HEADS * HEAD_DIM
    kern = functools.partial(_nsa_decode_kernel, dseq=dseq, past=past, n_sel=n_sel)
    grp = lambda rows, half, tiled: pl.BlockSpec(
        (NSA_KV_HEADS, 1, rows, HEAD_DIM), (lambda b, s: (half, b, s, 0)) if tiled else (lambda b, s: (half, b, 0, 0)))
    const = lambda shape: pl.BlockSpec(shape, lambda b, s: (0,) * len(shape))
    return pl.pallas_call(
        kern,
        out_shape=jax.ShapeDtypeStruct((nb, LANE, width), F32),
        grid=(nb, n_tiles),
        in_specs=[
            pl.BlockSpec((1, width, LANE), lambda b, s: (b, 0, 0)),
            grp(ncp, 0, False), grp(ncp, 1, False),
            const((nj, ncp)), const((LANE, LANE)), const((REL_BUCKETS, LANE)),
            grp(tile, 0, True), grp(tile, 1, True),
            grp(span, 0, False), grp(span, 1, False),
            pl.BlockSpec((1, 3, LANE, LANE), lambda b, s: (b, 0, 0, 0)),
        ],
        out_specs=pl.BlockSpec((1, LANE, width), lambda b, s: (b, 0, 0)),
        scratch_shapes=[
            pltpu.VMEM((nj, LANE), F32),
            pltpu.VMEM((1, LANE), F32),
            pltpu.VMEM((LANE, width + LANE), F32),
            pltpu.VMEM((LANE, width), F32),
        ],
        compiler_params=_cparams(("parallel", "arbitrary")),
        name="nsa_decode",
    )(qbd, kcv, kcv, mselt, pair, tcols, nsa_dec, nsa_dec, win_dec, win_dec, gcols)


def _largest_tile(n, candidates):
    for c in candidates:
        if n % c == 0:
            return c
    raise ValueError(f"no tile in {candidates} divides {n}")


FFN_ROW_TILES = (1024, 512, 256, 128)
ROW_TILES = (512, 256, 128)


def _token_stage_in(x, p):
    m = x.shape[0]
    tm = m if m < ROW_TILES[-1] else _largest_tile(m, ROW_TILES)
    tm_ffn = m if m < FFN_ROW_TILES[-1] else _largest_tile(m, FFN_ROW_TILES)
    x1 = _ffn(x, p["norm_ffn1"], p["wg1"], p["wu1"], p["wd1"], tm_ffn, p["tf"])
    tm_in = m if m < ROW_TILES[-1] else _largest_tile(m, ROW_TILES[1:])
    nsa_rows, win_rows, fox_rows, hm = _inproj(x1, p["norm_mix"], p["w_main"], p["colgain"], tm_in)
    small = _small(x1, p["norm_mix"], p["w_small"], p["b_small"], tm)
    return x1, nsa_rows, win_rows, fox_rows, hm, small


def _token_stage_out(x1, o_nsa, o_fox, p):
    m = x1.shape[0]
    tm = m if m < ROW_TILES[-1] else _largest_tile(m, ROW_TILES[1:])
    tm_ffn = m if m < FFN_ROW_TILES[-1] else _largest_tile(m, FFN_ROW_TILES)
    x2 = _outproj(o_nsa, o_fox, p["out_norm_nsa"], p["out_norm_fox"], p["w_out"], x1, tm)
    return _ffn(x2, p["norm_ffn2"], p["wg2"], p["wu2"], p["wd2"], tm_ffn, p["tf"])


def kernel(x_prompt, x_sample, cache_nsa_kv, cache_fox_kv, cache_fox_logf, state_win_kv, page_table, rel_table, norm_ffn1, ffn1_gate, ffn1_up, ffn1_down, norm_mix, w_in, nsa_gate_bias, fox_forget_bias, q_norm_nsa, k_norm_nsa, q_norm_fox, k_norm_fox, cmp_pos_k, cmp_w1_k, cmp_w2_k, cmp_pos_v, cmp_w1_v, cmp_w2_v, out_norm_nsa, out_norm_fox, w_out, norm_ffn2, ffn2_gate, ffn2_up, ffn2_down):
    depth = w_in.shape[0]
    assert depth == 1, "single-layer trunk"
    nbp, seq, d_model = x_prompt.shape
    nbd, dseq, _ = x_sample.shape
    n_pages = page_table.shape[1]
    past = n_pages * PAGE_SIZE
    d_ff = ffn1_gate.shape[2]
    nsa_w = NSA_HEADS * HEAD_DIM
    kv6_w = 6 * NSA_KV_HEADS * HEAD_DIM
    fox_w = 3 * FOX_HEADS * HEAD_DIM
    off_gate = nsa_w + kv6_w
    off_fox = off_gate + N_GATE_COLS
    off_forget = off_fox + fox_w
    assert w_in.shape[2] == off_forget + FOX_HEADS and d_model == nsa_w + FOX_HEADS * HEAD_DIM
    assert seq % LANE == 0 and seq >= WINDOW and past % LANE == 0 and n_pages % REGROUP_PAGES == 0
    assert dseq <= 16 and state_win_kv.shape[2] == WINDOW
    assert seq % FOX_KEY_BLOCK == 0 and seq % (SLC_BLOCK_TILES * NSA_TQ) == 0 and seq >= WIN_TILES * NSA_TQ

    w0 = w_in[0]
    ones = lambda n: jnp.ones((n,), F32)
    zeros = lambda n: jnp.zeros((n,), F32)
    kn, kvw = NSA_KV_HEADS, NSA_KV_HEADS * HEAD_DIM
    qk_scale = HEAD_DIM ** -0.5
    p = {
        "tf": _largest_tile(d_ff, (512, 256, 128)),
        "norm_ffn1": norm_ffn1[0][None], "norm_mix": norm_mix[0][None], "norm_ffn2": norm_ffn2[0][None],
        "wg1": ffn1_gate[0].astype(BF16), "wu1": ffn1_up[0].astype(BF16), "wd1": ffn1_down[0].astype(BF16),
        "wg2": ffn2_gate[0].astype(BF16), "wu2": ffn2_up[0].astype(BF16), "wd2": ffn2_down[0].astype(BF16),
        "w_main": jnp.concatenate([w0[:, :off_gate], w0[:, off_fox:off_forget]], axis=1).astype(BF16),
        "w_small": jnp.concatenate([w0[:, off_gate:off_fox], w0[:, off_forget:],
                                    jnp.zeros((d_model, LANE - N_GATE_COLS - FOX_HEADS), F32)], axis=1).astype(BF16),
        "b_small": jnp.concatenate([nsa_gate_bias[0].reshape(-1), fox_forget_bias[0],
                                    zeros(LANE - N_GATE_COLS - FOX_HEADS)])[None],
        "colgain": jnp.concatenate([
            jnp.tile(q_norm_nsa[0] * qk_scale, NSA_HEADS), ones(2 * kvw), jnp.tile(k_norm_nsa[0], kn), ones(kvw),
            jnp.tile(k_norm_nsa[0], kn), ones(kvw), jnp.tile(q_norm_fox[0] * qk_scale, FOX_HEADS),
            jnp.tile(k_norm_fox[0], FOX_HEADS), ones(FOX_HEADS * HEAD_DIM)])[None],
        "out_norm_nsa": out_norm_nsa[0][None], "out_norm_fox": out_norm_fox[0][None],
        "w_out": w_out[0].astype(BF16),
    }
    half = CMP_STRIDE * HEAD_DIM

    def cmp_w1(w):
        return jnp.concatenate([w[0, :half], w[0, half:]], axis=1)

    def cmp_pe(pe):
        return jnp.concatenate([pe[0].reshape(CMP_LEN // CMP_STRIDE, half), jnp.zeros((PE_ROWS - CMP_LEN // CMP_STRIDE, half), F32)], axis=0)

    w1cat = jnp.stack([cmp_w1(cmp_w1_k), cmp_w1(cmp_w1_v)]).astype(BF16)
    w2cat = jnp.stack([cmp_w2_k[0], cmp_w2_v[0]]).astype(BF16)
    pecat = jnp.stack([cmp_pe(cmp_pos_k), cmp_pe(cmp_pos_v)]).astype(BF16)
    k_norm_row = k_norm_nsa[0][None]

    mp = nbp * seq
    x1, nsa_rows, win_rows, fox_rows, hm, small = _token_stage_in(x_prompt.reshape(mp, d_model), p)
    hm4 = hm.reshape(N_HEAD_COLS, nbp, seq, HEAD_DIM)

    logf = small[:, N_GATE_COLS:N_GATE_COLS + FOX_HEADS]
    csum = _cumsum(logf.reshape(nbp, seq, FOX_HEADS).transpose(0, 2, 1))
    o_fox = _fox_prompt(hm, hm4, csum[:, :, None, :], nbp, seq)

    n_chunk = seq // CMP_STRIDE
    n_cmp = (seq - CMP_LEN) // CMP_STRIDE + 1
    n_sel = -(-seq // SEL_BLOCK)
    xc = _chunkify(nsa_rows.reshape(nbp, seq * N_CACHE_COLS, HEAD_DIM), N_CACHE_COLS, N_CMP_COLS, seq)
    kcv = _compress(xc, 0, w1cat, w2cat, pecat, k_norm_row, n_chunk)
    mselt = _cmp_to_sel(n_cmp, n_sel, n_chunk, -(-n_sel // 8) * 8).T
    nq = seq // NSA_TQ
    gates_t = small[:, :N_GATE_COLS].reshape(nbp, nq, NSA_TQ, NSA_KV_HEADS, NSA_GROUP, 3)
    gates_t = gates_t.transpose(0, 3, 1, 5, 4, 2).reshape(nbp, NSA_KV_HEADS, nq, 3, NSA_COLS)
    gates_t = jnp.pad(gates_t, ((0, 0), (0, 0), (0, 0), (0, 8 - 3), (0, 0)))
    o_nsa = _nsa_prompt(rel_table, hm, hm4, kcv, mselt, gates_t, nbp, seq, n_sel)
    y_p = _token_stage_out(x1, o_nsa, o_fox, p)

    ms = nbd * dseq
    lk = past + NSA_DEC_KEYS
    assert n_pages % FOX_DEC_PAGES == 0 and past % NSA_DEC_KEYS == 0 and FOX_HEADS * dseq <= LANE
    xs1, nsa_rows_s, win_rows_s, fox_rows_s, hm_s, small_s = _token_stage_in(x_sample.reshape(ms, d_model), p)
    xc_d, nsa_dec, lf_dec = _regroup(
        page_table,
        cache_nsa_kv.reshape(cache_nsa_kv.shape[1], PAGE_SIZE * N_CACHE_COLS, HEAD_DIM),
        cache_fox_logf[0], lk)
    hm_s4 = hm_s.reshape(N_HEAD_COLS, nbd, dseq, HEAD_DIM)
    nsa_dec = lax.dynamic_update_slice(nsa_dec, hm_s4[HM_K_SLC:HM_K_WIN], (0, 0, past, 0))
    logf_s = small_s[:, N_GATE_COLS:N_GATE_COLS + FOX_HEADS].reshape(nbd, dseq, FOX_HEADS)
    lf_dec = lax.dynamic_update_slice(lf_dec, logf_s, (0, past, 0))

    csum_d = _cumsum(lf_dec.transpose(0, 2, 1))
    n_cols = FOX_HEADS * dseq
    lane_pad = lambda a: jnp.pad(a, [(0, 0)] * (a.ndim - 1) + [(0, LANE - n_cols)])
    head_eye = jnp.eye(FOX_HEADS, dtype=BF16)
    qbd = jnp.einsum("hbtd,hg->bhdgt", hm_s4[HM_Q_FOX:HM_Q_FOX + FOX_HEADS], head_eye)
    qbd = lane_pad(qbd.reshape(nbd, FOX_HEADS * HEAD_DIM, n_cols))
    ccols = lane_pad(jnp.repeat(csum_d.transpose(0, 2, 1), dseq, axis=2))
    cq = lane_pad(csum_d[:, :, past:past + dseq].reshape(nbd, 1, n_cols))

    def new_rows(head0):
        rows = hm_s4[head0:head0 + FOX_HEADS].transpose(1, 2, 0, 3).reshape(nbd, dseq, FOX_HEADS * HEAD_DIM)
        return jnp.pad(rows, ((0, 0), (0, LANE - dseq), (0, 0)))

    o_full = _fox_decode(page_table, cache_fox_kv.reshape(cache_fox_kv.shape[1], PAGE_SIZE * N_CACHE_COLS, HEAD_DIM),
                         qbd, ccols, cq, new_rows(HM_K_FOX), new_rows(HM_V_FOX), dseq)
    o_fox_s = jnp.concatenate([o_full[:, h * dseq:(h + 1) * dseq, h * HEAD_DIM:(h + 1) * HEAD_DIM]
                               for h in range(FOX_HEADS)], axis=2).reshape(ms, FOX_HEADS * HEAD_DIM)

    n_chunk_d = past // CMP_STRIDE
    n_cmp_d = (past + dseq - CMP_LEN) // CMP_STRIDE + 1
    n_sel_d = -(-(past + dseq) // SEL_BLOCK)
    assert n_cmp_d + CMP_LEN // CMP_STRIDE - 1 <= n_chunk_d, "compressed blocks must lie in the cached rows"
    kcv_d = _compress(xc_d, 0, w1cat, w2cat, pecat, k_norm_row, n_chunk_d)
    mselt_d = _cmp_to_sel(n_cmp_d, n_sel_d, n_chunk_d, lk // SEL_BLOCK).T
    win_old = state_win_kv[0].transpose(2, 3, 0, 1, 4).reshape(2 * NSA_KV_HEADS, nbd, WINDOW, HEAD_DIM).astype(BF16)
    win_dec = jnp.concatenate([win_old, hm_s4[HM_K_WIN:HM_V_WIN + NSA_KV_HEADS],
                               jnp.zeros((2 * NSA_KV_HEADS, nbd, LANE - dseq, HEAD_DIM), BF16)], axis=2)
    grp_eye = jnp.eye(NSA_KV_HEADS, dtype=BF16)
    q_grp = hm_s4[HM_Q_NSA:HM_Q_NSA + NSA_HEADS].reshape(NSA_KV_HEADS, NSA_GROUP, nbd, dseq, HEAD_DIM)
    qbd_n = jnp.einsum("gjbtd,gk->bgdkjt", q_grp, grp_eye)
    qbd_n = lane_pad(qbd_n.reshape(nbd, NSA_KV_HEADS * HEAD_DIM, n_cols))
    tcols = lane_pad(jnp.repeat(rel_table, dseq, axis=1))
    col_id = np.arange(n_cols)
    same = ((col_id[:, None] // (NSA_GROUP * dseq) == col_id[None, :] // (NSA_GROUP * dseq))
            & (col_id[:, None] % dseq == col_id[None, :] % dseq))
    pair = jnp.asarray(np.pad(same.astype(np.float32), ((0, LANE - n_cols), (0, LANE - n_cols))))
    gcols = small_s[:, :N_GATE_COLS].reshape(nbd, dseq, NSA_HEADS, 3).transpose(0, 3, 2, 1).reshape(nbd, 3, n_cols)
    gcols = jnp.broadcast_to(lane_pad(gcols)[..., None], (nbd, 3, LANE, LANE))
    o_full_n = _nsa_decode(qbd_n, kcv_d, mselt_d, pair, tcols, nsa_dec, win_dec, gcols,
                           dseq=dseq, past=past, n_sel=n_sel_d)
    o_nsa_s = jnp.concatenate(
        [o_full_n[:, h * dseq:(h + 1) * dseq, (h // NSA_GROUP) * HEAD_DIM:(h // NSA_GROUP + 1) * HEAD_DIM]
         for h in range(NSA_HEADS)], axis=2).reshape(ms, NSA_HEADS * HEAD_DIM)
    y_s = _token_stage_out(xs1, o_nsa_s, o_fox_s, p)

    kvh = (NSA_KV_HEADS, HEAD_DIM)
    win_keep = min(WINDOW, seq)
    win_p = win_rows.reshape(nbp, seq, 2, *kvh)[:, seq - win_keep:]
    win_s = jnp.concatenate([state_win_kv[0], win_rows_s.reshape(nbd, dseq, 2, *kvh)], axis=1)[:, dseq:]
    return (
        y_p.reshape(nbp, seq, d_model),
        y_s.reshape(nbd, dseq, d_model),
        nsa_rows.reshape(1, nbp, seq, 4, *kvh),
        fox_rows.reshape(1, nbp, seq, 2, FOX_HEADS, HEAD_DIM),
        logf.reshape(1, nbp, seq, FOX_HEADS),
        win_p[None],
        nsa_rows_s.reshape(1, nbd, dseq, 4, *kvh),
        fox_rows_s.reshape(1, nbd, dseq, 2, FOX_HEADS, HEAD_DIM),
        logf_s[None],
        win_s[None],
    )
```

```python
import functools
import math

import numpy as np
import jax
import jax.numpy as jnp
from jax import lax
from jax.experimental import pallas as pl
from jax.experimental.pallas import tpu as pltpu

HEAD_DIM = 128
NSA_HEADS = 8
FOX_HEADS = 8
NSA_KV_HEADS = 4
NSA_GROUP = NSA_HEADS // NSA_KV_HEADS
CMP_LEN = 32
CMP_STRIDE = 16
CMP_HIDDEN = 512
SEL_BLOCK = 64
SEL_TOPK = 16
WINDOW = 512
REL_BUCKETS = 32
REL_MAX_DIST = 128
RMS_EPS = 1e-6
PAGE_SIZE = 128

LANE = 128
NEG = -1e30
FORCED_SCORE = 1e30
VMEM_LIMIT = 56 * 1024 * 1024

BF16 = jnp.bfloat16
F32 = jnp.float32


def _bucket_thresholds():
    n = np.arange(0, 4 * REL_MAX_DIST)
    max_exact = REL_BUCKETS // 2
    nf = np.maximum(n, 1).astype(np.float32)
    large = max_exact + (np.log(nf / max_exact) / math.log(REL_MAX_DIST / max_exact)
                         * (REL_BUCKETS - max_exact)).astype(np.int32)
    bucket = np.where(n < max_exact, n, np.minimum(large, REL_BUCKETS - 1))
    return [int(np.min(n[bucket >= k])) for k in range(1, REL_BUCKETS)]


BUCKET_THR = _bucket_thresholds()
FAR_DIST = BUCKET_THR[-1]


def _cparams(sem):
    return pltpu.CompilerParams(dimension_semantics=sem, vmem_limit_bytes=VMEM_LIMIT)


def _rms_rows(x, gain):
    ms = jnp.mean(x * x, axis=-1, keepdims=True)
    return x * lax.rsqrt(ms + RMS_EPS) * gain


def _ffn_kernel(x_ref, g_ref, wg_ref, wu_ref, wd_ref, o_ref, xn_ref):
    @pl.when(pl.program_id(1) == 0)
    def _():
        x = x_ref[...]
        xn_ref[...] = _rms_rows(x, g_ref[...]).astype(BF16)
        o_ref[...] = x

    xn = xn_ref[...]
    a = jnp.dot(xn, wg_ref[...], preferred_element_type=F32)
    u = jnp.dot(xn, wu_ref[...], preferred_element_type=F32)
    h = (a / (1.0 + jnp.exp(-a))) * u * 0.5
    o_ref[...] += jnp.dot(h.astype(BF16), wd_ref[...], preferred_element_type=F32)


def _ffn(x, gain, wg, wu, wd, tm, tf):
    m, d = x.shape
    f = wg.shape[1]
    return pl.pallas_call(
        _ffn_kernel,
        out_shape=jax.ShapeDtypeStruct((m, d), F32),
        grid=(m // tm, f // tf),
        in_specs=[
            pl.BlockSpec((tm, d), lambda i, j: (i, 0)),
            pl.BlockSpec((1, d), lambda i, j: (0, 0)),
            pl.BlockSpec((d, tf), lambda i, j: (0, j)),
            pl.BlockSpec((d, tf), lambda i, j: (0, j)),
            pl.BlockSpec((tf, d), lambda i, j: (j, 0)),
        ],
        out_specs=pl.BlockSpec((tm, d), lambda i, j: (i, 0)),
        scratch_shapes=[pltpu.VMEM((tm, d), BF16)],
        compiler_params=_cparams(("parallel", "arbitrary")),
        name="ffn",
    )(x, gain, wg, wu, wd)


IN_TN = 4 * HEAD_DIM
J_NSA = (2, 6)
J_WIN = (6, 8)
J_FOX = (10, 14)
N_HEAD_COLS = 56
HM_Q_NSA, HM_K_CMP, HM_K_SLC, HM_V_SLC, HM_K_WIN, HM_V_WIN = 0, 8, 16, 20, 24, 28
HM_Q_FOX, HM_K_FOX, HM_V_FOX = 32, 40, 48


IN_NORMED_TILES = (0, 1, 4, 6, 8, 9, 10, 11)


def _inproj_kernel(x_ref, g_ref, w_ref, cg_ref, ones_ref, nsa_ref, win_ref, fox_ref, hm_ref):
    xn = _rms_rows(x_ref[...], g_ref[...]).astype(BF16)
    tm = xn.shape[0]
    heads = IN_TN // HEAD_DIM
    for j in range(w_ref.shape[1] // IN_TN):
        cols = slice(j * IN_TN, (j + 1) * IN_TN)
        vals = jnp.dot(xn, w_ref[:, cols], preferred_element_type=F32)
        if j in IN_NORMED_TILES:
            sumsq = jnp.dot((vals * vals).astype(BF16), ones_ref[...], preferred_element_type=F32)
            vals = vals * lax.rsqrt(sumsq * (1.0 / HEAD_DIM) + RMS_EPS) * cg_ref[:, cols]
        for hh in range(heads):
            hm_ref[j * heads + hh] = vals[:, hh * HEAD_DIM:(hh + 1) * HEAD_DIM].astype(BF16)
        for ref, (jlo, jhi) in ((nsa_ref, J_NSA), (win_ref, J_WIN), (fox_ref, J_FOX)):
            if jlo <= j < jhi:
                n_cols = (jhi - jlo) * heads
                for hh in range(heads):
                    ref[pl.ds((j - jlo) * heads + hh, tm, stride=n_cols), :] = vals[:, hh * HEAD_DIM:(hh + 1) * HEAD_DIM]


def _inproj(x, gain, w_main, colgain, tm):
    m, d = x.shape
    ncol = w_main.shape[1]
    heads = IN_TN // HEAD_DIM
    n_nsa, n_win, n_fox = [(hi - lo) * heads for lo, hi in (J_NSA, J_WIN, J_FOX)]
    head_ones = jnp.asarray(np.kron(np.eye(heads, dtype=np.float32), np.ones((HEAD_DIM, HEAD_DIM), np.float32)), BF16)
    resident = lambda shape: pl.BlockSpec(shape, lambda i: (0, 0), pipeline_mode=pl.Buffered(1))
    return pl.pallas_call(
        _inproj_kernel,
        out_shape=(
            jax.ShapeDtypeStruct((m * n_nsa, HEAD_DIM), F32),
            jax.ShapeDtypeStruct((m * n_win, HEAD_DIM), F32),
            jax.ShapeDtypeStruct((m * n_fox, HEAD_DIM), F32),
            jax.ShapeDtypeStruct((N_HEAD_COLS, m, HEAD_DIM), BF16),
        ),
        grid=(m // tm,),
        in_specs=[
            pl.BlockSpec((tm, d), lambda i: (i, 0)),
            resident((1, d)),
            resident((d, ncol)),
            resident((1, ncol)),
            resident((IN_TN, IN_TN)),
        ],
        out_specs=(
            pl.BlockSpec((tm * n_nsa, HEAD_DIM), lambda i: (i, 0)),
            pl.BlockSpec((tm * n_win, HEAD_DIM), lambda i: (i, 0)),
            pl.BlockSpec((tm * n_fox, HEAD_DIM), lambda i: (i, 0)),
            pl.BlockSpec((N_HEAD_COLS, tm, HEAD_DIM), lambda i: (0, i, 0)),
        ),
        compiler_params=_cparams(("parallel",)),
        name="inproj",
    )(x, gain, w_main, colgain, head_ones)


N_GATE_COLS = 3 * NSA_HEADS


def _small_kernel(x_ref, g_ref, w_ref, b_ref, o_ref):
    xn = _rms_rows(x_ref[...], g_ref[...]).astype(BF16)
    z = jnp.dot(xn, w_ref[...], preferred_element_type=F32) + b_ref[...]
    lane = lax.broadcasted_iota(jnp.int32, z.shape, 1)
    sig = 1.0 / (1.0 + jnp.exp(-z))
    logsig = jnp.minimum(z, 0.0) - jnp.log(1.0 + jnp.exp(-jnp.abs(z)))
    o_ref[...] = jnp.where(lane < N_GATE_COLS, sig,
                           jnp.where(lane < N_GATE_COLS + FOX_HEADS, logsig, 0.0))


def _small(x, gain, w_small, b_small, tm):
    m, d = x.shape
    return pl.pallas_call(
        _small_kernel,
        out_shape=jax.ShapeDtypeStruct((m, LANE), F32),
        grid=(m // tm,),
        in_specs=[
            pl.BlockSpec((tm, d), lambda i: (i, 0)),
            pl.BlockSpec((1, d), lambda i: (0, 0)),
            pl.BlockSpec((d, LANE), lambda i: (0, 0)),
            pl.BlockSpec((1, LANE), lambda i: (0, 0)),
        ],
        out_specs=pl.BlockSpec((tm, LANE), lambda i: (i, 0)),
        compiler_params=_cparams(("parallel",)),
        name="gates",
    )(x, gain, w_small, b_small)


def _outproj_kernel(on_ref, of_ref, gn_ref, gf_ref, w_ref, x_ref, y_ref):
    a = _rms_rows(on_ref[...], gn_ref[...]).astype(BF16)
    b = _rms_rows(of_ref[...], gf_ref[...]).astype(BF16)
    half = a.shape[1]
    y = jnp.dot(a, w_ref[:half, :], preferred_element_type=F32)
    y = y + jnp.dot(b, w_ref[half:, :], preferred_element_type=F32)
    y_ref[...] = x_ref[...] + y


def _outproj(o_nsa, o_fox, g_nsa, g_fox, w_out, x, tm):
    m, d = x.shape
    wn = o_nsa.shape[1]
    wf = o_fox.shape[1]
    return pl.pallas_call(
        _outproj_kernel,
        out_shape=jax.ShapeDtypeStruct((m, d), F32),
        grid=(m // tm,),
        in_specs=[
            pl.BlockSpec((tm, wn), lambda i: (i, 0)),
            pl.BlockSpec((tm, wf), lambda i: (i, 0)),
            pl.BlockSpec((1, wn), lambda i: (0, 0)),
            pl.BlockSpec((1, wf), lambda i: (0, 0)),
            pl.BlockSpec((wn + wf, d), lambda i: (0, 0)),
            pl.BlockSpec((tm, d), lambda i: (i, 0)),
        ],
        out_specs=pl.BlockSpec((tm, d), lambda i: (i, 0)),
        compiler_params=_cparams(("parallel",)),
        name="outproj",
    )(o_nsa, o_fox, g_nsa, g_fox, w_out, x)


CUMSUM_CHUNK = 512


def _cumsum_kernel(x_ref, before_ref, o_ref):
    width = x_ref.shape[2]
    r = lax.broadcasted_iota(jnp.int32, (width, width), 0)
    c = lax.broadcasted_iota(jnp.int32, (width, width), 1)
    upper = (r <= c).astype(F32)
    local = jnp.dot(x_ref[0], upper, preferred_element_type=F32, precision=lax.Precision.HIGHEST)
    totals = jnp.broadcast_to(local[:, width - 1:width], (local.shape[0], LANE))
    offset = jnp.dot(before_ref[...], totals, preferred_element_type=F32, precision=lax.Precision.HIGHEST)
    o_ref[0] = local + jnp.concatenate([offset] * (width // LANE), axis=1)


def _cumsum(x):
    b, h, length = x.shape
    assert length % CUMSUM_CHUNK == 0
    pieces = length // CUMSUM_CHUNK
    used = h * pieces
    rows = -(-used // LANE) * LANE
    idx = np.arange(rows)
    before = ((idx[:, None] // pieces == idx[None, :] // pieces) & (idx[None, :] < idx[:, None])
              & (idx[:, None] < used))
    x = jnp.pad(x.reshape(b, used, CUMSUM_CHUNK), ((0, 0), (0, rows - used), (0, 0)))
    out = pl.pallas_call(
        _cumsum_kernel,
        out_shape=jax.ShapeDtypeStruct((b, rows, CUMSUM_CHUNK), F32),
        grid=(b,),
        in_specs=[pl.BlockSpec((1, rows, CUMSUM_CHUNK), lambda i: (i, 0, 0)),
                  pl.BlockSpec((rows, rows), lambda i: (0, 0))],
        out_specs=pl.BlockSpec((1, rows, CUMSUM_CHUNK), lambda i: (i, 0, 0)),
        compiler_params=_cparams(("parallel",)),
        name="logf_cumsum",
    )(x, jnp.asarray(before, F32))
    return out[:, :used].reshape(b, h, length)


PE_ROWS = 16


def _compress_kernel(x_ref, w1_ref, w2_ref, pe_ref, kn_ref, o_ref):
    kind = pl.program_id(0) // NSA_KV_HEADS
    n = x_ref.shape[2]
    w1 = w1_ref[0]
    h = jnp.dot(x_ref[0, 0], w1, preferred_element_type=F32)
    pw = jnp.dot(pe_ref[0], w1, preferred_element_type=F32)
    const = pw[0:1, :CMP_HIDDEN] + pw[1:2, CMP_HIDDEN:]
    hid = h[:, :CMP_HIDDEN] + pltpu.roll(h[:, CMP_HIDDEN:], n - 1, 0) + const
    act = hid / (1.0 + jnp.exp(-hid))
    out = jnp.dot(act.astype(BF16), w2_ref[0], preferred_element_type=F32)
    normed = _rms_rows(out, kn_ref[...])
    o_ref[0, 0] = jnp.where(kind == 0, normed, out).astype(BF16)


def _compress(xc, c_off, w1cat, w2, pe, k_norm, n_rows):
    nb = xc.shape[1]
    return pl.pallas_call(
        _compress_kernel,
        out_shape=jax.ShapeDtypeStruct((2 * NSA_KV_HEADS, nb, n_rows, HEAD_DIM), BF16),
        grid=(2 * NSA_KV_HEADS, nb),
        in_specs=[
            pl.BlockSpec((1, 1, n_rows, CMP_STRIDE * HEAD_DIM), lambda c, b: (c_off + c, b, 0, 0)),
            pl.BlockSpec((1, CMP_STRIDE * HEAD_DIM, 2 * CMP_HIDDEN), lambda c, b: (c // NSA_KV_HEADS, 0, 0)),
            pl.BlockSpec((1, CMP_HIDDEN, HEAD_DIM), lambda c, b: (c // NSA_KV_HEADS, 0, 0)),
            pl.BlockSpec((1, PE_ROWS, CMP_STRIDE * HEAD_DIM), lambda c, b: (c // NSA_KV_HEADS, 0, 0)),
            pl.BlockSpec((1, HEAD_DIM), lambda c, b: (0, 0)),
        ],
        out_specs=pl.BlockSpec((1, 1, n_rows, HEAD_DIM), lambda c, b: (c, b, 0, 0)),
        compiler_params=_cparams(("parallel", "parallel")),
        name="compress",
    )(xc, w1cat, w2, pe, k_norm)


def _rel_bias(dist, tbl_ref, head):
    out = jnp.full(dist.shape, tbl_ref[0, head], F32)
    for k, thr in enumerate(BUCKET_THR, start=1):
        out = jnp.where(dist >= thr, tbl_ref[k, head], out)
    return out


def _cmp_to_sel(n_cmp, n_sel, rows, cols):
    c0 = np.arange(n_cmp)[:, None] * CMP_STRIDE
    s0 = np.arange(n_sel)[None, :] * SEL_BLOCK
    inter = np.clip(np.minimum(c0 + CMP_LEN, s0 + SEL_BLOCK) - np.maximum(c0, s0), 0, None)
    m = np.zeros((rows, cols), np.float32)
    m[:n_cmp, :n_sel] = inter / CMP_LEN
    return jnp.asarray(m)


def _transpose_tiles(x):
    n = x.shape[0] // LANE
    xf = x.astype(F32)
    return jnp.concatenate([xf[i * LANE:(i + 1) * LANE].T for i in range(n)], axis=1)


def _untranspose_tiles(xt):
    n = xt.shape[1] // LANE
    return jnp.concatenate([xt[:, i * LANE:(i + 1) * LANE].T for i in range(n)], axis=0)


AUX_ROWS = 16


def _fill_transposed(dst_ref, src_ref):
    def body(k, carry):
        st = pl.multiple_of(k * LANE, LANE)
        dst_ref[:HEAD_DIM, pl.ds(st, LANE)] = src_ref[0, 0, pl.ds(st, LANE), :].astype(F32).T.astype(BF16)
        return carry

    lax.fori_loop(0, src_ref.shape[2] // LANE, body, 0)


FOX_TQ = 256
FOX_KEY_BLOCK = 1024


def _split3(x):
    hi = x.astype(BF16)
    r1 = x - hi.astype(F32)
    mid = r1.astype(BF16)
    lo = (r1 - mid.astype(F32)).astype(BF16)
    return hi, mid, lo


def _lane_select3(parts, shape):
    lane = lax.broadcasted_iota(jnp.int32, shape, 1)
    hi, mid, lo = [part.astype(F32) for part in parts]
    return jnp.where(lane == 0, hi, jnp.where(lane == 1, mid, jnp.where(lane == 2, lo, 0.0))).astype(BF16)


def _ones_rows(width):
    row = lax.broadcasted_iota(jnp.int32, (AUX_ROWS, width), 0)
    return jnp.where(row == 0, 1.0, 0.0).astype(BF16)


def _finish_aug(acc):
    return acc[:HEAD_DIM] * (1.0 / acc[HEAD_DIM:HEAD_DIM + 1])


def _fox_prompt_kernel(q_ref, k_ref, v_ref, crow_ref, o_ref, vt_ref, ka_ref, s_ref):
    qt = pl.program_id(2)
    tq = FOX_TQ

    @pl.when(qt == 0)
    def _():
        _fill_transposed(vt_ref, v_ref)
        vt_ref[HEAD_DIM:, :] = _ones_rows(vt_ref.shape[1])

        def body(k, carry):
            st = pl.multiple_of(k * LANE, LANE)
            c_col = jnp.broadcast_to(crow_ref[0, 0, :, pl.ds(st, LANE)], (LANE, LANE)).T
            ka_ref[pl.ds(st, LANE), :HEAD_DIM] = k_ref[0, 0, pl.ds(st, LANE), :]
            ka_ref[pl.ds(st, LANE), HEAD_DIM:] = _lane_select3(_split3(c_col), (LANE, LANE))
            return carry

        lax.fori_loop(0, k_ref.shape[2] // LANE, body, 0)

    q0 = pl.multiple_of(qt * tq, tq)
    q_t = _transpose_tiles(q_ref[0]).astype(BF16)
    row = lax.broadcasted_iota(jnp.int32, (LANE, tq), 0)
    q_aug = jnp.concatenate([q_t, jnp.where(row < 3, -1.0, 0.0).astype(BF16)], axis=0)
    c_q = crow_ref[0, 0, :, pl.ds(q0, tq)]
    blk = FOX_KEY_BLOCK

    last = q0 // blk
    rel = (lax.broadcasted_iota(jnp.int32, (blk, tq), 0) - lax.broadcasted_iota(jnp.int32, (blk, tq), 1))

    def run(n_blocks):
        top = jnp.full((8, tq), NEG, F32)
        for kb in range(n_blocks):
            start = kb * blk
            s = jnp.dot(ka_ref[start:start + blk, :], q_aug, preferred_element_type=F32)
            if kb == n_blocks - 1:
                s = jnp.where(rel <= q0 - start, s, NEG)
            s_ref[start:start + blk, :] = s
            top = jnp.maximum(top, jnp.max(s.reshape(blk // 8, 8, tq), axis=0))
        m = jnp.max(top, axis=0, keepdims=True) + c_q
        shift = m - c_q
        acc = jnp.zeros((HEAD_DIM + AUX_ROWS, tq), F32)
        for kb in range(n_blocks):
            start = kb * blk
            p = jnp.exp(s_ref[start:start + blk, :] - shift).astype(BF16)
            acc = acc + jnp.dot(vt_ref[:, start:start + blk], p, preferred_element_type=F32)
        return acc

    n_max = k_ref.shape[2] // blk
    acc = lax.switch(last, [functools.partial(run, n) for n in range(1, n_max + 1)])
    o_ref[...] = _untranspose_tiles(_finish_aug(acc))


def _fox_prompt(hm, hm4, crow, nb, seq):
    nq = seq // FOX_TQ
    return pl.pallas_call(
        _fox_prompt_kernel,
        out_shape=jax.ShapeDtypeStruct((nb * seq, FOX_HEADS * HEAD_DIM), F32),
        grid=(nb, FOX_HEADS, nq),
        in_specs=[
            pl.BlockSpec((1, FOX_TQ, HEAD_DIM), lambda b, h, t: (HM_Q_FOX + h, b * nq + t, 0)),
            pl.BlockSpec((1, 1, seq, HEAD_DIM), lambda b, h, t: (HM_K_FOX + h, b, 0, 0)),
            pl.BlockSpec((1, 1, seq, HEAD_DIM), lambda b, h, t: (HM_V_FOX + h, b, 0, 0)),
            pl.BlockSpec((1, 1, 1, seq), lambda b, h, t: (b, h, 0, 0)),
        ],
        out_specs=pl.BlockSpec((FOX_TQ, HEAD_DIM), lambda b, h, t: (b * nq + t, h)),
        scratch_shapes=[pltpu.VMEM((HEAD_DIM + AUX_ROWS, seq), BF16), pltpu.VMEM((seq, 2 * HEAD_DIM), BF16),
                        pltpu.VMEM((seq, FOX_TQ), F32)],
        compiler_params=_cparams(("parallel", "parallel", "arbitrary")),
        name="fox_prompt",
    )(hm, hm4, hm4, crow)


NSA_TQ = LANE
NSA_COLS = NSA_GROUP * NSA_TQ
WIN_TILES = WINDOW // NSA_TQ + 1
SLC_BLOCK_TILES = 4
SLC_MASK_ROWS = SLC_BLOCK_TILES * NSA_TQ // SEL_BLOCK


def _nsa_prompt_kernel(tbl_ref, q_ref, kc_ref, vc_ref, mselt_ref, ks_ref, vs_ref, kw_ref, vw_ref, gate_ref, o_ref,
                       vst_ref, vwt_ref, vct_ref, wb_ref, pc_ref, drop_ref, ksa_ref, wbd_ref, s_ref, *, n_sel, cmp_back):
    g = pl.program_id(1)
    qt = pl.program_id(2)
    tq = NSA_TQ
    cols = NSA_COLS
    heads = [NSA_GROUP * g + hh for hh in range(NSA_GROUP)]
    ncp = kc_ref.shape[2]
    nj = mselt_ref.shape[0]

    @pl.when(qt == 0)
    def _():
        _fill_transposed(vst_ref, vs_ref)
        _fill_transposed(vwt_ref, vw_ref)
        vst_ref[HEAD_DIM:, :] = _ones_rows(vst_ref.shape[1])
        vwt_ref[HEAD_DIM:, :] = _ones_rows(vwt_ref.shape[1])
        vct_ref[...] = _transpose_tiles(vc_ref[0, 0]).astype(BF16)

        def fill_keys(k, carry):
            st = pl.multiple_of(k * tq, tq)
            lane = lax.broadcasted_iota(jnp.int32, (tq, tq), 1)
            blk_in_step = (k % SLC_BLOCK_TILES) * (tq // SEL_BLOCK) + lax.broadcasted_iota(jnp.int32, (tq, tq), 0) // SEL_BLOCK
            extra = (lane == blk_in_step) | ((lane >= SLC_MASK_ROWS) & (lane < SLC_MASK_ROWS + 3))
            ksa_ref[pl.ds(st, tq), :HEAD_DIM] = ks_ref[0, 0, pl.ds(st, tq), :]
            ksa_ref[pl.ds(st, tq), HEAD_DIM:] = jnp.where(extra, 1.0, 0.0).astype(BF16)
            return carry

        lax.fori_loop(0, ks_ref.shape[2] // tq, fill_keys, 0)
        key = lax.broadcasted_iota(jnp.int32, (tq, tq), 0)
        qry = lax.broadcasted_iota(jnp.int32, (tq, tq), 1)
        blk = lax.broadcasted_iota(jnp.int32, (pc_ref.shape[0], tq), 0) - cmp_back
        d_cmp = lax.broadcasted_iota(jnp.int32, (pc_ref.shape[0], tq), 1) - (blk * CMP_STRIDE + (CMP_LEN - 1))
        for hh in range(NSA_GROUP):
            sl = slice(hh * tq, (hh + 1) * tq)
            far = tbl_ref[REL_BUCKETS - 1, heads[hh]]
            for delta in range(2):
                d = delta * tq + qry - key
                wb_ref[delta, :, sl] = jnp.where(d >= 0, _rel_bias(jnp.maximum(d, 0), tbl_ref, heads[hh]), NEG)
            for delta in range(2, WIN_TILES - 1):
                wb_ref[delta, :, sl] = jnp.full((tq, tq), far, F32)
            wb_ref[WIN_TILES - 1, :, sl] = jnp.where(qry < key, far, NEG)
            wb_ref[WIN_TILES, :, sl] = jnp.full((tq, tq), NEG, F32)
            for delta in range(2):
                wbd_ref[delta, :, sl] = wb_ref[delta, :, sl] - far
            wbd_ref[2, :, sl] = jnp.zeros((tq, tq), F32)
            wbd_ref[3, :, sl] = jnp.full((tq, tq), NEG, F32)
            pc_ref[:, sl] = jnp.where(d_cmp >= 0, _rel_bias(jnp.maximum(d_cmp, 0), tbl_ref, heads[hh]), NEG)

    q_t = jnp.concatenate([q_ref[hh].astype(F32).T for hh in range(NSA_GROUP)], axis=1).astype(BF16)

    off = pl.multiple_of(cmp_back - qt * (tq // CMP_STRIDE), 8)
    s_t = jnp.dot(kc_ref[0, 0], q_t, preferred_element_type=F32) + pc_ref[pl.ds(off, ncp), :]
    m = jnp.max(s_t, axis=0, keepdims=True)
    e = jnp.exp(s_t - m)
    inv = jnp.where(m > 0.5 * NEG, 1.0 / jnp.sum(e, axis=0, keepdims=True), 0.0)
    p_t = e * inv
    o_cmp = jnp.dot(vct_ref[...], p_t.astype(BF16), preferred_element_type=F32)
    p_sum = p_t[:, :tq]
    for hh in range(1, NSA_GROUP):
        p_sum = p_sum + p_t[:, hh * tq:(hh + 1) * tq]
    imp = jnp.dot(mselt_ref[...], p_sum, preferred_element_type=F32, precision=lax.Precision.HIGHEST)

    jrow = lax.broadcasted_iota(jnp.int32, (nj, tq), 0)
    pos = qt * tq + lax.broadcasted_iota(jnp.int32, (nj, tq), 1)
    blk_q = jnp.right_shift(pos, int(math.log2(SEL_BLOCK)))
    forced = (jrow == 0) | (jrow == blk_q) | (jrow == blk_q - 1)
    score = jnp.where(forced, FORCED_SCORE, jnp.where(jrow <= blk_q, imp, -1.0))
    score = jnp.where(jrow < n_sel, score, -2.0)
    ranks = []
    for r in range(nj // 8):
        mine = score[r * 8:(r + 1) * 8]
        jmine = jrow[r * 8:(r + 1) * 8]
        rank = jnp.zeros((8, tq), F32)
        for other in range(n_sel):
            row = jnp.broadcast_to(score[other:other + 1], (8, tq))
            if other < r * 8:
                beats = row >= mine
            elif other >= (r + 1) * 8:
                beats = row > mine
            else:
                beats = (row > mine) | ((row == mine) & (jmine > other))
            rank = rank + jnp.where(beats, 1.0, 0.0)
        ranks.append(rank)
    rank = jnp.concatenate(ranks, axis=0)
    drop = jnp.where((rank < SEL_TOPK) & (jrow < n_sel), 0.0, NEG)
    drop_ref[...] = jnp.concatenate([drop] * NSA_GROUP, axis=1)

    blk = SLC_BLOCK_TILES * tq
    col = lax.broadcasted_iota(jnp.int32, (8, cols), 1)
    far_row = jnp.full((8, cols), tbl_ref[REL_BUCKETS - 1, heads[NSA_GROUP - 1]], F32)
    for hh in range(NSA_GROUP - 1):
        far_row = jnp.where(col // tq == hh, tbl_ref[REL_BUCKETS - 1, heads[hh]], far_row)
    row8 = lax.broadcasted_iota(jnp.int32, (8, cols), 0)
    parts = [part.astype(F32) for part in _split3(far_row)]
    far_rows = jnp.where(row8 == 0, parts[0], jnp.where(row8 == 1, parts[1], jnp.where(row8 == 2, parts[2], 0.0)))
    pad_rows = jnp.zeros((HEAD_DIM - SLC_MASK_ROWS - 8, cols), BF16)

    last = qt // SLC_BLOCK_TILES

    def slc_run(n_blocks):
        top = jnp.full((8, cols), NEG, F32)
        for kb in range(n_blocks):
            start = kb * blk
            masks = drop_ref[kb * SLC_MASK_ROWS:(kb + 1) * SLC_MASK_ROWS, :]
            extra = jnp.concatenate([masks, far_rows], axis=0).astype(BF16)
            q_aug = jnp.concatenate([q_t, extra, pad_rows], axis=0)
            s = jnp.dot(ksa_ref[start:start + blk, :], q_aug, preferred_element_type=F32)
            if kb >= n_blocks - 2:
                terms = []
                for i in range(SLC_BLOCK_TILES):
                    delta = qt - (kb * SLC_BLOCK_TILES + i)
                    terms.append(wbd_ref[jnp.where(delta < 0, 3, jnp.minimum(delta, 2))])
                s = s + jnp.concatenate(terms, axis=0)
            s_ref[start:start + blk, :] = s
            top = jnp.maximum(top, jnp.max(s.reshape(blk // 8, 8, cols), axis=0))
        m = jnp.max(top, axis=0, keepdims=True)
        acc = jnp.zeros((HEAD_DIM + AUX_ROWS, cols), F32)
        for kb in range(n_blocks):
            start = kb * blk
            p = jnp.exp(s_ref[start:start + blk, :] - m).astype(BF16)
            acc = acc + jnp.dot(vst_ref[:, start:start + blk], p, preferred_element_type=F32)
        return acc

    n_max = ks_ref.shape[2] // blk
    o_slc = _finish_aug(lax.switch(last, [functools.partial(slc_run, n) for n in range(1, n_max + 1)]))

    w0 = jnp.maximum(qt - (WIN_TILES - 1), 0)
    start = pl.multiple_of(w0 * tq, tq)
    span = WIN_TILES * tq
    s = jnp.dot(kw_ref[0, 0, pl.ds(start, span), :], q_t, preferred_element_type=F32)
    terms = []
    for i in range(WIN_TILES):
        delta = qt - (w0 + i)
        terms.append(wb_ref[jnp.where(delta < 0, WIN_TILES, delta)])
    s = s + jnp.concatenate(terms, axis=0)
    p = jnp.exp(s - jnp.max(s, axis=0, keepdims=True))
    o_win = _finish_aug(jnp.dot(vwt_ref[:, pl.ds(start, span)], p.astype(BF16), preferred_element_type=F32))

    gate = gate_ref[0, 0, 0]
    o_t = gate[0:1] * o_cmp + gate[1:2] * o_slc + gate[2:3] * o_win
    for hh in range(NSA_GROUP):
        o_ref[:, hh * HEAD_DIM:(hh + 1) * HEAD_DIM] = o_t[:, hh * tq:(hh + 1) * tq].T


def _nsa_prompt(rel_table, hm, hm4, kcv, mselt, gates_t, nb, seq, n_sel):
    tq = NSA_TQ
    nq = seq // tq
    ncp = kcv.shape[2]
    nj = mselt.shape[0]
    cmp_back = (nq - 1) * (tq // CMP_STRIDE)
    kern = functools.partial(_nsa_prompt_kernel, n_sel=n_sel, cmp_back=cmp_back)
    kv_spec = lambda head0: pl.BlockSpec((1, 1, seq, HEAD_DIM), lambda b, g, t: (head0 + g, b, 0, 0))
    return pl.pallas_call(
        kern,
        out_shape=jax.ShapeDtypeStruct((nb * seq, NSA_HEADS * HEAD_DIM), F32),
        grid=(nb, NSA_KV_HEADS, nq),
        in_specs=[
            pl.BlockSpec(memory_space=pltpu.SMEM),
            pl.BlockSpec((NSA_GROUP, tq, HEAD_DIM), lambda b, g, t: (g, b * nq + t, 0)),
            pl.BlockSpec((1, 1, ncp, HEAD_DIM), lambda b, g, t: (g, b, 0, 0)),
            pl.BlockSpec((1, 1, ncp, HEAD_DIM), lambda b, g, t: (NSA_KV_HEADS + g, b, 0, 0)),
            pl.BlockSpec((nj, ncp), lambda b, g, t: (0, 0)),
            kv_spec(HM_K_SLC), kv_spec(HM_V_SLC), kv_spec(HM_K_WIN), kv_spec(HM_V_WIN),
            pl.BlockSpec((1, 1, 1, 8, NSA_COLS), lambda b, g, t: (b, g, t, 0, 0)),
        ],
        out_specs=pl.BlockSpec((tq, NSA_GROUP * HEAD_DIM), lambda b, g, t: (b * nq + t, g)),
        scratch_shapes=[
            pltpu.VMEM((HEAD_DIM + AUX_ROWS, seq), BF16),
            pltpu.VMEM((HEAD_DIM + AUX_ROWS, seq), BF16),
            pltpu.VMEM((HEAD_DIM, ncp), BF16),
            pltpu.VMEM((WIN_TILES + 1, tq, NSA_COLS), F32),
            pltpu.VMEM((cmp_back + ncp, NSA_COLS), F32),
            pltpu.VMEM((nj, NSA_COLS), F32),
            pltpu.VMEM((seq, 2 * HEAD_DIM), BF16),
            pltpu.VMEM((4, tq, NSA_COLS), F32),
            pltpu.VMEM((seq, NSA_COLS), F32),
        ],
        compiler_params=_cparams(("parallel", "parallel", "arbitrary")),
        name="nsa_prompt",
    )(rel_table, hm, kcv, kcv, mselt, hm4, hm4, hm4, hm4, gates_t)


CHUNK_ROWS = 512


def _chunkify_kernel(x_ref, o_ref, *, n_cols):
    n = o_ref.shape[2]
    for c in range(o_ref.shape[0]):
        for s in range(CMP_STRIDE):
            o_ref[c, 0, :, s * HEAD_DIM:(s + 1) * HEAD_DIM] = (
                x_ref[0, pl.ds(s * n_cols + c, n, stride=CMP_STRIDE * n_cols), :].astype(BF16))


def _chunkify(rows3, n_cols, n_heads, seq):
    nb = rows3.shape[0]
    tr = _largest_tile(seq, (CHUNK_ROWS, 256))
    return pl.pallas_call(
        functools.partial(_chunkify_kernel, n_cols=n_cols),
        out_shape=jax.ShapeDtypeStruct((n_heads, nb, seq // CMP_STRIDE, CMP_STRIDE * HEAD_DIM), BF16),
        grid=(nb, seq // tr),
        in_specs=[pl.BlockSpec((1, tr * n_cols, HEAD_DIM), lambda b, i: (b, i, 0))],
        out_specs=pl.BlockSpec((n_heads, 1, tr // CMP_STRIDE, CMP_STRIDE * HEAD_DIM), lambda b, i: (0, b, i, 0)),
        compiler_params=_cparams(("parallel", "parallel")),
        name="chunkify",
    )(rows3)


REGROUP_PAGES = 8


N_CACHE_COLS = 16
N_CMP_COLS = 2 * NSA_KV_HEADS


def _regroup_kernel(pt_ref, *refs, n_steps):
    del pt_ref
    npg = REGROUP_PAGES
    nsa_in, lf_in = refs[:npg], refs[npg:2 * npg]
    xc_out, slc_out, lf_out = refs[2 * npg:]
    is_tail = pl.program_id(1) >= n_steps
    chunks = PAGE_SIZE // CMP_STRIDE
    chunk_stride = CMP_STRIDE * N_CACHE_COLS

    @pl.when(is_tail)
    def _():
        slc_out[...] = jnp.zeros(slc_out.shape, slc_out.dtype)
        lf_out[...] = jnp.zeros(lf_out.shape, lf_out.dtype)

    @pl.when(jnp.logical_not(is_tail))
    def _():
        for p in range(npg):
            rows = slice(p * PAGE_SIZE, (p + 1) * PAGE_SIZE)
            for c in range(N_CACHE_COLS - N_CMP_COLS):
                slc_out[c, 0, rows, :] = nsa_in[p][0, pl.ds(N_CMP_COLS + c, PAGE_SIZE, stride=N_CACHE_COLS), :].astype(BF16)
            lf_out[0, rows, :] = lf_in[p][0]
        for pair in range(npg // 2):
            for c in range(N_CMP_COLS):
                for s in range(CMP_STRIDE):
                    first = s * N_CACHE_COLS + c
                    both = [nsa_in[2 * pair + i][0, pl.ds(first, chunks, stride=chunk_stride), :] for i in range(2)]
                    xc_out[c, 0, pair * 2 * chunks:(pair + 1) * 2 * chunks, s * HEAD_DIM:(s + 1) * HEAD_DIM] = (
                        jnp.concatenate(both, axis=0).astype(BF16))


def _regroup(page_table, cache_nsa, cache_logf, lk):
    nb, n_pages = page_table.shape
    npg = REGROUP_PAGES
    n_steps = n_pages // npg
    rows = npg * PAGE_SIZE
    last = n_steps - 1
    n_slc = N_CACHE_COLS - N_CMP_COLS
    chunk_w = CMP_STRIDE * HEAD_DIM

    def page_map(p):
        return lambda b, s, pt: (pt[b, jnp.minimum(s, last) * npg + p], 0, 0)

    def specs(arr):
        return [pl.BlockSpec((1,) + arr.shape[1:], page_map(p)) for p in range(npg)]

    assert (lk - n_pages * PAGE_SIZE) % rows == 0
    n_tail = (lk - n_pages * PAGE_SIZE) // rows
    grid_spec = pltpu.PrefetchScalarGridSpec(
        num_scalar_prefetch=1,
        grid=(nb, n_steps + n_tail),
        in_specs=specs(cache_nsa) + specs(cache_logf),
        out_specs=(
            pl.BlockSpec((N_CMP_COLS, 1, rows // CMP_STRIDE, chunk_w), lambda b, s, pt: (0, b, jnp.minimum(s, last), 0)),
            pl.BlockSpec((n_slc, 1, rows, HEAD_DIM), lambda b, s, pt: (0, b, s, 0)),
            pl.BlockSpec((1, rows, cache_logf.shape[2]), lambda b, s, pt: (b, s, 0)),
        ),
    )
    return pl.pallas_call(
        functools.partial(_regroup_kernel, n_steps=n_steps),
        out_shape=(
            jax.ShapeDtypeStruct((N_CMP_COLS, nb, n_pages * PAGE_SIZE // CMP_STRIDE, chunk_w), BF16),
            jax.ShapeDtypeStruct((n_slc, nb, lk, HEAD_DIM), BF16),
            jax.ShapeDtypeStruct((nb, lk, cache_logf.shape[2]), F32),
        ),
        grid_spec=grid_spec,
        compiler_params=_cparams(("parallel", "arbitrary")),
        name="cache_regroup",
    )(page_table, *([cache_nsa] * npg), *([cache_logf] * npg))


FOX_DEC_PAGES = 8


def _fox_decode_kernel(pt_ref, *refs, dseq):
    del pt_ref
    npg = FOX_DEC_PAGES
    pages = refs[:npg]
    qbd_ref, ccol_ref, cq_ref, knew_ref, vnew_ref, o_ref, m_ref, acc_ref = refs[npg:]
    step = pl.program_id(1)
    is_tail = step == pl.num_programs(1) - 1
    width = FOX_HEADS * HEAD_DIM

    @pl.when(step == 0)
    def _():
        m_ref[...] = jnp.full(m_ref.shape, NEG, F32)
        acc_ref[...] = jnp.zeros(acc_ref.shape, F32)

    def fold(k_blk, v_blk, bias):
        n = k_blk.shape[0]
        s_t = jnp.dot(k_blk, qbd_ref[0], preferred_element_type=F32) + bias
        m_new = jnp.maximum(m_ref[...], jnp.max(s_t, axis=0, keepdims=True))
        alpha = jnp.exp(m_ref[...] - m_new)
        p = jnp.exp(s_t - m_new)
        p_t = jnp.concatenate([p[i * LANE:(i + 1) * LANE].T for i in range(n // LANE)], axis=1).astype(BF16)
        v_aug = jnp.concatenate([v_blk, jnp.ones((n, LANE), BF16)], axis=1)
        upd = jnp.dot(p_t, v_aug, preferred_element_type=F32)
        alpha_col = jnp.broadcast_to(alpha, (LANE, LANE)).T
        acc_ref[...] = acc_ref[...] * jnp.concatenate([alpha_col] * (width // LANE + 1), axis=1) + upd
        m_ref[...] = m_new

    @pl.when(jnp.logical_not(is_tail))
    def _():
        def heads_of(p, first):
            cols = [pages[p][0, pl.ds(first + h, PAGE_SIZE, stride=N_CACHE_COLS), :] for h in range(FOX_HEADS)]
            return jnp.concatenate(cols, axis=1).astype(BF16)

        k_blk = jnp.concatenate([heads_of(p, 0) for p in range(npg)], axis=0)
        v_blk = jnp.concatenate([heads_of(p, FOX_HEADS) for p in range(npg)], axis=0)
        fold(k_blk, v_blk, cq_ref[0] - ccol_ref[0])

    @pl.when(is_tail)
    def _():
        row = lax.broadcasted_iota(jnp.int32, (LANE, LANE), 0)
        t_of_col = lax.broadcasted_iota(jnp.int32, (LANE, LANE), 1) % dseq
        bias = jnp.where(row <= t_of_col, cq_ref[0] - ccol_ref[0, :LANE, :], NEG)
        fold(knew_ref[0], vnew_ref[0], bias)
        acc = acc_ref[...]
        inv = 1.0 / acc[:, width:]
        o_ref[0] = acc[:, :width] * jnp.concatenate([inv] * (width // LANE), axis=1)


def _fox_decode(page_table, cache_fox, qbd, ccols, cq, knew, vnew, dseq):
    nb, n_pages = page_table.shape
    npg = FOX_DEC_PAGES
    n_steps = n_pages // npg
    last = n_steps - 1
    width = FOX_HEADS * HEAD_DIM

    def page_map(p):
        return lambda b, s, pt: (pt[b, jnp.minimum(s, last) * npg + p], 0, 0)

    per_b = lambda shape: pl.BlockSpec((1,) + shape, lambda b, s, pt: (b, 0, 0))
    grid_spec = pltpu.PrefetchScalarGridSpec(
        num_scalar_prefetch=1,
        grid=(nb, n_steps + 1),
        in_specs=[pl.BlockSpec((1,) + cache_fox.shape[1:], page_map(p)) for p in range(npg)] + [
            per_b((width, LANE)),
            pl.BlockSpec((1, npg * PAGE_SIZE, LANE), lambda b, s, pt: (b, s, 0)),
            per_b((1, LANE)),
            per_b((LANE, width)),
            per_b((LANE, width)),
        ],
        out_specs=per_b((LANE, width)),
        scratch_shapes=[pltpu.VMEM((1, LANE), F32), pltpu.VMEM((LANE, width + LANE), F32)],
    )
    return pl.pallas_call(
        functools.partial(_fox_decode_kernel, dseq=dseq),
        out_shape=jax.ShapeDtypeStruct((nb, LANE, width), F32),
        grid_spec=grid_spec,
        compiler_params=_cparams(("parallel", "arbitrary")),
        name="fox_decode",
    )(page_table, *([cache_fox] * npg), qbd, ccols, cq, knew, vnew)


NSA_DEC_KEYS = 2048


def _lane_transpose(p):
    return jnp.concatenate([p[i * LANE:(i + 1) * LANE].T for i in range(p.shape[0] // LANE)], axis=1).astype(BF16)


def _nsa_decode_kernel(qbd_ref, kc_ref, vc_ref, mselt_ref, pair_ref, tcol_ref, ks_ref, vs_ref, kw_ref, vw_ref,
                       gate_ref, o_ref, drop_ref, m_ref, acc_ref, ocmp_ref, *, dseq, past, n_sel):
    step = pl.program_id(1)
    n_tiles = pl.num_programs(1)
    width = NSA_KV_HEADS * HEAD_DIM
    qbd = qbd_ref[0]
    col = lax.broadcasted_iota(jnp.int32, (1, LANE), 1)
    q_pos = past + col % dseq

    def side_by_side(ref, rows=None):
        parts = [ref[g, 0] if rows is None else ref[g, 0, rows, :] for g in range(NSA_KV_HEADS)]
        return jnp.concatenate(parts, axis=1)

    def rel_bias_cols(dist):
        out = jnp.broadcast_to(tcol_ref[0:1, :], dist.shape)
        for k, thr in enumerate(BUCKET_THR, start=1):
            out = jnp.where(dist >= thr, tcol_ref[k:k + 1, :], out)
        return out

    def softmax_pv(s_t, v_all):
        m = jnp.max(s_t, axis=0, keepdims=True)
        e = jnp.exp(s_t - m)
        inv = jnp.where(m > 0.5 * NEG, 1.0 / jnp.sum(e, axis=0, keepdims=True), 0.0)
        p = e * inv
        return p, jnp.dot(_lane_transpose(p), v_all, preferred_element_type=F32)

    @pl.when(step == 0)
    def _():
        m_ref[...] = jnp.full(m_ref.shape, NEG, F32)
        acc_ref[...] = jnp.zeros(acc_ref.shape, F32)
        ncp = kc_ref.shape[2]
        blk_end = lax.broadcasted_iota(jnp.int32, (ncp, LANE), 0) * CMP_STRIDE + (CMP_LEN - 1)
        d = q_pos - blk_end
        s_t = jnp.dot(side_by_side(kc_ref), qbd, preferred_element_type=F32)
        s_t = jnp.where(d >= 0, s_t + rel_bias_cols(jnp.maximum(d, 0)), NEG)
        p, ocmp_ref[...] = softmax_pv(s_t, side_by_side(vc_ref))
        imp = jnp.dot(mselt_ref[...], p, preferred_element_type=F32, precision=lax.Precision.HIGHEST)
        imp = jnp.dot(imp, pair_ref[...], preferred_element_type=F32, precision=lax.Precision.HIGHEST)
        nj = imp.shape[0]
        jrow = lax.broadcasted_iota(jnp.int32, (nj, LANE), 0)
        blk_q = jnp.right_shift(q_pos, int(math.log2(SEL_BLOCK)))
        forced = (jrow == 0) | (jrow == blk_q) | (jrow == blk_q - 1)
        score = jnp.where(forced, FORCED_SCORE, jnp.where(jrow <= blk_q, imp, -1.0))
        score = jnp.where(jrow < n_sel, score, -2.0)
        drop_ref[...] = score
        rank = jnp.zeros((nj, LANE), F32)

        def count(other, rank):
            row = jnp.broadcast_to(drop_ref[pl.ds(other, 1), :], (nj, LANE))
            beats = (row > score) | ((row == score) & (jrow > other))
            return rank + jnp.where(beats, 1.0, 0.0)

        rank = lax.fori_loop(0, n_sel, count, rank)
        far = tcol_ref[REL_BUCKETS - 1:REL_BUCKETS, :]
        drop_ref[...] = jnp.where((rank < SEL_TOPK) & (jrow <= blk_q), far, NEG)

    tile = ks_ref.shape[2]
    per_tile = tile // SEL_BLOCK
    start = step * tile
    first_blk = pl.multiple_of(step * per_tile, 8)
    rows = [jnp.broadcast_to(drop_ref[pl.ds(first_blk + i, 1), :], (SEL_BLOCK, LANE)) for i in range(per_tile)]
    s_t = jnp.dot(side_by_side(ks_ref), qbd, preferred_element_type=F32) + jnp.concatenate(rows, axis=0)

    def near_fix(s_t):
        def fix(rows_at, s_rows):
            key_pos = start + rows_at + lax.broadcasted_iota(jnp.int32, (LANE, LANE), 0)
            d = q_pos - key_pos
            far = tcol_ref[REL_BUCKETS - 1:REL_BUCKETS, :]
            return jnp.where(d >= 0, s_rows + (rel_bias_cols(jnp.maximum(d, 0)) - far), NEG)

        head = fix(0, s_t[:LANE])
        tail = fix(tile - LANE, s_t[tile - LANE:])
        return jnp.concatenate([head, s_t[LANE:tile - LANE], tail], axis=0)

    is_near = (start + tile > past - FAR_DIST)
    s_t = lax.cond(is_near, near_fix, lambda s: s, s_t)
    m_new = jnp.maximum(m_ref[...], jnp.max(s_t, axis=0, keepdims=True))
    alpha = jnp.exp(m_ref[...] - m_new)
    p = jnp.exp(s_t - m_new)
    v_aug = jnp.concatenate([side_by_side(vs_ref), jnp.ones((tile, LANE), BF16)], axis=1)
    upd = jnp.dot(_lane_transpose(p), v_aug, preferred_element_type=F32)
    alpha_col = jnp.broadcast_to(alpha, (LANE, LANE)).T
    acc_ref[...] = acc_ref[...] * jnp.concatenate([alpha_col] * (width // LANE + 1), axis=1) + upd
    m_ref[...] = m_new

    @pl.when(step == n_tiles - 1)
    def _():
        acc = acc_ref[...]
        o_slc = acc[:, :width] * jnp.concatenate([1.0 / acc[:, width:]] * (width // LANE), axis=1)
        span = kw_ref.shape[2]
        key_pos = (past - WINDOW) + lax.broadcasted_iota(jnp.int32, (span, LANE), 0)
        d = q_pos - key_pos
        s_w = jnp.dot(side_by_side(kw_ref), qbd, preferred_element_type=F32)
        s_w = jnp.where((d >= 0) & (d < WINDOW), s_w + rel_bias_cols(jnp.maximum(d, 0)), NEG)
        _, o_win = softmax_pv(s_w, side_by_side(vw_ref))
        g = gate_ref[0]
        tile4 = lambda a: jnp.concatenate([a] * (width // LANE), axis=1)
        o_ref[0] = tile4(g[0]) * ocmp_ref[...] + tile4(g[1]) * o_slc + tile4(g[2]) * o_win


def _nsa_decode(qbd, kcv, mselt, pair, tcols, nsa_dec, win_dec, gcols, *, dseq, past, n_sel):
    nb = qbd.shape[0]
    ncp = kcv.shape[2]
    nj = mselt.shape[0]
    span = win_dec.shape[2]
    tile = NSA_DEC_KEYS
    n_tiles = -(-nsa_dec.shape[2] // tile)
    width = NSA_KV_HEADS * HEAD_DIM
    kern = functools.partial(_nsa_decode_kernel, dseq=dseq, past=past, n_sel=n_sel)
    grp = lambda rows, half, tiled: pl.BlockSpec(
        (NSA_KV_HEADS, 1, rows, HEAD_DIM), (lambda b, s: (half, b, s, 0)) if tiled else (lambda b, s: (half, b, 0, 0)))
    const = lambda shape: pl.BlockSpec(shape, lambda b, s: (0,) * len(shape))
    return pl.pallas_call(
        kern,
        out_shape=jax.ShapeDtypeStruct((nb, LANE, width), F32),
        grid=(nb, n_tiles),
        in_specs=[
            pl.BlockSpec((1, width, LANE), lambda b, s: (b, 0, 0)),
            grp(ncp, 0, False), grp(ncp, 1, False),
            const((nj, ncp)), const((LANE, LANE)), const((REL_BUCKETS, LANE)),
            grp(tile, 0, True), grp(tile, 1, True),
            grp(span, 0, False), grp(span, 1, False),
            pl.BlockSpec((1, 3, LANE, LANE), lambda b, s: (b, 0, 0, 0)),
        ],
        out_specs=pl.BlockSpec((1, LANE, width), lambda b, s: (b, 0, 0)),
        scratch_shapes=[
            pltpu.VMEM((nj, LANE), F32),
            pltpu.VMEM((1, LANE), F32),
            pltpu.VMEM((LANE, width + LANE), F32),
            pltpu.VMEM((LANE, width), F32),
        ],
        compiler_params=_cparams(("parallel", "arbitrary")),
        name="nsa_decode",
    )(qbd, kcv, kcv, mselt, pair, tcols, nsa_dec, nsa_dec, win_dec, win_dec, gcols)


def _largest_tile(n, candidates):
    for c in candidates:
        if n % c == 0:
            return c
    raise ValueError(f"no tile in {candidates} divides {n}")


FFN_ROW_TILES = (1024, 512, 256, 128)
ROW_TILES = (512, 256, 128)


def _token_stage_in(x, p):
    m = x.shape[0]
    tm = m if m < ROW_TILES[-1] else _largest_tile(m, ROW_TILES)
    tm_ffn = m if m < FFN_ROW_TILES[-1] else _largest_tile(m, FFN_ROW_TILES)
    x1 = _ffn(x, p["norm_ffn1"], p["wg1"], p["wu1"], p["wd1"], tm_ffn, p["tf"])
    tm_in = m if m < ROW_TILES[-1] else _largest_tile(m, ROW_TILES[1:])
    nsa_rows, win_rows, fox_rows, hm = _inproj(x1, p["norm_mix"], p["w_main"], p["colgain"], tm_in)
    small = _small(x1, p["norm_mix"], p["w_small"], p["b_small"], tm)
    return x1, nsa_rows, win_rows, fox_rows, hm, small


def _token_stage_out(x1, o_nsa, o_fox, p):
    m = x1.shape[0]
    tm = m if m < ROW_TILES[-1] else _largest_tile(m, ROW_TILES[1:])
    tm_ffn = m if m < FFN_ROW_TILES[-1] else _largest_tile(m, FFN_ROW_TILES)
    x2 = _outproj(o_nsa, o_fox, p["out_norm_nsa"], p["out_norm_fox"], p["w_out"], x1, tm)
    return _ffn(x2, p["norm_ffn2"], p["wg2"], p["wu2"], p["wd2"], tm_ffn, p["tf"])


def kernel(x_prompt, x_sample, cache_nsa_kv, cache_fox_kv, cache_fox_logf, state_win_kv, page_table, rel_table, norm_ffn1, ffn1_gate, ffn1_up, ffn1_down, norm_mix, w_in, nsa_gate_bias, fox_forget_bias, q_norm_nsa, k_norm_nsa, q_norm_fox, k_norm_fox, cmp_pos_k, cmp_w1_k, cmp_w2_k, cmp_pos_v, cmp_w1_v, cmp_w2_v, out_norm_nsa, out_norm_fox, w_out, norm_ffn2, ffn2_gate, ffn2_up, ffn2_down):
    depth = w_in.shape[0]
    assert depth == 1, "single-layer trunk"
    nbp, seq, d_model = x_prompt.shape
    nbd, dseq, _ = x_sample.shape
    n_pages = page_table.shape[1]
    past = n_pages * PAGE_SIZE
    d_ff = ffn1_gate.shape[2]
    nsa_w = NSA_HEADS * HEAD_DIM
    kv6_w = 6 * NSA_KV_HEADS * HEAD_DIM
    fox_w = 3 * FOX_HEADS * HEAD_DIM
    off_gate = nsa_w + kv6_w
    off_fox = off_gate + N_GATE_COLS
    off_forget = off_fox + fox_w
    assert w_in.shape[2] == off_forget + FOX_HEADS and d_model == nsa_w + FOX_HEADS * HEAD_DIM
    assert seq % LANE == 0 and seq >= WINDOW and past % LANE == 0 and n_pages % REGROUP_PAGES == 0
    assert dseq <= 16 and state_win_kv.shape[2] == WINDOW
    assert seq % FOX_KEY_BLOCK == 0 and seq % (SLC_BLOCK_TILES * NSA_TQ) == 0 and seq >= WIN_TILES * NSA_TQ

    w0 = w_in[0]
    ones = lambda n: jnp.ones((n,), F32)
    zeros = lambda n: jnp.zeros((n,), F32)
    kn, kvw = NSA_KV_HEADS, NSA_KV_HEADS * HEAD_DIM
    qk_scale = HEAD_DIM ** -0.5
    p = {
        "tf": _largest_tile(d_ff, (512, 256, 128)),
        "norm_ffn1": norm_ffn1[0][None], "norm_mix": norm_mix[0][None], "norm_ffn2": norm_ffn2[0][None],
        "wg1": ffn1_gate[0].astype(BF16), "wu1": ffn1_up[0].astype(BF16), "wd1": ffn1_down[0].astype(BF16),
        "wg2": ffn2_gate[0].astype(BF16), "wu2": ffn2_up[0].astype(BF16), "wd2": ffn2_down[0].astype(BF16),
        "w_main": jnp.concatenate([w0[:, :off_gate], w0[:, off_fox:off_forget]], axis=1).astype(BF16),
        "w_small": jnp.concatenate([w0[:, off_gate:off_fox], w0[:, off_forget:],
                                    jnp.zeros((d_model, LANE - N_GATE_COLS - FOX_HEADS), F32)], axis=1).astype(BF16),
        "b_small": jnp.concatenate([nsa_gate_bias[0].reshape(-1), fox_forget_bias[0],
                                    zeros(LANE - N_GATE_COLS - FOX_HEADS)])[None],
        "colgain": jnp.concatenate([
            jnp.tile(q_norm_nsa[0] * qk_scale, NSA_HEADS), ones(2 * kvw), jnp.tile(k_norm_nsa[0], kn), ones(kvw),
            jnp.tile(k_norm_nsa[0], kn), ones(kvw), jnp.tile(q_norm_fox[0] * qk_scale, FOX_HEADS),
            jnp.tile(k_norm_fox[0], FOX_HEADS), ones(FOX_HEADS * HEAD_DIM)])[None],
        "out_norm_nsa": out_norm_nsa[0][None], "out_norm_fox": out_norm_fox[0][None],
        "w_out": w_out[0].astype(BF16),
    }
    half = CMP_STRIDE * HEAD_DIM

    def cmp_w1(w):
        return jnp.concatenate([w[0, :half], w[0, half:]], axis=1)

    def cmp_pe(pe):
        return jnp.concatenate([pe[0].reshape(CMP_LEN // CMP_STRIDE, half), jnp.zeros((PE_ROWS - CMP_LEN // CMP_STRIDE, half), F32)], axis=0)

    w1cat = jnp.stack([cmp_w1(cmp_w1_k), cmp_w1(cmp_w1_v)]).astype(BF16)
    w2cat = jnp.stack([cmp_w2_k[0], cmp_w2_v[0]]).astype(BF16)
    pecat = jnp.stack([cmp_pe(cmp_pos_k), cmp_pe(cmp_pos_v)]).astype(BF16)
    k_norm_row = k_norm_nsa[0][None]

    mp = nbp * seq
    x1, nsa_rows, win_rows, fox_rows, hm, small = _token_stage_in(x_prompt.reshape(mp, d_model), p)
    hm4 = hm.reshape(N_HEAD_COLS, nbp, seq, HEAD_DIM)

    logf = small[:, N_GATE_COLS:N_GATE_COLS + FOX_HEADS]
    csum = _cumsum(logf.reshape(nbp, seq, FOX_HEADS).transpose(0, 2, 1))
    o_fox = _fox_prompt(hm, hm4, csum[:, :, None, :], nbp, seq)

    n_chunk = seq // CMP_STRIDE
    n_cmp = (seq - CMP_LEN) // CMP_STRIDE + 1
    n_sel = -(-seq // SEL_BLOCK)
    xc = _chunkify(nsa_rows.reshape(nbp, seq * N_CACHE_COLS, HEAD_DIM), N_CACHE_COLS, N_CMP_COLS, seq)
    kcv = _compress(xc, 0, w1cat, w2cat, pecat, k_norm_row, n_chunk)
    mselt = _cmp_to_sel(n_cmp, n_sel, n_chunk, -(-n_sel // 8) * 8).T
    nq = seq // NSA_TQ
    gates_t = small[:, :N_GATE_COLS].reshape(nbp, nq, NSA_TQ, NSA_KV_HEADS, NSA_GROUP, 3)
    gates_t = gates_t.transpose(0, 3, 1, 5, 4, 2).reshape(nbp, NSA_KV_HEADS, nq, 3, NSA_COLS)
    gates_t = jnp.pad(gates_t, ((0, 0), (0, 0), (0, 0), (0, 8 - 3), (0, 0)))
    o_nsa = _nsa_prompt(rel_table, hm, hm4, kcv, mselt, gates_t, nbp, seq, n_sel)
    y_p = _token_stage_out(x1, o_nsa, o_fox, p)

    ms = nbd * dseq
    lk = past + NSA_DEC_KEYS
    assert n_pages % FOX_DEC_PAGES == 0 and past % NSA_DEC_KEYS == 0 and FOX_HEADS * dseq <= LANE
    xs1, nsa_rows_s, win_rows_s, fox_rows_s, hm_s, small_s = _token_stage_in(x_sample.reshape(ms, d_model), p)
    xc_d, nsa_dec, lf_dec = _regroup(
        page_table,
        cache_nsa_kv.reshape(cache_nsa_kv.shape[1], PAGE_SIZE * N_CACHE_COLS, HEAD_DIM),
        cache_fox_logf[0], lk)
    hm_s4 = hm_s.reshape(N_HEAD_COLS, nbd, dseq, HEAD_DIM)
    nsa_dec = lax.dynamic_update_slice(nsa_dec, hm_s4[HM_K_SLC:HM_K_WIN], (0, 0, past, 0))
    logf_s = small_s[:, N_GATE_COLS:N_GATE_COLS + FOX_HEADS].reshape(nbd, dseq, FOX_HEADS)
    lf_dec = lax.dynamic_update_slice(lf_dec, logf_s, (0, past, 0))

    csum_d = _cumsum(lf_dec.transpose(0, 2, 1))
    n_cols = FOX_HEADS * dseq
    lane_pad = lambda a: jnp.pad(a, [(0, 0)] * (a.ndim - 1) + [(0, LANE - n_cols)])
    head_eye = jnp.eye(FOX_HEADS, dtype=BF16)
    qbd = jnp.einsum("hbtd,hg->bhdgt", hm_s4[HM_Q_FOX:HM_Q_FOX + FOX_HEADS], head_eye)
    qbd = lane_pad(qbd.reshape(nbd, FOX_HEADS * HEAD_DIM, n_cols))
    ccols = lane_pad(jnp.repeat(csum_d.transpose(0, 2, 1), dseq, axis=2))
    cq = lane_pad(csum_d[:, :, past:past + dseq].reshape(nbd, 1, n_cols))

    def new_rows(head0):
        rows = hm_s4[head0:head0 + FOX_HEADS].transpose(1, 2, 0, 3).reshape(nbd, dseq, FOX_HEADS * HEAD_DIM)
        return jnp.pad(rows, ((0, 0), (0, LANE - dseq), (0, 0)))

    o_full = _fox_decode(page_table, cache_fox_kv.reshape(cache_fox_kv.shape[1], PAGE_SIZE * N_CACHE_COLS, HEAD_DIM),
                         qbd, ccols, cq, new_rows(HM_K_FOX), new_rows(HM_V_FOX), dseq)
    o_fox_s = jnp.concatenate([o_full[:, h * dseq:(h + 1) * dseq, h * HEAD_DIM:(h + 1) * HEAD_DIM]
                               for h in range(FOX_HEADS)], axis=2).reshape(ms, FOX_HEADS * HEAD_DIM)

    n_chunk_d = past // CMP_STRIDE
    n_cmp_d = (past + dseq - CMP_LEN) // CMP_STRIDE + 1
    n_sel_d = -(-(past + dseq) // SEL_BLOCK)
    assert n_cmp_d + CMP_LEN // CMP_STRIDE - 1 <= n_chunk_d, "compressed blocks must lie in the cached rows"
    kcv_d = _compress(xc_d, 0, w1cat, w2cat, pecat, k_norm_row, n_chunk_d)
    mselt_d = _cmp_to_sel(n_cmp_d, n_sel_d, n_chunk_d, lk // SEL_BLOCK).T
    win_old = state_win_kv[0].transpose(2, 3, 0, 1, 4).reshape(2 * NSA_KV_HEADS, nbd, WINDOW, HEAD_DIM).astype(BF16)
    win_dec = jnp.concatenate([win_old, hm_s4[HM_K_WIN:HM_V_WIN + NSA_KV_HEADS],
                               jnp.zeros((2 * NSA_KV_HEADS, nbd, LANE - dseq, HEAD_DIM), BF16)], axis=2)
    grp_eye = jnp.eye(NSA_KV_HEADS, dtype=BF16)
    q_grp = hm_s4[HM_Q_NSA:HM_Q_NSA + NSA_HEADS].reshape(NSA_KV_HEADS, NSA_GROUP, nbd, dseq, HEAD_DIM)
    qbd_n = jnp.einsum("gjbtd,gk->bgdkjt", q_grp, grp_eye)
    qbd_n = lane_pad(qbd_n.reshape(nbd, NSA_KV_HEADS * HEAD_DIM, n_cols))
    tcols = lane_pad(jnp.repeat(rel_table, dseq, axis=1))
    col_id = np.arange(n_cols)
    same = ((col_id[:, None] // (NSA_GROUP * dseq) == col_id[None, :] // (NSA_GROUP * dseq))
            & (col_id[:, None] % dseq == col_id[None, :] % dseq))
    pair = jnp.asarray(np.pad(same.astype(np.float32), ((0, LANE - n_cols), (0, LANE - n_cols))))
    gcols = small_s[:, :N_GATE_COLS].reshape(nbd, dseq, NSA_HEADS, 3).transpose(0, 3, 2, 1).reshape(nbd, 3, n_cols)
    gcols = jnp.broadcast_to(lane_pad(gcols)[..., None], (nbd, 3, LANE, LANE))
    o_full_n = _nsa_decode(qbd_n, kcv_d, mselt_d, pair, tcols, nsa_dec, win_dec, gcols,
                           dseq=dseq, past=past, n_sel=n_sel_d)
    o_nsa_s = jnp.concatenate(
        [o_full_n[:, h * dseq:(h + 1) * dseq, (h // NSA_GROUP) * HEAD_DIM:(h // NSA_GROUP + 1) * HEAD_DIM]
         for h in range(NSA_HEADS)], axis=2).reshape(ms, NSA_HEADS * HEAD_DIM)
    y_s = _token_stage_out(xs1, o_nsa_s, o_fox_s, p)

    kvh = (NSA_KV_HEADS, HEAD_DIM)
    win_keep = min(WINDOW, seq)
    win_p = win_rows.reshape(nbp, seq, 2, *kvh)[:, seq - win_keep:]
    win_s = jnp.concatenate([state_win_kv[0], win_rows_s.reshape(nbd, dseq, 2, *kvh)], axis=1)[:, dseq:]
    return (
        y_p.reshape(nbp, seq, d_model),
        y_s.reshape(nbd, dseq, d_model),
        nsa_rows.reshape(1, nbp, seq, 4, *kvh),
        fox_rows.reshape(1, nbp, seq, 2, FOX_HEADS, HEAD_DIM),
        logf.reshape(1, nbp, seq, FOX_HEADS),
        win_p[None],
        nsa_rows_s.reshape(1, nbd, dseq, 4, *kvh),
        fox_rows_s.reshape(1, nbd, dseq, 2, FOX_HEADS, HEAD_DIM),
        logf_s[None],
        win_s[None],
    )
```

```python
import functools
import math

import numpy as np
import jax
import jax.numpy as jnp
from jax import lax
from jax.experimental import pallas as pl
from jax.experimental.pallas import tpu as pltpu

HEAD_DIM = 128
NSA_HEADS = 8
FOX_HEADS = 8
NSA_KV_HEADS = 4
NSA_GROUP = NSA_HEADS // NSA_KV_HEADS
CMP_LEN = 32
CMP_STRIDE = 16
CMP_HIDDEN = 512
SEL_BLOCK = 64
SEL_TOPK = 16
WINDOW = 512
REL_BUCKETS = 32
REL_MAX_DIST = 128
RMS_EPS = 1e-6
PAGE_SIZE = 128

LANE = 128
NEG = -1e30
FORCED_SCORE = 1e30
VMEM_LIMIT = 56 * 1024 * 1024

BF16 = jnp.bfloat16
F32 = jnp.float32


def _bucket_thresholds():
    n = np.arange(0, 4 * REL_MAX_DIST)
    max_exact = REL_BUCKETS // 2
    nf = np.maximum(n, 1).astype(np.float32)
    large = max_exact + (np.log(nf / max_exact) / math.log(REL_MAX_DIST / max_exact)
                         * (REL_BUCKETS - max_exact)).astype(np.int32)
    bucket = np.where(n < max_exact, n, np.minimum(large, REL_BUCKETS - 1))
    return [int(np.min(n[bucket >= k])) for k in range(1, REL_BUCKETS)]


BUCKET_THR = _bucket_thresholds()
FAR_DIST = BUCKET_THR[-1]


def _cparams(sem):
    return pltpu.CompilerParams(dimension_semantics=sem, vmem_limit_bytes=VMEM_LIMIT)


def _rms_rows(x, gain):
    ms = jnp.mean(x * x, axis=-1, keepdims=True)
    return x * lax.rsqrt(ms + RMS_EPS) * gain


def _ffn_kernel(x_ref, g_ref, wg_ref, wu_ref, wd_ref, o_ref, xn_ref):
    @pl.when(pl.program_id(1) == 0)
    def _():
        x = x_ref[...]
        xn_ref[...] = _rms_rows(x, g_ref[...]).astype(BF16)
        o_ref[...] = x

    xn = xn_ref[...]
    a = jnp.dot(xn, wg_ref[...], preferred_element_type=F32)
    u = jnp.dot(xn, wu_ref[...], preferred_element_type=F32)
    h = (a / (1.0 + jnp.exp(-a))) * u * 0.5
    o_ref[...] += jnp.dot(h.astype(BF16), wd_ref[...], preferred_element_type=F32)


def _ffn(x, gain, wg, wu, wd, tm, tf):
    m, d = x.shape
    f = wg.shape[1]
    return pl.pallas_call(
        _ffn_kernel,
        out_shape=jax.ShapeDtypeStruct((m, d), F32),
        grid=(m // tm, f // tf),
        in_specs=[
            pl.BlockSpec((tm, d), lambda i, j: (i, 0), pipeline_mode=pl.Buffered(1)),
            pl.BlockSpec((1, d), lambda i, j: (0, 0)),
            pl.BlockSpec((d, tf), lambda i, j: (0, j)),
            pl.BlockSpec((d, tf), lambda i, j: (0, j)),
            pl.BlockSpec((tf, d), lambda i, j: (j, 0)),
        ],
        out_specs=pl.BlockSpec((tm, d), lambda i, j: (i, 0)),
        scratch_shapes=[pltpu.VMEM((tm, d), BF16)],
        compiler_params=_cparams(("parallel", "arbitrary")),
        name="ffn",
    )(x, gain, wg, wu, wd)


IN_TN = 4 * HEAD_DIM
J_NSA = (2, 6)
J_WIN = (6, 8)
J_FOX = (10, 14)
N_HEAD_COLS = 56
HM_Q_NSA, HM_K_CMP, HM_K_SLC, HM_V_SLC, HM_K_WIN, HM_V_WIN = 0, 8, 16, 20, 24, 28
HM_Q_FOX, HM_K_FOX, HM_V_FOX = 32, 40, 48


IN_NORMED_TILES = (0, 1, 4, 6, 8, 9, 10, 11)


def _inproj_kernel(x_ref, g_ref, w_ref, cg_ref, ones_ref, nsa_ref, win_ref, fox_ref, hm_ref):
    xn = _rms_rows(x_ref[...], g_ref[...]).astype(BF16)
    tm = xn.shape[0]
    heads = IN_TN // HEAD_DIM
    for j in range(w_ref.shape[1] // IN_TN):
        cols = slice(j * IN_TN, (j + 1) * IN_TN)
        vals = jnp.dot(xn, w_ref[:, cols], preferred_element_type=F32)
        if j in IN_NORMED_TILES:
            sumsq = jnp.dot((vals * vals).astype(BF16), ones_ref[...], preferred_element_type=F32)
            vals = vals * lax.rsqrt(sumsq * (1.0 / HEAD_DIM) + RMS_EPS) * cg_ref[:, cols]
        for hh in range(heads):
            hm_ref[j * heads + hh] = vals[:, hh * HEAD_DIM:(hh + 1) * HEAD_DIM].astype(BF16)
        for ref, (jlo, jhi) in ((nsa_ref, J_NSA), (win_ref, J_WIN), (fox_ref, J_FOX)):
            if jlo <= j < jhi:
                n_cols = (jhi - jlo) * heads
                for hh in range(heads):
                    ref[pl.ds((j - jlo) * heads + hh, tm, stride=n_cols), :] = vals[:, hh * HEAD_DIM:(hh + 1) * HEAD_DIM]


def _inproj(x, gain, w_main, colgain, tm):
    m, d = x.shape
    ncol = w_main.shape[1]
    heads = IN_TN // HEAD_DIM
    n_nsa, n_win, n_fox = [(hi - lo) * heads for lo, hi in (J_NSA, J_WIN, J_FOX)]
    head_ones = jnp.asarray(np.kron(np.eye(heads, dtype=np.float32), np.ones((HEAD_DIM, HEAD_DIM), np.float32)), BF16)
    resident = lambda shape: pl.BlockSpec(shape, lambda i: (0, 0), pipeline_mode=pl.Buffered(1))
    return pl.pallas_call(
        _inproj_kernel,
        out_shape=(
            jax.ShapeDtypeStruct((m * n_nsa, HEAD_DIM), F32),
            jax.ShapeDtypeStruct((m * n_win, HEAD_DIM), F32),
            jax.ShapeDtypeStruct((m * n_fox, HEAD_DIM), F32),
            jax.ShapeDtypeStruct((N_HEAD_COLS, m, HEAD_DIM), BF16),
        ),
        grid=(m // tm,),
        in_specs=[
            pl.BlockSpec((tm, d), lambda i: (i, 0)),
            resident((1, d)),
            resident((d, ncol)),
            resident((1, ncol)),
            resident((IN_TN, IN_TN)),
        ],
        out_specs=(
            pl.BlockSpec((tm * n_nsa, HEAD_DIM), lambda i: (i, 0)),
            pl.BlockSpec((tm * n_win, HEAD_DIM), lambda i: (i, 0)),
            pl.BlockSpec((tm * n_fox, HEAD_DIM), lambda i: (i, 0)),
            pl.BlockSpec((N_HEAD_COLS, tm, HEAD_DIM), lambda i: (0, i, 0)),
        ),
        compiler_params=_cparams(("parallel",)),
        name="inproj",
    )(x, gain, w_main, colgain, head_ones)


N_GATE_COLS = 3 * NSA_HEADS


def _small_kernel(x_ref, g_ref, w_ref, b_ref, o_ref):
    xn = _rms_rows(x_ref[...], g_ref[...]).astype(BF16)
    z = jnp.dot(xn, w_ref[...], preferred_element_type=F32) + b_ref[...]
    lane = lax.broadcasted_iota(jnp.int32, z.shape, 1)
    sig = 1.0 / (1.0 + jnp.exp(-z))
    logsig = jnp.minimum(z, 0.0) - jnp.log(1.0 + jnp.exp(-jnp.abs(z)))
    o_ref[...] = jnp.where(lane < N_GATE_COLS, sig,
                           jnp.where(lane < N_GATE_COLS + FOX_HEADS, logsig, 0.0))


def _small(x, gain, w_small, b_small, tm):
    m, d = x.shape
    return pl.pallas_call(
        _small_kernel,
        out_shape=jax.ShapeDtypeStruct((m, LANE), F32),
        grid=(m // tm,),
        in_specs=[
            pl.BlockSpec((tm, d), lambda i: (i, 0)),
            pl.BlockSpec((1, d), lambda i: (0, 0)),
            pl.BlockSpec((d, LANE), lambda i: (0, 0)),
            pl.BlockSpec((1, LANE), lambda i: (0, 0)),
        ],
        out_specs=pl.BlockSpec((tm, LANE), lambda i: (i, 0)),
        compiler_params=_cparams(("parallel",)),
        name="gates",
    )(x, gain, w_small, b_small)


def _outproj_kernel(on_ref, of_ref, gn_ref, gf_ref, w_ref, x_ref, y_ref):
    a = _rms_rows(on_ref[...], gn_ref[...]).astype(BF16)
    b = _rms_rows(of_ref[...], gf_ref[...]).astype(BF16)
    half = a.shape[1]
    y = jnp.dot(a, w_ref[:half, :], preferred_element_type=F32)
    y = y + jnp.dot(b, w_ref[half:, :], preferred_element_type=F32)
    y_ref[...] = x_ref[...] + y


def _outproj(o_nsa, o_fox, g_nsa, g_fox, w_out, x, tm):
    m, d = x.shape
    wn = o_nsa.shape[1]
    wf = o_fox.shape[1]
    return pl.pallas_call(
        _outproj_kernel,
        out_shape=jax.ShapeDtypeStruct((m, d), F32),
        grid=(m // tm,),
        in_specs=[
            pl.BlockSpec((tm, wn), lambda i: (i, 0)),
            pl.BlockSpec((tm, wf), lambda i: (i, 0)),
            pl.BlockSpec((1, wn), lambda i: (0, 0)),
            pl.BlockSpec((1, wf), lambda i: (0, 0)),
            pl.BlockSpec((wn + wf, d), lambda i: (0, 0)),
            pl.BlockSpec((tm, d), lambda i: (i, 0)),
        ],
        out_specs=pl.BlockSpec((tm, d), lambda i: (i, 0)),
        compiler_params=_cparams(("parallel",)),
        name="outproj",
    )(o_nsa, o_fox, g_nsa, g_fox, w_out, x)


CUMSUM_CHUNK = 512


def _cumsum_kernel(x_ref, before_ref, o_ref):
    width = x_ref.shape[2]
    r = lax.broadcasted_iota(jnp.int32, (width, width), 0)
    c = lax.broadcasted_iota(jnp.int32, (width, width), 1)
    upper = (r <= c).astype(F32)
    local = jnp.dot(x_ref[0], upper, preferred_element_type=F32, precision=lax.Precision.HIGHEST)
    totals = jnp.broadcast_to(local[:, width - 1:width], (local.shape[0], LANE))
    offset = jnp.dot(before_ref[...], totals, preferred_element_type=F32, precision=lax.Precision.HIGHEST)
    o_ref[0] = local + jnp.concatenate([offset] * (width // LANE), axis=1)


def _cumsum(x):
    b, h, length = x.shape
    assert length % CUMSUM_CHUNK == 0
    pieces = length // CUMSUM_CHUNK
    used = h * pieces
    rows = -(-used // LANE) * LANE
    idx = np.arange(rows)
    before = ((idx[:, None] // pieces == idx[None, :] // pieces) & (idx[None, :] < idx[:, None])
              & (idx[:, None] < used))
    x = jnp.pad(x.reshape(b, used, CUMSUM_CHUNK), ((0, 0), (0, rows - used), (0, 0)))
    out = pl.pallas_call(
        _cumsum_kernel,
        out_shape=jax.ShapeDtypeStruct((b, rows, CUMSUM_CHUNK), F32),
        grid=(b,),
        in_specs=[pl.BlockSpec((1, rows, CUMSUM_CHUNK), lambda i: (i, 0, 0)),
                  pl.BlockSpec((rows, rows), lambda i: (0, 0))],
        out_specs=pl.BlockSpec((1, rows, CUMSUM_CHUNK), lambda i: (i, 0, 0)),
        compiler_params=_cparams(("parallel",)),
        name="logf_cumsum",
    )(x, jnp.asarray(before, F32))
    return out[:, :used].reshape(b, h, length)


PE_ROWS = 16


def _compress_kernel(x_ref, w1_ref, w2_ref, pe_ref, kn_ref, o_ref):
    kind = pl.program_id(0) // NSA_KV_HEADS
    n = x_ref.shape[2]
    w1 = w1_ref[0]
    h = jnp.dot(x_ref[0, 0], w1, preferred_element_type=F32)
    pw = jnp.dot(pe_ref[0], w1, preferred_element_type=F32)
    const = pw[0:1, :CMP_HIDDEN] + pw[1:2, CMP_HIDDEN:]
    hid = h[:, :CMP_HIDDEN] + pltpu.roll(h[:, CMP_HIDDEN:], n - 1, 0) + const
    act = hid / (1.0 + jnp.exp(-hid))
    out = jnp.dot(act.astype(BF16), w2_ref[0], preferred_element_type=F32)
    normed = _rms_rows(out, kn_ref[...])
    o_ref[0, 0] = jnp.where(kind == 0, normed, out).astype(BF16)


def _compress(xc, c_off, w1cat, w2, pe, k_norm, n_rows):
    nb = xc.shape[1]
    return pl.pallas_call(
        _compress_kernel,
        out_shape=jax.ShapeDtypeStruct((2 * NSA_KV_HEADS, nb, n_rows, HEAD_DIM), BF16),
        grid=(2 * NSA_KV_HEADS, nb),
        in_specs=[
            pl.BlockSpec((1, 1, n_rows, CMP_STRIDE * HEAD_DIM), lambda c, b: (c_off + c, b, 0, 0)),
            pl.BlockSpec((1, CMP_STRIDE * HEAD_DIM, 2 * CMP_HIDDEN), lambda c, b: (c // NSA_KV_HEADS, 0, 0)),
            pl.BlockSpec((1, CMP_HIDDEN, HEAD_DIM), lambda c, b: (c // NSA_KV_HEADS, 0, 0)),
            pl.BlockSpec((1, PE_ROWS, CMP_STRIDE * HEAD_DIM), lambda c, b: (c // NSA_KV_HEADS, 0, 0)),
            pl.BlockSpec((1, HEAD_DIM), lambda c, b: (0, 0)),
        ],
        out_specs=pl.BlockSpec((1, 1, n_rows, HEAD_DIM), lambda c, b: (c, b, 0, 0)),
        compiler_params=_cparams(("parallel", "parallel")),
        name="compress",
    )(xc, w1cat, w2, pe, k_norm)


def _rel_bias(dist, tbl_ref, head):
    out = jnp.full(dist.shape, tbl_ref[0, head], F32)
    for k, thr in enumerate(BUCKET_THR, start=1):
        out = jnp.where(dist >= thr, tbl_ref[k, head], out)
    return out


def _cmp_to_sel(n_cmp, n_sel, rows, cols):
    c0 = np.arange(n_cmp)[:, None] * CMP_STRIDE
    s0 = np.arange(n_sel)[None, :] * SEL_BLOCK
    inter = np.clip(np.minimum(c0 + CMP_LEN, s0 + SEL_BLOCK) - np.maximum(c0, s0), 0, None)
    m = np.zeros((rows, cols), np.float32)
    m[:n_cmp, :n_sel] = inter / CMP_LEN
    return jnp.asarray(m)


def _transpose_tiles(x):
    n = x.shape[0] // LANE
    xf = x.astype(F32)
    return jnp.concatenate([xf[i * LANE:(i + 1) * LANE].T for i in range(n)], axis=1)


def _untranspose_tiles(xt):
    n = xt.shape[1] // LANE
    return jnp.concatenate([xt[:, i * LANE:(i + 1) * LANE].T for i in range(n)], axis=0)


AUX_ROWS = 16


def _fill_transposed(dst_ref, src_ref):
    def body(k, carry):
        st = pl.multiple_of(k * LANE, LANE)
        dst_ref[:HEAD_DIM, pl.ds(st, LANE)] = src_ref[0, 0, pl.ds(st, LANE), :].astype(F32).T.astype(BF16)
        return carry

    lax.fori_loop(0, src_ref.shape[2] // LANE, body, 0)


FOX_TQ = 256
FOX_KEY_BLOCK = 1024


def _split3(x):
    hi = x.astype(BF16)
    r1 = x - hi.astype(F32)
    mid = r1.astype(BF16)
    lo = (r1 - mid.astype(F32)).astype(BF16)
    return hi, mid, lo


def _lane_select3(parts, shape):
    lane = lax.broadcasted_iota(jnp.int32, shape, 1)
    hi, mid, lo = [part.astype(F32) for part in parts]
    return jnp.where(lane == 0, hi, jnp.where(lane == 1, mid, jnp.where(lane == 2, lo, 0.0))).astype(BF16)


def _ones_rows(width):
    row = lax.broadcasted_iota(jnp.int32, (AUX_ROWS, width), 0)
    return jnp.where(row == 0, 1.0, 0.0).astype(BF16)


def _finish_aug(acc):
    return acc[:HEAD_DIM] * (1.0 / acc[HEAD_DIM:HEAD_DIM + 1])


def _fox_prompt_kernel(q_ref, k_ref, v_ref, crow_ref, o_ref, vt_ref, ka_ref, s_ref):
    qt = pl.program_id(2)
    tq = FOX_TQ

    @pl.when(qt == 0)
    def _():
        _fill_transposed(vt_ref, v_ref)
        vt_ref[HEAD_DIM:, :] = _ones_rows(vt_ref.shape[1])

        def body(k, carry):
            st = pl.multiple_of(k * LANE, LANE)
            c_col = jnp.broadcast_to(crow_ref[0, 0, :, pl.ds(st, LANE)], (LANE, LANE)).T
            ka_ref[pl.ds(st, LANE), :HEAD_DIM] = k_ref[0, 0, pl.ds(st, LANE), :]
            ka_ref[pl.ds(st, LANE), HEAD_DIM:] = _lane_select3(_split3(c_col), (LANE, LANE))
            return carry

        lax.fori_loop(0, k_ref.shape[2] // LANE, body, 0)

    q0 = pl.multiple_of(qt * tq, tq)
    q_t = _transpose_tiles(q_ref[0]).astype(BF16)
    row = lax.broadcasted_iota(jnp.int32, (LANE, tq), 0)
    q_aug = jnp.concatenate([q_t, jnp.where(row < 3, -1.0, 0.0).astype(BF16)], axis=0)
    c_q = crow_ref[0, 0, :, pl.ds(q0, tq)]
    blk = FOX_KEY_BLOCK

    last = q0 // blk
    rel = (lax.broadcasted_iota(jnp.int32, (blk, tq), 0) - lax.broadcasted_iota(jnp.int32, (blk, tq), 1))

    def run(n_blocks):
        top = jnp.full((8, tq), NEG, F32)
        for kb in range(n_blocks):
            start = kb * blk
            s = jnp.dot(ka_ref[start:start + blk, :], q_aug, preferred_element_type=F32)
            if kb == n_blocks - 1:
                s = jnp.where(rel <= q0 - start, s, NEG)
            s_ref[start:start + blk, :] = s
            top = jnp.maximum(top, jnp.max(s.reshape(blk // 8, 8, tq), axis=0))
        m = jnp.max(top, axis=0, keepdims=True) + c_q
        shift = m - c_q
        acc = jnp.zeros((HEAD_DIM + AUX_ROWS, tq), F32)
        for kb in range(n_blocks):
            start = kb * blk
            p = jnp.exp(s_ref[start:start + blk, :] - shift).astype(BF16)
            acc = acc + jnp.dot(vt_ref[:, start:start + blk], p, preferred_element_type=F32)
        return acc

    n_max = k_ref.shape[2] // blk
    acc = lax.switch(last, [functools.partial(run, n) for n in range(1, n_max + 1)])
    o_ref[...] = _untranspose_tiles(_finish_aug(acc))


def _fox_prompt(hm, hm4, crow, nb, seq):
    nq = seq // FOX_TQ
    return pl.pallas_call(
        _fox_prompt_kernel,
        out_shape=jax.ShapeDtypeStruct((nb * seq, FOX_HEADS * HEAD_DIM), F32),
        grid=(nb, FOX_HEADS, nq),
        in_specs=[
            pl.BlockSpec((1, FOX_TQ, HEAD_DIM), lambda b, h, t: (HM_Q_FOX + h, b * nq + t, 0)),
            pl.BlockSpec((1, 1, seq, HEAD_DIM), lambda b, h, t: (HM_K_FOX + h, b, 0, 0)),
            pl.BlockSpec((1, 1, seq, HEAD_DIM), lambda b, h, t: (HM_V_FOX + h, b, 0, 0)),
            pl.BlockSpec((1, 1, 1, seq), lambda b, h, t: (b, h, 0, 0)),
        ],
        out_specs=pl.BlockSpec((FOX_TQ, HEAD_DIM), lambda b, h, t: (b * nq + t, h)),
        scratch_shapes=[pltpu.VMEM((HEAD_DIM + AUX_ROWS, seq), BF16), pltpu.VMEM((seq, 2 * HEAD_DIM), BF16),
                        pltpu.VMEM((seq, FOX_TQ), F32)],
        compiler_params=_cparams(("parallel", "parallel", "arbitrary")),
        name="fox_prompt",
    )(hm, hm4, hm4, crow)


NSA_TQ = LANE
NSA_COLS = NSA_GROUP * NSA_TQ
WIN_TILES = WINDOW // NSA_TQ + 1
SLC_BLOCK_TILES = 4
SLC_MASK_ROWS = SLC_BLOCK_TILES * NSA_TQ // SEL_BLOCK


def _nsa_prompt_kernel(tbl_ref, q_ref, kc_ref, vc_ref, mselt_ref, ks_ref, vs_ref, kw_ref, vw_ref, gate_ref, o_ref,
                       vst_ref, vwt_ref, vct_ref, wb_ref, pc_ref, drop_ref, ksa_ref, wbd_ref, s_ref, *, n_sel, cmp_back):
    g = pl.program_id(1)
    qt = pl.program_id(2)
    tq = NSA_TQ
    cols = NSA_COLS
    heads = [NSA_GROUP * g + hh for hh in range(NSA_GROUP)]
    ncp = kc_ref.shape[2]
    nj = mselt_ref.shape[0]

    @pl.when(qt == 0)
    def _():
        _fill_transposed(vst_ref, vs_ref)
        _fill_transposed(vwt_ref, vw_ref)
        vst_ref[HEAD_DIM:, :] = _ones_rows(vst_ref.shape[1])
        vwt_ref[HEAD_DIM:, :] = _ones_rows(vwt_ref.shape[1])
        vct_ref[...] = _transpose_tiles(vc_ref[0, 0]).astype(BF16)

        def fill_keys(k, carry):
            st = pl.multiple_of(k * tq, tq)
            lane = lax.broadcasted_iota(jnp.int32, (tq, tq), 1)
            blk_in_step = (k % SLC_BLOCK_TILES) * (tq // SEL_BLOCK) + lax.broadcasted_iota(jnp.int32, (tq, tq), 0) // SEL_BLOCK
            extra = (lane == blk_in_step) | ((lane >= SLC_MASK_ROWS) & (lane < SLC_MASK_ROWS + 3))
            ksa_ref[pl.ds(st, tq), :HEAD_DIM] = ks_ref[0, 0, pl.ds(st, tq), :]
            ksa_ref[pl.ds(st, tq), HEAD_DIM:] = jnp.where(extra, 1.0, 0.0).astype(BF16)
            return carry

        lax.fori_loop(0, ks_ref.shape[2] // tq, fill_keys, 0)
        key = lax.broadcasted_iota(jnp.int32, (tq, tq), 0)
        qry = lax.broadcasted_iota(jnp.int32, (tq, tq), 1)
        blk = lax.broadcasted_iota(jnp.int32, (pc_ref.shape[0], tq), 0) - cmp_back
        d_cmp = lax.broadcasted_iota(jnp.int32, (pc_ref.shape[0], tq), 1) - (blk * CMP_STRIDE + (CMP_LEN - 1))
        for hh in range(NSA_GROUP):
            sl = slice(hh * tq, (hh + 1) * tq)
            far = tbl_ref[REL_BUCKETS - 1, heads[hh]]
            for delta in range(2):
                d = delta * tq + qry - key
                wb_ref[delta, :, sl] = jnp.where(d >= 0, _rel_bias(jnp.maximum(d, 0), tbl_ref, heads[hh]), NEG)
            for delta in range(2, WIN_TILES - 1):
                wb_ref[delta, :, sl] = jnp.full((tq, tq), far, F32)
            wb_ref[WIN_TILES - 1, :, sl] = jnp.where(qry < key, far, NEG)
            wb_ref[WIN_TILES, :, sl] = jnp.full((tq, tq), NEG, F32)
            for delta in range(2):
                wbd_ref[delta, :, sl] = wb_ref[delta, :, sl] - far
            wbd_ref[2, :, sl] = jnp.zeros((tq, tq), F32)
            wbd_ref[3, :, sl] = jnp.full((tq, tq), NEG, F32)
            pc_ref[:, sl] = jnp.where(d_cmp >= 0, _rel_bias(jnp.maximum(d_cmp, 0), tbl_ref, heads[hh]), NEG)

    q_t = jnp.concatenate([q_ref[hh].astype(F32).T for hh in range(NSA_GROUP)], axis=1).astype(BF16)

    off = pl.multiple_of(cmp_back - qt * (tq // CMP_STRIDE), 8)
    s_t = jnp.dot(kc_ref[0, 0], q_t, preferred_element_type=F32) + pc_ref[pl.ds(off, ncp), :]
    m = jnp.max(s_t, axis=0, keepdims=True)
    e = jnp.exp(s_t - m)
    inv = jnp.where(m > 0.5 * NEG, 1.0 / jnp.sum(e, axis=0, keepdims=True), 0.0)
    p_t = e * inv
    o_cmp = jnp.dot(vct_ref[...], p_t.astype(BF16), preferred_element_type=F32)
    p_sum = p_t[:, :tq]
    for hh in range(1, NSA_GROUP):
        p_sum = p_sum + p_t[:, hh * tq:(hh + 1) * tq]
    imp = jnp.dot(mselt_ref[...], p_sum, preferred_element_type=F32, precision=lax.Precision.HIGHEST)

    jrow = lax.broadcasted_iota(jnp.int32, (nj, tq), 0)
    pos = qt * tq + lax.broadcasted_iota(jnp.int32, (nj, tq), 1)
    blk_q = jnp.right_shift(pos, int(math.log2(SEL_BLOCK)))
    forced = (jrow == 0) | (jrow == blk_q) | (jrow == blk_q - 1)
    score = jnp.where(forced, FORCED_SCORE, jnp.where(jrow <= blk_q, imp, -1.0))
    score = jnp.where(jrow < n_sel, score, -2.0)
    ranks = []
    for r in range(nj // 8):
        mine = score[r * 8:(r + 1) * 8]
        jmine = jrow[r * 8:(r + 1) * 8]
        rank = jnp.zeros((8, tq), F32)
        for other in range(n_sel):
            row = jnp.broadcast_to(score[other:other + 1], (8, tq))
            if other < r * 8:
                beats = row >= mine
            elif other >= (r + 1) * 8:
                beats = row > mine
            else:
                beats = (row > mine) | ((row == mine) & (jmine > other))
            rank = rank + jnp.where(beats, 1.0, 0.0)
        ranks.append(rank)
    rank = jnp.concatenate(ranks, axis=0)
    drop = jnp.where((rank < SEL_TOPK) & (jrow < n_sel), 0.0, NEG)
    drop_ref[...] = jnp.concatenate([drop] * NSA_GROUP, axis=1)

    blk = SLC_BLOCK_TILES * tq
    col = lax.broadcasted_iota(jnp.int32, (8, cols), 1)
    far_row = jnp.full((8, cols), tbl_ref[REL_BUCKETS - 1, heads[NSA_GROUP - 1]], F32)
    for hh in range(NSA_GROUP - 1):
        far_row = jnp.where(col // tq == hh, tbl_ref[REL_BUCKETS - 1, heads[hh]], far_row)
    row8 = lax.broadcasted_iota(jnp.int32, (8, cols), 0)
    parts = [part.astype(F32) for part in _split3(far_row)]
    far_rows = jnp.where(row8 == 0, parts[0], jnp.where(row8 == 1, parts[1], jnp.where(row8 == 2, parts[2], 0.0)))
    pad_rows = jnp.zeros((HEAD_DIM - SLC_MASK_ROWS - 8, cols), BF16)

    last = qt // SLC_BLOCK_TILES

    def slc_run(n_blocks):
        top = jnp.full((8, cols), NEG, F32)
        for kb in range(n_blocks):
            start = kb * blk
            masks = drop_ref[kb * SLC_MASK_ROWS:(kb + 1) * SLC_MASK_ROWS, :]
            extra = jnp.concatenate([masks, far_rows], axis=0).astype(BF16)
            q_aug = jnp.concatenate([q_t, extra, pad_rows], axis=0)
            s = jnp.dot(ksa_ref[start:start + blk, :], q_aug, preferred_element_type=F32)
            if kb >= n_blocks - 2:
                terms = []
                for i in range(SLC_BLOCK_TILES):
                    delta = qt - (kb * SLC_BLOCK_TILES + i)
                    terms.append(wbd_ref[jnp.where(delta < 0, 3, jnp.minimum(delta, 2))])
                s = s + jnp.concatenate(terms, axis=0)
            s_ref[start:start + blk, :] = s
            top = jnp.maximum(top, jnp.max(s.reshape(blk // 8, 8, cols), axis=0))
        m = jnp.max(top, axis=0, keepdims=True)
        acc = jnp.zeros((HEAD_DIM + AUX_ROWS, cols), F32)
        for kb in range(n_blocks):
            start = kb * blk
            p = jnp.exp(s_ref[start:start + blk, :] - m).astype(BF16)
            acc = acc + jnp.dot(vst_ref[:, start:start + blk], p, preferred_element_type=F32)
        return acc

    n_max = ks_ref.shape[2] // blk
    o_slc = _finish_aug(lax.switch(last, [functools.partial(slc_run, n) for n in range(1, n_max + 1)]))

    w0 = jnp.maximum(qt - (WIN_TILES - 1), 0)
    start = pl.multiple_of(w0 * tq, tq)
    span = WIN_TILES * tq
    s = jnp.dot(kw_ref[0, 0, pl.ds(start, span), :], q_t, preferred_element_type=F32)
    terms = []
    for i in range(WIN_TILES):
        delta = qt - (w0 + i)
        terms.append(wb_ref[jnp.where(delta < 0, WIN_TILES, delta)])
    s = s + jnp.concatenate(terms, axis=0)
    p = jnp.exp(s - jnp.max(s, axis=0, keepdims=True))
    o_win = _finish_aug(jnp.dot(vwt_ref[:, pl.ds(start, span)], p.astype(BF16), preferred_element_type=F32))

    gate = gate_ref[0, 0, 0]
    o_t = gate[0:1] * o_cmp + gate[1:2] * o_slc + gate[2:3] * o_win
    for hh in range(NSA_GROUP):
        o_ref[:, hh * HEAD_DIM:(hh + 1) * HEAD_DIM] = o_t[:, hh * tq:(hh + 1) * tq].T


def _nsa_prompt(rel_table, hm, hm4, kcv, mselt, gates_t, nb, seq, n_sel):
    tq = NSA_TQ
    nq = seq // tq
    ncp = kcv.shape[2]
    nj = mselt.shape[0]
    cmp_back = (nq - 1) * (tq // CMP_STRIDE)
    kern = functools.partial(_nsa_prompt_kernel, n_sel=n_sel, cmp_back=cmp_back)
    kv_spec = lambda head0: pl.BlockSpec((1, 1, seq, HEAD_DIM), lambda b, g, t: (head0 + g, b, 0, 0))
    return pl.pallas_call(
        kern,
        out_shape=jax.ShapeDtypeStruct((nb * seq, NSA_HEADS * HEAD_DIM), F32),
        grid=(nb, NSA_KV_HEADS, nq),
        in_specs=[
            pl.BlockSpec(memory_space=pltpu.SMEM),
            pl.BlockSpec((NSA_GROUP, tq, HEAD_DIM), lambda b, g, t: (g, b * nq + t, 0)),
            pl.BlockSpec((1, 1, ncp, HEAD_DIM), lambda b, g, t: (g, b, 0, 0)),
            pl.BlockSpec((1, 1, ncp, HEAD_DIM), lambda b, g, t: (NSA_KV_HEADS + g, b, 0, 0)),
            pl.BlockSpec((nj, ncp), lambda b, g, t: (0, 0)),
            kv_spec(HM_K_SLC), kv_spec(HM_V_SLC), kv_spec(HM_K_WIN), kv_spec(HM_V_WIN),
            pl.BlockSpec((1, 1, 1, 8, NSA_COLS), lambda b, g, t: (b, g, t, 0, 0)),
        ],
        out_specs=pl.BlockSpec((tq, NSA_GROUP * HEAD_DIM), lambda b, g, t: (b * nq + t, g)),
        scratch_shapes=[
            pltpu.VMEM((HEAD_DIM + AUX_ROWS, seq), BF16),
            pltpu.VMEM((HEAD_DIM + AUX_ROWS, seq), BF16),
            pltpu.VMEM((HEAD_DIM, ncp), BF16),
            pltpu.VMEM((WIN_TILES + 1, tq, NSA_COLS), F32),
            pltpu.VMEM((cmp_back + ncp, NSA_COLS), F32),
            pltpu.VMEM((nj, NSA_COLS), F32),
            pltpu.VMEM((seq, 2 * HEAD_DIM), BF16),
            pltpu.VMEM((4, tq, NSA_COLS), F32),
            pltpu.VMEM((seq, NSA_COLS), F32),
        ],
        compiler_params=_cparams(("parallel", "parallel", "arbitrary")),
        name="nsa_prompt",
    )(rel_table, hm, kcv, kcv, mselt, hm4, hm4, hm4, hm4, gates_t)


CHUNK_ROWS = 512


def _chunkify_kernel(x_ref, o_ref, *, n_cols):
    n = o_ref.shape[2]
    for c in range(o_ref.shape[0]):
        for s in range(CMP_STRIDE):
            o_ref[c, 0, :, s * HEAD_DIM:(s + 1) * HEAD_DIM] = (
                x_ref[0, pl.ds(s * n_cols + c, n, stride=CMP_STRIDE * n_cols), :].astype(BF16))


def _chunkify(rows3, n_cols, n_heads, seq):
    nb = rows3.shape[0]
    tr = _largest_tile(seq, (CHUNK_ROWS, 256))
    return pl.pallas_call(
        functools.partial(_chunkify_kernel, n_cols=n_cols),
        out_shape=jax.ShapeDtypeStruct((n_heads, nb, seq // CMP_STRIDE, CMP_STRIDE * HEAD_DIM), BF16),
        grid=(nb, seq // tr),
        in_specs=[pl.BlockSpec((1, tr * n_cols, HEAD_DIM), lambda b, i: (b, i, 0))],
        out_specs=pl.BlockSpec((n_heads, 1, tr // CMP_STRIDE, CMP_STRIDE * HEAD_DIM), lambda b, i: (0, b, i, 0)),
        compiler_params=_cparams(("parallel", "parallel")),
        name="chunkify",
    )(rows3)


REGROUP_PAGES = 8


N_CACHE_COLS = 16
N_CMP_COLS = 2 * NSA_KV_HEADS


def _regroup_kernel(pt_ref, *refs, n_steps):
    del pt_ref
    npg = REGROUP_PAGES
    nsa_in, lf_in = refs[:npg], refs[npg:2 * npg]
    xc_out, slc_out, lf_out = refs[2 * npg:]
    is_tail = pl.program_id(1) >= n_steps
    chunks = PAGE_SIZE // CMP_STRIDE
    chunk_stride = CMP_STRIDE * N_CACHE_COLS

    @pl.when(is_tail)
    def _():
        slc_out[...] = jnp.zeros(slc_out.shape, slc_out.dtype)
        lf_out[...] = jnp.zeros(lf_out.shape, lf_out.dtype)

    @pl.when(jnp.logical_not(is_tail))
    def _():
        for p in range(npg):
            rows = slice(p * PAGE_SIZE, (p + 1) * PAGE_SIZE)
            for c in range(N_CACHE_COLS - N_CMP_COLS):
                slc_out[c, 0, rows, :] = nsa_in[p][0, pl.ds(N_CMP_COLS + c, PAGE_SIZE, stride=N_CACHE_COLS), :].astype(BF16)
            lf_out[0, rows, :] = lf_in[p][0]
        for pair in range(npg // 2):
            for c in range(N_CMP_COLS):
                for s in range(CMP_STRIDE):
                    first = s * N_CACHE_COLS + c
                    both = [nsa_in[2 * pair + i][0, pl.ds(first, chunks, stride=chunk_stride), :] for i in range(2)]
                    xc_out[c, 0, pair * 2 * chunks:(pair + 1) * 2 * chunks, s * HEAD_DIM:(s + 1) * HEAD_DIM] = (
                        jnp.concatenate(both, axis=0).astype(BF16))


def _regroup(page_table, cache_nsa, cache_logf, lk):
    nb, n_pages = page_table.shape
    npg = REGROUP_PAGES
    n_steps = n_pages // npg
    rows = npg * PAGE_SIZE
    last = n_steps - 1
    n_slc = N_CACHE_COLS - N_CMP_COLS
    chunk_w = CMP_STRIDE * HEAD_DIM

    def page_map(p):
        return lambda b, s, pt: (pt[b, jnp.minimum(s, last) * npg + p], 0, 0)

    def specs(arr):
        return [pl.BlockSpec((1,) + arr.shape[1:], page_map(p)) for p in range(npg)]

    assert (lk - n_pages * PAGE_SIZE) % rows == 0
    n_tail = (lk - n_pages * PAGE_SIZE) // rows
    grid_spec = pltpu.PrefetchScalarGridSpec(
        num_scalar_prefetch=1,
        grid=(nb, n_steps + n_tail),
        in_specs=specs(cache_nsa) + specs(cache_logf),
        out_specs=(
            pl.BlockSpec((N_CMP_COLS, 1, rows // CMP_STRIDE, chunk_w), lambda b, s, pt: (0, b, jnp.minimum(s, last), 0)),
            pl.BlockSpec((n_slc, 1, rows, HEAD_DIM), lambda b, s, pt: (0, b, s, 0)),
            pl.BlockSpec((1, rows, cache_logf.shape[2]), lambda b, s, pt: (b, s, 0)),
        ),
    )
    return pl.pallas_call(
        functools.partial(_regroup_kernel, n_steps=n_steps),
        out_shape=(
            jax.ShapeDtypeStruct((N_CMP_COLS, nb, n_pages * PAGE_SIZE // CMP_STRIDE, chunk_w), BF16),
            jax.ShapeDtypeStruct((n_slc, nb, lk, HEAD_DIM), BF16),
            jax.ShapeDtypeStruct((nb, lk, cache_logf.shape[2]), F32),
        ),
        grid_spec=grid_spec,
        compiler_params=_cparams(("parallel", "arbitrary")),
        name="cache_regroup",
    )(page_table, *([cache_nsa] * npg), *([cache_logf] * npg))


FOX_DEC_PAGES = 8


def _fox_decode_kernel(pt_ref, *refs, dseq):
    del pt_ref
    npg = FOX_DEC_PAGES
    pages = refs[:npg]
    qbd_ref, ccol_ref, cq_ref, knew_ref, vnew_ref, o_ref, m_ref, acc_ref = refs[npg:]
    step = pl.program_id(1)
    is_tail = step == pl.num_programs(1) - 1
    width = FOX_HEADS * HEAD_DIM

    @pl.when(step == 0)
    def _():
        m_ref[...] = jnp.full(m_ref.shape, NEG, F32)
        acc_ref[...] = jnp.zeros(acc_ref.shape, F32)

    def fold(k_blk, v_blk, bias):
        n = k_blk.shape[0]
        s_t = jnp.dot(k_blk, qbd_ref[0], preferred_element_type=F32) + bias
        m_new = jnp.maximum(m_ref[...], jnp.max(s_t, axis=0, keepdims=True))
        alpha = jnp.exp(m_ref[...] - m_new)
        p = jnp.exp(s_t - m_new)
        p_t = jnp.concatenate([p[i * LANE:(i + 1) * LANE].T for i in range(n // LANE)], axis=1).astype(BF16)
        v_aug = jnp.concatenate([v_blk, jnp.ones((n, LANE), BF16)], axis=1)
        upd = jnp.dot(p_t, v_aug, preferred_element_type=F32)
        alpha_col = jnp.broadcast_to(alpha, (LANE, LANE)).T
        acc_ref[...] = acc_ref[...] * jnp.concatenate([alpha_col] * (width // LANE + 1), axis=1) + upd
        m_ref[...] = m_new

    @pl.when(jnp.logical_not(is_tail))
    def _():
        by_col = [pltpu.einshape("tcd->ctd", pages[p][0].reshape(PAGE_SIZE, N_CACHE_COLS, HEAD_DIM))
                  for p in range(npg)]

        def heads_of(p, first):
            return jnp.concatenate([by_col[p][first + h] for h in range(FOX_HEADS)], axis=1).astype(BF16)

        k_blk = jnp.concatenate([heads_of(p, 0) for p in range(npg)], axis=0)
        v_blk = jnp.concatenate([heads_of(p, FOX_HEADS) for p in range(npg)], axis=0)
        fold(k_blk, v_blk, cq_ref[0] - ccol_ref[0])

    @pl.when(is_tail)
    def _():
        row = lax.broadcasted_iota(jnp.int32, (LANE, LANE), 0)
        t_of_col = lax.broadcasted_iota(jnp.int32, (LANE, LANE), 1) % dseq
        bias = jnp.where(row <= t_of_col, cq_ref[0] - ccol_ref[0, :LANE, :], NEG)
        fold(knew_ref[0], vnew_ref[0], bias)
        acc = acc_ref[...]
        inv = 1.0 / acc[:, width:]
        o_ref[0] = acc[:, :width] * jnp.concatenate([inv] * (width // LANE), axis=1)


def _fox_decode(page_table, cache_fox, qbd, ccols, cq, knew, vnew, dseq):
    nb, n_pages = page_table.shape
    npg = FOX_DEC_PAGES
    n_steps = n_pages // npg
    last = n_steps - 1
    width = FOX_HEADS * HEAD_DIM

    def page_map(p):
        return lambda b, s, pt: (pt[b, jnp.minimum(s, last) * npg + p], 0, 0)

    per_b = lambda shape: pl.BlockSpec((1,) + shape, lambda b, s, pt: (b, 0, 0))
    grid_spec = pltpu.PrefetchScalarGridSpec(
        num_scalar_prefetch=1,
        grid=(nb, n_steps + 1),
        in_specs=[pl.BlockSpec((1,) + cache_fox.shape[1:], page_map(p)) for p in range(npg)] + [
            per_b((width, LANE)),
            pl.BlockSpec((1, npg * PAGE_SIZE, LANE), lambda b, s, pt: (b, s, 0)),
            per_b((1, LANE)),
            per_b((LANE, width)),
            per_b((LANE, width)),
        ],
        out_specs=per_b((LANE, width)),
        scratch_shapes=[pltpu.VMEM((1, LANE), F32), pltpu.VMEM((LANE, width + LANE), F32)],
    )
    return pl.pallas_call(
        functools.partial(_fox_decode_kernel, dseq=dseq),
        out_shape=jax.ShapeDtypeStruct((nb, LANE, width), F32),
        grid_spec=grid_spec,
        compiler_params=_cparams(("parallel", "arbitrary")),
        name="fox_decode",
    )(page_table, *([cache_fox] * npg), qbd, ccols, cq, knew, vnew)


NSA_DEC_KEYS = 2048


def _lane_transpose(p):
    return jnp.concatenate([p[i * LANE:(i + 1) * LANE].T for i in range(p.shape[0] // LANE)], axis=1).astype(BF16)


def _nsa_decode_kernel(qbd_ref, kc_ref, vc_ref, mselt_ref, pair_ref, tcol_ref, ks_ref, vs_ref, kw_ref, vw_ref,
                       gate_ref, o_ref, drop_ref, m_ref, acc_ref, ocmp_ref, *, dseq, past, n_sel):
    step = pl.program_id(1)
    n_tiles = pl.num_programs(1)
    width = NSA_KV_HEADS * HEAD_DIM
    qbd = qbd_ref[0]
    col = lax.broadcasted_iota(jnp.int32, (1, LANE), 1)
    q_pos = past + col % dseq

    def side_by_side(ref, rows=None):
        parts = [ref[g, 0] if rows is None else ref[g, 0, rows, :] for g in range(NSA_KV_HEADS)]
        return jnp.concatenate(parts, axis=1)

    def rel_bias_cols(dist):
        out = jnp.broadcast_to(tcol_ref[0:1, :], dist.shape)
        for k, thr in enumerate(BUCKET_THR, start=1):
            out = jnp.where(dist >= thr, tcol_ref[k:k + 1, :], out)
        return out

    def softmax_pv(s_t, v_all):
        m = jnp.max(s_t, axis=0, keepdims=True)
        e = jnp.exp(s_t - m)
        inv = jnp.where(m > 0.5 * NEG, 1.0 / jnp.sum(e, axis=0, keepdims=True), 0.0)
        p = e * inv
        return p, jnp.dot(_lane_transpose(p), v_all, preferred_element_type=F32)

    @pl.when(step == 0)
    def _():
        m_ref[...] = jnp.full(m_ref.shape, NEG, F32)
        acc_ref[...] = jnp.zeros(acc_ref.shape, F32)
        ncp = kc_ref.shape[2]
        blk_end = lax.broadcasted_iota(jnp.int32, (ncp, LANE), 0) * CMP_STRIDE + (CMP_LEN - 1)
        d = q_pos - blk_end
        s_t = jnp.dot(side_by_side(kc_ref), qbd, preferred_element_type=F32)
        s_t = jnp.where(d >= 0, s_t + rel_bias_cols(jnp.maximum(d, 0)), NEG)
        p, ocmp_ref[...] = softmax_pv(s_t, side_by_side(vc_ref))
        imp = jnp.dot(mselt_ref[...], p, preferred_element_type=F32, precision=lax.Precision.HIGHEST)
        imp = jnp.dot(imp, pair_ref[...], preferred_element_type=F32, precision=lax.Precision.HIGHEST)
        nj = imp.shape[0]
        jrow = lax.broadcasted_iota(jnp.int32, (nj, LANE), 0)
        blk_q = jnp.right_shift(q_pos, int(math.log2(SEL_BLOCK)))
        forced = (jrow == 0) | (jrow == blk_q) | (jrow == blk_q - 1)
        score = jnp.where(forced, FORCED_SCORE, jnp.where(jrow <= blk_q, imp, -1.0))
        score = jnp.where(jrow < n_sel, score, -2.0)
        drop_ref[...] = score
        rank = jnp.zeros((nj, LANE), F32)

        def count(other, rank):
            row = jnp.broadcast_to(drop_ref[pl.ds(other, 1), :], (nj, LANE))
            beats = (row > score) | ((row == score) & (jrow > other))
            return rank + jnp.where(beats, 1.0, 0.0)

        rank = lax.fori_loop(0, n_sel, count, rank)
        far = tcol_ref[REL_BUCKETS - 1:REL_BUCKETS, :]
        drop_ref[...] = jnp.where((rank < SEL_TOPK) & (jrow <= blk_q), far, NEG)

    tile = ks_ref.shape[2]
    per_tile = tile // SEL_BLOCK
    start = step * tile
    first_blk = pl.multiple_of(step * per_tile, 8)
    rows = [jnp.broadcast_to(drop_ref[pl.ds(first_blk + i, 1), :], (SEL_BLOCK, LANE)) for i in range(per_tile)]
    s_t = jnp.dot(side_by_side(ks_ref), qbd, preferred_element_type=F32) + jnp.concatenate(rows, axis=0)

    def near_fix(s_t):
        def fix(rows_at, s_rows):
            key_pos = start + rows_at + lax.broadcasted_iota(jnp.int32, (LANE, LANE), 0)
            d = q_pos - key_pos
            far = tcol_ref[REL_BUCKETS - 1:REL_BUCKETS, :]
            return jnp.where(d >= 0, s_rows + (rel_bias_cols(jnp.maximum(d, 0)) - far), NEG)

        head = fix(0, s_t[:LANE])
        tail = fix(tile - LANE, s_t[tile - LANE:])
        return jnp.concatenate([head, s_t[LANE:tile - LANE], tail], axis=0)

    is_near = (start + tile > past - FAR_DIST)
    s_t = lax.cond(is_near, near_fix, lambda s: s, s_t)
    m_new = jnp.maximum(m_ref[...], jnp.max(s_t, axis=0, keepdims=True))
    alpha = jnp.exp(m_ref[...] - m_new)
    p = jnp.exp(s_t - m_new)
    v_aug = jnp.concatenate([side_by_side(vs_ref), jnp.ones((tile, LANE), BF16)], axis=1)
    upd = jnp.dot(_lane_transpose(p), v_aug, preferred_element_type=F32)
    alpha_col = jnp.broadcast_to(alpha, (LANE, LANE)).T
    acc_ref[...] = acc_ref[...] * jnp.concatenate([alpha_col] * (width // LANE + 1), axis=1) + upd
    m_ref[...] = m_new

    @pl.when(step == n_tiles - 1)
    def _():
        acc = acc_ref[...]
        o_slc = acc[:, :width] * jnp.concatenate([1.0 / acc[:, width:]] * (width // LANE), axis=1)
        span = kw_ref.shape[2]
        key_pos = (past - WINDOW) + lax.broadcasted_iota(jnp.int32, (span, LANE), 0)
        d = q_pos - key_pos
        s_w = jnp.dot(side_by_side(kw_ref), qbd, preferred_element_type=F32)
        s_w = jnp.where((d >= 0) & (d < WINDOW), s_w + rel_bias_cols(jnp.maximum(d, 0)), NEG)
        _, o_win = softmax_pv(s_w, side_by_side(vw_ref))
        g = gate_ref[0]
        tile4 = lambda a: jnp.concatenate([a] * (width // LANE), axis=1)
        o_ref[0] = tile4(g[0]) * ocmp_ref[...] + tile4(g[1]) * o_slc + tile4(g[2]) * o_win


def _nsa_decode(qbd, kcv, mselt, pair, tcols, nsa_dec, win_dec, gcols, *, dseq, past, n_sel):
    nb = qbd.shape[0]
    ncp = kcv.shape[2]
    nj = mselt.shape[0]
    span = win_dec.shape[2]
    tile = NSA_DEC_KEYS
    n_tiles = -(-nsa_dec.shape[2] // tile)
    width = NSA_KV_HEADS * HEAD_DIM
    kern = functools.partial(_nsa_decode_kernel, dseq=dseq, past=past, n_sel=n_sel)
    grp = lambda rows, half, tiled: pl.BlockSpec(
        (NSA_KV_HEADS, 1, rows, HEAD_DIM), (lambda b, s: (half, b, s, 0)) if tiled else (lambda b, s: (half, b, 0, 0)))
    const = lambda shape: pl.BlockSpec(shape, lambda b, s: (0,) * len(shape))
    return pl.pallas_call(
        kern,
        out_shape=jax.ShapeDtypeStruct((nb, LANE, width), F32),
        grid=(nb, n_tiles),
        in_specs=[
            pl.BlockSpec((1, width, LANE), lambda b, s: (b, 0, 0)),
            grp(ncp, 0, False), grp(ncp, 1, False),
            const((nj, ncp)), const((LANE, LANE)), const((REL_BUCKETS, LANE)),
            grp(tile, 0, True), grp(tile, 1, True),
            grp(span, 0, False), grp(span, 1, False),
            pl.BlockSpec((1, 3, LANE, LANE), lambda b, s: (b, 0, 0, 0)),
        ],
        out_specs=pl.BlockSpec((1, LANE, width), lambda b, s: (b, 0, 0)),
        scratch_shapes=[
            pltpu.VMEM((nj, LANE), F32),
            pltpu.VMEM((1, LANE), F32),
            pltpu.VMEM((LANE, width + LANE), F32),
            pltpu.VMEM((LANE, width), F32),
        ],
        compiler_params=_cparams(("parallel", "arbitrary")),
        name="nsa_decode",
    )(qbd, kcv, kcv, mselt, pair, tcols, nsa_dec, nsa_dec, win_dec, win_dec, gcols)


def _largest_tile(n, candidates):
    for c in candidates:
        if n % c == 0:
            return c
    raise ValueError(f"no tile in {candidates} divides {n}")


FFN_ROW_TILES = (512, 256, 128)
ROW_TILES = (512, 256, 128)


def _token_stage_in(x, p):
    m = x.shape[0]
    tm = m if m < ROW_TILES[-1] else _largest_tile(m, ROW_TILES)
    tm_ffn = m if m < FFN_ROW_TILES[-1] else _largest_tile(m, FFN_ROW_TILES)
    x1 = _ffn(x, p["norm_ffn1"], p["wg1"], p["wu1"], p["wd1"], tm_ffn, p["tf"])
    tm_in = m if m < ROW_TILES[-1] else _largest_tile(m, ROW_TILES[1:])
    nsa_rows, win_rows, fox_rows, hm = _inproj(x1, p["norm_mix"], p["w_main"], p["colgain"], tm_in)
    small = _small(x1, p["norm_mix"], p["w_small"], p["b_small"], tm)
    return x1, nsa_rows, win_rows, fox_rows, hm, small


def _token_stage_out(x1, o_nsa, o_fox, p):
    m = x1.shape[0]
    tm = m if m < ROW_TILES[-1] else _largest_tile(m, ROW_TILES[1:])
    tm_ffn = m if m < FFN_ROW_TILES[-1] else _largest_tile(m, FFN_ROW_TILES)
    x2 = _outproj(o_nsa, o_fox, p["out_norm_nsa"], p["out_norm_fox"], p["w_out"], x1, tm)
    return _ffn(x2, p["norm_ffn2"], p["wg2"], p["wu2"], p["wd2"], tm_ffn, p["tf"])


def kernel(x_prompt, x_sample, cache_nsa_kv, cache_fox_kv, cache_fox_logf, state_win_kv, page_table, rel_table, norm_ffn1, ffn1_gate, ffn1_up, ffn1_down, norm_mix, w_in, nsa_gate_bias, fox_forget_bias, q_norm_nsa, k_norm_nsa, q_norm_fox, k_norm_fox, cmp_pos_k, cmp_w1_k, cmp_w2_k, cmp_pos_v, cmp_w1_v, cmp_w2_v, out_norm_nsa, out_norm_fox, w_out, norm_ffn2, ffn2_gate, ffn2_up, ffn2_down):
    depth = w_in.shape[0]
    assert depth == 1, "single-layer trunk"
    nbp, seq, d_model = x_prompt.shape
    nbd, dseq, _ = x_sample.shape
    n_pages = page_table.shape[1]
    past = n_pages * PAGE_SIZE
    d_ff = ffn1_gate.shape[2]
    nsa_w = NSA_HEADS * HEAD_DIM
    kv6_w = 6 * NSA_KV_HEADS * HEAD_DIM
    fox_w = 3 * FOX_HEADS * HEAD_DIM
    off_gate = nsa_w + kv6_w
    off_fox = off_gate + N_GATE_COLS
    off_forget = off_fox + fox_w
    assert w_in.shape[2] == off_forget + FOX_HEADS and d_model == nsa_w + FOX_HEADS * HEAD_DIM
    assert seq % LANE == 0 and seq >= WINDOW and past % LANE == 0 and n_pages % REGROUP_PAGES == 0
    assert dseq <= 16 and state_win_kv.shape[2] == WINDOW
    assert seq % FOX_KEY_BLOCK == 0 and seq % (SLC_BLOCK_TILES * NSA_TQ) == 0 and seq >= WIN_TILES * NSA_TQ

    w0 = w_in[0]
    ones = lambda n: jnp.ones((n,), F32)
    zeros = lambda n: jnp.zeros((n,), F32)
    kn, kvw = NSA_KV_HEADS, NSA_KV_HEADS * HEAD_DIM
    qk_scale = HEAD_DIM ** -0.5
    p = {
        "tf": _largest_tile(d_ff, (1408, 512, 256, 128)),
        "norm_ffn1": norm_ffn1[0][None], "norm_mix": norm_mix[0][None], "norm_ffn2": norm_ffn2[0][None],
        "wg1": ffn1_gate[0].astype(BF16), "wu1": ffn1_up[0].astype(BF16), "wd1": ffn1_down[0].astype(BF16),
        "wg2": ffn2_gate[0].astype(BF16), "wu2": ffn2_up[0].astype(BF16), "wd2": ffn2_down[0].astype(BF16),
        "w_main": jnp.concatenate([w0[:, :off_gate], w0[:, off_fox:off_forget]], axis=1).astype(BF16),
        "w_small": jnp.concatenate([w0[:, off_gate:off_fox], w0[:, off_forget:],
                                    jnp.zeros((d_model, LANE - N_GATE_COLS - FOX_HEADS), F32)], axis=1).astype(BF16),
        "b_small": jnp.concatenate([nsa_gate_bias[0].reshape(-1), fox_forget_bias[0],
                                    zeros(LANE - N_GATE_COLS - FOX_HEADS)])[None],
        "colgain": jnp.concatenate([
            jnp.tile(q_norm_nsa[0] * qk_scale, NSA_HEADS), ones(2 * kvw), jnp.tile(k_norm_nsa[0], kn), ones(kvw),
            jnp.tile(k_norm_nsa[0], kn), ones(kvw), jnp.tile(q_norm_fox[0] * qk_scale, FOX_HEADS),
            jnp.tile(k_norm_fox[0], FOX_HEADS), ones(FOX_HEADS * HEAD_DIM)])[None],
        "out_norm_nsa": out_norm_nsa[0][None], "out_norm_fox": out_norm_fox[0][None],
        "w_out": w_out[0].astype(BF16),
    }
    half = CMP_STRIDE * HEAD_DIM

    def cmp_w1(w):
        return jnp.concatenate([w[0, :half], w[0, half:]], axis=1)

    def cmp_pe(pe):
        return jnp.concatenate([pe[0].reshape(CMP_LEN // CMP_STRIDE, half), jnp.zeros((PE_ROWS - CMP_LEN // CMP_STRIDE, half), F32)], axis=0)

    w1cat = jnp.stack([cmp_w1(cmp_w1_k), cmp_w1(cmp_w1_v)]).astype(BF16)
    w2cat = jnp.stack([cmp_w2_k[0], cmp_w2_v[0]]).astype(BF16)
    pecat = jnp.stack([cmp_pe(cmp_pos_k), cmp_pe(cmp_pos_v)]).astype(BF16)
    k_norm_row = k_norm_nsa[0][None]

    mp = nbp * seq
    x1, nsa_rows, win_rows, fox_rows, hm, small = _token_stage_in(x_prompt.reshape(mp, d_model), p)
    hm4 = hm.reshape(N_HEAD_COLS, nbp, seq, HEAD_DIM)

    logf = small[:, N_GATE_COLS:N_GATE_COLS + FOX_HEADS]
    csum = _cumsum(logf.reshape(nbp, seq, FOX_HEADS).transpose(0, 2, 1))
    o_fox = _fox_prompt(hm, hm4, csum[:, :, None, :], nbp, seq)

    n_chunk = seq // CMP_STRIDE
    n_cmp = (seq - CMP_LEN) // CMP_STRIDE + 1
    n_sel = -(-seq // SEL_BLOCK)
    xc = _chunkify(nsa_rows.reshape(nbp, seq * N_CACHE_COLS, HEAD_DIM), N_CACHE_COLS, N_CMP_COLS, seq)
    kcv = _compress(xc, 0, w1cat, w2cat, pecat, k_norm_row, n_chunk)
    mselt = _cmp_to_sel(n_cmp, n_sel, n_chunk, -(-n_sel // 8) * 8).T
    nq = seq // NSA_TQ
    gates_t = small[:, :N_GATE_COLS].reshape(nbp, nq, NSA_TQ, NSA_KV_HEADS, NSA_GROUP, 3)
    gates_t = gates_t.transpose(0, 3, 1, 5, 4, 2).reshape(nbp, NSA_KV_HEADS, nq, 3, NSA_COLS)
    gates_t = jnp.pad(gates_t, ((0, 0), (0, 0), (0, 0), (0, 8 - 3), (0, 0)))
    o_nsa = _nsa_prompt(rel_table, hm, hm4, kcv, mselt, gates_t, nbp, seq, n_sel)
    y_p = _token_stage_out(x1, o_nsa, o_fox, p)

    ms = nbd * dseq
    lk = past + NSA_DEC_KEYS
    assert n_pages % FOX_DEC_PAGES == 0 and past % NSA_DEC_KEYS == 0 and FOX_HEADS * dseq <= LANE
    xs1, nsa_rows_s, win_rows_s, fox_rows_s, hm_s, small_s = _token_stage_in(x_sample.reshape(ms, d_model), p)
    xc_d, nsa_dec, lf_dec = _regroup(
        page_table,
        cache_nsa_kv.reshape(cache_nsa_kv.shape[1], PAGE_SIZE * N_CACHE_COLS, HEAD_DIM),
        cache_fox_logf[0], lk)
    hm_s4 = hm_s.reshape(N_HEAD_COLS, nbd, dseq, HEAD_DIM)
    nsa_dec = lax.dynamic_update_slice(nsa_dec, hm_s4[HM_K_SLC:HM_K_WIN], (0, 0, past, 0))
    logf_s = small_s[:, N_GATE_COLS:N_GATE_COLS + FOX_HEADS].reshape(nbd, dseq, FOX_HEADS)
    lf_dec = lax.dynamic_update_slice(lf_dec, logf_s, (0, past, 0))

    csum_d = _cumsum(lf_dec.transpose(0, 2, 1))
    n_cols = FOX_HEADS * dseq
    lane_pad = lambda a: jnp.pad(a, [(0, 0)] * (a.ndim - 1) + [(0, LANE - n_cols)])
    head_eye = jnp.eye(FOX_HEADS, dtype=BF16)
    qbd = jnp.einsum("hbtd,hg->bhdgt", hm_s4[HM_Q_FOX:HM_Q_FOX + FOX_HEADS], head_eye)
    qbd = lane_pad(qbd.reshape(nbd, FOX_HEADS * HEAD_DIM, n_cols))
    ccols = lane_pad(jnp.repeat(csum_d.transpose(0, 2, 1), dseq, axis=2))
    cq = lane_pad(csum_d[:, :, past:past + dseq].reshape(nbd, 1, n_cols))

    def new_rows(head0):
        rows = hm_s4[head0:head0 + FOX_HEADS].transpose(1, 2, 0, 3).reshape(nbd, dseq, FOX_HEADS * HEAD_DIM)
        return jnp.pad(rows, ((0, 0), (0, LANE - dseq), (0, 0)))

    o_full = _fox_decode(page_table, cache_fox_kv.reshape(cache_fox_kv.shape[1], PAGE_SIZE * N_CACHE_COLS, HEAD_DIM),
                         qbd, ccols, cq, new_rows(HM_K_FOX), new_rows(HM_V_FOX), dseq)
    o_fox_s = jnp.concatenate([o_full[:, h * dseq:(h + 1) * dseq, h * HEAD_DIM:(h + 1) * HEAD_DIM]
                               for h in range(FOX_HEADS)], axis=2).reshape(ms, FOX_HEADS * HEAD_DIM)

    n_chunk_d = past // CMP_STRIDE
    n_cmp_d = (past + dseq - CMP_LEN) // CMP_STRIDE + 1
    n_sel_d = -(-(past + dseq) // SEL_BLOCK)
    assert n_cmp_d + CMP_LEN // CMP_STRIDE - 1 <= n_chunk_d, "compressed blocks must lie in the cached rows"
    kcv_d = _compress(xc_d, 0, w1cat, w2cat, pecat, k_norm_row, n_chunk_d)
    mselt_d = _cmp_to_sel(n_cmp_d, n_sel_d, n_chunk_d, lk // SEL_BLOCK).T
    win_old = state_win_kv[0].transpose(2, 3, 0, 1, 4).reshape(2 * NSA_KV_HEADS, nbd, WINDOW, HEAD_DIM).astype(BF16)
    win_dec = jnp.concatenate([win_old, hm_s4[HM_K_WIN:HM_V_WIN + NSA_KV_HEADS],
                               jnp.zeros((2 * NSA_KV_HEADS, nbd, LANE - dseq, HEAD_DIM), BF16)], axis=2)
    grp_eye = jnp.eye(NSA_KV_HEADS, dtype=BF16)
    q_grp = hm_s4[HM_Q_NSA:HM_Q_NSA + NSA_HEADS].reshape(NSA_KV_HEADS, NSA_GROUP, nbd, dseq, HEAD_DIM)
    qbd_n = jnp.einsum("gjbtd,gk->bgdkjt", q_grp, grp_eye)
    qbd_n = lane_pad(qbd_n.reshape(nbd, NSA_KV_HEADS * HEAD_DIM, n_cols))
    tcols = lane_pad(jnp.repeat(rel_table, dseq, axis=1))
    col_id = np.arange(n_cols)
    same = ((col_id[:, None] // (NSA_GROUP * dseq) == col_id[None, :] // (NSA_GROUP * dseq))
            & (col_id[:, None] % dseq == col_id[None, :] % dseq))
    pair = jnp.asarray(np.pad(same.astype(np.float32), ((0, LANE - n_cols), (0, LANE - n_cols))))
    gcols = small_s[:, :N_GATE_COLS].reshape(nbd, dseq, NSA_HEADS, 3).transpose(0, 3, 2, 1).reshape(nbd, 3, n_cols)
    gcols = jnp.broadcast_to(lane_pad(gcols)[..., None], (nbd, 3, LANE, LANE))
    o_full_n = _nsa_decode(qbd_n, kcv_d, mselt_d, pair, tcols, nsa_dec, win_dec, gcols,
                           dseq=dseq, past=past, n_sel=n_sel_d)
    o_nsa_s = jnp.concatenate(
        [o_full_n[:, h * dseq:(h + 1) * dseq, (h // NSA_GROUP) * HEAD_DIM:(h // NSA_GROUP + 1) * HEAD_DIM]
         for h in range(NSA_HEADS)], axis=2).reshape(ms, NSA_HEADS * HEAD_DIM)
    y_s = _token_stage_out(xs1, o_nsa_s, o_fox_s, p)

    kvh = (NSA_KV_HEADS, HEAD_DIM)
    win_keep = min(WINDOW, seq)
    win_p = win_rows.reshape(nbp, seq, 2, *kvh)[:, seq - win_keep:]
    win_s = jnp.concatenate([state_win_kv[0], win_rows_s.reshape(nbd, dseq, 2, *kvh)], axis=1)[:, dseq:]
    return (
        y_p.reshape(nbp, seq, d_model),
        y_s.reshape(nbd, dseq, d_model),
        nsa_rows.reshape(1, nbp, seq, 4, *kvh),
        fox_rows.reshape(1, nbp, seq, 2, FOX_HEADS, HEAD_DIM),
        logf.reshape(1, nbp, seq, FOX_HEADS),
        win_p[None],
        nsa_rows_s.reshape(1, nbd, dseq, 4, *kvh),
        fox_rows_s.reshape(1, nbd, dseq, 2, FOX_HEADS, HEAD_DIM),
        logf_s[None],
        win_s[None],
    )
```

```python
import functools
import math

import numpy as np
import jax
import jax.numpy as jnp
from jax import lax
from jax.experimental import pallas as pl
from jax.experimental.pallas import tpu as pltpu

HEAD_DIM = 128
NSA_HEADS = 8
FOX_HEADS = 8
NSA_KV_HEADS = 4
NSA_GROUP = NSA_HEADS // NSA_KV_HEADS
CMP_LEN = 32
CMP_STRIDE = 16
CMP_HIDDEN = 512
SEL_BLOCK = 64
SEL_TOPK = 16
WINDOW = 512
REL_BUCKETS = 32
REL_MAX_DIST = 128
RMS_EPS = 1e-6
PAGE_SIZE = 128

LANE = 128
NEG = -1e30
FORCED_SCORE = 1e30
VMEM_LIMIT = 56 * 1024 * 1024

BF16 = jnp.bfloat16
F32 = jnp.float32


def _bucket_thresholds():
    n = np.arange(0, 4 * REL_MAX_DIST)
    max_exact = REL_BUCKETS // 2
    nf = np.maximum(n, 1).astype(np.float32)
    large = max_exact + (np.log(nf / max_exact) / math.log(REL_MAX_DIST / max_exact)
                         * (REL_BUCKETS - max_exact)).astype(np.int32)
    bucket = np.where(n < max_exact, n, np.minimum(large, REL_BUCKETS - 1))
    return [int(np.min(n[bucket >= k])) for k in range(1, REL_BUCKETS)]


BUCKET_THR = _bucket_thresholds()
FAR_DIST = BUCKET_THR[-1]


def _cparams(sem):
    return pltpu.CompilerParams(dimension_semantics=sem, vmem_limit_bytes=VMEM_LIMIT)


def _rms_rows(x, gain):
    ms = jnp.mean(x * x, axis=-1, keepdims=True)
    return x * lax.rsqrt(ms + RMS_EPS) * gain


def _ffn_kernel(x_ref, g_ref, wg_ref, wu_ref, wd_ref, o_ref, xn_ref):
    @pl.when(pl.program_id(1) == 0)
    def _():
        x = x_ref[...]
        xn_ref[...] = _rms_rows(x, g_ref[...]).astype(BF16)
        o_ref[...] = x

    xn = xn_ref[...]
    a = jnp.dot(xn, wg_ref[...], preferred_element_type=F32)
    u = jnp.dot(xn, wu_ref[...], preferred_element_type=F32)
    h = (a / (1.0 + jnp.exp(-a))) * u * 0.5
    o_ref[...] += jnp.dot(h.astype(BF16), wd_ref[...], preferred_element_type=F32)


def _ffn(x, gain, wg, wu, wd, tm, tf):
    m, d = x.shape
    f = wg.shape[1]
    return pl.pallas_call(
        _ffn_kernel,
        out_shape=jax.ShapeDtypeStruct((m, d), F32),
        grid=(m // tm, f // tf),
        in_specs=[
            pl.BlockSpec((tm, d), lambda i, j: (i, 0)),
            pl.BlockSpec((1, d), lambda i, j: (0, 0)),
            pl.BlockSpec((d, tf), lambda i, j: (0, j)),
            pl.BlockSpec((d, tf), lambda i, j: (0, j)),
            pl.BlockSpec((tf, d), lambda i, j: (j, 0)),
        ],
        out_specs=pl.BlockSpec((tm, d), lambda i, j: (i, 0)),
        scratch_shapes=[pltpu.VMEM((tm, d), BF16)],
        compiler_params=_cparams(("parallel", "arbitrary")),
        name="ffn",
    )(x, gain, wg, wu, wd)


IN_TN = 4 * HEAD_DIM
J_NSA = (2, 6)
J_WIN = (6, 8)
J_FOX = (10, 14)
N_HEAD_COLS = 56
HM_Q_NSA, HM_K_CMP, HM_K_SLC, HM_V_SLC, HM_K_WIN, HM_V_WIN = 0, 8, 16, 20, 24, 28
HM_Q_FOX, HM_K_FOX, HM_V_FOX = 32, 40, 48


IN_NORMED_TILES = (0, 1, 4, 6, 8, 9, 10, 11)


def _inproj_kernel(x_ref, g_ref, w_ref, cg_ref, ones_ref, nsa_ref, win_ref, fox_ref, hm_ref):
    xn = _rms_rows(x_ref[...], g_ref[...]).astype(BF16)
    tm = xn.shape[0]
    heads = IN_TN // HEAD_DIM
    for j in range(w_ref.shape[1] // IN_TN):
        cols = slice(j * IN_TN, (j + 1) * IN_TN)
        vals = jnp.dot(xn, w_ref[:, cols], preferred_element_type=F32)
        if j in IN_NORMED_TILES:
            sumsq = jnp.dot((vals * vals).astype(BF16), ones_ref[...], preferred_element_type=F32)
            vals = vals * lax.rsqrt(sumsq * (1.0 / HEAD_DIM) + RMS_EPS) * cg_ref[:, cols]
        for hh in range(heads):
            hm_ref[j * heads + hh] = vals[:, hh * HEAD_DIM:(hh + 1) * HEAD_DIM].astype(BF16)
        for ref, (jlo, jhi) in ((nsa_ref, J_NSA), (win_ref, J_WIN), (fox_ref, J_FOX)):
            if jlo <= j < jhi:
                n_cols = (jhi - jlo) * heads
                for hh in range(heads):
                    ref[pl.ds((j - jlo) * heads + hh, tm, stride=n_cols), :] = vals[:, hh * HEAD_DIM:(hh + 1) * HEAD_DIM]


def _inproj(x, gain, w_main, colgain, tm):
    m, d = x.shape
    ncol = w_main.shape[1]
    heads = IN_TN // HEAD_DIM
    n_nsa, n_win, n_fox = [(hi - lo) * heads for lo, hi in (J_NSA, J_WIN, J_FOX)]
    head_ones = jnp.asarray(np.kron(np.eye(heads, dtype=np.float32), np.ones((HEAD_DIM, HEAD_DIM), np.float32)), BF16)
    resident = lambda shape: pl.BlockSpec(shape, lambda i: (0, 0), pipeline_mode=pl.Buffered(1))
    return pl.pallas_call(
        _inproj_kernel,
        out_shape=(
            jax.ShapeDtypeStruct((m * n_nsa, HEAD_DIM), F32),
            jax.ShapeDtypeStruct((m * n_win, HEAD_DIM), F32),
            jax.ShapeDtypeStruct((m * n_fox, HEAD_DIM), F32),
            jax.ShapeDtypeStruct((N_HEAD_COLS, m, HEAD_DIM), BF16),
        ),
        grid=(m // tm,),
        in_specs=[
            pl.BlockSpec((tm, d), lambda i: (i, 0)),
            resident((1, d)),
            resident((d, ncol)),
            resident((1, ncol)),
            resident((IN_TN, IN_TN)),
        ],
        out_specs=(
            pl.BlockSpec((tm * n_nsa, HEAD_DIM), lambda i: (i, 0)),
            pl.BlockSpec((tm * n_win, HEAD_DIM), lambda i: (i, 0)),
            pl.BlockSpec((tm * n_fox, HEAD_DIM), lambda i: (i, 0)),
            pl.BlockSpec((N_HEAD_COLS, tm, HEAD_DIM), lambda i: (0, i, 0)),
        ),
        compiler_params=_cparams(("parallel",)),
        name="inproj",
    )(x, gain, w_main, colgain, head_ones)


N_GATE_COLS = 3 * NSA_HEADS


def _small_kernel(x_ref, g_ref, w_ref, b_ref, o_ref):
    xn = _rms_rows(x_ref[...], g_ref[...]).astype(BF16)
    z = jnp.dot(xn, w_ref[...], preferred_element_type=F32) + b_ref[...]
    lane = lax.broadcasted_iota(jnp.int32, z.shape, 1)
    sig = 1.0 / (1.0 + jnp.exp(-z))
    logsig = jnp.minimum(z, 0.0) - jnp.log(1.0 + jnp.exp(-jnp.abs(z)))
    o_ref[...] = jnp.where(lane < N_GATE_COLS, sig,
                           jnp.where(lane < N_GATE_COLS + FOX_HEADS, logsig, 0.0))


def _small(x, gain, w_small, b_small, tm):
    m, d = x.shape
    return pl.pallas_call(
        _small_kernel,
        out_shape=jax.ShapeDtypeStruct((m, LANE), F32),
        grid=(m // tm,),
        in_specs=[
            pl.BlockSpec((tm, d), lambda i: (i, 0)),
            pl.BlockSpec((1, d), lambda i: (0, 0)),
            pl.BlockSpec((d, LANE), lambda i: (0, 0)),
            pl.BlockSpec((1, LANE), lambda i: (0, 0)),
        ],
        out_specs=pl.BlockSpec((tm, LANE), lambda i: (i, 0)),
        compiler_params=_cparams(("parallel",)),
        name="gates",
    )(x, gain, w_small, b_small)


def _outproj_kernel(on_ref, of_ref, gn_ref, gf_ref, w_ref, x_ref, y_ref):
    a = _rms_rows(on_ref[...], gn_ref[...]).astype(BF16)
    b = _rms_rows(of_ref[...], gf_ref[...]).astype(BF16)
    half = a.shape[1]
    y = jnp.dot(a, w_ref[:half, :], preferred_element_type=F32)
    y = y + jnp.dot(b, w_ref[half:, :], preferred_element_type=F32)
    y_ref[...] = x_ref[...] + y


def _outproj(o_nsa, o_fox, g_nsa, g_fox, w_out, x, tm):
    m, d = x.shape
    wn = o_nsa.shape[1]
    wf = o_fox.shape[1]
    return pl.pallas_call(
        _outproj_kernel,
        out_shape=jax.ShapeDtypeStruct((m, d), F32),
        grid=(m // tm,),
        in_specs=[
            pl.BlockSpec((tm, wn), lambda i: (i, 0)),
            pl.BlockSpec((tm, wf), lambda i: (i, 0)),
            pl.BlockSpec((1, wn), lambda i: (0, 0)),
            pl.BlockSpec((1, wf), lambda i: (0, 0)),
            pl.BlockSpec((wn + wf, d), lambda i: (0, 0)),
            pl.BlockSpec((tm, d), lambda i: (i, 0)),
        ],
        out_specs=pl.BlockSpec((tm, d), lambda i: (i, 0)),
        compiler_params=_cparams(("parallel",)),
        name="outproj",
    )(o_nsa, o_fox, g_nsa, g_fox, w_out, x)


CUMSUM_CHUNK = 512


def _cumsum_kernel(x_ref, before_ref, o_ref):
    width = x_ref.shape[2]
    r = lax.broadcasted_iota(jnp.int32, (width, width), 0)
    c = lax.broadcasted_iota(jnp.int32, (width, width), 1)
    upper = (r <= c).astype(F32)
    local = jnp.dot(x_ref[0], upper, preferred_element_type=F32, precision=lax.Precision.HIGHEST)
    totals = jnp.broadcast_to(local[:, width - 1:width], (local.shape[0], LANE))
    offset = jnp.dot(before_ref[...], totals, preferred_element_type=F32, precision=lax.Precision.HIGHEST)
    o_ref[0] = local + jnp.concatenate([offset] * (width // LANE), axis=1)


def _cumsum(x):
    b, h, length = x.shape
    assert length % CUMSUM_CHUNK == 0
    pieces = length // CUMSUM_CHUNK
    used = h * pieces
    rows = -(-used // LANE) * LANE
    idx = np.arange(rows)
    before = ((idx[:, None] // pieces == idx[None, :] // pieces) & (idx[None, :] < idx[:, None])
              & (idx[:, None] < used))
    x = jnp.pad(x.reshape(b, used, CUMSUM_CHUNK), ((0, 0), (0, rows - used), (0, 0)))
    out = pl.pallas_call(
        _cumsum_kernel,
        out_shape=jax.ShapeDtypeStruct((b, rows, CUMSUM_CHUNK), F32),
        grid=(b,),
        in_specs=[pl.BlockSpec((1, rows, CUMSUM_CHUNK), lambda i: (i, 0, 0)),
                  pl.BlockSpec((rows, rows), lambda i: (0, 0))],
        out_specs=pl.BlockSpec((1, rows, CUMSUM_CHUNK), lambda i: (i, 0, 0)),
        compiler_params=_cparams(("parallel",)),
        name="logf_cumsum",
    )(x, jnp.asarray(before, F32))
    return out[:, :used].reshape(b, h, length)


PE_ROWS = 16


def _compress_kernel(x_ref, w1_ref, w2_ref, pe_ref, kn_ref, o_ref):
    kind = pl.program_id(0) // NSA_KV_HEADS
    n = x_ref.shape[2]
    w1 = w1_ref[0]
    h = jnp.dot(x_ref[0, 0], w1, preferred_element_type=F32)
    pw = jnp.dot(pe_ref[0], w1, preferred_element_type=F32)
    const = pw[0:1, :CMP_HIDDEN] + pw[1:2, CMP_HIDDEN:]
    hid = h[:, :CMP_HIDDEN] + pltpu.roll(h[:, CMP_HIDDEN:], n - 1, 0) + const
    act = hid / (1.0 + jnp.exp(-hid))
    out = jnp.dot(act.astype(BF16), w2_ref[0], preferred_element_type=F32)
    normed = _rms_rows(out, kn_ref[...])
    o_ref[0, 0] = jnp.where(kind == 0, normed, out).astype(BF16)


def _compress(xc, c_off, w1cat, w2, pe, k_norm, n_rows):
    nb = xc.shape[1]
    return pl.pallas_call(
        _compress_kernel,
        out_shape=jax.ShapeDtypeStruct((2 * NSA_KV_HEADS, nb, n_rows, HEAD_DIM), BF16),
        grid=(2 * NSA_KV_HEADS, nb),
        in_specs=[
            pl.BlockSpec((1, 1, n_rows, CMP_STRIDE * HEAD_DIM), lambda c, b: (c_off + c, b, 0, 0)),
            pl.BlockSpec((1, CMP_STRIDE * HEAD_DIM, 2 * CMP_HIDDEN), lambda c, b: (c // NSA_KV_HEADS, 0, 0)),
            pl.BlockSpec((1, CMP_HIDDEN, HEAD_DIM), lambda c, b: (c // NSA_KV_HEADS, 0, 0)),
            pl.BlockSpec((1, PE_ROWS, CMP_STRIDE * HEAD_DIM), lambda c, b: (c // NSA_KV_HEADS, 0, 0)),
            pl.BlockSpec((1, HEAD_DIM), lambda c, b: (0, 0)),
        ],
        out_specs=pl.BlockSpec((1, 1, n_rows, HEAD_DIM), lambda c, b: (c, b, 0, 0)),
        compiler_params=_cparams(("parallel", "parallel")),
        name="compress",
    )(xc, w1cat, w2, pe, k_norm)


def _rel_bias(dist, tbl_ref, head):
    out = jnp.full(dist.shape, tbl_ref[0, head], F32)
    for k, thr in enumerate(BUCKET_THR, start=1):
        out = jnp.where(dist >= thr, tbl_ref[k, head], out)
    return out


def _cmp_to_sel(n_cmp, n_sel, rows, cols):
    c0 = np.arange(n_cmp)[:, None] * CMP_STRIDE
    s0 = np.arange(n_sel)[None, :] * SEL_BLOCK
    inter = np.clip(np.minimum(c0 + CMP_LEN, s0 + SEL_BLOCK) - np.maximum(c0, s0), 0, None)
    m = np.zeros((rows, cols), np.float32)
    m[:n_cmp, :n_sel] = inter / CMP_LEN
    return jnp.asarray(m)


def _transpose_tiles(x):
    n = x.shape[0] // LANE
    xf = x.astype(F32)
    return jnp.concatenate([xf[i * LANE:(i + 1) * LANE].T for i in range(n)], axis=1)


def _untranspose_tiles(xt):
    n = xt.shape[1] // LANE
    return jnp.concatenate([xt[:, i * LANE:(i + 1) * LANE].T for i in range(n)], axis=0)


AUX_ROWS = 16


def _fill_transposed(dst_ref, src_ref):
    def body(k, carry):
        st = pl.multiple_of(k * LANE, LANE)
        dst_ref[:HEAD_DIM, pl.ds(st, LANE)] = src_ref[0, 0, pl.ds(st, LANE), :].astype(F32).T.astype(BF16)
        return carry

    lax.fori_loop(0, src_ref.shape[2] // LANE, body, 0)


FOX_TQ = 256
FOX_KEY_BLOCK = 1024


def _split3(x):
    hi = x.astype(BF16)
    r1 = x - hi.astype(F32)
    mid = r1.astype(BF16)
    lo = (r1 - mid.astype(F32)).astype(BF16)
    return hi, mid, lo


def _lane_select3(parts, shape):
    lane = lax.broadcasted_iota(jnp.int32, shape, 1)
    hi, mid, lo = [part.astype(F32) for part in parts]
    return jnp.where(lane == 0, hi, jnp.where(lane == 1, mid, jnp.where(lane == 2, lo, 0.0))).astype(BF16)


def _ones_rows(width):
    row = lax.broadcasted_iota(jnp.int32, (AUX_ROWS, width), 0)
    return jnp.where(row == 0, 1.0, 0.0).astype(BF16)


def _finish_aug(acc):
    return acc[:HEAD_DIM] * (1.0 / acc[HEAD_DIM:HEAD_DIM + 1])


def _fox_prompt_kernel(q_ref, k_ref, v_ref, crow_ref, o_ref, vt_ref, ka_ref, s_ref):
    qt = pl.program_id(2)
    tq = FOX_TQ

    @pl.when(qt == 0)
    def _():
        _fill_transposed(vt_ref, v_ref)
        vt_ref[HEAD_DIM:, :] = _ones_rows(vt_ref.shape[1])

        def body(k, carry):
            st = pl.multiple_of(k * LANE, LANE)
            c_col = jnp.broadcast_to(crow_ref[0, 0, :, pl.ds(st, LANE)], (LANE, LANE)).T
            ka_ref[pl.ds(st, LANE), :HEAD_DIM] = k_ref[0, 0, pl.ds(st, LANE), :]
            ka_ref[pl.ds(st, LANE), HEAD_DIM:] = _lane_select3(_split3(c_col), (LANE, LANE))
            return carry

        lax.fori_loop(0, k_ref.shape[2] // LANE, body, 0)

    q0 = pl.multiple_of(qt * tq, tq)
    q_t = _transpose_tiles(q_ref[0]).astype(BF16)
    row = lax.broadcasted_iota(jnp.int32, (LANE, tq), 0)
    q_aug = jnp.concatenate([q_t, jnp.where(row < 3, -1.0, 0.0).astype(BF16)], axis=0)
    c_q = crow_ref[0, 0, :, pl.ds(q0, tq)]
    blk = FOX_KEY_BLOCK

    last = q0 // blk
    rel = (lax.broadcasted_iota(jnp.int32, (blk, tq), 0) - lax.broadcasted_iota(jnp.int32, (blk, tq), 1))

    def run(n_blocks):
        top = jnp.full((8, tq), NEG, F32)
        for kb in range(n_blocks):
            start = kb * blk
            s = jnp.dot(ka_ref[start:start + blk, :], q_aug, preferred_element_type=F32)
            if kb == n_blocks - 1:
                s = jnp.where(rel <= q0 - start, s, NEG)
            s_ref[start:start + blk, :] = s
            top = jnp.maximum(top, jnp.max(s.reshape(blk // 8, 8, tq), axis=0))
        m = jnp.max(top, axis=0, keepdims=True) + c_q
        shift = m - c_q
        acc = jnp.zeros((HEAD_DIM + AUX_ROWS, tq), F32)
        for kb in range(n_blocks):
            start = kb * blk
            p = jnp.exp(s_ref[start:start + blk, :] - shift).astype(BF16)
            acc = acc + jnp.dot(vt_ref[:, start:start + blk], p, preferred_element_type=F32)
        return acc

    n_max = k_ref.shape[2] // blk
    acc = lax.switch(last, [functools.partial(run, n) for n in range(1, n_max + 1)])
    o_ref[...] = _untranspose_tiles(_finish_aug(acc))


def _fox_prompt(hm, hm4, crow, nb, seq):
    nq = seq // FOX_TQ
    return pl.pallas_call(
        _fox_prompt_kernel,
        out_shape=jax.ShapeDtypeStruct((nb * seq, FOX_HEADS * HEAD_DIM), F32),
        grid=(nb, FOX_HEADS, nq),
        in_specs=[
            pl.BlockSpec((1, FOX_TQ, HEAD_DIM), lambda b, h, t: (HM_Q_FOX + h, b * nq + t, 0)),
            pl.BlockSpec((1, 1, seq, HEAD_DIM), lambda b, h, t: (HM_K_FOX + h, b, 0, 0)),
            pl.BlockSpec((1, 1, seq, HEAD_DIM), lambda b, h, t: (HM_V_FOX + h, b, 0, 0)),
            pl.BlockSpec((1, 1, 1, seq), lambda b, h, t: (b, h, 0, 0)),
        ],
        out_specs=pl.BlockSpec((FOX_TQ, HEAD_DIM), lambda b, h, t: (b * nq + t, h)),
        scratch_shapes=[pltpu.VMEM((HEAD_DIM + AUX_ROWS, seq), BF16), pltpu.VMEM((seq, 2 * HEAD_DIM), BF16),
                        pltpu.VMEM((seq, FOX_TQ), F32)],
        compiler_params=_cparams(("parallel", "parallel", "arbitrary")),
        name="fox_prompt",
    )(hm, hm4, hm4, crow)


NSA_TQ = LANE
NSA_COLS = NSA_GROUP * NSA_TQ
WIN_TILES = WINDOW // NSA_TQ + 1
SLC_BLOCK_TILES = 4
SLC_MASK_ROWS = SLC_BLOCK_TILES * NSA_TQ // SEL_BLOCK


def _nsa_prompt_kernel(tbl_ref, q_ref, kc_ref, vc_ref, mselt_ref, ks_ref, vs_ref, kw_ref, vw_ref, gate_ref, o_ref,
                       vst_ref, vwt_ref, vct_ref, wb_ref, pc_ref, drop_ref, ksa_ref, wbd_ref, s_ref, *, n_sel, cmp_back):
    g = pl.program_id(1)
    qt = pl.program_id(2)
    tq = NSA_TQ
    cols = NSA_COLS
    heads = [NSA_GROUP * g + hh for hh in range(NSA_GROUP)]
    ncp = kc_ref.shape[2]
    nj = mselt_ref.shape[0]

    @pl.when(qt == 0)
    def _():
        _fill_transposed(vst_ref, vs_ref)
        _fill_transposed(vwt_ref, vw_ref)
        vst_ref[HEAD_DIM:, :] = _ones_rows(vst_ref.shape[1])
        vwt_ref[HEAD_DIM:, :] = _ones_rows(vwt_ref.shape[1])
        vct_ref[...] = _transpose_tiles(vc_ref[0, 0]).astype(BF16)

        def fill_keys(k, carry):
            st = pl.multiple_of(k * tq, tq)
            lane = lax.broadcasted_iota(jnp.int32, (tq, tq), 1)
            blk_in_step = (k % SLC_BLOCK_TILES) * (tq // SEL_BLOCK) + lax.broadcasted_iota(jnp.int32, (tq, tq), 0) // SEL_BLOCK
            extra = (lane == blk_in_step) | ((lane >= SLC_MASK_ROWS) & (lane < SLC_MASK_ROWS + 3))
            ksa_ref[pl.ds(st, tq), :HEAD_DIM] = ks_ref[0, 0, pl.ds(st, tq), :]
            ksa_ref[pl.ds(st, tq), HEAD_DIM:] = jnp.where(extra, 1.0, 0.0).astype(BF16)
            return carry

        lax.fori_loop(0, ks_ref.shape[2] // tq, fill_keys, 0)
        key = lax.broadcasted_iota(jnp.int32, (tq, tq), 0)
        qry = lax.broadcasted_iota(jnp.int32, (tq, tq), 1)
        blk = lax.broadcasted_iota(jnp.int32, (pc_ref.shape[0], tq), 0) - cmp_back
        d_cmp = lax.broadcasted_iota(jnp.int32, (pc_ref.shape[0], tq), 1) - (blk * CMP_STRIDE + (CMP_LEN - 1))
        for hh in range(NSA_GROUP):
            sl = slice(hh * tq, (hh + 1) * tq)
            far = tbl_ref[REL_BUCKETS - 1, heads[hh]]
            for delta in range(2):
                d = delta * tq + qry - key
                wb_ref[delta, :, sl] = jnp.where(d >= 0, _rel_bias(jnp.maximum(d, 0), tbl_ref, heads[hh]), NEG)
            for delta in range(2, WIN_TILES - 1):
                wb_ref[delta, :, sl] = jnp.full((tq, tq), far, F32)
            wb_ref[WIN_TILES - 1, :, sl] = jnp.where(qry < key, far, NEG)
            wb_ref[WIN_TILES, :, sl] = jnp.full((tq, tq), NEG, F32)
            for delta in range(2):
                wbd_ref[delta, :, sl] = wb_ref[delta, :, sl] - far
            wbd_ref[2, :, sl] = jnp.zeros((tq, tq), F32)
            wbd_ref[3, :, sl] = jnp.full((tq, tq), NEG, F32)
            pc_ref[:, sl] = jnp.where(d_cmp >= 0, _rel_bias(jnp.maximum(d_cmp, 0), tbl_ref, heads[hh]), NEG)

    q_t = jnp.concatenate([q_ref[hh].astype(F32).T for hh in range(NSA_GROUP)], axis=1).astype(BF16)

    off = pl.multiple_of(cmp_back - qt * (tq // CMP_STRIDE), 8)
    s_t = jnp.dot(kc_ref[0, 0], q_t, preferred_element_type=F32) + pc_ref[pl.ds(off, ncp), :]
    m = jnp.max(s_t, axis=0, keepdims=True)
    e = jnp.exp(s_t - m)
    inv = jnp.where(m > 0.5 * NEG, 1.0 / jnp.sum(e, axis=0, keepdims=True), 0.0)
    p_t = e * inv
    o_cmp = jnp.dot(vct_ref[...], p_t.astype(BF16), preferred_element_type=F32)
    p_sum = p_t[:, :tq]
    for hh in range(1, NSA_GROUP):
        p_sum = p_sum + p_t[:, hh * tq:(hh + 1) * tq]
    imp = jnp.dot(mselt_ref[...], p_sum, preferred_element_type=F32, precision=lax.Precision.HIGHEST)

    jrow = lax.broadcasted_iota(jnp.int32, (nj, tq), 0)
    pos = qt * tq + lax.broadcasted_iota(jnp.int32, (nj, tq), 1)
    blk_q = jnp.right_shift(pos, int(math.log2(SEL_BLOCK)))
    forced = (jrow == 0) | (jrow == blk_q) | (jrow == blk_q - 1)
    score = jnp.where(forced, FORCED_SCORE, jnp.where(jrow <= blk_q, imp, -1.0))
    score = jnp.where(jrow < n_sel, score, -2.0)
    ranks = []
    for r in range(nj // 8):
        mine = score[r * 8:(r + 1) * 8]
        jmine = jrow[r * 8:(r + 1) * 8]
        rank = jnp.zeros((8, tq), F32)
        for other in range(n_sel):
            row = jnp.broadcast_to(score[other:other + 1], (8, tq))
            if other < r * 8:
                beats = row >= mine
            elif other >= (r + 1) * 8:
                beats = row > mine
            else:
                beats = (row > mine) | ((row == mine) & (jmine > other))
            rank = rank + jnp.where(beats, 1.0, 0.0)
        ranks.append(rank)
    rank = jnp.concatenate(ranks, axis=0)
    drop = jnp.where((rank < SEL_TOPK) & (jrow < n_sel), 0.0, NEG)
    drop_ref[...] = jnp.concatenate([drop] * NSA_GROUP, axis=1)

    blk = SLC_BLOCK_TILES * tq
    col = lax.broadcasted_iota(jnp.int32, (8, cols), 1)
    far_row = jnp.full((8, cols), tbl_ref[REL_BUCKETS - 1, heads[NSA_GROUP - 1]], F32)
    for hh in range(NSA_GROUP - 1):
        far_row = jnp.where(col // tq == hh, tbl_ref[REL_BUCKETS - 1, heads[hh]], far_row)
    row8 = lax.broadcasted_iota(jnp.int32, (8, cols), 0)
    parts = [part.astype(F32) for part in _split3(far_row)]
    far_rows = jnp.where(row8 == 0, parts[0], jnp.where(row8 == 1, parts[1], jnp.where(row8 == 2, parts[2], 0.0)))
    pad_rows = jnp.zeros((HEAD_DIM - SLC_MASK_ROWS - 8, cols), BF16)

    last = qt // SLC_BLOCK_TILES

    def slc_run(n_blocks):
        top = jnp.full((8, cols), NEG, F32)
        for kb in range(n_blocks):
            start = kb * blk
            masks = drop_ref[kb * SLC_MASK_ROWS:(kb + 1) * SLC_MASK_ROWS, :]
            extra = jnp.concatenate([masks, far_rows], axis=0).astype(BF16)
            q_aug = jnp.concatenate([q_t, extra, pad_rows], axis=0)
            s = jnp.dot(ksa_ref[start:start + blk, :], q_aug, preferred_element_type=F32)
            if kb >= n_blocks - 2:
                terms = []
                for i in range(SLC_BLOCK_TILES):
                    delta = qt - (kb * SLC_BLOCK_TILES + i)
                    terms.append(wbd_ref[jnp.where(delta < 0, 3, jnp.minimum(delta, 2))])
                s = s + jnp.concatenate(terms, axis=0)
            s_ref[start:start + blk, :] = s
            top = jnp.maximum(top, jnp.max(s.reshape(blk // 8, 8, cols), axis=0))
        m = jnp.max(top, axis=0, keepdims=True)
        acc = jnp.zeros((HEAD_DIM + AUX_ROWS, cols), F32)
        for kb in range(n_blocks):
            start = kb * blk
            p = jnp.exp(s_ref[start:start + blk, :] - m).astype(BF16)
            acc = acc + jnp.dot(vst_ref[:, start:start + blk], p, preferred_element_type=F32)
        return acc

    n_max = ks_ref.shape[2] // blk
    o_slc = _finish_aug(lax.switch(last, [functools.partial(slc_run, n) for n in range(1, n_max + 1)]))

    w0 = jnp.maximum(qt - (WIN_TILES - 1), 0)
    start = pl.multiple_of(w0 * tq, tq)
    span = WIN_TILES * tq
    s = jnp.dot(kw_ref[0, 0, pl.ds(start, span), :], q_t, preferred_element_type=F32)
    terms = []
    for i in range(WIN_TILES):
        delta = qt - (w0 + i)
        terms.append(wb_ref[jnp.where(delta < 0, WIN_TILES, delta)])
    s = s + jnp.concatenate(terms, axis=0)
    p = jnp.exp(s - jnp.max(s, axis=0, keepdims=True))
    o_win = _finish_aug(jnp.dot(vwt_ref[:, pl.ds(start, span)], p.astype(BF16), preferred_element_type=F32))

    gate = gate_ref[0, 0, 0]
    o_t = gate[0:1] * o_cmp + gate[1:2] * o_slc + gate[2:3] * o_win
    for hh in range(NSA_GROUP):
        o_ref[:, hh * HEAD_DIM:(hh + 1) * HEAD_DIM] = o_t[:, hh * tq:(hh + 1) * tq].T


def _nsa_prompt(rel_table, hm, hm4, kcv, mselt, gates_t, nb, seq, n_sel):
    tq = NSA_TQ
    nq = seq // tq
    ncp = kcv.shape[2]
    nj = mselt.shape[0]
    cmp_back = (nq - 1) * (tq // CMP_STRIDE)
    kern = functools.partial(_nsa_prompt_kernel, n_sel=n_sel, cmp_back=cmp_back)
    kv_spec = lambda head0: pl.BlockSpec((1, 1, seq, HEAD_DIM), lambda b, g, t: (head0 + g, b, 0, 0))
    return pl.pallas_call(
        kern,
        out_shape=jax.ShapeDtypeStruct((nb * seq, NSA_HEADS * HEAD_DIM), F32),
        grid=(nb, NSA_KV_HEADS, nq),
        in_specs=[
            pl.BlockSpec(memory_space=pltpu.SMEM),
            pl.BlockSpec((NSA_GROUP, tq, HEAD_DIM), lambda b, g, t: (g, b * nq + t, 0)),
            pl.BlockSpec((1, 1, ncp, HEAD_DIM), lambda b, g, t: (g, b, 0, 0)),
            pl.BlockSpec((1, 1, ncp, HEAD_DIM), lambda b, g, t: (NSA_KV_HEADS + g, b, 0, 0)),
            pl.BlockSpec((nj, ncp), lambda b, g, t: (0, 0)),
            kv_spec(HM_K_SLC), kv_spec(HM_V_SLC), kv_spec(HM_K_WIN), kv_spec(HM_V_WIN),
            pl.BlockSpec((1, 1, 1, 8, NSA_COLS), lambda b, g, t: (b, g, t, 0, 0)),
        ],
        out_specs=pl.BlockSpec((tq, NSA_GROUP * HEAD_DIM), lambda b, g, t: (b * nq + t, g)),
        scratch_shapes=[
            pltpu.VMEM((HEAD_DIM + AUX_ROWS, seq), BF16),
            pltpu.VMEM((HEAD_DIM + AUX_ROWS, seq), BF16),
            pltpu.VMEM((HEAD_DIM, ncp), BF16),
            pltpu.VMEM((WIN_TILES + 1, tq, NSA_COLS), F32),
            pltpu.VMEM((cmp_back + ncp, NSA_COLS), F32),
            pltpu.VMEM((nj, NSA_COLS), F32),
            pltpu.VMEM((seq, 2 * HEAD_DIM), BF16),
            pltpu.VMEM((4, tq, NSA_COLS), F32),
            pltpu.VMEM((seq, NSA_COLS), F32),
        ],
        compiler_params=_cparams(("parallel", "parallel", "arbitrary")),
        name="nsa_prompt",
    )(rel_table, hm, kcv, kcv, mselt, hm4, hm4, hm4, hm4, gates_t)


CHUNK_ROWS = 512


def _chunkify_kernel(x_ref, o_ref, *, n_cols):
    n = o_ref.shape[2]
    for c in range(o_ref.shape[0]):
        for s in range(CMP_STRIDE):
            o_ref[c, 0, :, s * HEAD_DIM:(s + 1) * HEAD_DIM] = (
                x_ref[0, pl.ds(s * n_cols + c, n, stride=CMP_STRIDE * n_cols), :].astype(BF16))


def _chunkify(rows3, n_cols, n_heads, seq):
    nb = rows3.shape[0]
    tr = _largest_tile(seq, (CHUNK_ROWS, 256))
    return pl.pallas_call(
        functools.partial(_chunkify_kernel, n_cols=n_cols),
        out_shape=jax.ShapeDtypeStruct((n_heads, nb, seq // CMP_STRIDE, CMP_STRIDE * HEAD_DIM), BF16),
        grid=(nb, seq // tr),
        in_specs=[pl.BlockSpec((1, tr * n_cols, HEAD_DIM), lambda b, i: (b, i, 0))],
        out_specs=pl.BlockSpec((n_heads, 1, tr // CMP_STRIDE, CMP_STRIDE * HEAD_DIM), lambda b, i: (0, b, i, 0)),
        compiler_params=_cparams(("parallel", "parallel")),
        name="chunkify",
    )(rows3)


REGROUP_PAGES = 8


N_CACHE_COLS = 16
N_CMP_COLS = 2 * NSA_KV_HEADS


def _regroup_kernel(pt_ref, *refs, n_steps):
    del pt_ref
    npg = REGROUP_PAGES
    nsa_in, lf_in = refs[:npg], refs[npg:2 * npg]
    xc_out, slc_out, lf_out = refs[2 * npg:]
    is_tail = pl.program_id(1) >= n_steps
    chunks = PAGE_SIZE // CMP_STRIDE
    chunk_stride = CMP_STRIDE * N_CACHE_COLS

    @pl.when(is_tail)
    def _():
        slc_out[...] = jnp.zeros(slc_out.shape, slc_out.dtype)
        lf_out[...] = jnp.zeros(lf_out.shape, lf_out.dtype)

    @pl.when(jnp.logical_not(is_tail))
    def _():
        for p in range(npg):
            rows = slice(p * PAGE_SIZE, (p + 1) * PAGE_SIZE)
            for c in range(N_CACHE_COLS - N_CMP_COLS):
                slc_out[c, 0, rows, :] = nsa_in[p][0, pl.ds(N_CMP_COLS + c, PAGE_SIZE, stride=N_CACHE_COLS), :].astype(BF16)
            lf_out[0, rows, :] = lf_in[p][0]
        for pair in range(npg // 2):
            for c in range(N_CMP_COLS):
                for s in range(CMP_STRIDE):
                    first = s * N_CACHE_COLS + c
                    both = [nsa_in[2 * pair + i][0, pl.ds(first, chunks, stride=chunk_stride), :] for i in range(2)]
                    xc_out[c, 0, pair * 2 * chunks:(pair + 1) * 2 * chunks, s * HEAD_DIM:(s + 1) * HEAD_DIM] = (
                        jnp.concatenate(both, axis=0).astype(BF16))


def _regroup(page_table, cache_nsa, cache_logf, lk):
    nb, n_pages = page_table.shape
    npg = REGROUP_PAGES
    n_steps = n_pages // npg
    rows = npg * PAGE_SIZE
    last = n_steps - 1
    n_slc = N_CACHE_COLS - N_CMP_COLS
    chunk_w = CMP_STRIDE * HEAD_DIM

    def page_map(p):
        return lambda b, s, pt: (pt[b, jnp.minimum(s, last) * npg + p], 0, 0)

    def specs(arr):
        return [pl.BlockSpec((1,) + arr.shape[1:], page_map(p)) for p in range(npg)]

    assert (lk - n_pages * PAGE_SIZE) % rows == 0
    n_tail = (lk - n_pages * PAGE_SIZE) // rows
    grid_spec = pltpu.PrefetchScalarGridSpec(
        num_scalar_prefetch=1,
        grid=(nb, n_steps + n_tail),
        in_specs=specs(cache_nsa) + specs(cache_logf),
        out_specs=(
            pl.BlockSpec((N_CMP_COLS, 1, rows // CMP_STRIDE, chunk_w), lambda b, s, pt: (0, b, jnp.minimum(s, last), 0)),
            pl.BlockSpec((n_slc, 1, rows, HEAD_DIM), lambda b, s, pt: (0, b, s, 0)),
            pl.BlockSpec((1, rows, cache_logf.shape[2]), lambda b, s, pt: (b, s, 0)),
        ),
    )
    return pl.pallas_call(
        functools.partial(_regroup_kernel, n_steps=n_steps),
        out_shape=(
            jax.ShapeDtypeStruct((N_CMP_COLS, nb, n_pages * PAGE_SIZE // CMP_STRIDE, chunk_w), BF16),
            jax.ShapeDtypeStruct((n_slc, nb, lk, HEAD_DIM), BF16),
            jax.ShapeDtypeStruct((nb, lk, cache_logf.shape[2]), F32),
        ),
        grid_spec=grid_spec,
        compiler_params=_cparams(("parallel", "arbitrary")),
        name="cache_regroup",
    )(page_table, *([cache_nsa] * npg), *([cache_logf] * npg))


FOX_DEC_PAGES = 8


def _fox_decode_kernel(pt_ref, *refs, dseq):
    del pt_ref
    npg = FOX_DEC_PAGES
    pages = refs[:npg]
    qbd_ref, ccol_ref, cq_ref, knew_ref, vnew_ref, o_ref, m_ref, acc_ref = refs[npg:]
    step = pl.program_id(1)
    is_tail = step == pl.num_programs(1) - 1
    width = FOX_HEADS * HEAD_DIM

    @pl.when(step == 0)
    def _():
        m_ref[...] = jnp.full(m_ref.shape, NEG, F32)
        acc_ref[...] = jnp.zeros(acc_ref.shape, F32)

    def fold(k_blk, v_blk, bias):
        n = k_blk.shape[0]
        s_t = jnp.dot(k_blk, qbd_ref[0], preferred_element_type=F32) + bias
        m_new = jnp.maximum(m_ref[...], jnp.max(s_t, axis=0, keepdims=True))
        alpha = jnp.exp(m_ref[...] - m_new)
        p = jnp.exp(s_t - m_new)
        p_t = jnp.concatenate([p[i * LANE:(i + 1) * LANE].T for i in range(n // LANE)], axis=1).astype(BF16)
        v_aug = jnp.concatenate([v_blk, jnp.ones((n, LANE), BF16)], axis=1)
        upd = jnp.dot(p_t, v_aug, preferred_element_type=F32)
        alpha_col = jnp.broadcast_to(alpha, (LANE, LANE)).T
        acc_ref[...] = acc_ref[...] * jnp.concatenate([alpha_col] * (width // LANE + 1), axis=1) + upd
        m_ref[...] = m_new

    @pl.when(jnp.logical_not(is_tail))
    def _():
        by_col = [pltpu.einshape("tcd->ctd", pages[p][0].reshape(PAGE_SIZE, N_CACHE_COLS, HEAD_DIM))
                  for p in range(npg)]

        def heads_of(p, first):
            return jnp.concatenate([by_col[p][first + h] for h in range(FOX_HEADS)], axis=1).astype(BF16)

        k_blk = jnp.concatenate([heads_of(p, 0) for p in range(npg)], axis=0)
        v_blk = jnp.concatenate([heads_of(p, FOX_HEADS) for p in range(npg)], axis=0)
        fold(k_blk, v_blk, cq_ref[0] - ccol_ref[0])

    @pl.when(is_tail)
    def _():
        row = lax.broadcasted_iota(jnp.int32, (LANE, LANE), 0)
        t_of_col = lax.broadcasted_iota(jnp.int32, (LANE, LANE), 1) % dseq
        bias = jnp.where(row <= t_of_col, cq_ref[0] - ccol_ref[0, :LANE, :], NEG)
        fold(knew_ref[0], vnew_ref[0], bias)
        acc = acc_ref[...]
        inv = 1.0 / acc[:, width:]
        o_ref[0] = acc[:, :width] * jnp.concatenate([inv] * (width // LANE), axis=1)


def _fox_decode(page_table, cache_fox, qbd, ccols, cq, knew, vnew, dseq):
    nb, n_pages = page_table.shape
    npg = FOX_DEC_PAGES
    n_steps = n_pages // npg
    last = n_steps - 1
    width = FOX_HEADS * HEAD_DIM

    def page_map(p):
        return lambda b, s, pt: (pt[b, jnp.minimum(s, last) * npg + p], 0, 0)

    per_b = lambda shape: pl.BlockSpec((1,) + shape, lambda b, s, pt: (b, 0, 0))
    grid_spec = pltpu.PrefetchScalarGridSpec(
        num_scalar_prefetch=1,
        grid=(nb, n_steps + 1),
        in_specs=[pl.BlockSpec((1,) + cache_fox.shape[1:], page_map(p)) for p in range(npg)] + [
            per_b((width, LANE)),
            pl.BlockSpec((1, npg * PAGE_SIZE, LANE), lambda b, s, pt: (b, s, 0)),
            per_b((1, LANE)),
            per_b((LANE, width)),
            per_b((LANE, width)),
        ],
        out_specs=per_b((LANE, width)),
        scratch_shapes=[pltpu.VMEM((1, LANE), F32), pltpu.VMEM((LANE, width + LANE), F32)],
    )
    return pl.pallas_call(
        functools.partial(_fox_decode_kernel, dseq=dseq),
        out_shape=jax.ShapeDtypeStruct((nb, LANE, width), F32),
        grid_spec=grid_spec,
        compiler_params=_cparams(("parallel", "arbitrary")),
        name="fox_decode",
    )(page_table, *([cache_fox] * npg), qbd, ccols, cq, knew, vnew)


NSA_DEC_KEYS = 2048


def _lane_transpose(p):
    return jnp.concatenate([p[i * LANE:(i + 1) * LANE].T for i in range(p.shape[0] // LANE)], axis=1).astype(BF16)


def _nsa_decode_kernel(qbd_ref, kc_ref, vc_ref, mselt_ref, pair_ref, tcol_ref, ks_ref, vs_ref, kw_ref, vw_ref,
                       gate_ref, o_ref, drop_ref, m_ref, acc_ref, ocmp_ref, *, dseq, past, n_sel):
    step = pl.program_id(1)
    n_tiles = pl.num_programs(1)
    width = NSA_KV_HEADS * HEAD_DIM
    qbd = qbd_ref[0]
    col = lax.broadcasted_iota(jnp.int32, (1, LANE), 1)
    q_pos = past + col % dseq

    def side_by_side(ref, rows=None):
        parts = [ref[g, 0] if rows is None else ref[g, 0, rows, :] for g in range(NSA_KV_HEADS)]
        return jnp.concatenate(parts, axis=1)

    def rel_bias_cols(dist):
        out = jnp.broadcast_to(tcol_ref[0:1, :], dist.shape)
        for k, thr in enumerate(BUCKET_THR, start=1):
            out = jnp.where(dist >= thr, tcol_ref[k:k + 1, :], out)
        return out

    def softmax_pv(s_t, v_all):
        m = jnp.max(s_t, axis=0, keepdims=True)
        e = jnp.exp(s_t - m)
        inv = jnp.where(m > 0.5 * NEG, 1.0 / jnp.sum(e, axis=0, keepdims=True), 0.0)
        p = e * inv
        return p, jnp.dot(_lane_transpose(p), v_all, preferred_element_type=F32)

    @pl.when(step == 0)
    def _():
        m_ref[...] = jnp.full(m_ref.shape, NEG, F32)
        acc_ref[...] = jnp.zeros(acc_ref.shape, F32)
        ncp = kc_ref.shape[2]
        blk_end = lax.broadcasted_iota(jnp.int32, (ncp, LANE), 0) * CMP_STRIDE + (CMP_LEN - 1)
        d = q_pos - blk_end
        s_t = jnp.dot(side_by_side(kc_ref), qbd, preferred_element_type=F32)
        s_t = jnp.where(d >= 0, s_t + rel_bias_cols(jnp.maximum(d, 0)), NEG)
        p, ocmp_ref[...] = softmax_pv(s_t, side_by_side(vc_ref))
        imp = jnp.dot(mselt_ref[...], p, preferred_element_type=F32, precision=lax.Precision.HIGHEST)
        imp = jnp.dot(imp, pair_ref[...], preferred_element_type=F32, precision=lax.Precision.HIGHEST)
        nj = imp.shape[0]
        jrow = lax.broadcasted_iota(jnp.int32, (nj, LANE), 0)
        blk_q = jnp.right_shift(q_pos, int(math.log2(SEL_BLOCK)))
        forced = (jrow == 0) | (jrow == blk_q) | (jrow == blk_q - 1)
        score = jnp.where(forced, FORCED_SCORE, jnp.where(jrow <= blk_q, imp, -1.0))
        score = jnp.where(jrow < n_sel, score, -2.0)
        drop_ref[...] = score
        rank = jnp.zeros((nj, LANE), F32)

        def count(other, rank):
            row = jnp.broadcast_to(drop_ref[pl.ds(other, 1), :], (nj, LANE))
            beats = (row > score) | ((row == score) & (jrow > other))
            return rank + jnp.where(beats, 1.0, 0.0)

        rank = lax.fori_loop(0, n_sel, count, rank)
        far = tcol_ref[REL_BUCKETS - 1:REL_BUCKETS, :]
        drop_ref[...] = jnp.where((rank < SEL_TOPK) & (jrow <= blk_q), far, NEG)

    tile = ks_ref.shape[2]
    per_tile = tile // SEL_BLOCK
    start = step * tile
    first_blk = pl.multiple_of(step * per_tile, 8)
    rows = [jnp.broadcast_to(drop_ref[pl.ds(first_blk + i, 1), :], (SEL_BLOCK, LANE)) for i in range(per_tile)]
    s_t = jnp.dot(side_by_side(ks_ref), qbd, preferred_element_type=F32) + jnp.concatenate(rows, axis=0)

    def near_fix(s_t):
        def fix(rows_at, s_rows):
            key_pos = start + rows_at + lax.broadcasted_iota(jnp.int32, (LANE, LANE), 0)
            d = q_pos - key_pos
            far = tcol_ref[REL_BUCKETS - 1:REL_BUCKETS, :]
            return jnp.where(d >= 0, s_rows + (rel_bias_cols(jnp.maximum(d, 0)) - far), NEG)

        head = fix(0, s_t[:LANE])
        tail = fix(tile - LANE, s_t[tile - LANE:])
        return jnp.concatenate([head, s_t[LANE:tile - LANE], tail], axis=0)

    is_near = (start + tile > past - FAR_DIST)
    s_t = lax.cond(is_near, near_fix, lambda s: s, s_t)
    m_new = jnp.maximum(m_ref[...], jnp.max(s_t, axis=0, keepdims=True))
    alpha = jnp.exp(m_ref[...] - m_new)
    p = jnp.exp(s_t - m_new)
    v_aug = jnp.concatenate([side_by_side(vs_ref), jnp.ones((tile, LANE), BF16)], axis=1)
    upd = jnp.dot(_lane_transpose(p), v_aug, preferred_element_type=F32)
    alpha_col = jnp.broadcast_to(alpha, (LANE, LANE)).T
    acc_ref[...] = acc_ref[...] * jnp.concatenate([alpha_col] * (width // LANE + 1), axis=1) + upd
    m_ref[...] = m_new

    @pl.when(step == n_tiles - 1)
    def _():
        acc = acc_ref[...]
        o_slc = acc[:, :width] * jnp.concatenate([1.0 / acc[:, width:]] * (width // LANE), axis=1)
        span = kw_ref.shape[2]
        key_pos = (past - WINDOW) + lax.broadcasted_iota(jnp.int32, (span, LANE), 0)
        d = q_pos - key_pos
        s_w = jnp.dot(side_by_side(kw_ref), qbd, preferred_element_type=F32)
        s_w = jnp.where((d >= 0) & (d < WINDOW), s_w + rel_bias_cols(jnp.maximum(d, 0)), NEG)
        _, o_win = softmax_pv(s_w, side_by_side(vw_ref))
        g = gate_ref[0]
        tile4 = lambda a: jnp.concatenate([a] * (width // LANE), axis=1)
        o_ref[0] = tile4(g[0]) * ocmp_ref[...] + tile4(g[1]) * o_slc + tile4(g[2]) * o_win


def _nsa_decode(qbd, kcv, mselt, pair, tcols, nsa_dec, win_dec, gcols, *, dseq, past, n_sel):
    nb = qbd.shape[0]
    ncp = kcv.shape[2]
    nj = mselt.shape[0]
    span = win_dec.shape[2]
    tile = NSA_DEC_KEYS
    n_tiles = -(-nsa_dec.shape[2] // tile)
    width = NSA_KV_HEADS * HEAD_DIM
    kern = functools.partial(_nsa_decode_kernel, dseq=dseq, past=past, n_sel=n_sel)
    grp = lambda rows, half, tiled: pl.BlockSpec(
        (NSA_KV_HEADS, 1, rows, HEAD_DIM), (lambda b, s: (half, b, s, 0)) if tiled else (lambda b, s: (half, b, 0, 0)))
    const = lambda shape: pl.BlockSpec(shape, lambda b, s: (0,) * len(shape))
    return pl.pallas_call(
        kern,
        out_shape=jax.ShapeDtypeStruct((nb, LANE, width), F32),
        grid=(nb, n_tiles),
        in_specs=[
            pl.BlockSpec((1, width, LANE), lambda b, s: (b, 0, 0)),
            grp(ncp, 0, False), grp(ncp, 1, False),
            const((nj, ncp)), const((LANE, LANE)), const((REL_BUCKETS, LANE)),
            grp(tile, 0, True), grp(tile, 1, True),
            grp(span, 0, False), grp(span, 1, False),
            pl.BlockSpec((1, 3, LANE, LANE), lambda b, s: (b, 0, 0, 0)),
        ],
        out_specs=pl.BlockSpec((1, LANE, width), lambda b, s: (b, 0, 0)),
        scratch_shapes=[
            pltpu.VMEM((nj, LANE), F32),
            pltpu.VMEM((1, LANE), F32),
            pltpu.VMEM((LANE, width + LANE), F32),
            pltpu.VMEM((LANE, width), F32),
        ],
        compiler_params=_cparams(("parallel", "arbitrary")),
        name="nsa_decode",
    )(qbd, kcv, kcv, mselt, pair, tcols, nsa_dec, nsa_dec, win_dec, win_dec, gcols)


def _largest_tile(n, candidates):
    for c in candidates:
        if n % c == 0:
            return c
    raise ValueError(f"no tile in {candidates} divides {n}")


FFN_ROW_TILES = (1024, 512, 256, 128)
ROW_TILES = (512, 256, 128)


def _token_stage_in(x, p):
    m = x.shape[0]
    tm = m if m < ROW_TILES[-1] else _largest_tile(m, ROW_TILES)
    tm_ffn = m if m < FFN_ROW_TILES[-1] else _largest_tile(m, FFN_ROW_TILES)
    x1 = _ffn(x, p["norm_ffn1"], p["wg1"], p["wu1"], p["wd1"], tm_ffn, p["tf"])
    tm_in = m if m < ROW_TILES[-1] else _largest_tile(m, ROW_TILES[1:])
    nsa_rows, win_rows, fox_rows, hm = _inproj(x1, p["norm_mix"], p["w_main"], p["colgain"], tm_in)
    small = _small(x1, p["norm_mix"], p["w_small"], p["b_small"], tm)
    return x1, nsa_rows, win_rows, fox_rows, hm, small


def _token_stage_out(x1, o_nsa, o_fox, p):
    m = x1.shape[0]
    tm = m if m < ROW_TILES[-1] else _largest_tile(m, ROW_TILES[1:])
    tm_ffn = m if m < FFN_ROW_TILES[-1] else _largest_tile(m, FFN_ROW_TILES)
    x2 = _outproj(o_nsa, o_fox, p["out_norm_nsa"], p["out_norm_fox"], p["w_out"], x1, tm)
    return _ffn(x2, p["norm_ffn2"], p["wg2"], p["wu2"], p["wd2"], tm_ffn, p["tf"])


def kernel(x_prompt, x_sample, cache_nsa_kv, cache_fox_kv, cache_fox_logf, state_win_kv, page_table, rel_table, norm_ffn1, ffn1_gate, ffn1_up, ffn1_down, norm_mix, w_in, nsa_gate_bias, fox_forget_bias, q_norm_nsa, k_norm_nsa, q_norm_fox, k_norm_fox, cmp_pos_k, cmp_w1_k, cmp_w2_k, cmp_pos_v, cmp_w1_v, cmp_w2_v, out_norm_nsa, out_norm_fox, w_out, norm_ffn2, ffn2_gate, ffn2_up, ffn2_down):
    depth = w_in.shape[0]
    assert depth == 1, "single-layer trunk"
    nbp, seq, d_model = x_prompt.shape
    nbd, dseq, _ = x_sample.shape
    n_pages = page_table.shape[1]
    past = n_pages * PAGE_SIZE
    d_ff = ffn1_gate.shape[2]
    nsa_w = NSA_HEADS * HEAD_DIM
    kv6_w = 6 * NSA_KV_HEADS * HEAD_DIM
    fox_w = 3 * FOX_HEADS * HEAD_DIM
    off_gate = nsa_w + kv6_w
    off_fox = off_gate + N_GATE_COLS
    off_forget = off_fox + fox_w
    assert w_in.shape[2] == off_forget + FOX_HEADS and d_model == nsa_w + FOX_HEADS * HEAD_DIM
    assert seq % LANE == 0 and seq >= WINDOW and past % LANE == 0 and n_pages % REGROUP_PAGES == 0
    assert dseq <= 16 and state_win_kv.shape[2] == WINDOW
    assert seq % FOX_KEY_BLOCK == 0 and seq % (SLC_BLOCK_TILES * NSA_TQ) == 0 and seq >= WIN_TILES * NSA_TQ

    w0 = w_in[0]
    ones = lambda n: jnp.ones((n,), F32)
    zeros = lambda n: jnp.zeros((n,), F32)
    kn, kvw = NSA_KV_HEADS, NSA_KV_HEADS * HEAD_DIM
    qk_scale = HEAD_DIM ** -0.5
    p = {
        "tf": _largest_tile(d_ff, (512, 256, 128)),
        "norm_ffn1": norm_ffn1[0][None], "norm_mix": norm_mix[0][None], "norm_ffn2": norm_ffn2[0][None],
        "wg1": ffn1_gate[0].astype(BF16), "wu1": ffn1_up[0].astype(BF16), "wd1": ffn1_down[0].astype(BF16),
        "wg2": ffn2_gate[0].astype(BF16), "wu2": ffn2_up[0].astype(BF16), "wd2": ffn2_down[0].astype(BF16),
        "w_main": jnp.concatenate([w0[:, :off_gate], w0[:, off_fox:off_forget]], axis=1).astype(BF16),
        "w_small": jnp.concatenate([w0[:, off_gate:off_fox], w0[:, off_forget:],
                                    jnp.zeros((d_model, LANE - N_GATE_COLS - FOX_HEADS), F32)], axis=1).astype(BF16),
        "b_small": jnp.concatenate([nsa_gate_bias[0].reshape(-1), fox_forget_bias[0],
                                    zeros(LANE - N_GATE_COLS - FOX_HEADS)])[None],
        "colgain": jnp.concatenate([
            jnp.tile(q_norm_nsa[0] * qk_scale, NSA_HEADS), ones(2 * kvw), jnp.tile(k_norm_nsa[0], kn), ones(kvw),
            jnp.tile(k_norm_nsa[0], kn), ones(kvw), jnp.tile(q_norm_fox[0] * qk_scale, FOX_HEADS),
            jnp.tile(k_norm_fox[0], FOX_HEADS), ones(FOX_HEADS * HEAD_DIM)])[None],
        "out_norm_nsa": out_norm_nsa[0][None], "out_norm_fox": out_norm_fox[0][None],
        "w_out": w_out[0].astype(BF16),
    }
    half = CMP_STRIDE * HEAD_DIM

    def cmp_w1(w):
        return jnp.concatenate([w[0, :half], w[0, half:]], axis=1)

    def cmp_pe(pe):
        return jnp.concatenate([pe[0].reshape(CMP_LEN // CMP_STRIDE, half), jnp.zeros((PE_ROWS - CMP_LEN // CMP_STRIDE, half), F32)], axis=0)

    w1cat = jnp.stack([cmp_w1(cmp_w1_k), cmp_w1(cmp_w1_v)]).astype(BF16)
    w2cat = jnp.stack([cmp_w2_k[0], cmp_w2_v[0]]).astype(BF16)
    pecat = jnp.stack([cmp_pe(cmp_pos_k), cmp_pe(cmp_pos_v)]).astype(BF16)
    k_norm_row = k_norm_nsa[0][None]

    mp = nbp * seq
    x1, nsa_rows, win_rows, fox_rows, hm, small = _token_stage_in(x_prompt.reshape(mp, d_model), p)
    hm4 = hm.reshape(N_HEAD_COLS, nbp, seq, HEAD_DIM)

    logf = small[:, N_GATE_COLS:N_GATE_COLS + FOX_HEADS]
    csum = _cumsum(logf.reshape(nbp, seq, FOX_HEADS).transpose(0, 2, 1))
    o_fox = _fox_prompt(hm, hm4, csum[:, :, None, :], nbp, seq)

    n_chunk = seq // CMP_STRIDE
    n_cmp = (seq - CMP_LEN) // CMP_STRIDE + 1
    n_sel = -(-seq // SEL_BLOCK)
    xc = _chunkify(nsa_rows.reshape(nbp, seq * N_CACHE_COLS, HEAD_DIM), N_CACHE_COLS, N_CMP_COLS, seq)
    kcv = _compress(xc, 0, w1cat, w2cat, pecat, k_norm_row, n_chunk)
    mselt = _cmp_to_sel(n_cmp, n_sel, n_chunk, -(-n_sel // 8) * 8).T
    nq = seq // NSA_TQ
    gates_t = small[:, :N_GATE_COLS].reshape(nbp, nq, NSA_TQ, NSA_KV_HEADS, NSA_GROUP, 3)
    gates_t = gates_t.transpose(0, 3, 1, 5, 4, 2).reshape(nbp, NSA_KV_HEADS, nq, 3, NSA_COLS)
    gates_t = jnp.pad(gates_t, ((0, 0), (0, 0), (0, 0), (0, 8 - 3), (0, 0)))
    o_nsa = _nsa_prompt(rel_table, hm, hm4, kcv, mselt, gates_t, nbp, seq, n_sel)
    y_p = _token_stage_out(x1, o_nsa, o_fox, p)

    ms = nbd * dseq
    lk = past + NSA_DEC_KEYS
    assert n_pages % FOX_DEC_PAGES == 0 and past % NSA_DEC_KEYS == 0 and FOX_HEADS * dseq <= LANE
    xs1, nsa_rows_s, win_rows_s, fox_rows_s, hm_s, small_s = _token_stage_in(x_sample.reshape(ms, d_model), p)
    xc_d, nsa_dec, lf_dec = _regroup(
        page_table,
        cache_nsa_kv.reshape(cache_nsa_kv.shape[1], PAGE_SIZE * N_CACHE_COLS, HEAD_DIM),
        cache_fox_logf[0], lk)
    hm_s4 = hm_s.reshape(N_HEAD_COLS, nbd, dseq, HEAD_DIM)
    nsa_dec = lax.dynamic_update_slice(nsa_dec, hm_s4[HM_K_SLC:HM_K_WIN], (0, 0, past, 0))
    logf_s = small_s[:, N_GATE_COLS:N_GATE_COLS + FOX_HEADS].reshape(nbd, dseq, FOX_HEADS)
    lf_dec = lax.dynamic_update_slice(lf_dec, logf_s, (0, past, 0))

    csum_d = _cumsum(lf_dec.transpose(0, 2, 1))
    n_cols = FOX_HEADS * dseq
    lane_pad = lambda a: jnp.pad(a, [(0, 0)] * (a.ndim - 1) + [(0, LANE - n_cols)])
    head_eye = jnp.eye(FOX_HEADS, dtype=BF16)
    qbd = jnp.einsum("hbtd,hg->bhdgt", hm_s4[HM_Q_FOX:HM_Q_FOX + FOX_HEADS], head_eye)
    qbd = lane_pad(qbd.reshape(nbd, FOX_HEADS * HEAD_DIM, n_cols))
    ccols = lane_pad(jnp.repeat(csum_d.transpose(0, 2, 1), dseq, axis=2))
    cq = lane_pad(csum_d[:, :, past:past + dseq].reshape(nbd, 1, n_cols))

    def new_rows(head0):
        rows = hm_s4[head0:head0 + FOX_HEADS].transpose(1, 2, 0, 3).reshape(nbd, dseq, FOX_HEADS * HEAD_DIM)
        return jnp.pad(rows, ((0, 0), (0, LANE - dseq), (0, 0)))

    o_full = _fox_decode(page_table, cache_fox_kv.reshape(cache_fox_kv.shape[1], PAGE_SIZE * N_CACHE_COLS, HEAD_DIM),
                         qbd, ccols, cq, new_rows(HM_K_FOX), new_rows(HM_V_FOX), dseq)
    o_fox_s = jnp.concatenate([o_full[:, h * dseq:(h + 1) * dseq, h * HEAD_DIM:(h + 1) * HEAD_DIM]
                               for h in range(FOX_HEADS)], axis=2).reshape(ms, FOX_HEADS * HEAD_DIM)

    n_chunk_d = past // CMP_STRIDE
    n_cmp_d = (past + dseq - CMP_LEN) // CMP_STRIDE + 1
    n_sel_d = -(-(past + dseq) // SEL_BLOCK)
    assert n_cmp_d + CMP_LEN // CMP_STRIDE - 1 <= n_chunk_d, "compressed blocks must lie in the cached rows"
    kcv_d = _compress(xc_d, 0, w1cat, w2cat, pecat, k_norm_row, n_chunk_d)
    mselt_d = _cmp_to_sel(n_cmp_d, n_sel_d, n_chunk_d, lk // SEL_BLOCK).T
    win_old = state_win_kv[0].transpose(2, 3, 0, 1, 4).reshape(2 * NSA_KV_HEADS, nbd, WINDOW, HEAD_DIM).astype(BF16)
    win_dec = jnp.concatenate([win_old, hm_s4[HM_K_WIN:HM_V_WIN + NSA_KV_HEADS],
                               jnp.zeros((2 * NSA_KV_HEADS, nbd, LANE - dseq, HEAD_DIM), BF16)], axis=2)
    grp_eye = jnp.eye(NSA_KV_HEADS, dtype=BF16)
    q_grp = hm_s4[HM_Q_NSA:HM_Q_NSA + NSA_HEADS].reshape(NSA_KV_HEADS, NSA_GROUP, nbd, dseq, HEAD_DIM)
    qbd_n = jnp.einsum("gjbtd,gk->bgdkjt", q_grp, grp_eye)
    qbd_n = lane_pad(qbd_n.reshape(nbd, NSA_KV_HEADS * HEAD_DIM, n_cols))
    tcols = lane_pad(jnp.repeat(rel_table, dseq, axis=1))
    col_id = np.arange(n_cols)
    same = ((col_id[:, None] // (NSA_GROUP * dseq) == col_id[None, :] // (NSA_GROUP * dseq))
            & (col_id[:, None] % dseq == col_id[None, :] % dseq))
    pair = jnp.asarray(np.pad(same.astype(np.float32), ((0, LANE - n_cols), (0, LANE - n_cols))))
    gcols = small_s[:, :N_GATE_COLS].reshape(nbd, dseq, NSA_HEADS, 3).transpose(0, 3, 2, 1).reshape(nbd, 3, n_cols)
    gcols = jnp.broadcast_to(lane_pad(gcols)[..., None], (nbd, 3, LANE, LANE))
    o_full_n = _nsa_decode(qbd_n, kcv_d, mselt_d, pair, tcols, nsa_dec, win_dec, gcols,
                           dseq=dseq, past=past, n_sel=n_sel_d)
    o_nsa_s = jnp.concatenate(
        [o_full_n[:, h * dseq:(h + 1) * dseq, (h // NSA_GROUP) * HEAD_DIM:(h // NSA_GROUP + 1) * HEAD_DIM]
         for h in range(NSA_HEADS)], axis=2).reshape(ms, NSA_HEADS * HEAD_DIM)
    y_s = _token_stage_out(xs1, o_nsa_s, o_fox_s, p)

    kvh = (NSA_KV_HEADS, HEAD_DIM)
    win_keep = min(WINDOW, seq)
    win_p = win_rows.reshape(nbp, seq, 2, *kvh)[:, seq - win_keep:]
    win_s = jnp.concatenate([state_win_kv[0], win_rows_s.reshape(nbd, dseq, 2, *kvh)], axis=1)[:, dseq:]
    return (
        y_p.reshape(nbp, seq, d_model),
        y_s.reshape(nbd, dseq, d_model),
        nsa_rows.reshape(1, nbp, seq, 4, *kvh),
        fox_rows.reshape(1, nbp, seq, 2, FOX_HEADS, HEAD_DIM),
        logf.reshape(1, nbp, seq, FOX_HEADS),
        win_p[None],
        nsa_rows_s.reshape(1, nbd, dseq, 4, *kvh),
        fox_rows_s.reshape(1, nbd, dseq, 2, FOX_HEADS, HEAD_DIM),
        logf_s[None],
        win_s[None],
    )
```

```python
import functools
import math

import numpy as np
import jax
import jax.numpy as jnp
from jax import lax
from jax.experimental import pallas as pl
from jax.experimental.pallas import tpu as pltpu

HEAD_DIM = 128
NSA_HEADS = 8
FOX_HEADS = 8
NSA_KV_HEADS = 4
NSA_GROUP = NSA_HEADS // NSA_KV_HEADS
CMP_LEN = 32
CMP_STRIDE = 16
CMP_HIDDEN = 512
SEL_BLOCK = 64
SEL_TOPK = 16
WINDOW = 512
REL_BUCKETS = 32
REL_MAX_DIST = 128
RMS_EPS = 1e-6
PAGE_SIZE = 128

LANE = 128
NEG = -1e30
FORCED_SCORE = 1e30
VMEM_LIMIT = 56 * 1024 * 1024

BF16 = jnp.bfloat16
F32 = jnp.float32


def _bucket_thresholds():
    n = np.arange(0, 4 * REL_MAX_DIST)
    max_exact = REL_BUCKETS // 2
    nf = np.maximum(n, 1).astype(np.float32)
    large = max_exact + (np.log(nf / max_exact) / math.log(REL_MAX_DIST / max_exact)
                         * (REL_BUCKETS - max_exact)).astype(np.int32)
    bucket = np.where(n < max_exact, n, np.minimum(large, REL_BUCKETS - 1))
    return [int(np.min(n[bucket >= k])) for k in range(1, REL_BUCKETS)]


BUCKET_THR = _bucket_thresholds()
FAR_DIST = BUCKET_THR[-1]


def _cparams(sem):
    return pltpu.CompilerParams(dimension_semantics=sem, vmem_limit_bytes=VMEM_LIMIT)


def _rms_rows(x, gain):
    ms = jnp.mean(x * x, axis=-1, keepdims=True)
    return x * lax.rsqrt(ms + RMS_EPS) * gain


def _ffn_kernel(x_ref, g_ref, wg_ref, wu_ref, wd_ref, o_ref, xn_ref):
    @pl.when(pl.program_id(1) == 0)
    def _():
        x = x_ref[...]
        xn_ref[...] = _rms_rows(x, g_ref[...]).astype(BF16)
        o_ref[...] = x

    xn = xn_ref[...]
    a = jnp.dot(xn, wg_ref[...], preferred_element_type=F32)
    u = jnp.dot(xn, wu_ref[...], preferred_element_type=F32)
    h = (a / (1.0 + jnp.exp(-a))) * u * 0.5
    o_ref[...] += jnp.dot(h.astype(BF16), wd_ref[...], preferred_element_type=F32)


def _ffn(x, gain, wg, wu, wd, tm, tf):
    m, d = x.shape
    f = wg.shape[1]
    return pl.pallas_call(
        _ffn_kernel,
        out_shape=jax.ShapeDtypeStruct((m, d), F32),
        grid=(m // tm, f // tf),
        in_specs=[
            pl.BlockSpec((tm, d), lambda i, j: (i, 0)),
            pl.BlockSpec((1, d), lambda i, j: (0, 0)),
            pl.BlockSpec((d, tf), lambda i, j: (0, j)),
            pl.BlockSpec((d, tf), lambda i, j: (0, j)),
            pl.BlockSpec((tf, d), lambda i, j: (j, 0)),
        ],
        out_specs=pl.BlockSpec((tm, d), lambda i, j: (i, 0)),
        scratch_shapes=[pltpu.VMEM((tm, d), BF16)],
        compiler_params=_cparams(("parallel", "arbitrary")),
        name="ffn",
    )(x, gain, wg, wu, wd)


IN_TN = 4 * HEAD_DIM
J_NSA = (2, 6)
J_WIN = (6, 8)
J_FOX = (10, 14)
N_HEAD_COLS = 56
HM_Q_NSA, HM_K_CMP, HM_K_SLC, HM_V_SLC, HM_K_WIN, HM_V_WIN = 0, 8, 16, 20, 24, 28
HM_Q_FOX, HM_K_FOX, HM_V_FOX = 32, 40, 48


IN_NORMED_TILES = (0, 1, 4, 6, 8, 9, 10, 11)


def _inproj_kernel(x_ref, g_ref, w_ref, cg_ref, ones_ref, nsa_ref, win_ref, fox_ref, hm_ref):
    xn = _rms_rows(x_ref[...], g_ref[...]).astype(BF16)
    tm = xn.shape[0]
    heads = IN_TN // HEAD_DIM
    for j in range(w_ref.shape[1] // IN_TN):
        cols = slice(j * IN_TN, (j + 1) * IN_TN)
        vals = jnp.dot(xn, w_ref[:, cols], preferred_element_type=F32)
        if j in IN_NORMED_TILES:
            sumsq = jnp.dot((vals * vals).astype(BF16), ones_ref[...], preferred_element_type=F32)
            vals = vals * lax.rsqrt(sumsq * (1.0 / HEAD_DIM) + RMS_EPS) * cg_ref[:, cols]
        for hh in range(heads):
            hm_ref[j * heads + hh] = vals[:, hh * HEAD_DIM:(hh + 1) * HEAD_DIM].astype(BF16)
        for ref, (jlo, jhi) in ((nsa_ref, J_NSA), (win_ref, J_WIN), (fox_ref, J_FOX)):
            if jlo <= j < jhi:
                n_cols = (jhi - jlo) * heads
                for hh in range(heads):
                    ref[pl.ds((j - jlo) * heads + hh, tm, stride=n_cols), :] = vals[:, hh * HEAD_DIM:(hh + 1) * HEAD_DIM]


def _inproj(x, gain, w_main, colgain, tm):
    m, d = x.shape
    ncol = w_main.shape[1]
    heads = IN_TN // HEAD_DIM
    n_nsa, n_win, n_fox = [(hi - lo) * heads for lo, hi in (J_NSA, J_WIN, J_FOX)]
    head_ones = jnp.asarray(np.kron(np.eye(heads, dtype=np.float32), np.ones((HEAD_DIM, HEAD_DIM), np.float32)), BF16)
    resident = lambda shape: pl.BlockSpec(shape, lambda i: (0, 0), pipeline_mode=pl.Buffered(1))
    return pl.pallas_call(
        _inproj_kernel,
        out_shape=(
            jax.ShapeDtypeStruct((m * n_nsa, HEAD_DIM), F32),
            jax.ShapeDtypeStruct((m * n_win, HEAD_DIM), F32),
            jax.ShapeDtypeStruct((m * n_fox, HEAD_DIM), F32),
            jax.ShapeDtypeStruct((N_HEAD_COLS, m, HEAD_DIM), BF16),
        ),
        grid=(m // tm,),
        in_specs=[
            pl.BlockSpec((tm, d), lambda i: (i, 0)),
            resident((1, d)),
            resident((d, ncol)),
            resident((1, ncol)),
            resident((IN_TN, IN_TN)),
        ],
        out_specs=(
            pl.BlockSpec((tm * n_nsa, HEAD_DIM), lambda i: (i, 0)),
            pl.BlockSpec((tm * n_win, HEAD_DIM), lambda i: (i, 0)),
            pl.BlockSpec((tm * n_fox, HEAD_DIM), lambda i: (i, 0)),
            pl.BlockSpec((N_HEAD_COLS, tm, HEAD_DIM), lambda i: (0, i, 0)),
        ),
        compiler_params=_cparams(("parallel",)),
        name="inproj",
    )(x, gain, w_main, colgain, head_ones)


N_GATE_COLS = 3 * NSA_HEADS


def _small_kernel(x_ref, g_ref, w_ref, b_ref, o_ref):
    xn = _rms_rows(x_ref[...], g_ref[...]).astype(BF16)
    z = jnp.dot(xn, w_ref[...], preferred_element_type=F32) + b_ref[...]
    lane = lax.broadcasted_iota(jnp.int32, z.shape, 1)
    sig = 1.0 / (1.0 + jnp.exp(-z))
    logsig = jnp.minimum(z, 0.0) - jnp.log(1.0 + jnp.exp(-jnp.abs(z)))
    o_ref[...] = jnp.where(lane < N_GATE_COLS, sig,
                           jnp.where(lane < N_GATE_COLS + FOX_HEADS, logsig, 0.0))


def _small(x, gain, w_small, b_small, tm):
    m, d = x.shape
    return pl.pallas_call(
        _small_kernel,
        out_shape=jax.ShapeDtypeStruct((m, LANE), F32),
        grid=(m // tm,),
        in_specs=[
            pl.BlockSpec((tm, d), lambda i: (i, 0)),
            pl.BlockSpec((1, d), lambda i: (0, 0)),
            pl.BlockSpec((d, LANE), lambda i: (0, 0)),
            pl.BlockSpec((1, LANE), lambda i: (0, 0)),
        ],
        out_specs=pl.BlockSpec((tm, LANE), lambda i: (i, 0)),
        compiler_params=_cparams(("parallel",)),
        name="gates",
    )(x, gain, w_small, b_small)


def _outproj_kernel(on_ref, of_ref, gn_ref, gf_ref, w_ref, x_ref, y_ref):
    a = _rms_rows(on_ref[...], gn_ref[...]).astype(BF16)
    b = _rms_rows(of_ref[...], gf_ref[...]).astype(BF16)
    half = a.shape[1]
    y = jnp.dot(a, w_ref[:half, :], preferred_element_type=F32)
    y = y + jnp.dot(b, w_ref[half:, :], preferred_element_type=F32)
    y_ref[...] = x_ref[...] + y


def _outproj(o_nsa, o_fox, g_nsa, g_fox, w_out, x, tm):
    m, d = x.shape
    wn = o_nsa.shape[1]
    wf = o_fox.shape[1]
    return pl.pallas_call(
        _outproj_kernel,
        out_shape=jax.ShapeDtypeStruct((m, d), F32),
        grid=(m // tm,),
        in_specs=[
            pl.BlockSpec((tm, wn), lambda i: (i, 0)),
            pl.BlockSpec((tm, wf), lambda i: (i, 0)),
            pl.BlockSpec((1, wn), lambda i: (0, 0)),
            pl.BlockSpec((1, wf), lambda i: (0, 0)),
            pl.BlockSpec((wn + wf, d), lambda i: (0, 0)),
            pl.BlockSpec((tm, d), lambda i: (i, 0)),
        ],
        out_specs=pl.BlockSpec((tm, d), lambda i: (i, 0)),
        compiler_params=_cparams(("parallel",)),
        name="outproj",
    )(o_nsa, o_fox, g_nsa, g_fox, w_out, x)


CUMSUM_CHUNK = 512


def _cumsum_kernel(x_ref, before_ref, o_ref):
    width = x_ref.shape[2]
    r = lax.broadcasted_iota(jnp.int32, (width, width), 0)
    c = lax.broadcasted_iota(jnp.int32, (width, width), 1)
    upper = (r <= c).astype(F32)
    local = jnp.dot(x_ref[0], upper, preferred_element_type=F32, precision=lax.Precision.HIGHEST)
    totals = jnp.broadcast_to(local[:, width - 1:width], (local.shape[0], LANE))
    offset = jnp.dot(before_ref[...], totals, preferred_element_type=F32, precision=lax.Precision.HIGHEST)
    o_ref[0] = local + jnp.concatenate([offset] * (width // LANE), axis=1)


def _cumsum(x):
    b, h, length = x.shape
    assert length % CUMSUM_CHUNK == 0
    pieces = length // CUMSUM_CHUNK
    used = h * pieces
    rows = -(-used // LANE) * LANE
    idx = np.arange(rows)
    before = ((idx[:, None] // pieces == idx[None, :] // pieces) & (idx[None, :] < idx[:, None])
              & (idx[:, None] < used))
    x = jnp.pad(x.reshape(b, used, CUMSUM_CHUNK), ((0, 0), (0, rows - used), (0, 0)))
    out = pl.pallas_call(
        _cumsum_kernel,
        out_shape=jax.ShapeDtypeStruct((b, rows, CUMSUM_CHUNK), F32),
        grid=(b,),
        in_specs=[pl.BlockSpec((1, rows, CUMSUM_CHUNK), lambda i: (i, 0, 0)),
                  pl.BlockSpec((rows, rows), lambda i: (0, 0))],
        out_specs=pl.BlockSpec((1, rows, CUMSUM_CHUNK), lambda i: (i, 0, 0)),
        compiler_params=_cparams(("parallel",)),
        name="logf_cumsum",
    )(x, jnp.asarray(before, F32))
    return out[:, :used].reshape(b, h, length)


PE_ROWS = 16


def _compress_kernel(x_ref, w1_ref, w2_ref, pe_ref, kn_ref, o_ref):
    kind = pl.program_id(0) // NSA_KV_HEADS
    n = x_ref.shape[2]
    w1 = w1_ref[0]
    h = jnp.dot(x_ref[0, 0], w1, preferred_element_type=F32)
    pw = jnp.dot(pe_ref[0], w1, preferred_element_type=F32)
    const = pw[0:1, :CMP_HIDDEN] + pw[1:2, CMP_HIDDEN:]
    hid = h[:, :CMP_HIDDEN] + pltpu.roll(h[:, CMP_HIDDEN:], n - 1, 0) + const
    act = hid / (1.0 + jnp.exp(-hid))
    out = jnp.dot(act.astype(BF16), w2_ref[0], preferred_element_type=F32)
    normed = _rms_rows(out, kn_ref[...])
    o_ref[0, 0] = jnp.where(kind == 0, normed, out).astype(BF16)


def _compress(xc, c_off, w1cat, w2, pe, k_norm, n_rows):
    nb = xc.shape[1]
    return pl.pallas_call(
        _compress_kernel,
        out_shape=jax.ShapeDtypeStruct((2 * NSA_KV_HEADS, nb, n_rows, HEAD_DIM), BF16),
        grid=(2 * NSA_KV_HEADS, nb),
        in_specs=[
            pl.BlockSpec((1, 1, n_rows, CMP_STRIDE * HEAD_DIM), lambda c, b: (c_off + c, b, 0, 0)),
            pl.BlockSpec((1, CMP_STRIDE * HEAD_DIM, 2 * CMP_HIDDEN), lambda c, b: (c // NSA_KV_HEADS, 0, 0)),
            pl.BlockSpec((1, CMP_HIDDEN, HEAD_DIM), lambda c, b: (c // NSA_KV_HEADS, 0, 0)),
            pl.BlockSpec((1, PE_ROWS, CMP_STRIDE * HEAD_DIM), lambda c, b: (c // NSA_KV_HEADS, 0, 0)),
            pl.BlockSpec((1, HEAD_DIM), lambda c, b: (0, 0)),
        ],
        out_specs=pl.BlockSpec((1, 1, n_rows, HEAD_DIM), lambda c, b: (c, b, 0, 0)),
        compiler_params=_cparams(("parallel", "parallel")),
        name="compress",
    )(xc, w1cat, w2, pe, k_norm)


def _rel_bias(dist, tbl_ref, head):
    out = jnp.full(dist.shape, tbl_ref[0, head], F32)
    for k, thr in enumerate(BUCKET_THR, start=1):
        out = jnp.where(dist >= thr, tbl_ref[k, head], out)
    return out


def _cmp_to_sel(n_cmp, n_sel, rows, cols):
    c0 = np.arange(n_cmp)[:, None] * CMP_STRIDE
    s0 = np.arange(n_sel)[None, :] * SEL_BLOCK
    inter = np.clip(np.minimum(c0 + CMP_LEN, s0 + SEL_BLOCK) - np.maximum(c0, s0), 0, None)
    m = np.zeros((rows, cols), np.float32)
    m[:n_cmp, :n_sel] = inter / CMP_LEN
    return jnp.asarray(m)


def _transpose_tiles(x):
    n = x.shape[0] // LANE
    xf = x.astype(F32)
    return jnp.concatenate([xf[i * LANE:(i + 1) * LANE].T for i in range(n)], axis=1)


def _untranspose_tiles(xt):
    n = xt.shape[1] // LANE
    return jnp.concatenate([xt[:, i * LANE:(i + 1) * LANE].T for i in range(n)], axis=0)


AUX_ROWS = 16


def _fill_transposed(dst_ref, src_ref):
    def body(k, carry):
        st = pl.multiple_of(k * LANE, LANE)
        dst_ref[:HEAD_DIM, pl.ds(st, LANE)] = src_ref[0, 0, pl.ds(st, LANE), :].astype(F32).T.astype(BF16)
        return carry

    lax.fori_loop(0, src_ref.shape[2] // LANE, body, 0)


FOX_TQ = 256
FOX_KEY_BLOCK = 1024


def _split3(x):
    hi = x.astype(BF16)
    r1 = x - hi.astype(F32)
    mid = r1.astype(BF16)
    lo = (r1 - mid.astype(F32)).astype(BF16)
    return hi, mid, lo


def _lane_select3(parts, shape):
    lane = lax.broadcasted_iota(jnp.int32, shape, 1)
    hi, mid, lo = [part.astype(F32) for part in parts]
    return jnp.where(lane == 0, hi, jnp.where(lane == 1, mid, jnp.where(lane == 2, lo, 0.0))).astype(BF16)


def _ones_rows(width):
    row = lax.broadcasted_iota(jnp.int32, (AUX_ROWS, width), 0)
    return jnp.where(row == 0, 1.0, 0.0).astype(BF16)


def _finish_aug(acc):
    return acc[:HEAD_DIM] * (1.0 / acc[HEAD_DIM:HEAD_DIM + 1])


def _fox_prompt_kernel(q_ref, k_ref, v_ref, crow_ref, o_ref, vt_ref, ka_ref, s_ref):
    qt = pl.program_id(2)
    tq = FOX_TQ

    @pl.when(qt == 0)
    def _():
        _fill_transposed(vt_ref, v_ref)
        vt_ref[HEAD_DIM:, :] = _ones_rows(vt_ref.shape[1])

        def body(k, carry):
            st = pl.multiple_of(k * LANE, LANE)
            c_col = jnp.broadcast_to(crow_ref[0, 0, :, pl.ds(st, LANE)], (LANE, LANE)).T
            ka_ref[pl.ds(st, LANE), :HEAD_DIM] = k_ref[0, 0, pl.ds(st, LANE), :]
            ka_ref[pl.ds(st, LANE), HEAD_DIM:] = _lane_select3(_split3(c_col), (LANE, LANE))
            return carry

        lax.fori_loop(0, k_ref.shape[2] // LANE, body, 0)

    q0 = pl.multiple_of(qt * tq, tq)
    q_t = _transpose_tiles(q_ref[0]).astype(BF16)
    row = lax.broadcasted_iota(jnp.int32, (LANE, tq), 0)
    q_aug = jnp.concatenate([q_t, jnp.where(row < 3, -1.0, 0.0).astype(BF16)], axis=0)
    c_q = crow_ref[0, 0, :, pl.ds(q0, tq)]
    blk = FOX_KEY_BLOCK

    last = q0 // blk
    rel = (lax.broadcasted_iota(jnp.int32, (blk, tq), 0) - lax.broadcasted_iota(jnp.int32, (blk, tq), 1))

    def run(n_blocks):
        top = jnp.full((8, tq), NEG, F32)
        for kb in range(n_blocks):
            start = kb * blk
            s = jnp.dot(ka_ref[start:start + blk, :], q_aug, preferred_element_type=F32)
            if kb == n_blocks - 1:
                s = jnp.where(rel <= q0 - start, s, NEG)
            s_ref[start:start + blk, :] = s
            top = jnp.maximum(top, jnp.max(s.reshape(blk // 8, 8, tq), axis=0))
        m = jnp.max(top, axis=0, keepdims=True) + c_q
        shift = m - c_q
        acc = jnp.zeros((HEAD_DIM + AUX_ROWS, tq), F32)
        for kb in range(n_blocks):
            start = kb * blk
            p = jnp.exp(s_ref[start:start + blk, :] - shift).astype(BF16)
            acc = acc + jnp.dot(vt_ref[:, start:start + blk], p, preferred_element_type=F32)
        return acc

    n_max = k_ref.shape[2] // blk
    acc = lax.switch(last, [functools.partial(run, n) for n in range(1, n_max + 1)])
    o_ref[...] = _untranspose_tiles(_finish_aug(acc))


def _fox_prompt(hm, hm4, crow, nb, seq):
    nq = seq // FOX_TQ
    return pl.pallas_call(
        _fox_prompt_kernel,
        out_shape=jax.ShapeDtypeStruct((nb * seq, FOX_HEADS * HEAD_DIM), F32),
        grid=(nb, FOX_HEADS, nq),
        in_specs=[
            pl.BlockSpec((1, FOX_TQ, HEAD_DIM), lambda b, h, t: (HM_Q_FOX + h, b * nq + t, 0)),
            pl.BlockSpec((1, 1, seq, HEAD_DIM), lambda b, h, t: (HM_K_FOX + h, b, 0, 0)),
            pl.BlockSpec((1, 1, seq, HEAD_DIM), lambda b, h, t: (HM_V_FOX + h, b, 0, 0)),
            pl.BlockSpec((1, 1, 1, seq), lambda b, h, t: (b, h, 0, 0)),
        ],
        out_specs=pl.BlockSpec((FOX_TQ, HEAD_DIM), lambda b, h, t: (b * nq + t, h)),
        scratch_shapes=[pltpu.VMEM((HEAD_DIM + AUX_ROWS, seq), BF16), pltpu.VMEM((seq, 2 * HEAD_DIM), BF16),
                        pltpu.VMEM((seq, FOX_TQ), F32)],
        compiler_params=_cparams(("parallel", "parallel", "arbitrary")),
        name="fox_prompt",
    )(hm, hm4, hm4, crow)


NSA_TQ = LANE
NSA_COLS = NSA_GROUP * NSA_TQ
WIN_TILES = WINDOW // NSA_TQ + 1
SLC_BLOCK_TILES = 4
SLC_MASK_ROWS = SLC_BLOCK_TILES * NSA_TQ // SEL_BLOCK


def _nsa_prompt_kernel(tbl_ref, q_ref, kc_ref, vc_ref, mselt_ref, ks_ref, vs_ref, kw_ref, vw_ref, gate_ref, o_ref,
                       vst_ref, vwt_ref, vct_ref, wb_ref, pc_ref, drop_ref, ksa_ref, wbd_ref, s_ref, *, n_sel, cmp_back):
    g = pl.program_id(1)
    qt = pl.program_id(2)
    tq = NSA_TQ
    cols = NSA_COLS
    heads = [NSA_GROUP * g + hh for hh in range(NSA_GROUP)]
    ncp = kc_ref.shape[2]
    nj = mselt_ref.shape[0]

    @pl.when(qt == 0)
    def _():
        _fill_transposed(vst_ref, vs_ref)
        _fill_transposed(vwt_ref, vw_ref)
        vst_ref[HEAD_DIM:, :] = _ones_rows(vst_ref.shape[1])
        vwt_ref[HEAD_DIM:, :] = _ones_rows(vwt_ref.shape[1])
        vct_ref[...] = _transpose_tiles(vc_ref[0, 0]).astype(BF16)

        def fill_keys(k, carry):
            st = pl.multiple_of(k * tq, tq)
            lane = lax.broadcasted_iota(jnp.int32, (tq, tq), 1)
            blk_in_step = (k % SLC_BLOCK_TILES) * (tq // SEL_BLOCK) + lax.broadcasted_iota(jnp.int32, (tq, tq), 0) // SEL_BLOCK
            extra = (lane == blk_in_step) | ((lane >= SLC_MASK_ROWS) & (lane < SLC_MASK_ROWS + 3))
            ksa_ref[pl.ds(st, tq), :HEAD_DIM] = ks_ref[0, 0, pl.ds(st, tq), :]
            ksa_ref[pl.ds(st, tq), HEAD_DIM:] = jnp.where(extra, 1.0, 0.0).astype(BF16)
            return carry

        lax.fori_loop(0, ks_ref.shape[2] // tq, fill_keys, 0)
        key = lax.broadcasted_iota(jnp.int32, (tq, tq), 0)
        qry = lax.broadcasted_iota(jnp.int32, (tq, tq), 1)
        blk = lax.broadcasted_iota(jnp.int32, (pc_ref.shape[0], tq), 0) - cmp_back
        d_cmp = lax.broadcasted_iota(jnp.int32, (pc_ref.shape[0], tq), 1) - (blk * CMP_STRIDE + (CMP_LEN - 1))
        for hh in range(NSA_GROUP):
            sl = slice(hh * tq, (hh + 1) * tq)
            far = tbl_ref[REL_BUCKETS - 1, heads[hh]]
            for delta in range(2):
                d = delta * tq + qry - key
                wb_ref[delta, :, sl] = jnp.where(d >= 0, _rel_bias(jnp.maximum(d, 0), tbl_ref, heads[hh]), NEG)
            for delta in range(2, WIN_TILES - 1):
                wb_ref[delta, :, sl] = jnp.full((tq, tq), far, F32)
            wb_ref[WIN_TILES - 1, :, sl] = jnp.where(qry < key, far, NEG)
            wb_ref[WIN_TILES, :, sl] = jnp.full((tq, tq), NEG, F32)
            for delta in range(2):
                wbd_ref[delta, :, sl] = wb_ref[delta, :, sl] - far
            wbd_ref[2, :, sl] = jnp.zeros((tq, tq), F32)
            wbd_ref[3, :, sl] = jnp.full((tq, tq), NEG, F32)
            pc_ref[:, sl] = jnp.where(d_cmp >= 0, _rel_bias(jnp.maximum(d_cmp, 0), tbl_ref, heads[hh]), NEG)

    q_t = jnp.concatenate([q_ref[hh].astype(F32).T for hh in range(NSA_GROUP)], axis=1).astype(BF16)

    off = pl.multiple_of(cmp_back - qt * (tq // CMP_STRIDE), 8)
    s_t = jnp.dot(kc_ref[0, 0], q_t, preferred_element_type=F32) + pc_ref[pl.ds(off, ncp), :]
    m = jnp.max(s_t, axis=0, keepdims=True)
    e = jnp.exp(s_t - m)
    inv = jnp.where(m > 0.5 * NEG, 1.0 / jnp.sum(e, axis=0, keepdims=True), 0.0)
    p_t = e * inv
    o_cmp = jnp.dot(vct_ref[...], p_t.astype(BF16), preferred_element_type=F32)
    p_sum = p_t[:, :tq]
    for hh in range(1, NSA_GROUP):
        p_sum = p_sum + p_t[:, hh * tq:(hh + 1) * tq]
    imp = jnp.dot(mselt_ref[...], p_sum, preferred_element_type=F32, precision=lax.Precision.HIGHEST)

    jrow = lax.broadcasted_iota(jnp.int32, (nj, tq), 0)
    pos = qt * tq + lax.broadcasted_iota(jnp.int32, (nj, tq), 1)
    blk_q = jnp.right_shift(pos, int(math.log2(SEL_BLOCK)))
    forced = (jrow == 0) | (jrow == blk_q) | (jrow == blk_q - 1)
    score = jnp.where(forced, FORCED_SCORE, jnp.where(jrow <= blk_q, imp, -1.0))
    score = jnp.where(jrow < n_sel, score, -2.0)
    ranks = []
    for r in range(nj // 8):
        mine = score[r * 8:(r + 1) * 8]
        jmine = jrow[r * 8:(r + 1) * 8]
        rank = jnp.zeros((8, tq), F32)
        for other in range(n_sel):
            row = jnp.broadcast_to(score[other:other + 1], (8, tq))
            if other < r * 8:
                beats = row >= mine
            elif other >= (r + 1) * 8:
                beats = row > mine
            else:
                beats = (row > mine) | ((row == mine) & (jmine > other))
            rank = rank + jnp.where(beats, 1.0, 0.0)
        ranks.append(rank)
    rank = jnp.concatenate(ranks, axis=0)
    drop = jnp.where((rank < SEL_TOPK) & (jrow < n_sel), 0.0, NEG)
    drop_ref[...] = jnp.concatenate([drop] * NSA_GROUP, axis=1)

    blk = SLC_BLOCK_TILES * tq
    col = lax.broadcasted_iota(jnp.int32, (8, cols), 1)
    far_row = jnp.full((8, cols), tbl_ref[REL_BUCKETS - 1, heads[NSA_GROUP - 1]], F32)
    for hh in range(NSA_GROUP - 1):
        far_row = jnp.where(col // tq == hh, tbl_ref[REL_BUCKETS - 1, heads[hh]], far_row)
    row8 = lax.broadcasted_iota(jnp.int32, (8, cols), 0)
    parts = [part.astype(F32) for part in _split3(far_row)]
    far_rows = jnp.where(row8 == 0, parts[0], jnp.where(row8 == 1, parts[1], jnp.where(row8 == 2, parts[2], 0.0)))
    pad_rows = jnp.zeros((HEAD_DIM - SLC_MASK_ROWS - 8, cols), BF16)

    last = qt // SLC_BLOCK_TILES

    def slc_run(n_blocks):
        top = jnp.full((8, cols), NEG, F32)
        for kb in range(n_blocks):
            start = kb * blk
            masks = drop_ref[kb * SLC_MASK_ROWS:(kb + 1) * SLC_MASK_ROWS, :]
            extra = jnp.concatenate([masks, far_rows], axis=0).astype(BF16)
            q_aug = jnp.concatenate([q_t, extra, pad_rows], axis=0)
            s = jnp.dot(ksa_ref[start:start + blk, :], q_aug, preferred_element_type=F32)
            if kb >= n_blocks - 2:
                terms = []
                for i in range(SLC_BLOCK_TILES):
                    delta = qt - (kb * SLC_BLOCK_TILES + i)
                    terms.append(wbd_ref[jnp.where(delta < 0, 3, jnp.minimum(delta, 2))])
                s = s + jnp.concatenate(terms, axis=0)
            s_ref[start:start + blk, :] = s
            top = jnp.maximum(top, jnp.max(s.reshape(blk // 8, 8, cols), axis=0))
        m = jnp.max(top, axis=0, keepdims=True)
        acc = jnp.zeros((HEAD_DIM + AUX_ROWS, cols), F32)
        for kb in range(n_blocks):
            start = kb * blk
            p = jnp.exp(s_ref[start:start + blk, :] - m).astype(BF16)
            acc = acc + jnp.dot(vst_ref[:, start:start + blk], p, preferred_element_type=F32)
        return acc

    n_max = ks_ref.shape[2] // blk
    o_slc = _finish_aug(lax.switch(last, [functools.partial(slc_run, n) for n in range(1, n_max + 1)]))

    w0 = jnp.maximum(qt - (WIN_TILES - 1), 0)
    start = pl.multiple_of(w0 * tq, tq)
    span = WIN_TILES * tq
    s = jnp.dot(kw_ref[0, 0, pl.ds(start, span), :], q_t, preferred_element_type=F32)
    terms = []
    for i in range(WIN_TILES):
        delta = qt - (w0 + i)
        terms.append(wb_ref[jnp.where(delta < 0, WIN_TILES, delta)])
    s = s + jnp.concatenate(terms, axis=0)
    p = jnp.exp(s - jnp.max(s, axis=0, keepdims=True))
    o_win = _finish_aug(jnp.dot(vwt_ref[:, pl.ds(start, span)], p.astype(BF16), preferred_element_type=F32))

    gate = gate_ref[0, 0, 0]
    o_t = gate[0:1] * o_cmp + gate[1:2] * o_slc + gate[2:3] * o_win
    for hh in range(NSA_GROUP):
        o_ref[:, hh * HEAD_DIM:(hh + 1) * HEAD_DIM] = o_t[:, hh * tq:(hh + 1) * tq].T


def _nsa_prompt(rel_table, hm, hm4, kcv, mselt, gates_t, nb, seq, n_sel):
    tq = NSA_TQ
    nq = seq // tq
    ncp = kcv.shape[2]
    nj = mselt.shape[0]
    cmp_back = (nq - 1) * (tq // CMP_STRIDE)
    kern = functools.partial(_nsa_prompt_kernel, n_sel=n_sel, cmp_back=cmp_back)
    kv_spec = lambda head0: pl.BlockSpec((1, 1, seq, HEAD_DIM), lambda b, g, t: (head0 + g, b, 0, 0))
    return pl.pallas_call(
        kern,
        out_shape=jax.ShapeDtypeStruct((nb * seq, NSA_HEADS * HEAD_DIM), F32),
        grid=(nb, NSA_KV_HEADS, nq),
        in_specs=[
            pl.BlockSpec(memory_space=pltpu.SMEM),
            pl.BlockSpec((NSA_GROUP, tq, HEAD_DIM), lambda b, g, t: (g, b * nq + t, 0)),
            pl.BlockSpec((1, 1, ncp, HEAD_DIM), lambda b, g, t: (g, b, 0, 0)),
            pl.BlockSpec((1, 1, ncp, HEAD_DIM), lambda b, g, t: (NSA_KV_HEADS + g, b, 0, 0)),
            pl.BlockSpec((nj, ncp), lambda b, g, t: (0, 0)),
            kv_spec(HM_K_SLC), kv_spec(HM_V_SLC), kv_spec(HM_K_WIN), kv_spec(HM_V_WIN),
            pl.BlockSpec((1, 1, 1, 8, NSA_COLS), lambda b, g, t: (b, g, t, 0, 0)),
        ],
        out_specs=pl.BlockSpec((tq, NSA_GROUP * HEAD_DIM), lambda b, g, t: (b * nq + t, g)),
        scratch_shapes=[
            pltpu.VMEM((HEAD_DIM + AUX_ROWS, seq), BF16),
            pltpu.VMEM((HEAD_DIM + AUX_ROWS, seq), BF16),
            pltpu.VMEM((HEAD_DIM, ncp), BF16),
            pltpu.VMEM((WIN_TILES + 1, tq, NSA_COLS), F32),
            pltpu.VMEM((cmp_back + ncp, NSA_COLS), F32),
            pltpu.VMEM((nj, NSA_COLS), F32),
            pltpu.VMEM((seq, 2 * HEAD_DIM), BF16),
            pltpu.VMEM((4, tq, NSA_COLS), F32),
            pltpu.VMEM((seq, NSA_COLS), F32),
        ],
        compiler_params=_cparams(("parallel", "parallel", "arbitrary")),
        name="nsa_prompt",
    )(rel_table, hm, kcv, kcv, mselt, hm4, hm4, hm4, hm4, gates_t)


CHUNK_ROWS = 512


def _chunkify_kernel(x_ref, o_ref, *, n_cols):
    n = o_ref.shape[2]
    for c in range(o_ref.shape[0]):
        for s in range(CMP_STRIDE):
            o_ref[c, 0, :, s * HEAD_DIM:(s + 1) * HEAD_DIM] = (
                x_ref[0, pl.ds(s * n_cols + c, n, stride=CMP_STRIDE * n_cols), :].astype(BF16))


def _chunkify(rows3, n_cols, n_heads, seq):
    nb = rows3.shape[0]
    tr = _largest_tile(seq, (CHUNK_ROWS, 256))
    return pl.pallas_call(
        functools.partial(_chunkify_kernel, n_cols=n_cols),
        out_shape=jax.ShapeDtypeStruct((n_heads, nb, seq // CMP_STRIDE, CMP_STRIDE * HEAD_DIM), BF16),
        grid=(nb, seq // tr),
        in_specs=[pl.BlockSpec((1, tr * n_cols, HEAD_DIM), lambda b, i: (b, i, 0))],
        out_specs=pl.BlockSpec((n_heads, 1, tr // CMP_STRIDE, CMP_STRIDE * HEAD_DIM), lambda b, i: (0, b, i, 0)),
        compiler_params=_cparams(("parallel", "parallel")),
        name="chunkify",
    )(rows3)


REGROUP_PAGES = 8


N_CACHE_COLS = 16
N_CMP_COLS = 2 * NSA_KV_HEADS


def _regroup_kernel(pt_ref, *refs, n_steps):
    del pt_ref
    npg = REGROUP_PAGES
    nsa_in, lf_in = refs[:npg], refs[npg:2 * npg]
    xc_out, slc_out, lf_out = refs[2 * npg:]
    is_tail = pl.program_id(1) >= n_steps
    chunks = PAGE_SIZE // CMP_STRIDE
    chunk_stride = CMP_STRIDE * N_CACHE_COLS

    @pl.when(is_tail)
    def _():
        slc_out[...] = jnp.zeros(slc_out.shape, slc_out.dtype)
        lf_out[...] = jnp.zeros(lf_out.shape, lf_out.dtype)

    @pl.when(jnp.logical_not(is_tail))
    def _():
        for p in range(npg):
            rows = slice(p * PAGE_SIZE, (p + 1) * PAGE_SIZE)
            for c in range(N_CACHE_COLS - N_CMP_COLS):
                slc_out[c, 0, rows, :] = nsa_in[p][0, pl.ds(N_CMP_COLS + c, PAGE_SIZE, stride=N_CACHE_COLS), :].astype(BF16)
            lf_out[0, rows, :] = lf_in[p][0]
        for pair in range(npg // 2):
            for c in range(N_CMP_COLS):
                for s in range(CMP_STRIDE):
                    first = s * N_CACHE_COLS + c
                    both = [nsa_in[2 * pair + i][0, pl.ds(first, chunks, stride=chunk_stride), :] for i in range(2)]
                    xc_out[c, 0, pair * 2 * chunks:(pair + 1) * 2 * chunks, s * HEAD_DIM:(s + 1) * HEAD_DIM] = (
                        jnp.concatenate(both, axis=0).astype(BF16))


def _regroup(page_table, cache_nsa, cache_logf, lk):
    nb, n_pages = page_table.shape
    npg = REGROUP_PAGES
    n_steps = n_pages // npg
    rows = npg * PAGE_SIZE
    last = n_steps - 1
    n_slc = N_CACHE_COLS - N_CMP_COLS
    chunk_w = CMP_STRIDE * HEAD_DIM

    def page_map(p):
        return lambda b, s, pt: (pt[b, jnp.minimum(s, last) * npg + p], 0, 0)

    def specs(arr):
        return [pl.BlockSpec((1,) + arr.shape[1:], page_map(p)) for p in range(npg)]

    assert (lk - n_pages * PAGE_SIZE) % rows == 0
    n_tail = (lk - n_pages * PAGE_SIZE) // rows
    grid_spec = pltpu.PrefetchScalarGridSpec(
        num_scalar_prefetch=1,
        grid=(nb, n_steps + n_tail),
        in_specs=specs(cache_nsa) + specs(cache_logf),
        out_specs=(
            pl.BlockSpec((N_CMP_COLS, 1, rows // CMP_STRIDE, chunk_w), lambda b, s, pt: (0, b, jnp.minimum(s, last), 0)),
            pl.BlockSpec((n_slc, 1, rows, HEAD_DIM), lambda b, s, pt: (0, b, s, 0)),
            pl.BlockSpec((1, rows, cache_logf.shape[2]), lambda b, s, pt: (b, s, 0)),
        ),
    )
    return pl.pallas_call(
        functools.partial(_regroup_kernel, n_steps=n_steps),
        out_shape=(
            jax.ShapeDtypeStruct((N_CMP_COLS, nb, n_pages * PAGE_SIZE // CMP_STRIDE, chunk_w), BF16),
            jax.ShapeDtypeStruct((n_slc, nb, lk, HEAD_DIM), BF16),
            jax.ShapeDtypeStruct((nb, lk, cache_logf.shape[2]), F32),
        ),
        grid_spec=grid_spec,
        compiler_params=_cparams(("parallel", "arbitrary")),
        name="cache_regroup",
    )(page_table, *([cache_nsa] * npg), *([cache_logf] * npg))


FOX_DEC_PAGES = 8


def _fox_decode_kernel(pt_ref, *refs, dseq):
    del pt_ref
    npg = FOX_DEC_PAGES
    pages = refs[:npg]
    qbd_ref, ccol_ref, cq_ref, knew_ref, vnew_ref, o_ref, m_ref, acc_ref = refs[npg:]
    step = pl.program_id(1)
    is_tail = step == pl.num_programs(1) - 1
    width = FOX_HEADS * HEAD_DIM

    @pl.when(step == 0)
    def _():
        m_ref[...] = jnp.full(m_ref.shape, NEG, F32)
        acc_ref[...] = jnp.zeros(acc_ref.shape, F32)

    def fold(k_blk, v_blk, bias):
        n = k_blk.shape[0]
        s_t = jnp.dot(k_blk, qbd_ref[0], preferred_element_type=F32) + bias
        m_new = jnp.maximum(m_ref[...], jnp.max(s_t, axis=0, keepdims=True))
        alpha = jnp.exp(m_ref[...] - m_new)
        p = jnp.exp(s_t - m_new)
        p_t = jnp.concatenate([p[i * LANE:(i + 1) * LANE].T for i in range(n // LANE)], axis=1).astype(BF16)
        v_aug = jnp.concatenate([v_blk, jnp.ones((n, LANE), BF16)], axis=1)
        upd = jnp.dot(p_t, v_aug, preferred_element_type=F32)
        alpha_col = jnp.broadcast_to(alpha, (LANE, LANE)).T
        acc_ref[...] = acc_ref[...] * jnp.concatenate([alpha_col] * (width // LANE + 1), axis=1) + upd
        m_ref[...] = m_new

    @pl.when(jnp.logical_not(is_tail))
    def _():
        def heads_of(p, first):
            cols = [pages[p][0, pl.ds(first + h, PAGE_SIZE, stride=N_CACHE_COLS), :] for h in range(FOX_HEADS)]
            return jnp.concatenate(cols, axis=1).astype(BF16)

        k_blk = jnp.concatenate([heads_of(p, 0) for p in range(npg)], axis=0)
        v_blk = jnp.concatenate([heads_of(p, FOX_HEADS) for p in range(npg)], axis=0)
        fold(k_blk, v_blk, cq_ref[0] - ccol_ref[0])

    @pl.when(is_tail)
    def _():
        row = lax.broadcasted_iota(jnp.int32, (LANE, LANE), 0)
        t_of_col = lax.broadcasted_iota(jnp.int32, (LANE, LANE), 1) % dseq
        bias = jnp.where(row <= t_of_col, cq_ref[0] - ccol_ref[0, :LANE, :], NEG)
        fold(knew_ref[0], vnew_ref[0], bias)
        acc = acc_ref[...]
        inv = 1.0 / acc[:, width:]
        o_ref[0] = acc[:, :width] * jnp.concatenate([inv] * (width // LANE), axis=1)


def _fox_decode(page_table, cache_fox, qbd, ccols, cq, knew, vnew, dseq):
    nb, n_pages = page_table.shape
    npg = FOX_DEC_PAGES
    n_steps = n_pages // npg
    last = n_steps - 1
    width = FOX_HEADS * HEAD_DIM

    def page_map(p):
        return lambda b, s, pt: (pt[b, jnp.minimum(s, last) * npg + p], 0, 0)

    per_b = lambda shape: pl.BlockSpec((1,) + shape, lambda b, s, pt: (b, 0, 0))
    grid_spec = pltpu.PrefetchScalarGridSpec(
        num_scalar_prefetch=1,
        grid=(nb, n_steps + 1),
        in_specs=[pl.BlockSpec((1,) + cache_fox.shape[1:], page_map(p)) for p in range(npg)] + [
            per_b((width, LANE)),
            pl.BlockSpec((1, npg * PAGE_SIZE, LANE), lambda b, s, pt: (b, s, 0)),
            per_b((1, LANE)),
            per_b((LANE, width)),
            per_b((LANE, width)),
        ],
        out_specs=per_b((LANE, width)),
        scratch_shapes=[pltpu.VMEM((1, LANE), F32), pltpu.VMEM((LANE, width + LANE), F32)],
    )
    return pl.pallas_call(
        functools.partial(_fox_decode_kernel, dseq=dseq),
        out_shape=jax.ShapeDtypeStruct((nb, LANE, width), F32),
        grid_spec=grid_spec,
        compiler_params=_cparams(("parallel", "arbitrary")),
        name="fox_decode",
    )(page_table, *([cache_fox] * npg), qbd, ccols, cq, knew, vnew)


NSA_DEC_KEYS = 2048


def _lane_transpose(p):
    return jnp.concatenate([p[i * LANE:(i + 1) * LANE].T for i in range(p.shape[0] // LANE)], axis=1).astype(BF16)


def _nsa_decode_kernel(qbd_ref, kc_ref, vc_ref, mselt_ref, pair_ref, tcol_ref, ks_ref, vs_ref, kw_ref, vw_ref,
                       gate_ref, o_ref, drop_ref, m_ref, acc_ref, ocmp_ref, *, dseq, past, n_sel):
    step = pl.program_id(1)
    n_tiles = pl.num_programs(1)
    width = NSA_KV_HEADS * HEAD_DIM
    qbd = qbd_ref[0]
    col = lax.broadcasted_iota(jnp.int32, (1, LANE), 1)
    q_pos = past + col % dseq

    def side_by_side(ref, rows=None):
        parts = [ref[g, 0] if rows is None else ref[g, 0, rows, :] for g in range(NSA_KV_HEADS)]
        return jnp.concatenate(parts, axis=1)

    def rel_bias_cols(dist):
        out = jnp.broadcast_to(tcol_ref[0:1, :], dist.shape)
        for k, thr in enumerate(BUCKET_THR, start=1):
            out = jnp.where(dist >= thr, tcol_ref[k:k + 1, :], out)
        return out

    def softmax_pv(s_t, v_all):
        m = jnp.max(s_t, axis=0, keepdims=True)
        e = jnp.exp(s_t - m)
        inv = jnp.where(m > 0.5 * NEG, 1.0 / jnp.sum(e, axis=0, keepdims=True), 0.0)
        p = e * inv
        return p, jnp.dot(_lane_transpose(p), v_all, preferred_element_type=F32)

    @pl.when(step == 0)
    def _():
        m_ref[...] = jnp.full(m_ref.shape, NEG, F32)
        acc_ref[...] = jnp.zeros(acc_ref.shape, F32)
        ncp = kc_ref.shape[2]
        blk_end = lax.broadcasted_iota(jnp.int32, (ncp, LANE), 0) * CMP_STRIDE + (CMP_LEN - 1)
        d = q_pos - blk_end
        s_t = jnp.dot(side_by_side(kc_ref), qbd, preferred_element_type=F32)
        s_t = jnp.where(d >= 0, s_t + rel_bias_cols(jnp.maximum(d, 0)), NEG)
        p, ocmp_ref[...] = softmax_pv(s_t, side_by_side(vc_ref))
        imp = jnp.dot(mselt_ref[...], p, preferred_element_type=F32, precision=lax.Precision.HIGHEST)
        imp = jnp.dot(imp, pair_ref[...], preferred_element_type=F32, precision=lax.Precision.HIGHEST)
        nj = imp.shape[0]
        jrow = lax.broadcasted_iota(jnp.int32, (nj, LANE), 0)
        blk_q = jnp.right_shift(q_pos, int(math.log2(SEL_BLOCK)))
        forced = (jrow == 0) | (jrow == blk_q) | (jrow == blk_q - 1)
        score = jnp.where(forced, FORCED_SCORE, jnp.where(jrow <= blk_q, imp, -1.0))
        score = jnp.where(jrow < n_sel, score, -2.0)
        ranks = []
        for r in range(-(-n_sel // 8)):
            mine = score[r * 8:(r + 1) * 8]
            jmine = jrow[r * 8:(r + 1) * 8]
            rank = jnp.zeros((8, LANE), F32)
            for other in range(n_sel):
                row = jnp.broadcast_to(score[other:other + 1], (8, LANE))
                if other < r * 8:
                    beats = row >= mine
                elif other >= (r + 1) * 8:
                    beats = row > mine
                else:
                    beats = (row > mine) | ((row == mine) & (jmine > other))
                rank = rank + jnp.where(beats, 1.0, 0.0)
            ranks.append(rank)
        if nj > 8 * len(ranks):
            ranks.append(jnp.full((nj - 8 * len(ranks), LANE), float(SEL_TOPK), F32))
        rank = jnp.concatenate(ranks, axis=0)
        far = tcol_ref[REL_BUCKETS - 1:REL_BUCKETS, :]
        drop_ref[...] = jnp.where((rank < SEL_TOPK) & (jrow <= blk_q), far, NEG)

    tile = ks_ref.shape[2]
    per_tile = tile // SEL_BLOCK
    start = step * tile
    first_blk = pl.multiple_of(step * per_tile, 8)
    rows = [jnp.broadcast_to(drop_ref[pl.ds(first_blk + i, 1), :], (SEL_BLOCK, LANE)) for i in range(per_tile)]
    s_t = jnp.dot(side_by_side(ks_ref), qbd, preferred_element_type=F32) + jnp.concatenate(rows, axis=0)

    def near_fix(s_t):
        def fix(rows_at, s_rows):
            key_pos = start + rows_at + lax.broadcasted_iota(jnp.int32, (LANE, LANE), 0)
            d = q_pos - key_pos
            far = tcol_ref[REL_BUCKETS - 1:REL_BUCKETS, :]
            return jnp.where(d >= 0, s_rows + (rel_bias_cols(jnp.maximum(d, 0)) - far), NEG)

        head = fix(0, s_t[:LANE])
        tail = fix(tile - LANE, s_t[tile - LANE:])
        return jnp.concatenate([head, s_t[LANE:tile - LANE], tail], axis=0)

    is_near = (start + tile > past - FAR_DIST)
    s_t = lax.cond(is_near, near_fix, lambda s: s, s_t)
    m_new = jnp.maximum(m_ref[...], jnp.max(s_t, axis=0, keepdims=True))
    alpha = jnp.exp(m_ref[...] - m_new)
    p = jnp.exp(s_t - m_new)
    v_aug = jnp.concatenate([side_by_side(vs_ref), jnp.ones((tile, LANE), BF16)], axis=1)
    upd = jnp.dot(_lane_transpose(p), v_aug, preferred_element_type=F32)
    alpha_col = jnp.broadcast_to(alpha, (LANE, LANE)).T
    acc_ref[...] = acc_ref[...] * jnp.concatenate([alpha_col] * (width // LANE + 1), axis=1) + upd
    m_ref[...] = m_new

    @pl.when(step == n_tiles - 1)
    def _():
        acc = acc_ref[...]
        o_slc = acc[:, :width] * jnp.concatenate([1.0 / acc[:, width:]] * (width // LANE), axis=1)
        span = kw_ref.shape[2]
        key_pos = (past - WINDOW) + lax.broadcasted_iota(jnp.int32, (span, LANE), 0)
        d = q_pos - key_pos
        s_w = jnp.dot(side_by_side(kw_ref), qbd, preferred_element_type=F32)
        s_w = jnp.where((d >= 0) & (d < WINDOW), s_w + rel_bias_cols(jnp.maximum(d, 0)), NEG)
        _, o_win = softmax_pv(s_w, side_by_side(vw_ref))
        g = gate_ref[0]
        tile4 = lambda a: jnp.concatenate([a] * (width // LANE), axis=1)
        o_ref[0] = tile4(g[0]) * ocmp_ref[...] + tile4(g[1]) * o_slc + tile4(g[2]) * o_win


def _nsa_decode(qbd, kcv, mselt, pair, tcols, nsa_dec, win_dec, gcols, *, dseq, past, n_sel):
    nb = qbd.shape[0]
    ncp = kcv.shape[2]
    nj = mselt.shape[0]
    span = win_dec.shape[2]
    tile = NSA_DEC_KEYS
    n_tiles = -(-nsa_dec.shape[2] // tile)
    width = NSA_KV_HEADS * HEAD_DIM
    kern = functools.partial(_nsa_decode_kernel, dseq=dseq, past=past, n_sel=n_sel)
    grp = lambda rows, half, tiled: pl.BlockSpec(
        (NSA_KV_HEADS, 1, rows, HEAD_DIM), (lambda b, s: (half, b, s, 0)) if tiled else (lambda b, s: (half, b, 0, 0)))
    const = lambda shape: pl.BlockSpec(shape, lambda b, s: (0,) * len(shape))
    return pl.pallas_call(
        kern,
        out_shape=jax.ShapeDtypeStruct((nb, LANE, width), F32),
        grid=(nb, n_tiles),
        in_specs=[
            pl.BlockSpec((1, width, LANE), lambda b, s: (b, 0, 0)),
            grp(ncp, 0, False), grp(ncp, 1, False),
            const((nj, ncp)), const((LANE, LANE)), const((REL_BUCKETS, LANE)),
            grp(tile, 0, True), grp(tile, 1, True),
            grp(span, 0, False), grp(span, 1, False),
            pl.BlockSpec((1, 3, LANE, LANE), lambda b, s: (b, 0, 0, 0)),
        ],
        out_specs=pl.BlockSpec((1, LANE, width), lambda b, s: (b, 0, 0)),
        scratch_shapes=[
            pltpu.VMEM((nj, LANE), F32),
            pltpu.VMEM((1, LANE), F32),
            pltpu.VMEM((LANE, width + LANE), F32),
            pltpu.VMEM((LANE, width), F32),
        ],
        compiler_params=_cparams(("parallel", "arbitrary")),
        name="nsa_decode",
    )(qbd, kcv, kcv, mselt, pair, tcols, nsa_dec, nsa_dec, win_dec, win_dec, gcols)


def _largest_tile(n, candidates):
    for c in candidates:
        if n % c == 0:
            return c
    raise ValueError(f"no tile in {candidates} divides {n}")


FFN_ROW_TILES = (1024, 512, 256, 128)
ROW_TILES = (512, 256, 128)


def _token_stage_in(x, p):
    m = x.shape[0]
    tm = m if m < ROW_TILES[-1] else _largest_tile(m, ROW_TILES)
    tm_ffn = m if m < FFN_ROW_TILES[-1] else _largest_tile(m, FFN_ROW_TILES)
    x1 = _ffn(x, p["norm_ffn1"], p["wg1"], p["wu1"], p["wd1"], tm_ffn, p["tf"])
    tm_in = m if m < ROW_TILES[-1] else _largest_tile(m, ROW_TILES[1:])
    nsa_rows, win_rows, fox_rows, hm = _inproj(x1, p["norm_mix"], p["w_main"], p["colgain"], tm_in)
    small = _small(x1, p["norm_mix"], p["w_small"], p["b_small"], tm)
    return x1, nsa_rows, win_rows, fox_rows, hm, small


def _token_stage_out(x1, o_nsa, o_fox, p):
    m = x1.shape[0]
    tm = m if m < ROW_TILES[-1] else _largest_tile(m, ROW_TILES[1:])
    tm_ffn = m if m < FFN_ROW_TILES[-1] else _largest_tile(m, FFN_ROW_TILES)
    x2 = _outproj(o_nsa, o_fox, p["out_norm_nsa"], p["out_norm_fox"], p["w_out"], x1, tm)
    return _ffn(x2, p["norm_ffn2"], p["wg2"], p["wu2"], p["wd2"], tm_ffn, p["tf"])


def kernel(x_prompt, x_sample, cache_nsa_kv, cache_fox_kv, cache_fox_logf, state_win_kv, page_table, rel_table, norm_ffn1, ffn1_gate, ffn1_up, ffn1_down, norm_mix, w_in, nsa_gate_bias, fox_forget_bias, q_norm_nsa, k_norm_nsa, q_norm_fox, k_norm_fox, cmp_pos_k, cmp_w1_k, cmp_w2_k, cmp_pos_v, cmp_w1_v, cmp_w2_v, out_norm_nsa, out_norm_fox, w_out, norm_ffn2, ffn2_gate, ffn2_up, ffn2_down):
    depth = w_in.shape[0]
    assert depth == 1, "single-layer trunk"
    nbp, seq, d_model = x_prompt.shape
    nbd, dseq, _ = x_sample.shape
    n_pages = page_table.shape[1]
    past = n_pages * PAGE_SIZE
    d_ff = ffn1_gate.shape[2]
    nsa_w = NSA_HEADS * HEAD_DIM
    kv6_w = 6 * NSA_KV_HEADS * HEAD_DIM
    fox_w = 3 * FOX_HEADS * HEAD_DIM
    off_gate = nsa_w + kv6_w
    off_fox = off_gate + N_GATE_COLS
    off_forget = off_fox + fox_w
    assert w_in.shape[2] == off_forget + FOX_HEADS and d_model == nsa_w + FOX_HEADS * HEAD_DIM
    assert seq % LANE == 0 and seq >= WINDOW and past % LANE == 0 and n_pages % REGROUP_PAGES == 0
    assert dseq <= 16 and state_win_kv.shape[2] == WINDOW
    assert seq % FOX_KEY_BLOCK == 0 and seq % (SLC_BLOCK_TILES * NSA_TQ) == 0 and seq >= WIN_TILES * NSA_TQ

    w0 = w_in[0]
    ones = lambda n: jnp.ones((n,), F32)
    zeros = lambda n: jnp.zeros((n,), F32)
    kn, kvw = NSA_KV_HEADS, NSA_KV_HEADS * HEAD_DIM
    qk_scale = HEAD_DIM ** -0.5
    p = {
        "tf": _largest_tile(d_ff, (512, 256, 128)),
        "norm_ffn1": norm_ffn1[0][None], "norm_mix": norm_mix[0][None], "norm_ffn2": norm_ffn2[0][None],
        "wg1": ffn1_gate[0].astype(BF16), "wu1": ffn1_up[0].astype(BF16), "wd1": ffn1_down[0].astype(BF16),
        "wg2": ffn2_gate[0].astype(BF16), "wu2": ffn2_up[0].astype(BF16), "wd2": ffn2_down[0].astype(BF16),
        "w_main": jnp.concatenate([w0[:, :off_gate], w0[:, off_fox:off_forget]], axis=1).astype(BF16),
        "w_small": jnp.concatenate([w0[:, off_gate:off_fox], w0[:, off_forget:],
                                    jnp.zeros((d_model, LANE - N_GATE_COLS - FOX_HEADS), F32)], axis=1).astype(BF16),
        "b_small": jnp.concatenate([nsa_gate_bias[0].reshape(-1), fox_forget_bias[0],
                                    zeros(LANE - N_GATE_COLS - FOX_HEADS)])[None],
        "colgain": jnp.concatenate([
            jnp.tile(q_norm_nsa[0] * qk_scale, NSA_HEADS), ones(2 * kvw), jnp.tile(k_norm_nsa[0], kn), ones(kvw),
            jnp.tile(k_norm_nsa[0], kn), ones(kvw), jnp.tile(q_norm_fox[0] * qk_scale, FOX_HEADS),
            jnp.tile(k_norm_fox[0], FOX_HEADS), ones(FOX_HEADS * HEAD_DIM)])[None],
        "out_norm_nsa": out_norm_nsa[0][None], "out_norm_fox": out_norm_fox[0][None],
        "w_out": w_out[0].astype(BF16),
    }
    half = CMP_STRIDE * HEAD_DIM

    def cmp_w1(w):
        return jnp.concatenate([w[0, :half], w[0, half:]], axis=1)

    def cmp_pe(pe):
        return jnp.concatenate([pe[0].reshape(CMP_LEN // CMP_STRIDE, half), jnp.zeros((PE_ROWS - CMP_LEN // CMP_STRIDE, half), F32)], axis=0)

    w1cat = jnp.stack([cmp_w1(cmp_w1_k), cmp_w1(cmp_w1_v)]).astype(BF16)
    w2cat = jnp.stack([cmp_w2_k[0], cmp_w2_v[0]]).astype(BF16)
    pecat = jnp.stack([cmp_pe(cmp_pos_k), cmp_pe(cmp_pos_v)]).astype(BF16)
    k_norm_row = k_norm_nsa[0][None]

    mp = nbp * seq
    x1, nsa_rows, win_rows, fox_rows, hm, small = _token_stage_in(x_prompt.reshape(mp, d_model), p)
    hm4 = hm.reshape(N_HEAD_COLS, nbp, seq, HEAD_DIM)

    logf = small[:, N_GATE_COLS:N_GATE_COLS + FOX_HEADS]
    csum = _cumsum(logf.reshape(nbp, seq, FOX_HEADS).transpose(0, 2, 1))
    o_fox = _fox_prompt(hm, hm4, csum[:, :, None, :], nbp, seq)

    n_chunk = seq // CMP_STRIDE
    n_cmp = (seq - CMP_LEN) // CMP_STRIDE + 1
    n_sel = -(-seq // SEL_BLOCK)
    xc = _chunkify(nsa_rows.reshape(nbp, seq * N_CACHE_COLS, HEAD_DIM), N_CACHE_COLS, N_CMP_COLS, seq)
    kcv = _compress(xc, 0, w1cat, w2cat, pecat, k_norm_row, n_chunk)
    mselt = _cmp_to_sel(n_cmp, n_sel, n_chunk, -(-n_sel // 8) * 8).T
    nq = seq // NSA_TQ
    gates_t = small[:, :N_GATE_COLS].reshape(nbp, nq, NSA_TQ, NSA_KV_HEADS, NSA_GROUP, 3)
    gates_t = gates_t.transpose(0, 3, 1, 5, 4, 2).reshape(nbp, NSA_KV_HEADS, nq, 3, NSA_COLS)
    gates_t = jnp.pad(gates_t, ((0, 0), (0, 0), (0, 0), (0, 8 - 3), (0, 0)))
    o_nsa = _nsa_prompt(rel_table, hm, hm4, kcv, mselt, gates_t, nbp, seq, n_sel)
    y_p = _token_stage_out(x1, o_nsa, o_fox, p)

    ms = nbd * dseq
    lk = past + NSA_DEC_KEYS
    assert n_pages % FOX_DEC_PAGES == 0 and past % NSA_DEC_KEYS == 0 and FOX_HEADS * dseq <= LANE
    xs1, nsa_rows_s, win_rows_s, fox_rows_s, hm_s, small_s = _token_stage_in(x_sample.reshape(ms, d_model), p)
    xc_d, nsa_dec, lf_dec = _regroup(
        page_table,
        cache_nsa_kv.reshape(cache_nsa_kv.shape[1], PAGE_SIZE * N_CACHE_COLS, HEAD_DIM),
        cache_fox_logf[0], lk)
    hm_s4 = hm_s.reshape(N_HEAD_COLS, nbd, dseq, HEAD_DIM)
    nsa_dec = lax.dynamic_update_slice(nsa_dec, hm_s4[HM_K_SLC:HM_K_WIN], (0, 0, past, 0))
    logf_s = small_s[:, N_GATE_COLS:N_GATE_COLS + FOX_HEADS].reshape(nbd, dseq, FOX_HEADS)
    lf_dec = lax.dynamic_update_slice(lf_dec, logf_s, (0, past, 0))

    csum_d = _cumsum(lf_dec.transpose(0, 2, 1))
    n_cols = FOX_HEADS * dseq
    lane_pad = lambda a: jnp.pad(a, [(0, 0)] * (a.ndim - 1) + [(0, LANE - n_cols)])
    head_eye = jnp.eye(FOX_HEADS, dtype=BF16)
    qbd = jnp.einsum("hbtd,hg->bhdgt", hm_s4[HM_Q_FOX:HM_Q_FOX + FOX_HEADS], head_eye)
    qbd = lane_pad(qbd.reshape(nbd, FOX_HEADS * HEAD_DIM, n_cols))
    ccols = lane_pad(jnp.repeat(csum_d.transpose(0, 2, 1), dseq, axis=2))
    cq = lane_pad(csum_d[:, :, past:past + dseq].reshape(nbd, 1, n_cols))

    def new_rows(head0):
        rows = hm_s4[head0:head0 + FOX_HEADS].transpose(1, 2, 0, 3).reshape(nbd, dseq, FOX_HEADS * HEAD_DIM)
        return jnp.pad(rows, ((0, 0), (0, LANE - dseq), (0, 0)))

    o_full = _fox_decode(page_table, cache_fox_kv.reshape(cache_fox_kv.shape[1], PAGE_SIZE * N_CACHE_COLS, HEAD_DIM),
                         qbd, ccols, cq, new_rows(HM_K_FOX), new_rows(HM_V_FOX), dseq)
    o_fox_s = jnp.concatenate([o_full[:, h * dseq:(h + 1) * dseq, h * HEAD_DIM:(h + 1) * HEAD_DIM]
                               for h in range(FOX_HEADS)], axis=2).reshape(ms, FOX_HEADS * HEAD_DIM)

    n_chunk_d = past // CMP_STRIDE
    n_cmp_d = (past + dseq - CMP_LEN) // CMP_STRIDE + 1
    n_sel_d = -(-(past + dseq) // SEL_BLOCK)
    assert n_cmp_d + CMP_LEN // CMP_STRIDE - 1 <= n_chunk_d, "compressed blocks must lie in the cached rows"
    kcv_d = _compress(xc_d, 0, w1cat, w2cat, pecat, k_norm_row, n_chunk_d)
    mselt_d = _cmp_to_sel(n_cmp_d, n_sel_d, n_chunk_d, lk // SEL_BLOCK).T
    win_old = state_win_kv[0].transpose(2, 3, 0, 1, 4).reshape(2 * NSA_KV_HEADS, nbd, WINDOW, HEAD_DIM).astype(BF16)
    win_dec = jnp.concatenate([win_old, hm_s4[HM_K_WIN:HM_V_WIN + NSA_KV_HEADS],
                               jnp.zeros((2 * NSA_KV_HEADS, nbd, LANE - dseq, HEAD_DIM), BF16)], axis=2)
    grp_eye = jnp.eye(NSA_KV_HEADS, dtype=BF16)
    q_grp = hm_s4[HM_Q_NSA:HM_Q_NSA + NSA_HEADS].reshape(NSA_KV_HEADS, NSA_GROUP, nbd, dseq, HEAD_DIM)
    qbd_n = jnp.einsum("gjbtd,gk->bgdkjt", q_grp, grp_eye)
    qbd_n = lane_pad(qbd_n.reshape(nbd, NSA_KV_HEADS * HEAD_DIM, n_cols))
    tcols = lane_pad(jnp.repeat(rel_table, dseq, axis=1))
    col_id = np.arange(n_cols)
    same = ((col_id[:, None] // (NSA_GROUP * dseq) == col_id[None, :] // (NSA_GROUP * dseq))
            & (col_id[:, None] % dseq == col_id[None, :] % dseq))
    pair = jnp.asarray(np.pad(same.astype(np.float32), ((0, LANE - n_cols), (0, LANE - n_cols))))
    gcols = small_s[:, :N_GATE_COLS].reshape(nbd, dseq, NSA_HEADS, 3).transpose(0, 3, 2, 1).reshape(nbd, 3, n_cols)
    gcols = jnp.broadcast_to(lane_pad(gcols)[..., None], (nbd, 3, LANE, LANE))
    o_full_n = _nsa_decode(qbd_n, kcv_d, mselt_d, pair, tcols, nsa_dec, win_dec, gcols,
                           dseq=dseq, past=past, n_sel=n_sel_d)
    o_nsa_s = jnp.concatenate(
        [o_full_n[:, h * dseq:(h + 1) * dseq, (h // NSA_GROUP) * HEAD_DIM:(h // NSA_GROUP + 1) * HEAD_DIM]
         for h in range(NSA_HEADS)], axis=2).reshape(ms, NSA_HEADS * HEAD_DIM)
    y_s = _token_stage_out(xs1, o_nsa_s, o_fox_s, p)

    kvh = (NSA_KV_HEADS, HEAD_DIM)
    win_keep = min(WINDOW, seq)
    win_p = win_rows.reshape(nbp, seq, 2, *kvh)[:, seq - win_keep:]
    win_s = jnp.concatenate([state_win_kv[0], win_rows_s.reshape(nbd, dseq, 2, *kvh)], axis=1)[:, dseq:]
    return (
        y_p.reshape(nbp, seq, d_model),
        y_s.reshape(nbd, dseq, d_model),
        nsa_rows.reshape(1, nbp, seq, 4, *kvh),
        fox_rows.reshape(1, nbp, seq, 2, FOX_HEADS, HEAD_DIM),
        logf.reshape(1, nbp, seq, FOX_HEADS),
        win_p[None],
        nsa_rows_s.reshape(1, nbd, dseq, 4, *kvh),
        fox_rows_s.reshape(1, nbd, dseq, 2, FOX_HEADS, HEAD_DIM),
        logf_s[None],
        win_s[None],
    )
```

```python
import functools
import math

import numpy as np
import jax
import jax.numpy as jnp
from jax import lax
from jax.experimental import pallas as pl
from jax.experimental.pallas import tpu as pltpu

HEAD_DIM = 128
NSA_HEADS = 8
FOX_HEADS = 8
NSA_KV_HEADS = 4
NSA_GROUP = NSA_HEADS // NSA_KV_HEADS
CMP_LEN = 32
CMP_STRIDE = 16
CMP_HIDDEN = 512
SEL_BLOCK = 64
SEL_TOPK = 16
WINDOW = 512
REL_BUCKETS = 32
REL_MAX_DIST = 128
RMS_EPS = 1e-6
PAGE_SIZE = 128

LANE = 128
NEG = -1e30
FORCED_SCORE = 1e30
VMEM_LIMIT = 56 * 1024 * 1024

BF16 = jnp.bfloat16
F32 = jnp.float32


def _bucket_thresholds():
    n = np.arange(0, 4 * REL_MAX_DIST)
    max_exact = REL_BUCKETS // 2
    nf = np.maximum(n, 1).astype(np.float32)
    large = max_exact + (np.log(nf / max_exact) / math.log(REL_MAX_DIST / max_exact)
                         * (REL_BUCKETS - max_exact)).astype(np.int32)
    bucket = np.where(n < max_exact, n, np.minimum(large, REL_BUCKETS - 1))
    return [int(np.min(n[bucket >= k])) for k in range(1, REL_BUCKETS)]


BUCKET_THR = _bucket_thresholds()
FAR_DIST = BUCKET_THR[-1]


def _cparams(sem):
    return pltpu.CompilerParams(dimension_semantics=sem, vmem_limit_bytes=VMEM_LIMIT)


def _rms_rows(x, gain):
    ms = jnp.mean(x * x, axis=-1, keepdims=True)
    return x * lax.rsqrt(ms + RMS_EPS) * gain


def _ffn_kernel(x_ref, g_ref, wg_ref, wu_ref, wd_ref, o_ref, xn_ref):
    @pl.when(pl.program_id(1) == 0)
    def _():
        x = x_ref[...]
        xn_ref[...] = _rms_rows(x, g_ref[...]).astype(BF16)
        o_ref[...] = x

    xn = xn_ref[...]
    a = jnp.dot(xn, wg_ref[...], preferred_element_type=F32)
    u = jnp.dot(xn, wu_ref[...], preferred_element_type=F32)
    h = (a / (1.0 + jnp.exp(-a))) * u * 0.5
    o_ref[...] += jnp.dot(h.astype(BF16), wd_ref[...], preferred_element_type=F32)


def _ffn(x, gain, wg, wu, wd, tm, tf):
    m, d = x.shape
    f = wg.shape[1]
    return pl.pallas_call(
        _ffn_kernel,
        out_shape=jax.ShapeDtypeStruct((m, d), F32),
        grid=(m // tm, f // tf),
        in_specs=[
            pl.BlockSpec((tm, d), lambda i, j: (i, 0)),
            pl.BlockSpec((1, d), lambda i, j: (0, 0)),
            pl.BlockSpec((d, tf), lambda i, j: (0, j)),
            pl.BlockSpec((d, tf), lambda i, j: (0, j)),
            pl.BlockSpec((tf, d), lambda i, j: (j, 0)),
        ],
        out_specs=pl.BlockSpec((tm, d), lambda i, j: (i, 0)),
        scratch_shapes=[pltpu.VMEM((tm, d), BF16)],
        compiler_params=_cparams(("parallel", "arbitrary")),
        name="ffn",
    )(x, gain, wg, wu, wd)


IN_TN = 4 * HEAD_DIM
J_NSA = (2, 6)
J_WIN = (6, 8)
J_FOX = (10, 14)
N_HEAD_COLS = 56
HM_Q_NSA, HM_K_CMP, HM_K_SLC, HM_V_SLC, HM_K_WIN, HM_V_WIN = 0, 8, 16, 20, 24, 28
HM_Q_FOX, HM_K_FOX, HM_V_FOX = 32, 40, 48


IN_NORMED_TILES = (0, 1, 4, 6, 8, 9, 10, 11)


def _inproj_kernel(x_ref, g_ref, w_ref, cg_ref, ones_ref, nsa_ref, win_ref, fox_ref, hm_ref):
    xn = _rms_rows(x_ref[...], g_ref[...]).astype(BF16)
    tm = xn.shape[0]
    heads = IN_TN // HEAD_DIM
    for j in range(w_ref.shape[1] // IN_TN):
        cols = slice(j * IN_TN, (j + 1) * IN_TN)
        vals = jnp.dot(xn, w_ref[:, cols], preferred_element_type=F32)
        if j in IN_NORMED_TILES:
            sumsq = jnp.dot((vals * vals).astype(BF16), ones_ref[...], preferred_element_type=F32)
            vals = vals * lax.rsqrt(sumsq * (1.0 / HEAD_DIM) + RMS_EPS) * cg_ref[:, cols]
        for hh in range(heads):
            hm_ref[j * heads + hh] = vals[:, hh * HEAD_DIM:(hh + 1) * HEAD_DIM].astype(BF16)
        for ref, (jlo, jhi) in ((nsa_ref, J_NSA), (win_ref, J_WIN), (fox_ref, J_FOX)):
            if jlo <= j < jhi:
                n_cols = (jhi - jlo) * heads
                for hh in range(heads):
                    ref[pl.ds((j - jlo) * heads + hh, tm, stride=n_cols), :] = vals[:, hh * HEAD_DIM:(hh + 1) * HEAD_DIM]


def _inproj(x, gain, w_main, colgain, tm):
    m, d = x.shape
    ncol = w_main.shape[1]
    heads = IN_TN // HEAD_DIM
    n_nsa, n_win, n_fox = [(hi - lo) * heads for lo, hi in (J_NSA, J_WIN, J_FOX)]
    head_ones = jnp.asarray(np.kron(np.eye(heads, dtype=np.float32), np.ones((HEAD_DIM, HEAD_DIM), np.float32)), BF16)
    resident = lambda shape: pl.BlockSpec(shape, lambda i: (0, 0), pipeline_mode=pl.Buffered(1))
    return pl.pallas_call(
        _inproj_kernel,
        out_shape=(
            jax.ShapeDtypeStruct((m * n_nsa, HEAD_DIM), F32),
            jax.ShapeDtypeStruct((m * n_win, HEAD_DIM), F32),
            jax.ShapeDtypeStruct((m * n_fox, HEAD_DIM), F32),
            jax.ShapeDtypeStruct((N_HEAD_COLS, m, HEAD_DIM), BF16),
        ),
        grid=(m // tm,),
        in_specs=[
            pl.BlockSpec((tm, d), lambda i: (i, 0)),
            resident((1, d)),
            resident((d, ncol)),
            resident((1, ncol)),
            resident((IN_TN, IN_TN)),
        ],
        out_specs=(
            pl.BlockSpec((tm * n_nsa, HEAD_DIM), lambda i: (i, 0)),
            pl.BlockSpec((tm * n_win, HEAD_DIM), lambda i: (i, 0)),
            pl.BlockSpec((tm * n_fox, HEAD_DIM), lambda i: (i, 0)),
            pl.BlockSpec((N_HEAD_COLS, tm, HEAD_DIM), lambda i: (0, i, 0)),
        ),
        compiler_params=_cparams(("parallel",)),
        name="inproj",
    )(x, gain, w_main, colgain, head_ones)


N_GATE_COLS = 3 * NSA_HEADS


def _small_kernel(x_ref, g_ref, w_ref, b_ref, o_ref):
    xn = _rms_rows(x_ref[...], g_ref[...]).astype(BF16)
    z = jnp.dot(xn, w_ref[...], preferred_element_type=F32) + b_ref[...]
    lane = lax.broadcasted_iota(jnp.int32, z.shape, 1)
    sig = 1.0 / (1.0 + jnp.exp(-z))
    logsig = jnp.minimum(z, 0.0) - jnp.log(1.0 + jnp.exp(-jnp.abs(z)))
    o_ref[...] = jnp.where(lane < N_GATE_COLS, sig,
                           jnp.where(lane < N_GATE_COLS + FOX_HEADS, logsig, 0.0))


def _small(x, gain, w_small, b_small, tm):
    m, d = x.shape
    return pl.pallas_call(
        _small_kernel,
        out_shape=jax.ShapeDtypeStruct((m, LANE), F32),
        grid=(m // tm,),
        in_specs=[
            pl.BlockSpec((tm, d), lambda i: (i, 0)),
            pl.BlockSpec((1, d), lambda i: (0, 0)),
            pl.BlockSpec((d, LANE), lambda i: (0, 0)),
            pl.BlockSpec((1, LANE), lambda i: (0, 0)),
        ],
        out_specs=pl.BlockSpec((tm, LANE), lambda i: (i, 0)),
        compiler_params=_cparams(("parallel",)),
        name="gates",
    )(x, gain, w_small, b_small)


def _outproj_kernel(on_ref, of_ref, gn_ref, gf_ref, w_ref, x_ref, y_ref):
    a = _rms_rows(on_ref[...], gn_ref[...]).astype(BF16)
    b = _rms_rows(of_ref[...], gf_ref[...]).astype(BF16)
    half = a.shape[1]
    y = jnp.dot(a, w_ref[:half, :], preferred_element_type=F32)
    y = y + jnp.dot(b, w_ref[half:, :], preferred_element_type=F32)
    y_ref[...] = x_ref[...] + y


def _outproj(o_nsa, o_fox, g_nsa, g_fox, w_out, x, tm):
    m, d = x.shape
    wn = o_nsa.shape[1]
    wf = o_fox.shape[1]
    return pl.pallas_call(
        _outproj_kernel,
        out_shape=jax.ShapeDtypeStruct((m, d), F32),
        grid=(m // tm,),
        in_specs=[
            pl.BlockSpec((tm, wn), lambda i: (i, 0)),
            pl.BlockSpec((tm, wf), lambda i: (i, 0)),
            pl.BlockSpec((1, wn), lambda i: (0, 0)),
            pl.BlockSpec((1, wf), lambda i: (0, 0)),
            pl.BlockSpec((wn + wf, d), lambda i: (0, 0)),
            pl.BlockSpec((tm, d), lambda i: (i, 0)),
        ],
        out_specs=pl.BlockSpec((tm, d), lambda i: (i, 0)),
        compiler_params=_cparams(("parallel",)),
        name="outproj",
    )(o_nsa, o_fox, g_nsa, g_fox, w_out, x)


CUMSUM_CHUNK = 512


def _cumsum_kernel(x_ref, before_ref, o_ref):
    width = x_ref.shape[2]
    r = lax.broadcasted_iota(jnp.int32, (width, width), 0)
    c = lax.broadcasted_iota(jnp.int32, (width, width), 1)
    upper = (r <= c).astype(F32)
    local = jnp.dot(x_ref[0], upper, preferred_element_type=F32, precision=lax.Precision.HIGHEST)
    totals = jnp.broadcast_to(local[:, width - 1:width], (local.shape[0], LANE))
    offset = jnp.dot(before_ref[...], totals, preferred_element_type=F32, precision=lax.Precision.HIGHEST)
    o_ref[0] = local + jnp.concatenate([offset] * (width // LANE), axis=1)


def _cumsum(x):
    b, h, length = x.shape
    assert length % CUMSUM_CHUNK == 0
    pieces = length // CUMSUM_CHUNK
    used = h * pieces
    rows = -(-used // LANE) * LANE
    idx = np.arange(rows)
    before = ((idx[:, None] // pieces == idx[None, :] // pieces) & (idx[None, :] < idx[:, None])
              & (idx[:, None] < used))
    x = jnp.pad(x.reshape(b, used, CUMSUM_CHUNK), ((0, 0), (0, rows - used), (0, 0)))
    out = pl.pallas_call(
        _cumsum_kernel,
        out_shape=jax.ShapeDtypeStruct((b, rows, CUMSUM_CHUNK), F32),
        grid=(b,),
        in_specs=[pl.BlockSpec((1, rows, CUMSUM_CHUNK), lambda i: (i, 0, 0)),
                  pl.BlockSpec((rows, rows), lambda i: (0, 0))],
        out_specs=pl.BlockSpec((1, rows, CUMSUM_CHUNK), lambda i: (i, 0, 0)),
        compiler_params=_cparams(("parallel",)),
        name="logf_cumsum",
    )(x, jnp.asarray(before, F32))
    return out[:, :used].reshape(b, h, length)


PE_ROWS = 16


def _compress_kernel(x_ref, w1_ref, w2_ref, pe_ref, kn_ref, o_ref):
    kind = pl.program_id(0) // NSA_KV_HEADS
    n = x_ref.shape[2]
    w1 = w1_ref[0]
    pw = jnp.dot(pe_ref[0], w1, preferred_element_type=F32)
    const = pw[0:1, :CMP_HIDDEN] + pw[1:2, CMP_HIDDEN:]
    for i in range(x_ref.shape[1]):
        h = jnp.dot(x_ref[0, i], w1, preferred_element_type=F32)
        hid = h[:, :CMP_HIDDEN] + pltpu.roll(h[:, CMP_HIDDEN:], n - 1, 0) + const
        act = hid / (1.0 + jnp.exp(-hid))
        out = jnp.dot(act.astype(BF16), w2_ref[0], preferred_element_type=F32)
        normed = _rms_rows(out, kn_ref[...])
        o_ref[0, i] = jnp.where(kind == 0, normed, out).astype(BF16)


COMPRESS_SEQS = 2


def _compress(xc, c_off, w1cat, w2, pe, k_norm, n_rows):
    nb = xc.shape[1]
    per = COMPRESS_SEQS if nb % COMPRESS_SEQS == 0 else 1
    return pl.pallas_call(
        _compress_kernel,
        out_shape=jax.ShapeDtypeStruct((2 * NSA_KV_HEADS, nb, n_rows, HEAD_DIM), BF16),
        grid=(2 * NSA_KV_HEADS, nb // per),
        in_specs=[
            pl.BlockSpec((1, per, n_rows, CMP_STRIDE * HEAD_DIM), lambda c, b: (c_off + c, b, 0, 0)),
            pl.BlockSpec((1, CMP_STRIDE * HEAD_DIM, 2 * CMP_HIDDEN), lambda c, b: (c // NSA_KV_HEADS, 0, 0)),
            pl.BlockSpec((1, CMP_HIDDEN, HEAD_DIM), lambda c, b: (c // NSA_KV_HEADS, 0, 0)),
            pl.BlockSpec((1, PE_ROWS, CMP_STRIDE * HEAD_DIM), lambda c, b: (c // NSA_KV_HEADS, 0, 0)),
            pl.BlockSpec((1, HEAD_DIM), lambda c, b: (0, 0)),
        ],
        out_specs=pl.BlockSpec((1, per, n_rows, HEAD_DIM), lambda c, b: (c, b, 0, 0)),
        compiler_params=_cparams(("parallel", "parallel")),
        name="compress",
    )(xc, w1cat, w2, pe, k_norm)


def _rel_bias(dist, tbl_ref, head):
    out = jnp.full(dist.shape, tbl_ref[0, head], F32)
    for k, thr in enumerate(BUCKET_THR, start=1):
        out = jnp.where(dist >= thr, tbl_ref[k, head], out)
    return out


def _cmp_to_sel(n_cmp, n_sel, rows, cols):
    c0 = np.arange(n_cmp)[:, None] * CMP_STRIDE
    s0 = np.arange(n_sel)[None, :] * SEL_BLOCK
    inter = np.clip(np.minimum(c0 + CMP_LEN, s0 + SEL_BLOCK) - np.maximum(c0, s0), 0, None)
    m = np.zeros((rows, cols), np.float32)
    m[:n_cmp, :n_sel] = inter / CMP_LEN
    return jnp.asarray(m)


def _transpose_tiles(x):
    n = x.shape[0] // LANE
    xf = x.astype(F32)
    return jnp.concatenate([xf[i * LANE:(i + 1) * LANE].T for i in range(n)], axis=1)


def _untranspose_tiles(xt):
    n = xt.shape[1] // LANE
    return jnp.concatenate([xt[:, i * LANE:(i + 1) * LANE].T for i in range(n)], axis=0)


AUX_ROWS = 16


def _fill_transposed(dst_ref, src_ref):
    def body(k, carry):
        st = pl.multiple_of(k * LANE, LANE)
        dst_ref[:HEAD_DIM, pl.ds(st, LANE)] = src_ref[0, 0, pl.ds(st, LANE), :].astype(F32).T.astype(BF16)
        return carry

    lax.fori_loop(0, src_ref.shape[2] // LANE, body, 0)


FOX_TQ = 256
FOX_KEY_BLOCK = 1024


def _split3(x):
    hi = x.astype(BF16)
    r1 = x - hi.astype(F32)
    mid = r1.astype(BF16)
    lo = (r1 - mid.astype(F32)).astype(BF16)
    return hi, mid, lo


def _lane_select3(parts, shape):
    lane = lax.broadcasted_iota(jnp.int32, shape, 1)
    hi, mid, lo = [part.astype(F32) for part in parts]
    return jnp.where(lane == 0, hi, jnp.where(lane == 1, mid, jnp.where(lane == 2, lo, 0.0))).astype(BF16)


def _ones_rows(width):
    row = lax.broadcasted_iota(jnp.int32, (AUX_ROWS, width), 0)
    return jnp.where(row == 0, 1.0, 0.0).astype(BF16)


def _finish_aug(acc):
    return acc[:HEAD_DIM] * (1.0 / acc[HEAD_DIM:HEAD_DIM + 1])


def _fox_prompt_kernel(q_ref, k_ref, v_ref, crow_ref, o_ref, vt_ref, ka_ref, s_ref):
    qt = pl.program_id(2)
    tq = FOX_TQ

    @pl.when(qt == 0)
    def _():
        _fill_transposed(vt_ref, v_ref)
        vt_ref[HEAD_DIM:, :] = _ones_rows(vt_ref.shape[1])

        def body(k, carry):
            st = pl.multiple_of(k * LANE, LANE)
            c_col = jnp.broadcast_to(crow_ref[0, 0, :, pl.ds(st, LANE)], (LANE, LANE)).T
            ka_ref[pl.ds(st, LANE), :HEAD_DIM] = k_ref[0, 0, pl.ds(st, LANE), :]
            ka_ref[pl.ds(st, LANE), HEAD_DIM:] = _lane_select3(_split3(c_col), (LANE, LANE))
            return carry

        lax.fori_loop(0, k_ref.shape[2] // LANE, body, 0)

    q0 = pl.multiple_of(qt * tq, tq)
    q_t = _transpose_tiles(q_ref[0]).astype(BF16)
    row = lax.broadcasted_iota(jnp.int32, (LANE, tq), 0)
    q_aug = jnp.concatenate([q_t, jnp.where(row < 3, -1.0, 0.0).astype(BF16)], axis=0)
    c_q = crow_ref[0, 0, :, pl.ds(q0, tq)]
    blk = FOX_KEY_BLOCK

    last = q0 // blk
    rel = (lax.broadcasted_iota(jnp.int32, (blk, tq), 0) - lax.broadcasted_iota(jnp.int32, (blk, tq), 1))

    def run(n_blocks):
        top = jnp.full((8, tq), NEG, F32)
        for kb in range(n_blocks):
            start = kb * blk
            s = jnp.dot(ka_ref[start:start + blk, :], q_aug, preferred_element_type=F32)
            if kb == n_blocks - 1:
                s = jnp.where(rel <= q0 - start, s, NEG)
            s_ref[start:start + blk, :] = s
            top = jnp.maximum(top, jnp.max(s.reshape(blk // 8, 8, tq), axis=0))
        m = jnp.max(top, axis=0, keepdims=True) + c_q
        shift = m - c_q
        acc = jnp.zeros((HEAD_DIM + AUX_ROWS, tq), F32)
        for kb in range(n_blocks):
            start = kb * blk
            p = jnp.exp(s_ref[start:start + blk, :] - shift).astype(BF16)
            acc = acc + jnp.dot(vt_ref[:, start:start + blk], p, preferred_element_type=F32)
        return acc

    n_max = k_ref.shape[2] // blk
    acc = lax.switch(last, [functools.partial(run, n) for n in range(1, n_max + 1)])
    o_ref[...] = _untranspose_tiles(_finish_aug(acc))


def _fox_prompt(hm, hm4, crow, nb, seq):
    nq = seq // FOX_TQ
    return pl.pallas_call(
        _fox_prompt_kernel,
        out_shape=jax.ShapeDtypeStruct((nb * seq, FOX_HEADS * HEAD_DIM), F32),
        grid=(nb, FOX_HEADS, nq),
        in_specs=[
            pl.BlockSpec((1, FOX_TQ, HEAD_DIM), lambda b, h, t: (HM_Q_FOX + h, b * nq + t, 0)),
            pl.BlockSpec((1, 1, seq, HEAD_DIM), lambda b, h, t: (HM_K_FOX + h, b, 0, 0)),
            pl.BlockSpec((1, 1, seq, HEAD_DIM), lambda b, h, t: (HM_V_FOX + h, b, 0, 0)),
            pl.BlockSpec((1, 1, 1, seq), lambda b, h, t: (b, h, 0, 0)),
        ],
        out_specs=pl.BlockSpec((FOX_TQ, HEAD_DIM), lambda b, h, t: (b * nq + t, h)),
        scratch_shapes=[pltpu.VMEM((HEAD_DIM + AUX_ROWS, seq), BF16), pltpu.VMEM((seq, 2 * HEAD_DIM), BF16),
                        pltpu.VMEM((seq, FOX_TQ), F32)],
        compiler_params=_cparams(("parallel", "parallel", "arbitrary")),
        name="fox_prompt",
    )(hm, hm4, hm4, crow)


NSA_TQ = LANE
NSA_COLS = NSA_GROUP * NSA_TQ
WIN_TILES = WINDOW // NSA_TQ + 1
SLC_BLOCK_TILES = 4
SLC_MASK_ROWS = SLC_BLOCK_TILES * NSA_TQ // SEL_BLOCK


def _nsa_prompt_kernel(tbl_ref, q_ref, kc_ref, vc_ref, mselt_ref, ks_ref, vs_ref, kw_ref, vw_ref, gate_ref, o_ref,
                       vst_ref, vwt_ref, vct_ref, wb_ref, pc_ref, drop_ref, ksa_ref, wbd_ref, s_ref, *, n_sel, cmp_back):
    g = pl.program_id(1)
    qt = pl.program_id(2)
    tq = NSA_TQ
    cols = NSA_COLS
    heads = [NSA_GROUP * g + hh for hh in range(NSA_GROUP)]
    ncp = kc_ref.shape[2]
    nj = mselt_ref.shape[0]

    @pl.when(qt == 0)
    def _():
        _fill_transposed(vst_ref, vs_ref)
        _fill_transposed(vwt_ref, vw_ref)
        vst_ref[HEAD_DIM:, :] = _ones_rows(vst_ref.shape[1])
        vwt_ref[HEAD_DIM:, :] = _ones_rows(vwt_ref.shape[1])
        vct_ref[...] = _transpose_tiles(vc_ref[0, 0]).astype(BF16)

        def fill_keys(k, carry):
            st = pl.multiple_of(k * tq, tq)
            lane = lax.broadcasted_iota(jnp.int32, (tq, tq), 1)
            blk_in_step = (k % SLC_BLOCK_TILES) * (tq // SEL_BLOCK) + lax.broadcasted_iota(jnp.int32, (tq, tq), 0) // SEL_BLOCK
            extra = (lane == blk_in_step) | ((lane >= SLC_MASK_ROWS) & (lane < SLC_MASK_ROWS + 3))
            ksa_ref[pl.ds(st, tq), :HEAD_DIM] = ks_ref[0, 0, pl.ds(st, tq), :]
            ksa_ref[pl.ds(st, tq), HEAD_DIM:] = jnp.where(extra, 1.0, 0.0).astype(BF16)
            return carry

        lax.fori_loop(0, ks_ref.shape[2] // tq, fill_keys, 0)
        key = lax.broadcasted_iota(jnp.int32, (tq, tq), 0)
        qry = lax.broadcasted_iota(jnp.int32, (tq, tq), 1)
        blk = lax.broadcasted_iota(jnp.int32, (pc_ref.shape[0], tq), 0) - cmp_back
        d_cmp = lax.broadcasted_iota(jnp.int32, (pc_ref.shape[0], tq), 1) - (blk * CMP_STRIDE + (CMP_LEN - 1))
        for hh in range(NSA_GROUP):
            sl = slice(hh * tq, (hh + 1) * tq)
            far = tbl_ref[REL_BUCKETS - 1, heads[hh]]
            for delta in range(2):
                d = delta * tq + qry - key
                wb_ref[delta, :, sl] = jnp.where(d >= 0, _rel_bias(jnp.maximum(d, 0), tbl_ref, heads[hh]), NEG)
            for delta in range(2, WIN_TILES - 1):
                wb_ref[delta, :, sl] = jnp.full((tq, tq), far, F32)
            wb_ref[WIN_TILES - 1, :, sl] = jnp.where(qry < key, far, NEG)
            wb_ref[WIN_TILES, :, sl] = jnp.full((tq, tq), NEG, F32)
            for delta in range(2):
                wbd_ref[delta, :, sl] = wb_ref[delta, :, sl] - far
            wbd_ref[2, :, sl] = jnp.zeros((tq, tq), F32)
            wbd_ref[3, :, sl] = jnp.full((tq, tq), NEG, F32)
            pc_ref[:, sl] = jnp.where(d_cmp >= 0, _rel_bias(jnp.maximum(d_cmp, 0), tbl_ref, heads[hh]), NEG)

    q_t = jnp.concatenate([q_ref[hh].astype(F32).T for hh in range(NSA_GROUP)], axis=1).astype(BF16)

    off = pl.multiple_of(cmp_back - qt * (tq // CMP_STRIDE), 8)
    s_t = jnp.dot(kc_ref[0, 0], q_t, preferred_element_type=F32) + pc_ref[pl.ds(off, ncp), :]
    m = jnp.max(s_t, axis=0, keepdims=True)
    e = jnp.exp(s_t - m)
    inv = jnp.where(m > 0.5 * NEG, 1.0 / jnp.sum(e, axis=0, keepdims=True), 0.0)
    p_t = e * inv
    o_cmp = jnp.dot(vct_ref[...], p_t.astype(BF16), preferred_element_type=F32)
    p_sum = p_t[:, :tq]
    for hh in range(1, NSA_GROUP):
        p_sum = p_sum + p_t[:, hh * tq:(hh + 1) * tq]
    imp = jnp.dot(mselt_ref[...], p_sum, preferred_element_type=F32, precision=lax.Precision.HIGHEST)

    jrow = lax.broadcasted_iota(jnp.int32, (nj, tq), 0)
    pos = qt * tq + lax.broadcasted_iota(jnp.int32, (nj, tq), 1)
    blk_q = jnp.right_shift(pos, int(math.log2(SEL_BLOCK)))
    forced = (jrow == 0) | (jrow == blk_q) | (jrow == blk_q - 1)
    score = jnp.where(forced, FORCED_SCORE, jnp.where(jrow <= blk_q, imp, -1.0))
    score = jnp.where(jrow < n_sel, score, -2.0)
    ranks = []
    for r in range(nj // 8):
        mine = score[r * 8:(r + 1) * 8]
        jmine = jrow[r * 8:(r + 1) * 8]
        rank = jnp.zeros((8, tq), F32)
        for other in range(n_sel):
            row = jnp.broadcast_to(score[other:other + 1], (8, tq))
            if other < r * 8:
                beats = row >= mine
            elif other >= (r + 1) * 8:
                beats = row > mine
            else:
                beats = (row > mine) | ((row == mine) & (jmine > other))
            rank = rank + jnp.where(beats, 1.0, 0.0)
        ranks.append(rank)
    rank = jnp.concatenate(ranks, axis=0)
    drop = jnp.where((rank < SEL_TOPK) & (jrow < n_sel), 0.0, NEG)
    drop_ref[...] = jnp.concatenate([drop] * NSA_GROUP, axis=1)

    blk = SLC_BLOCK_TILES * tq
    col = lax.broadcasted_iota(jnp.int32, (8, cols), 1)
    far_row = jnp.full((8, cols), tbl_ref[REL_BUCKETS - 1, heads[NSA_GROUP - 1]], F32)
    for hh in range(NSA_GROUP - 1):
        far_row = jnp.where(col // tq == hh, tbl_ref[REL_BUCKETS - 1, heads[hh]], far_row)
    row8 = lax.broadcasted_iota(jnp.int32, (8, cols), 0)
    parts = [part.astype(F32) for part in _split3(far_row)]
    far_rows = jnp.where(row8 == 0, parts[0], jnp.where(row8 == 1, parts[1], jnp.where(row8 == 2, parts[2], 0.0)))
    pad_rows = jnp.zeros((HEAD_DIM - SLC_MASK_ROWS - 8, cols), BF16)

    last = qt // SLC_BLOCK_TILES

    def slc_run(n_blocks):
        top = jnp.full((8, cols), NEG, F32)
        for kb in range(n_blocks):
            start = kb * blk
            masks = drop_ref[kb * SLC_MASK_ROWS:(kb + 1) * SLC_MASK_ROWS, :]
            extra = jnp.concatenate([masks, far_rows], axis=0).astype(BF16)
            q_aug = jnp.concatenate([q_t, extra, pad_rows], axis=0)
            s = jnp.dot(ksa_ref[start:start + blk, :], q_aug, preferred_element_type=F32)
            if kb >= n_blocks - 2:
                terms = []
                for i in range(SLC_BLOCK_TILES):
                    delta = qt - (kb * SLC_BLOCK_TILES + i)
                    terms.append(wbd_ref[jnp.where(delta < 0, 3, jnp.minimum(delta, 2))])
                s = s + jnp.concatenate(terms, axis=0)
            s_ref[start:start + blk, :] = s
            top = jnp.maximum(top, jnp.max(s.reshape(blk // 8, 8, cols), axis=0))
        m = jnp.max(top, axis=0, keepdims=True)
        acc = jnp.zeros((HEAD_DIM + AUX_ROWS, cols), F32)
        for kb in range(n_blocks):
            start = kb * blk
            p = jnp.exp(s_ref[start:start + blk, :] - m).astype(BF16)
            acc = acc + jnp.dot(vst_ref[:, start:start + blk], p, preferred_element_type=F32)
        return acc

    n_max = ks_ref.shape[2] // blk
    o_slc = _finish_aug(lax.switch(last, [functools.partial(slc_run, n) for n in range(1, n_max + 1)]))

    w0 = jnp.maximum(qt - (WIN_TILES - 1), 0)
    start = pl.multiple_of(w0 * tq, tq)
    span = WIN_TILES * tq
    s = jnp.dot(kw_ref[0, 0, pl.ds(start, span), :], q_t, preferred_element_type=F32)
    terms = []
    for i in range(WIN_TILES):
        delta = qt - (w0 + i)
        terms.append(wb_ref[jnp.where(delta < 0, WIN_TILES, delta)])
    s = s + jnp.concatenate(terms, axis=0)
    p = jnp.exp(s - jnp.max(s, axis=0, keepdims=True))
    o_win = _finish_aug(jnp.dot(vwt_ref[:, pl.ds(start, span)], p.astype(BF16), preferred_element_type=F32))

    gate = gate_ref[0, 0, 0]
    o_t = gate[0:1] * o_cmp + gate[1:2] * o_slc + gate[2:3] * o_win
    for hh in range(NSA_GROUP):
        o_ref[:, hh * HEAD_DIM:(hh + 1) * HEAD_DIM] = o_t[:, hh * tq:(hh + 1) * tq].T


def _nsa_prompt(rel_table, hm, hm4, kcv, mselt, gates_t, nb, seq, n_sel):
    tq = NSA_TQ
    nq = seq // tq
    ncp = kcv.shape[2]
    nj = mselt.shape[0]
    cmp_back = (nq - 1) * (tq // CMP_STRIDE)
    kern = functools.partial(_nsa_prompt_kernel, n_sel=n_sel, cmp_back=cmp_back)
    kv_spec = lambda head0: pl.BlockSpec((1, 1, seq, HEAD_DIM), lambda b, g, t: (head0 + g, b, 0, 0))
    return pl.pallas_call(
        kern,
        out_shape=jax.ShapeDtypeStruct((nb * seq, NSA_HEADS * HEAD_DIM), F32),
        grid=(nb, NSA_KV_HEADS, nq),
        in_specs=[
            pl.BlockSpec(memory_space=pltpu.SMEM),
            pl.BlockSpec((NSA_GROUP, tq, HEAD_DIM), lambda b, g, t: (g, b * nq + t, 0)),
            pl.BlockSpec((1, 1, ncp, HEAD_DIM), lambda b, g, t: (g, b, 0, 0)),
            pl.BlockSpec((1, 1, ncp, HEAD_DIM), lambda b, g, t: (NSA_KV_HEADS + g, b, 0, 0)),
            pl.BlockSpec((nj, ncp), lambda b, g, t: (0, 0)),
            kv_spec(HM_K_SLC), kv_spec(HM_V_SLC), kv_spec(HM_K_WIN), kv_spec(HM_V_WIN),
            pl.BlockSpec((1, 1, 1, 8, NSA_COLS), lambda b, g, t: (b, g, t, 0, 0)),
        ],
        out_specs=pl.BlockSpec((tq, NSA_GROUP * HEAD_DIM), lambda b, g, t: (b * nq + t, g)),
        scratch_shapes=[
            pltpu.VMEM((HEAD_DIM + AUX_ROWS, seq), BF16),
            pltpu.VMEM((HEAD_DIM + AUX_ROWS, seq), BF16),
            pltpu.VMEM((HEAD_DIM, ncp), BF16),
            pltpu.VMEM((WIN_TILES + 1, tq, NSA_COLS), F32),
            pltpu.VMEM((cmp_back + ncp, NSA_COLS), F32),
            pltpu.VMEM((nj, NSA_COLS), F32),
            pltpu.VMEM((seq, 2 * HEAD_DIM), BF16),
            pltpu.VMEM((4, tq, NSA_COLS), F32),
            pltpu.VMEM((seq, NSA_COLS), F32),
        ],
        compiler_params=_cparams(("parallel", "parallel", "arbitrary")),
        name="nsa_prompt",
    )(rel_table, hm, kcv, kcv, mselt, hm4, hm4, hm4, hm4, gates_t)


CHUNK_ROWS = 512


def _chunkify_kernel(x_ref, o_ref, *, n_cols):
    n = o_ref.shape[2]
    for c in range(o_ref.shape[0]):
        for s in range(CMP_STRIDE):
            o_ref[c, 0, :, s * HEAD_DIM:(s + 1) * HEAD_DIM] = (
                x_ref[0, pl.ds(s * n_cols + c, n, stride=CMP_STRIDE * n_cols), :].astype(BF16))


def _chunkify(rows3, n_cols, n_heads, seq):
    nb = rows3.shape[0]
    tr = _largest_tile(seq, (CHUNK_ROWS, 256))
    return pl.pallas_call(
        functools.partial(_chunkify_kernel, n_cols=n_cols),
        out_shape=jax.ShapeDtypeStruct((n_heads, nb, seq // CMP_STRIDE, CMP_STRIDE * HEAD_DIM), BF16),
        grid=(nb, seq // tr),
        in_specs=[pl.BlockSpec((1, tr * n_cols, HEAD_DIM), lambda b, i: (b, i, 0))],
        out_specs=pl.BlockSpec((n_heads, 1, tr // CMP_STRIDE, CMP_STRIDE * HEAD_DIM), lambda b, i: (0, b, i, 0)),
        compiler_params=_cparams(("parallel", "parallel")),
        name="chunkify",
    )(rows3)


REGROUP_PAGES = 8


N_CACHE_COLS = 16
N_CMP_COLS = 2 * NSA_KV_HEADS


def _regroup_kernel(pt_ref, *refs, n_steps):
    del pt_ref
    npg = REGROUP_PAGES
    nsa_in, lf_in = refs[:npg], refs[npg:2 * npg]
    xc_out, slc_out, lf_out = refs[2 * npg:]
    is_tail = pl.program_id(1) >= n_steps
    chunks = PAGE_SIZE // CMP_STRIDE
    chunk_stride = CMP_STRIDE * N_CACHE_COLS

    @pl.when(is_tail)
    def _():
        slc_out[...] = jnp.zeros(slc_out.shape, slc_out.dtype)
        lf_out[...] = jnp.zeros(lf_out.shape, lf_out.dtype)

    @pl.when(jnp.logical_not(is_tail))
    def _():
        for p in range(npg):
            rows = slice(p * PAGE_SIZE, (p + 1) * PAGE_SIZE)
            for c in range(N_CACHE_COLS - N_CMP_COLS):
                slc_out[c, 0, rows, :] = nsa_in[p][0, pl.ds(N_CMP_COLS + c, PAGE_SIZE, stride=N_CACHE_COLS), :].astype(BF16)
            lf_out[0, rows, :] = lf_in[p][0]
        for pair in range(npg // 2):
            for c in range(N_CMP_COLS):
                for s in range(CMP_STRIDE):
                    first = s * N_CACHE_COLS + c
                    both = [nsa_in[2 * pair + i][0, pl.ds(first, chunks, stride=chunk_stride), :] for i in range(2)]
                    xc_out[c, 0, pair * 2 * chunks:(pair + 1) * 2 * chunks, s * HEAD_DIM:(s + 1) * HEAD_DIM] = (
                        jnp.concatenate(both, axis=0).astype(BF16))


def _regroup(page_table, cache_nsa, cache_logf, lk):
    nb, n_pages = page_table.shape
    npg = REGROUP_PAGES
    n_steps = n_pages // npg
    rows = npg * PAGE_SIZE
    last = n_steps - 1
    n_slc = N_CACHE_COLS - N_CMP_COLS
    chunk_w = CMP_STRIDE * HEAD_DIM

    def page_map(p):
        return lambda b, s, pt: (pt[b, jnp.minimum(s, last) * npg + p], 0, 0)

    def specs(arr):
        return [pl.BlockSpec((1,) + arr.shape[1:], page_map(p)) for p in range(npg)]

    assert (lk - n_pages * PAGE_SIZE) % rows == 0
    n_tail = (lk - n_pages * PAGE_SIZE) // rows
    grid_spec = pltpu.PrefetchScalarGridSpec(
        num_scalar_prefetch=1,
        grid=(nb, n_steps + n_tail),
        in_specs=specs(cache_nsa) + specs(cache_logf),
        out_specs=(
            pl.BlockSpec((N_CMP_COLS, 1, rows // CMP_STRIDE, chunk_w), lambda b, s, pt: (0, b, jnp.minimum(s, last), 0)),
            pl.BlockSpec((n_slc, 1, rows, HEAD_DIM), lambda b, s, pt: (0, b, s, 0)),
            pl.BlockSpec((1, rows, cache_logf.shape[2]), lambda b, s, pt: (b, s, 0)),
        ),
    )
    return pl.pallas_call(
        functools.partial(_regroup_kernel, n_steps=n_steps),
        out_shape=(
            jax.ShapeDtypeStruct((N_CMP_COLS, nb, n_pages * PAGE_SIZE // CMP_STRIDE, chunk_w), BF16),
            jax.ShapeDtypeStruct((n_slc, nb, lk, HEAD_DIM), BF16),
            jax.ShapeDtypeStruct((nb, lk, cache_logf.shape[2]), F32),
        ),
        grid_spec=grid_spec,
        compiler_params=_cparams(("parallel", "arbitrary")),
        name="cache_regroup",
    )(page_table, *([cache_nsa] * npg), *([cache_logf] * npg))


FOX_DEC_PAGES = 8


def _fox_decode_kernel(pt_ref, *refs, dseq):
    del pt_ref
    npg = FOX_DEC_PAGES
    pages = refs[:npg]
    qbd_ref, ccol_ref, cq_ref, knew_ref, vnew_ref, o_ref, m_ref, acc_ref = refs[npg:]
    step = pl.program_id(1)
    is_tail = step == pl.num_programs(1) - 1
    width = FOX_HEADS * HEAD_DIM

    @pl.when(step == 0)
    def _():
        m_ref[...] = jnp.full(m_ref.shape, NEG, F32)
        acc_ref[...] = jnp.zeros(acc_ref.shape, F32)

    def fold(k_blk, v_blk, bias):
        n = k_blk.shape[0]
        s_t = jnp.dot(k_blk, qbd_ref[0], preferred_element_type=F32) + bias
        m_new = jnp.maximum(m_ref[...], jnp.max(s_t, axis=0, keepdims=True))
        alpha = jnp.exp(m_ref[...] - m_new)
        p = jnp.exp(s_t - m_new)
        p_t = jnp.concatenate([p[i * LANE:(i + 1) * LANE].T for i in range(n // LANE)], axis=1).astype(BF16)
        v_aug = jnp.concatenate([v_blk, jnp.ones((n, LANE), BF16)], axis=1)
        upd = jnp.dot(p_t, v_aug, preferred_element_type=F32)
        alpha_col = jnp.broadcast_to(alpha, (LANE, LANE)).T
        acc_ref[...] = acc_ref[...] * jnp.concatenate([alpha_col] * (width // LANE + 1), axis=1) + upd
        m_ref[...] = m_new

    @pl.when(jnp.logical_not(is_tail))
    def _():
        def heads_of(p, first):
            cols = [pages[p][0, pl.ds(first + h, PAGE_SIZE, stride=N_CACHE_COLS), :] for h in range(FOX_HEADS)]
            return jnp.concatenate(cols, axis=1).astype(BF16)

        k_blk = jnp.concatenate([heads_of(p, 0) for p in range(npg)], axis=0)
        v_blk = jnp.concatenate([heads_of(p, FOX_HEADS) for p in range(npg)], axis=0)
        fold(k_blk, v_blk, cq_ref[0] - ccol_ref[0])

    @pl.when(is_tail)
    def _():
        row = lax.broadcasted_iota(jnp.int32, (LANE, LANE), 0)
        t_of_col = lax.broadcasted_iota(jnp.int32, (LANE, LANE), 1) % dseq
        bias = jnp.where(row <= t_of_col, cq_ref[0] - ccol_ref[0, :LANE, :], NEG)
        fold(knew_ref[0], vnew_ref[0], bias)
        acc = acc_ref[...]
        inv = 1.0 / acc[:, width:]
        o_ref[0] = acc[:, :width] * jnp.concatenate([inv] * (width // LANE), axis=1)


def _fox_decode(page_table, cache_fox, qbd, ccols, cq, knew, vnew, dseq):
    nb, n_pages = page_table.shape
    npg = FOX_DEC_PAGES
    n_steps = n_pages // npg
    last = n_steps - 1
    width = FOX_HEADS * HEAD_DIM

    def page_map(p):
        return lambda b, s, pt: (pt[b, jnp.minimum(s, last) * npg + p], 0, 0)

    per_b = lambda shape: pl.BlockSpec((1,) + shape, lambda b, s, pt: (b, 0, 0))
    grid_spec = pltpu.PrefetchScalarGridSpec(
        num_scalar_prefetch=1,
        grid=(nb, n_steps + 1),
        in_specs=[pl.BlockSpec((1,) + cache_fox.shape[1:], page_map(p)) for p in range(npg)] + [
            per_b((width, LANE)),
            pl.BlockSpec((1, npg * PAGE_SIZE, LANE), lambda b, s, pt: (b, s, 0)),
            per_b((1, LANE)),
            per_b((LANE, width)),
            per_b((LANE, width)),
        ],
        out_specs=per_b((LANE, width)),
        scratch_shapes=[pltpu.VMEM((1, LANE), F32), pltpu.VMEM((LANE, width + LANE), F32)],
    )
    return pl.pallas_call(
        functools.partial(_fox_decode_kernel, dseq=dseq),
        out_shape=jax.ShapeDtypeStruct((nb, LANE, width), F32),
        grid_spec=grid_spec,
        compiler_params=_cparams(("parallel", "arbitrary")),
        name="fox_decode",
    )(page_table, *([cache_fox] * npg), qbd, ccols, cq, knew, vnew)


NSA_DEC_KEYS = 2048


def _lane_transpose(p):
    return jnp.concatenate([p[i * LANE:(i + 1) * LANE].T for i in range(p.shape[0] // LANE)], axis=1).astype(BF16)


def _nsa_decode_kernel(qbd_ref, kc_ref, vc_ref, mselt_ref, pair_ref, tcol_ref, ks_ref, vs_ref, kw_ref, vw_ref,
                       gate_ref, o_ref, drop_ref, m_ref, acc_ref, ocmp_ref, *, dseq, past, n_sel):
    step = pl.program_id(1)
    n_tiles = pl.num_programs(1)
    width = NSA_KV_HEADS * HEAD_DIM
    qbd = qbd_ref[0]
    col = lax.broadcasted_iota(jnp.int32, (1, LANE), 1)
    q_pos = past + col % dseq

    def side_by_side(ref, rows=None):
        parts = [ref[g, 0] if rows is None else ref[g, 0, rows, :] for g in range(NSA_KV_HEADS)]
        return jnp.concatenate(parts, axis=1)

    def rel_bias_cols(dist):
        out = jnp.broadcast_to(tcol_ref[0:1, :], dist.shape)
        for k, thr in enumerate(BUCKET_THR, start=1):
            out = jnp.where(dist >= thr, tcol_ref[k:k + 1, :], out)
        return out

    def softmax_pv(s_t, v_all):
        m = jnp.max(s_t, axis=0, keepdims=True)
        e = jnp.exp(s_t - m)
        inv = jnp.where(m > 0.5 * NEG, 1.0 / jnp.sum(e, axis=0, keepdims=True), 0.0)
        p = e * inv
        return p, jnp.dot(_lane_transpose(p), v_all, preferred_element_type=F32)

    @pl.when(step == 0)
    def _():
        m_ref[...] = jnp.full(m_ref.shape, NEG, F32)
        acc_ref[...] = jnp.zeros(acc_ref.shape, F32)
        ncp = kc_ref.shape[2]
        blk_end = lax.broadcasted_iota(jnp.int32, (ncp, LANE), 0) * CMP_STRIDE + (CMP_LEN - 1)
        d = q_pos - blk_end
        s_t = jnp.dot(side_by_side(kc_ref), qbd, preferred_element_type=F32)
        s_t = jnp.where(d >= 0, s_t + rel_bias_cols(jnp.maximum(d, 0)), NEG)
        p, ocmp_ref[...] = softmax_pv(s_t, side_by_side(vc_ref))
        imp = jnp.dot(mselt_ref[...], p, preferred_element_type=F32, precision=lax.Precision.HIGHEST)
        imp = jnp.dot(imp, pair_ref[...], preferred_element_type=F32, precision=lax.Precision.HIGHEST)
        nj = imp.shape[0]
        jrow = lax.broadcasted_iota(jnp.int32, (nj, LANE), 0)
        blk_q = jnp.right_shift(q_pos, int(math.log2(SEL_BLOCK)))
        forced = (jrow == 0) | (jrow == blk_q) | (jrow == blk_q - 1)
        score = jnp.where(forced, FORCED_SCORE, jnp.where(jrow <= blk_q, imp, -1.0))
        score = jnp.where(jrow < n_sel, score, -2.0)
        ranks = []
        for r in range(-(-n_sel // 8)):
            mine = score[r * 8:(r + 1) * 8]
            jmine = jrow[r * 8:(r + 1) * 8]
            rank = jnp.zeros((8, LANE), F32)
            for other in range(n_sel):
                row = jnp.broadcast_to(score[other:other + 1], (8, LANE))
                if other < r * 8:
                    beats = row >= mine
                elif other >= (r + 1) * 8:
                    beats = row > mine
                else:
                    beats = (row > mine) | ((row == mine) & (jmine > other))
                rank = rank + jnp.where(beats, 1.0, 0.0)
            ranks.append(rank)
        if nj > 8 * len(ranks):
            ranks.append(jnp.full((nj - 8 * len(ranks), LANE), float(SEL_TOPK), F32))
        rank = jnp.concatenate(ranks, axis=0)
        far = tcol_ref[REL_BUCKETS - 1:REL_BUCKETS, :]
        drop_ref[...] = jnp.where((rank < SEL_TOPK) & (jrow <= blk_q), far, NEG)

    tile = ks_ref.shape[2]
    per_tile = tile // SEL_BLOCK
    start = step * tile
    first_blk = pl.multiple_of(step * per_tile, 8)
    rows = [jnp.broadcast_to(drop_ref[pl.ds(first_blk + i, 1), :], (SEL_BLOCK, LANE)) for i in range(per_tile)]
    s_t = jnp.dot(side_by_side(ks_ref), qbd, preferred_element_type=F32) + jnp.concatenate(rows, axis=0)

    def near_fix(s_t):
        def fix(rows_at, s_rows):
            key_pos = start + rows_at + lax.broadcasted_iota(jnp.int32, (LANE, LANE), 0)
            d = q_pos - key_pos
            far = tcol_ref[REL_BUCKETS - 1:REL_BUCKETS, :]
            return jnp.where(d >= 0, s_rows + (rel_bias_cols(jnp.maximum(d, 0)) - far), NEG)

        head = fix(0, s_t[:LANE])
        tail = fix(tile - LANE, s_t[tile - LANE:])
        return jnp.concatenate([head, s_t[LANE:tile - LANE], tail], axis=0)

    is_near = (start + tile > past - FAR_DIST)
    s_t = lax.cond(is_near, near_fix, lambda s: s, s_t)
    m_new = jnp.maximum(m_ref[...], jnp.max(s_t, axis=0, keepdims=True))
    alpha = jnp.exp(m_ref[...] - m_new)
    p = jnp.exp(s_t - m_new)
    v_aug = jnp.concatenate([side_by_side(vs_ref), jnp.ones((tile, LANE), BF16)], axis=1)
    upd = jnp.dot(_lane_transpose(p), v_aug, preferred_element_type=F32)
    alpha_col = jnp.broadcast_to(alpha, (LANE, LANE)).T
    acc_ref[...] = acc_ref[...] * jnp.concatenate([alpha_col] * (width // LANE + 1), axis=1) + upd
    m_ref[...] = m_new

    @pl.when(step == n_tiles - 1)
    def _():
        acc = acc_ref[...]
        o_slc = acc[:, :width] * jnp.concatenate([1.0 / acc[:, width:]] * (width // LANE), axis=1)
        span = kw_ref.shape[2]
        key_pos = (past - WINDOW) + lax.broadcasted_iota(jnp.int32, (span, LANE), 0)
        d = q_pos - key_pos
        s_w = jnp.dot(side_by_side(kw_ref), qbd, preferred_element_type=F32)
        s_w = jnp.where((d >= 0) & (d < WINDOW), s_w + rel_bias_cols(jnp.maximum(d, 0)), NEG)
        _, o_win = softmax_pv(s_w, side_by_side(vw_ref))
        g = gate_ref[0]
        tile4 = lambda a: jnp.concatenate([a] * (width // LANE), axis=1)
        o_ref[0] = tile4(g[0]) * ocmp_ref[...] + tile4(g[1]) * o_slc + tile4(g[2]) * o_win


def _nsa_decode(qbd, kcv, mselt, pair, tcols, nsa_dec, win_dec, gcols, *, dseq, past, n_sel):
    nb = qbd.shape[0]
    ncp = kcv.shape[2]
    nj = mselt.shape[0]
    span = win_dec.shape[2]
    tile = NSA_DEC_KEYS
    n_tiles = -(-nsa_dec.shape[2] // tile)
    width = NSA_KV_HEADS * HEAD_DIM
    kern = functools.partial(_nsa_decode_kernel, dseq=dseq, past=past, n_sel=n_sel)
    grp = lambda rows, half, tiled: pl.BlockSpec(
        (NSA_KV_HEADS, 1, rows, HEAD_DIM), (lambda b, s: (half, b, s, 0)) if tiled else (lambda b, s: (half, b, 0, 0)))
    const = lambda shape: pl.BlockSpec(shape, lambda b, s: (0,) * len(shape))
    return pl.pallas_call(
        kern,
        out_shape=jax.ShapeDtypeStruct((nb, LANE, width), F32),
        grid=(nb, n_tiles),
        in_specs=[
            pl.BlockSpec((1, width, LANE), lambda b, s: (b, 0, 0)),
            grp(ncp, 0, False), grp(ncp, 1, False),
            const((nj, ncp)), const((LANE, LANE)), const((REL_BUCKETS, LANE)),
            grp(tile, 0, True), grp(tile, 1, True),
            grp(span, 0, False), grp(span, 1, False),
            pl.BlockSpec((1, 3, LANE, LANE), lambda b, s: (b, 0, 0, 0)),
        ],
        out_specs=pl.BlockSpec((1, LANE, width), lambda b, s: (b, 0, 0)),
        scratch_shapes=[
            pltpu.VMEM((nj, LANE), F32),
            pltpu.VMEM((1, LANE), F32),
            pltpu.VMEM((LANE, width + LANE), F32),
            pltpu.VMEM((LANE, width), F32),
        ],
        compiler_params=_cparams(("parallel", "arbitrary")),
        name="nsa_decode",
    )(qbd, kcv, kcv, mselt, pair, tcols, nsa_dec, nsa_dec, win_dec, win_dec, gcols)


def _largest_tile(n, candidates):
    for c in candidates:
        if n % c == 0:
            return c
    raise ValueError(f"no tile in {candidates} divides {n}")


FFN_ROW_TILES = (1024, 512, 256, 128)
ROW_TILES = (512, 256, 128)


def _token_stage_in(x, p):
    m = x.shape[0]
    tm = m if m < ROW_TILES[-1] else _largest_tile(m, ROW_TILES)
    tm_ffn = m if m < FFN_ROW_TILES[-1] else _largest_tile(m, FFN_ROW_TILES)
    x1 = _ffn(x, p["norm_ffn1"], p["wg1"], p["wu1"], p["wd1"], tm_ffn, p["tf"])
    tm_in = m if m < ROW_TILES[-1] else _largest_tile(m, ROW_TILES[1:])
    nsa_rows, win_rows, fox_rows, hm = _inproj(x1, p["norm_mix"], p["w_main"], p["colgain"], tm_in)
    small = _small(x1, p["norm_mix"], p["w_small"], p["b_small"], tm)
    return x1, nsa_rows, win_rows, fox_rows, hm, small


def _token_stage_out(x1, o_nsa, o_fox, p):
    m = x1.shape[0]
    tm = m if m < ROW_TILES[-1] else _largest_tile(m, ROW_TILES[1:])
    tm_ffn = m if m < FFN_ROW_TILES[-1] else _largest_tile(m, FFN_ROW_TILES)
    x2 = _outproj(o_nsa, o_fox, p["out_norm_nsa"], p["out_norm_fox"], p["w_out"], x1, tm)
    return _ffn(x2, p["norm_ffn2"], p["wg2"], p["wu2"], p["wd2"], tm_ffn, p["tf"])


def kernel(x_prompt, x_sample, cache_nsa_kv, cache_fox_kv, cache_fox_logf, state_win_kv, page_table, rel_table, norm_ffn1, ffn1_gate, ffn1_up, ffn1_down, norm_mix, w_in, nsa_gate_bias, fox_forget_bias, q_norm_nsa, k_norm_nsa, q_norm_fox, k_norm_fox, cmp_pos_k, cmp_w1_k, cmp_w2_k, cmp_pos_v, cmp_w1_v, cmp_w2_v, out_norm_nsa, out_norm_fox, w_out, norm_ffn2, ffn2_gate, ffn2_up, ffn2_down):
    depth = w_in.shape[0]
    assert depth == 1, "single-layer trunk"
    nbp, seq, d_model = x_prompt.shape
    nbd, dseq, _ = x_sample.shape
    n_pages = page_table.shape[1]
    past = n_pages * PAGE_SIZE
    d_ff = ffn1_gate.shape[2]
    nsa_w = NSA_HEADS * HEAD_DIM
    kv6_w = 6 * NSA_KV_HEADS * HEAD_DIM
    fox_w = 3 * FOX_HEADS * HEAD_DIM
    off_gate = nsa_w + kv6_w
    off_fox = off_gate + N_GATE_COLS
    off_forget = off_fox + fox_w
    assert w_in.shape[2] == off_forget + FOX_HEADS and d_model == nsa_w + FOX_HEADS * HEAD_DIM
    assert seq % LANE == 0 and seq >= WINDOW and past % LANE == 0 and n_pages % REGROUP_PAGES == 0
    assert dseq <= 16 and state_win_kv.shape[2] == WINDOW
    assert seq % FOX_KEY_BLOCK == 0 and seq % (SLC_BLOCK_TILES * NSA_TQ) == 0 and seq >= WIN_TILES * NSA_TQ

    w0 = w_in[0]
    ones = lambda n: jnp.ones((n,), F32)
    zeros = lambda n: jnp.zeros((n,), F32)
    kn, kvw = NSA_KV_HEADS, NSA_KV_HEADS * HEAD_DIM
    qk_scale = HEAD_DIM ** -0.5
    p = {
        "tf": _largest_tile(d_ff, (512, 256, 128)),
        "norm_ffn1": norm_ffn1[0][None], "norm_mix": norm_mix[0][None], "norm_ffn2": norm_ffn2[0][None],
        "wg1": ffn1_gate[0].astype(BF16), "wu1": ffn1_up[0].astype(BF16), "wd1": ffn1_down[0].astype(BF16),
        "wg2": ffn2_gate[0].astype(BF16), "wu2": ffn2_up[0].astype(BF16), "wd2": ffn2_down[0].astype(BF16),
        "w_main": jnp.concatenate([w0[:, :off_gate], w0[:, off_fox:off_forget]], axis=1).astype(BF16),
        "w_small": jnp.concatenate([w0[:, off_gate:off_fox], w0[:, off_forget:],
                                    jnp.zeros((d_model, LANE - N_GATE_COLS - FOX_HEADS), F32)], axis=1).astype(BF16),
        "b_small": jnp.concatenate([nsa_gate_bias[0].reshape(-1), fox_forget_bias[0],
                                    zeros(LANE - N_GATE_COLS - FOX_HEADS)])[None],
        "colgain": jnp.concatenate([
            jnp.tile(q_norm_nsa[0] * qk_scale, NSA_HEADS), ones(2 * kvw), jnp.tile(k_norm_nsa[0], kn), ones(kvw),
            jnp.tile(k_norm_nsa[0], kn), ones(kvw), jnp.tile(q_norm_fox[0] * qk_scale, FOX_HEADS),
            jnp.tile(k_norm_fox[0], FOX_HEADS), ones(FOX_HEADS * HEAD_DIM)])[None],
        "out_norm_nsa": out_norm_nsa[0][None], "out_norm_fox": out_norm_fox[0][None],
        "w_out": w_out[0].astype(BF16),
    }
    half = CMP_STRIDE * HEAD_DIM

    def cmp_w1(w):
        return jnp.concatenate([w[0, :half], w[0, half:]], axis=1)

    def cmp_pe(pe):
        return jnp.concatenate([pe[0].reshape(CMP_LEN // CMP_STRIDE, half), jnp.zeros((PE_ROWS - CMP_LEN // CMP_STRIDE, half), F32)], axis=0)

    w1cat = jnp.stack([cmp_w1(cmp_w1_k), cmp_w1(cmp_w1_v)]).astype(BF16)
    w2cat = jnp.stack([cmp_w2_k[0], cmp_w2_v[0]]).astype(BF16)
    pecat = jnp.stack([cmp_pe(cmp_pos_k), cmp_pe(cmp_pos_v)]).astype(BF16)
    k_norm_row = k_norm_nsa[0][None]

    mp = nbp * seq
    x1, nsa_rows, win_rows, fox_rows, hm, small = _token_stage_in(x_prompt.reshape(mp, d_model), p)
    hm4 = hm.reshape(N_HEAD_COLS, nbp, seq, HEAD_DIM)

    logf = small[:, N_GATE_COLS:N_GATE_COLS + FOX_HEADS]
    csum = _cumsum(logf.reshape(nbp, seq, FOX_HEADS).transpose(0, 2, 1))
    o_fox = _fox_prompt(hm, hm4, csum[:, :, None, :], nbp, seq)

    n_chunk = seq // CMP_STRIDE
    n_cmp = (seq - CMP_LEN) // CMP_STRIDE + 1
    n_sel = -(-seq // SEL_BLOCK)
    xc = _chunkify(nsa_rows.reshape(nbp, seq * N_CACHE_COLS, HEAD_DIM), N_CACHE_COLS, N_CMP_COLS, seq)
    kcv = _compress(xc, 0, w1cat, w2cat, pecat, k_norm_row, n_chunk)
    mselt = _cmp_to_sel(n_cmp, n_sel, n_chunk, -(-n_sel // 8) * 8).T
    nq = seq // NSA_TQ
    gates_t = small[:, :N_GATE_COLS].reshape(nbp, nq, NSA_TQ, NSA_KV_HEADS, NSA_GROUP, 3)
    gates_t = gates_t.transpose(0, 3, 1, 5, 4, 2).reshape(nbp, NSA_KV_HEADS, nq, 3, NSA_COLS)
    gates_t = jnp.pad(gates_t, ((0, 0), (0, 0), (0, 0), (0, 8 - 3), (0, 0)))
    o_nsa = _nsa_prompt(rel_table, hm, hm4, kcv, mselt, gates_t, nbp, seq, n_sel)
    y_p = _token_stage_out(x1, o_nsa, o_fox, p)

    ms = nbd * dseq
    lk = past + NSA_DEC_KEYS
    assert n_pages % FOX_DEC_PAGES == 0 and past % NSA_DEC_KEYS == 0 and FOX_HEADS * dseq <= LANE
    xs1, nsa_rows_s, win_rows_s, fox_rows_s, hm_s, small_s = _token_stage_in(x_sample.reshape(ms, d_model), p)
    xc_d, nsa_dec, lf_dec = _regroup(
        page_table,
        cache_nsa_kv.reshape(cache_nsa_kv.shape[1], PAGE_SIZE * N_CACHE_COLS, HEAD_DIM),
        cache_fox_logf[0], lk)
    hm_s4 = hm_s.reshape(N_HEAD_COLS, nbd, dseq, HEAD_DIM)
    nsa_dec = lax.dynamic_update_slice(nsa_dec, hm_s4[HM_K_SLC:HM_K_WIN], (0, 0, past, 0))
    logf_s = small_s[:, N_GATE_COLS:N_GATE_COLS + FOX_HEADS].reshape(nbd, dseq, FOX_HEADS)
    lf_dec = lax.dynamic_update_slice(lf_dec, logf_s, (0, past, 0))

    csum_d = _cumsum(lf_dec.transpose(0, 2, 1))
    n_cols = FOX_HEADS * dseq
    lane_pad = lambda a: jnp.pad(a, [(0, 0)] * (a.ndim - 1) + [(0, LANE - n_cols)])
    head_eye = jnp.eye(FOX_HEADS, dtype=BF16)
    qbd = jnp.einsum("hbtd,hg->bhdgt", hm_s4[HM_Q_FOX:HM_Q_FOX + FOX_HEADS], head_eye)
    qbd = lane_pad(qbd.reshape(nbd, FOX_HEADS * HEAD_DIM, n_cols))
    ccols = lane_pad(jnp.repeat(csum_d.transpose(0, 2, 1), dseq, axis=2))
    cq = lane_pad(csum_d[:, :, past:past + dseq].reshape(nbd, 1, n_cols))

    def new_rows(head0):
        rows = hm_s4[head0:head0 + FOX_HEADS].transpose(1, 2, 0, 3).reshape(nbd, dseq, FOX_HEADS * HEAD_DIM)
        return jnp.pad(rows, ((0, 0), (0, LANE - dseq), (0, 0)))

    o_full = _fox_decode(page_table, cache_fox_kv.reshape(cache_fox_kv.shape[1], PAGE_SIZE * N_CACHE_COLS, HEAD_DIM),
                         qbd, ccols, cq, new_rows(HM_K_FOX), new_rows(HM_V_FOX), dseq)
    o_fox_s = jnp.concatenate([o_full[:, h * dseq:(h + 1) * dseq, h * HEAD_DIM:(h + 1) * HEAD_DIM]
                               for h in range(FOX_HEADS)], axis=2).reshape(ms, FOX_HEADS * HEAD_DIM)

    n_chunk_d = past // CMP_STRIDE
    n_cmp_d = (past + dseq - CMP_LEN) // CMP_STRIDE + 1
    n_sel_d = -(-(past + dseq) // SEL_BLOCK)
    assert n_cmp_d + CMP_LEN // CMP_STRIDE - 1 <= n_chunk_d, "compressed blocks must lie in the cached rows"
    kcv_d = _compress(xc_d, 0, w1cat, w2cat, pecat, k_norm_row, n_chunk_d)
    mselt_d = _cmp_to_sel(n_cmp_d, n_sel_d, n_chunk_d, lk // SEL_BLOCK).T
    win_old = state_win_kv[0].transpose(2, 3, 0, 1, 4).reshape(2 * NSA_KV_HEADS, nbd, WINDOW, HEAD_DIM).astype(BF16)
    win_dec = jnp.concatenate([win_old, hm_s4[HM_K_WIN:HM_V_WIN + NSA_KV_HEADS],
                               jnp.zeros((2 * NSA_KV_HEADS, nbd, LANE - dseq, HEAD_DIM), BF16)], axis=2)
    grp_eye = jnp.eye(NSA_KV_HEADS, dtype=BF16)
    q_grp = hm_s4[HM_Q_NSA:HM_Q_NSA + NSA_HEADS].reshape(NSA_KV_HEADS, NSA_GROUP, nbd, dseq, HEAD_DIM)
    qbd_n = jnp.einsum("gjbtd,gk->bgdkjt", q_grp, grp_eye)
    qbd_n = lane_pad(qbd_n.reshape(nbd, NSA_KV_HEADS * HEAD_DIM, n_cols))
    tcols = lane_pad(jnp.repeat(rel_table, dseq, axis=1))
    col_id = np.arange(n_cols)
    same = ((col_id[:, None] // (NSA_GROUP * dseq) == col_id[None, :] // (NSA_GROUP * dseq))
            & (col_id[:, None] % dseq == col_id[None, :] % dseq))
    pair = jnp.asarray(np.pad(same.astype(np.float32), ((0, LANE - n_cols), (0, LANE - n_cols))))
    gcols = small_s[:, :N_GATE_COLS].reshape(nbd, dseq, NSA_HEADS, 3).transpose(0, 3, 2, 1).reshape(nbd, 3, n_cols)
    gcols = jnp.broadcast_to(lane_pad(gcols)[..., None], (nbd, 3, LANE, LANE))
    o_full_n = _nsa_decode(qbd_n, kcv_d, mselt_d, pair, tcols, nsa_dec, win_dec, gcols,
                           dseq=dseq, past=past, n_sel=n_sel_d)
    o_nsa_s = jnp.concatenate(
        [o_full_n[:, h * dseq:(h + 1) * dseq, (h // NSA_GROUP) * HEAD_DIM:(h // NSA_GROUP + 1) * HEAD_DIM]
         for h in range(NSA_HEADS)], axis=2).reshape(ms, NSA_HEADS * HEAD_DIM)
    y_s = _token_stage_out(xs1, o_nsa_s, o_fox_s, p)

    kvh = (NSA_KV_HEADS, HEAD_DIM)
    win_keep = min(WINDOW, seq)
    win_p = win_rows.reshape(nbp, seq, 2, *kvh)[:, seq - win_keep:]
    win_s = jnp.concatenate([state_win_kv[0], win_rows_s.reshape(nbd, dseq, 2, *kvh)], axis=1)[:, dseq:]
    return (
        y_p.reshape(nbp, seq, d_model),
        y_s.reshape(nbd, dseq, d_model),
        nsa_rows.reshape(1, nbp, seq, 4, *kvh),
        fox_rows.reshape(1, nbp, seq, 2, FOX_HEADS, HEAD_DIM),
        logf.reshape(1, nbp, seq, FOX_HEADS),
        win_p[None],
        nsa_rows_s.reshape(1, nbd, dseq, 4, *kvh),
        fox_rows_s.reshape(1, nbd, dseq, 2, FOX_HEADS, HEAD_DIM),
        logf_s[None],
        win_s[None],
    )
```

```python
import functools
import math

import numpy as np
import jax
import jax.numpy as jnp
from jax import lax
from jax.experimental import pallas as pl
from jax.experimental.pallas import tpu as pltpu

HEAD_DIM = 128
NSA_HEADS = 8
FOX_HEADS = 8
NSA_KV_HEADS = 4
NSA_GROUP = NSA_HEADS // NSA_KV_HEADS
CMP_LEN = 32
CMP_STRIDE = 16
CMP_HIDDEN = 512
SEL_BLOCK = 64
SEL_TOPK = 16
WINDOW = 512
REL_BUCKETS = 32
REL_MAX_DIST = 128
RMS_EPS = 1e-6
PAGE_SIZE = 128

LANE = 128
NEG = -1e30
FORCED_SCORE = 1e30
VMEM_LIMIT = 56 * 1024 * 1024

BF16 = jnp.bfloat16
F32 = jnp.float32


def _bucket_thresholds():
    n = np.arange(0, 4 * REL_MAX_DIST)
    max_exact = REL_BUCKETS // 2
    nf = np.maximum(n, 1).astype(np.float32)
    large = max_exact + (np.log(nf / max_exact) / math.log(REL_MAX_DIST / max_exact)
                         * (REL_BUCKETS - max_exact)).astype(np.int32)
    bucket = np.where(n < max_exact, n, np.minimum(large, REL_BUCKETS - 1))
    return [int(np.min(n[bucket >= k])) for k in range(1, REL_BUCKETS)]


BUCKET_THR = _bucket_thresholds()
FAR_DIST = BUCKET_THR[-1]


def _cparams(sem):
    return pltpu.CompilerParams(dimension_semantics=sem, vmem_limit_bytes=VMEM_LIMIT)


def _rms_rows(x, gain):
    ms = jnp.mean(x * x, axis=-1, keepdims=True)
    return x * lax.rsqrt(ms + RMS_EPS) * gain


def _ffn_kernel(x_ref, g_ref, wg_ref, wu_ref, wd_ref, o_ref, xn_ref):
    @pl.when(pl.program_id(1) == 0)
    def _():
        x = x_ref[...]
        xn_ref[...] = _rms_rows(x, g_ref[...]).astype(BF16)
        o_ref[...] = x

    xn = xn_ref[...]
    a = jnp.dot(xn, wg_ref[...], preferred_element_type=F32)
    u = jnp.dot(xn, wu_ref[...], preferred_element_type=F32)
    h = (a / (1.0 + jnp.exp(-a))) * u * 0.5
    o_ref[...] += jnp.dot(h.astype(BF16), wd_ref[...], preferred_element_type=F32)


def _ffn(x, gain, wg, wu, wd, tm, tf):
    m, d = x.shape
    f = wg.shape[1]
    return pl.pallas_call(
        _ffn_kernel,
        out_shape=jax.ShapeDtypeStruct((m, d), F32),
        grid=(m // tm, f // tf),
        in_specs=[
            pl.BlockSpec((tm, d), lambda i, j: (i, 0)),
            pl.BlockSpec((1, d), lambda i, j: (0, 0)),
            pl.BlockSpec((d, tf), lambda i, j: (0, j)),
            pl.BlockSpec((d, tf), lambda i, j: (0, j)),
            pl.BlockSpec((tf, d), lambda i, j: (j, 0)),
        ],
        out_specs=pl.BlockSpec((tm, d), lambda i, j: (i, 0)),
        scratch_shapes=[pltpu.VMEM((tm, d), BF16)],
        compiler_params=_cparams(("parallel", "arbitrary")),
        name="ffn",
    )(x, gain, wg, wu, wd)


IN_TN = 4 * HEAD_DIM
J_NSA = (2, 6)
J_WIN = (6, 8)
J_FOX = (10, 14)
N_HEAD_COLS = 56
HM_Q_NSA, HM_K_CMP, HM_K_SLC, HM_V_SLC, HM_K_WIN, HM_V_WIN = 0, 8, 16, 20, 24, 28
HM_Q_FOX, HM_K_FOX, HM_V_FOX = 32, 40, 48


IN_NORMED_TILES = (0, 1, 4, 6, 8, 9, 10, 11)


def _inproj_kernel(x_ref, g_ref, w_ref, cg_ref, ones_ref, nsa_ref, win_ref, fox_ref, hm_ref):
    xn = _rms_rows(x_ref[...], g_ref[...]).astype(BF16)
    tm = xn.shape[0]
    heads = IN_TN // HEAD_DIM
    for j in range(w_ref.shape[1] // IN_TN):
        cols = slice(j * IN_TN, (j + 1) * IN_TN)
        vals = jnp.dot(xn, w_ref[:, cols], preferred_element_type=F32)
        if j in IN_NORMED_TILES:
            sumsq = jnp.dot((vals * vals).astype(BF16), ones_ref[...], preferred_element_type=F32)
            vals = vals * lax.rsqrt(sumsq * (1.0 / HEAD_DIM) + RMS_EPS) * cg_ref[:, cols]
        for hh in range(heads):
            hm_ref[j * heads + hh] = vals[:, hh * HEAD_DIM:(hh + 1) * HEAD_DIM].astype(BF16)
        for ref, (jlo, jhi) in ((nsa_ref, J_NSA), (win_ref, J_WIN), (fox_ref, J_FOX)):
            if jlo <= j < jhi:
                n_cols = (jhi - jlo) * heads
                for hh in range(heads):
                    ref[pl.ds((j - jlo) * heads + hh, tm, stride=n_cols), :] = vals[:, hh * HEAD_DIM:(hh + 1) * HEAD_DIM]


def _inproj(x, gain, w_main, colgain, tm):
    m, d = x.shape
    ncol = w_main.shape[1]
    heads = IN_TN // HEAD_DIM
    n_nsa, n_win, n_fox = [(hi - lo) * heads for lo, hi in (J_NSA, J_WIN, J_FOX)]
    head_ones = jnp.asarray(np.kron(np.eye(heads, dtype=np.float32), np.ones((HEAD_DIM, HEAD_DIM), np.float32)), BF16)
    resident = lambda shape: pl.BlockSpec(shape, lambda i: (0, 0), pipeline_mode=pl.Buffered(1))
    return pl.pallas_call(
        _inproj_kernel,
        out_shape=(
            jax.ShapeDtypeStruct((m * n_nsa, HEAD_DIM), F32),
            jax.ShapeDtypeStruct((m * n_win, HEAD_DIM), F32),
            jax.ShapeDtypeStruct((m * n_fox, HEAD_DIM), F32),
            jax.ShapeDtypeStruct((N_HEAD_COLS, m, HEAD_DIM), BF16),
        ),
        grid=(m // tm,),
        in_specs=[
            pl.BlockSpec((tm, d), lambda i: (i, 0)),
            resident((1, d)),
            resident((d, ncol)),
            resident((1, ncol)),
            resident((IN_TN, IN_TN)),
        ],
        out_specs=(
            pl.BlockSpec((tm * n_nsa, HEAD_DIM), lambda i: (i, 0)),
            pl.BlockSpec((tm * n_win, HEAD_DIM), lambda i: (i, 0)),
            pl.BlockSpec((tm * n_fox, HEAD_DIM), lambda i: (i, 0)),
            pl.BlockSpec((N_HEAD_COLS, tm, HEAD_DIM), lambda i: (0, i, 0)),
        ),
        compiler_params=_cparams(("parallel",)),
        name="inproj",
    )(x, gain, w_main, colgain, head_ones)


N_GATE_COLS = 3 * NSA_HEADS


def _small_kernel(x_ref, g_ref, w_ref, b_ref, o_ref):
    xn = _rms_rows(x_ref[...], g_ref[...]).astype(BF16)
    z = jnp.dot(xn, w_ref[...], preferred_element_type=F32) + b_ref[...]
    lane = lax.broadcasted_iota(jnp.int32, z.shape, 1)
    sig = 1.0 / (1.0 + jnp.exp(-z))
    logsig = jnp.minimum(z, 0.0) - jnp.log(1.0 + jnp.exp(-jnp.abs(z)))
    o_ref[...] = jnp.where(lane < N_GATE_COLS, sig,
                           jnp.where(lane < N_GATE_COLS + FOX_HEADS, logsig, 0.0))


def _small(x, gain, w_small, b_small, tm):
    m, d = x.shape
    return pl.pallas_call(
        _small_kernel,
        out_shape=jax.ShapeDtypeStruct((m, LANE), F32),
        grid=(m // tm,),
        in_specs=[
            pl.BlockSpec((tm, d), lambda i: (i, 0)),
            pl.BlockSpec((1, d), lambda i: (0, 0)),
            pl.BlockSpec((d, LANE), lambda i: (0, 0)),
            pl.BlockSpec((1, LANE), lambda i: (0, 0)),
        ],
        out_specs=pl.BlockSpec((tm, LANE), lambda i: (i, 0)),
        compiler_params=_cparams(("parallel",)),
        name="gates",
    )(x, gain, w_small, b_small)


def _outproj_kernel(on_ref, of_ref, gn_ref, gf_ref, w_ref, x_ref, y_ref):
    a = _rms_rows(on_ref[...], gn_ref[...]).astype(BF16)
    b = _rms_rows(of_ref[...], gf_ref[...]).astype(BF16)
    half = a.shape[1]
    y = jnp.dot(a, w_ref[:half, :], preferred_element_type=F32)
    y = y + jnp.dot(b, w_ref[half:, :], preferred_element_type=F32)
    y_ref[...] = x_ref[...] + y


def _outproj(o_nsa, o_fox, g_nsa, g_fox, w_out, x, tm):
    m, d = x.shape
    wn = o_nsa.shape[1]
    wf = o_fox.shape[1]
    return pl.pallas_call(
        _outproj_kernel,
        out_shape=jax.ShapeDtypeStruct((m, d), F32),
        grid=(m // tm,),
        in_specs=[
            pl.BlockSpec((tm, wn), lambda i: (i, 0)),
            pl.BlockSpec((tm, wf), lambda i: (i, 0)),
            pl.BlockSpec((1, wn), lambda i: (0, 0)),
            pl.BlockSpec((1, wf), lambda i: (0, 0)),
            pl.BlockSpec((wn + wf, d), lambda i: (0, 0)),
            pl.BlockSpec((tm, d), lambda i: (i, 0)),
        ],
        out_specs=pl.BlockSpec((tm, d), lambda i: (i, 0)),
        compiler_params=_cparams(("parallel",)),
        name="outproj",
    )(o_nsa, o_fox, g_nsa, g_fox, w_out, x)


CUMSUM_CHUNK = 512


def _cumsum_kernel(x_ref, before_ref, o_ref):
    width = x_ref.shape[2]
    r = lax.broadcasted_iota(jnp.int32, (width, width), 0)
    c = lax.broadcasted_iota(jnp.int32, (width, width), 1)
    upper = (r <= c).astype(F32)
    local = jnp.dot(x_ref[0], upper, preferred_element_type=F32, precision=lax.Precision.HIGHEST)
    totals = jnp.broadcast_to(local[:, width - 1:width], (local.shape[0], LANE))
    offset = jnp.dot(before_ref[...], totals, preferred_element_type=F32, precision=lax.Precision.HIGHEST)
    o_ref[0] = local + jnp.concatenate([offset] * (width // LANE), axis=1)


def _cumsum(x):
    b, h, length = x.shape
    assert length % CUMSUM_CHUNK == 0
    pieces = length // CUMSUM_CHUNK
    used = h * pieces
    rows = -(-used // LANE) * LANE
    idx = np.arange(rows)
    before = ((idx[:, None] // pieces == idx[None, :] // pieces) & (idx[None, :] < idx[:, None])
              & (idx[:, None] < used))
    x = jnp.pad(x.reshape(b, used, CUMSUM_CHUNK), ((0, 0), (0, rows - used), (0, 0)))
    out = pl.pallas_call(
        _cumsum_kernel,
        out_shape=jax.ShapeDtypeStruct((b, rows, CUMSUM_CHUNK), F32),
        grid=(b,),
        in_specs=[pl.BlockSpec((1, rows, CUMSUM_CHUNK), lambda i: (i, 0, 0)),
                  pl.BlockSpec((rows, rows), lambda i: (0, 0))],
        out_specs=pl.BlockSpec((1, rows, CUMSUM_CHUNK), lambda i: (i, 0, 0)),
        compiler_params=_cparams(("parallel",)),
        name="logf_cumsum",
    )(x, jnp.asarray(before, F32))
    return out[:, :used].reshape(b, h, length)


PE_ROWS = 16


def _compress_kernel(x_ref, w1_ref, w2_ref, pe_ref, kn_ref, o_ref):
    kind = pl.program_id(0) // NSA_KV_HEADS
    n = x_ref.shape[2]
    w1 = w1_ref[0]
    pw = jnp.dot(pe_ref[0], w1, preferred_element_type=F32)
    const = pw[0:1, :CMP_HIDDEN] + pw[1:2, CMP_HIDDEN:]
    for i in range(x_ref.shape[1]):
        h = jnp.dot(x_ref[0, i], w1, preferred_element_type=F32)
        hid = h[:, :CMP_HIDDEN] + pltpu.roll(h[:, CMP_HIDDEN:], n - 1, 0) + const
        act = hid / (1.0 + jnp.exp(-hid))
        out = jnp.dot(act.astype(BF16), w2_ref[0], preferred_element_type=F32)
        normed = _rms_rows(out, kn_ref[...])
        o_ref[0, i] = jnp.where(kind == 0, normed, out).astype(BF16)


COMPRESS_SEQS = 2


def _compress(xc, c_off, w1cat, w2, pe, k_norm, n_rows):
    nb = xc.shape[1]
    per = COMPRESS_SEQS if nb % COMPRESS_SEQS == 0 else 1
    return pl.pallas_call(
        _compress_kernel,
        out_shape=jax.ShapeDtypeStruct((2 * NSA_KV_HEADS, nb, n_rows, HEAD_DIM), BF16),
        grid=(2 * NSA_KV_HEADS, nb // per),
        in_specs=[
            pl.BlockSpec((1, per, n_rows, CMP_STRIDE * HEAD_DIM), lambda c, b: (c_off + c, b, 0, 0)),
            pl.BlockSpec((1, CMP_STRIDE * HEAD_DIM, 2 * CMP_HIDDEN), lambda c, b: (c // NSA_KV_HEADS, 0, 0)),
            pl.BlockSpec((1, CMP_HIDDEN, HEAD_DIM), lambda c, b: (c // NSA_KV_HEADS, 0, 0)),
            pl.BlockSpec((1, PE_ROWS, CMP_STRIDE * HEAD_DIM), lambda c, b: (c // NSA_KV_HEADS, 0, 0)),
            pl.BlockSpec((1, HEAD_DIM), lambda c, b: (0, 0)),
        ],
        out_specs=pl.BlockSpec((1, per, n_rows, HEAD_DIM), lambda c, b: (c, b, 0, 0)),
        compiler_params=_cparams(("parallel", "parallel")),
        name="compress",
    )(xc, w1cat, w2, pe, k_norm)


def _rel_bias(dist, tbl_ref, head):
    out = jnp.full(dist.shape, tbl_ref[0, head], F32)
    for k, thr in enumerate(BUCKET_THR, start=1):
        out = jnp.where(dist >= thr, tbl_ref[k, head], out)
    return out


def _cmp_to_sel(n_cmp, n_sel, rows, cols):
    c0 = np.arange(n_cmp)[:, None] * CMP_STRIDE
    s0 = np.arange(n_sel)[None, :] * SEL_BLOCK
    inter = np.clip(np.minimum(c0 + CMP_LEN, s0 + SEL_BLOCK) - np.maximum(c0, s0), 0, None)
    m = np.zeros((rows, cols), np.float32)
    m[:n_cmp, :n_sel] = inter / CMP_LEN
    return jnp.asarray(m)


def _transpose_tiles(x):
    n = x.shape[0] // LANE
    xf = x.astype(F32)
    return jnp.concatenate([xf[i * LANE:(i + 1) * LANE].T for i in range(n)], axis=1)


def _untranspose_tiles(xt):
    n = xt.shape[1] // LANE
    return jnp.concatenate([xt[:, i * LANE:(i + 1) * LANE].T for i in range(n)], axis=0)


AUX_ROWS = 16


PREP_UNROLL = 8


def _fill_transposed(dst_ref, src_ref):
    def body(k, carry):
        st = pl.multiple_of(k * LANE, LANE)
        dst_ref[:HEAD_DIM, pl.ds(st, LANE)] = src_ref[0, 0, pl.ds(st, LANE), :].astype(F32).T.astype(BF16)
        return carry

    lax.fori_loop(0, src_ref.shape[2] // LANE, body, 0, unroll=PREP_UNROLL)


FOX_TQ = 256
FOX_KEY_BLOCK = 1024


def _split3(x):
    hi = x.astype(BF16)
    r1 = x - hi.astype(F32)
    mid = r1.astype(BF16)
    lo = (r1 - mid.astype(F32)).astype(BF16)
    return hi, mid, lo


def _lane_select3(parts, shape):
    lane = lax.broadcasted_iota(jnp.int32, shape, 1)
    hi, mid, lo = [part.astype(F32) for part in parts]
    return jnp.where(lane == 0, hi, jnp.where(lane == 1, mid, jnp.where(lane == 2, lo, 0.0))).astype(BF16)


def _ones_rows(width):
    row = lax.broadcasted_iota(jnp.int32, (AUX_ROWS, width), 0)
    return jnp.where(row == 0, 1.0, 0.0).astype(BF16)


def _finish_aug(acc):
    return acc[:HEAD_DIM] * (1.0 / acc[HEAD_DIM:HEAD_DIM + 1])


def _fox_prompt_kernel(q_ref, k_ref, v_ref, crow_ref, o_ref, vt_ref, ka_ref, s_ref):
    qt = pl.program_id(2)
    tq = FOX_TQ

    @pl.when(qt == 0)
    def _():
        _fill_transposed(vt_ref, v_ref)
        vt_ref[HEAD_DIM:, :] = _ones_rows(vt_ref.shape[1])

        def body(k, carry):
            st = pl.multiple_of(k * LANE, LANE)
            c_col = jnp.broadcast_to(crow_ref[0, 0, :, pl.ds(st, LANE)], (LANE, LANE)).T
            ka_ref[pl.ds(st, LANE), :HEAD_DIM] = k_ref[0, 0, pl.ds(st, LANE), :]
            ka_ref[pl.ds(st, LANE), HEAD_DIM:] = _lane_select3(_split3(c_col), (LANE, LANE))
            return carry

        lax.fori_loop(0, k_ref.shape[2] // LANE, body, 0, unroll=PREP_UNROLL)

    q0 = pl.multiple_of(qt * tq, tq)
    q_t = _transpose_tiles(q_ref[0]).astype(BF16)
    row = lax.broadcasted_iota(jnp.int32, (LANE, tq), 0)
    q_aug = jnp.concatenate([q_t, jnp.where(row < 3, -1.0, 0.0).astype(BF16)], axis=0)
    c_q = crow_ref[0, 0, :, pl.ds(q0, tq)]
    blk = FOX_KEY_BLOCK

    last = q0 // blk
    rel = (lax.broadcasted_iota(jnp.int32, (blk, tq), 0) - lax.broadcasted_iota(jnp.int32, (blk, tq), 1))

    def run(n_blocks):
        top = jnp.full((8, tq), NEG, F32)
        for kb in range(n_blocks):
            start = kb * blk
            s = jnp.dot(ka_ref[start:start + blk, :], q_aug, preferred_element_type=F32)
            if kb == n_blocks - 1:
                s = jnp.where(rel <= q0 - start, s, NEG)
            s_ref[start:start + blk, :] = s
            top = jnp.maximum(top, jnp.max(s.reshape(blk // 8, 8, tq), axis=0))
        m = jnp.max(top, axis=0, keepdims=True) + c_q
        shift = m - c_q
        acc = jnp.zeros((HEAD_DIM + AUX_ROWS, tq), F32)
        for kb in range(n_blocks):
            start = kb * blk
            p = jnp.exp(s_ref[start:start + blk, :] - shift).astype(BF16)
            acc = acc + jnp.dot(vt_ref[:, start:start + blk], p, preferred_element_type=F32)
        return acc

    n_max = k_ref.shape[2] // blk
    acc = lax.switch(last, [functools.partial(run, n) for n in range(1, n_max + 1)])
    o_ref[...] = _untranspose_tiles(_finish_aug(acc))


def _fox_prompt(hm, hm4, crow, nb, seq):
    nq = seq // FOX_TQ
    return pl.pallas_call(
        _fox_prompt_kernel,
        out_shape=jax.ShapeDtypeStruct((nb * seq, FOX_HEADS * HEAD_DIM), F32),
        grid=(nb, FOX_HEADS, nq),
        in_specs=[
            pl.BlockSpec((1, FOX_TQ, HEAD_DIM), lambda b, h, t: (HM_Q_FOX + h, b * nq + t, 0)),
            pl.BlockSpec((1, 1, seq, HEAD_DIM), lambda b, h, t: (HM_K_FOX + h, b, 0, 0)),
            pl.BlockSpec((1, 1, seq, HEAD_DIM), lambda b, h, t: (HM_V_FOX + h, b, 0, 0)),
            pl.BlockSpec((1, 1, 1, seq), lambda b, h, t: (b, h, 0, 0)),
        ],
        out_specs=pl.BlockSpec((FOX_TQ, HEAD_DIM), lambda b, h, t: (b * nq + t, h)),
        scratch_shapes=[pltpu.VMEM((HEAD_DIM + AUX_ROWS, seq), BF16), pltpu.VMEM((seq, 2 * HEAD_DIM), BF16),
                        pltpu.VMEM((seq, FOX_TQ), F32)],
        compiler_params=_cparams(("parallel", "parallel", "arbitrary")),
        name="fox_prompt",
    )(hm, hm4, hm4, crow)


NSA_TQ = LANE
NSA_COLS = NSA_GROUP * NSA_TQ
WIN_TILES = WINDOW // NSA_TQ + 1
SLC_BLOCK_TILES = 4
SLC_MASK_ROWS = SLC_BLOCK_TILES * NSA_TQ // SEL_BLOCK


def _nsa_prompt_kernel(tbl_ref, q_ref, kc_ref, vc_ref, mselt_ref, ks_ref, vs_ref, kw_ref, vw_ref, gate_ref, o_ref,
                       vst_ref, vwt_ref, vct_ref, wb_ref, pc_ref, drop_ref, ksa_ref, wbd_ref, s_ref, *, n_sel, cmp_back):
    g = pl.program_id(1)
    qt = pl.program_id(2)
    tq = NSA_TQ
    cols = NSA_COLS
    heads = [NSA_GROUP * g + hh for hh in range(NSA_GROUP)]
    ncp = kc_ref.shape[2]
    nj = mselt_ref.shape[0]

    @pl.when(qt == 0)
    def _():
        _fill_transposed(vst_ref, vs_ref)
        _fill_transposed(vwt_ref, vw_ref)
        vst_ref[HEAD_DIM:, :] = _ones_rows(vst_ref.shape[1])
        vwt_ref[HEAD_DIM:, :] = _ones_rows(vwt_ref.shape[1])
        vct_ref[...] = _transpose_tiles(vc_ref[0, 0]).astype(BF16)

        def fill_keys(k, carry):
            st = pl.multiple_of(k * tq, tq)
            lane = lax.broadcasted_iota(jnp.int32, (tq, tq), 1)
            blk_in_step = (k % SLC_BLOCK_TILES) * (tq // SEL_BLOCK) + lax.broadcasted_iota(jnp.int32, (tq, tq), 0) // SEL_BLOCK
            extra = (lane == blk_in_step) | ((lane >= SLC_MASK_ROWS) & (lane < SLC_MASK_ROWS + 3))
            ksa_ref[pl.ds(st, tq), :HEAD_DIM] = ks_ref[0, 0, pl.ds(st, tq), :]
            ksa_ref[pl.ds(st, tq), HEAD_DIM:] = jnp.where(extra, 1.0, 0.0).astype(BF16)
            return carry

        lax.fori_loop(0, ks_ref.shape[2] // tq, fill_keys, 0, unroll=PREP_UNROLL)
        key = lax.broadcasted_iota(jnp.int32, (tq, tq), 0)
        qry = lax.broadcasted_iota(jnp.int32, (tq, tq), 1)
        blk = lax.broadcasted_iota(jnp.int32, (pc_ref.shape[0], tq), 0) - cmp_back
        d_cmp = lax.broadcasted_iota(jnp.int32, (pc_ref.shape[0], tq), 1) - (blk * CMP_STRIDE + (CMP_LEN - 1))
        for hh in range(NSA_GROUP):
            sl = slice(hh * tq, (hh + 1) * tq)
            far = tbl_ref[REL_BUCKETS - 1, heads[hh]]
            for delta in range(2):
                d = delta * tq + qry - key
                wb_ref[delta, :, sl] = jnp.where(d >= 0, _rel_bias(jnp.maximum(d, 0), tbl_ref, heads[hh]), NEG)
            for delta in range(2, WIN_TILES - 1):
                wb_ref[delta, :, sl] = jnp.full((tq, tq), far, F32)
            wb_ref[WIN_TILES - 1, :, sl] = jnp.where(qry < key, far, NEG)
            wb_ref[WIN_TILES, :, sl] = jnp.full((tq, tq), NEG, F32)
            for delta in range(2):
                wbd_ref[delta, :, sl] = wb_ref[delta, :, sl] - far
            wbd_ref[2, :, sl] = jnp.zeros((tq, tq), F32)
            wbd_ref[3, :, sl] = jnp.full((tq, tq), NEG, F32)
            pc_ref[:, sl] = jnp.where(d_cmp >= 0, _rel_bias(jnp.maximum(d_cmp, 0), tbl_ref, heads[hh]), NEG)

    q_t = jnp.concatenate([q_ref[hh].astype(F32).T for hh in range(NSA_GROUP)], axis=1).astype(BF16)

    off = pl.multiple_of(cmp_back - qt * (tq // CMP_STRIDE), 8)
    s_t = jnp.dot(kc_ref[0, 0], q_t, preferred_element_type=F32) + pc_ref[pl.ds(off, ncp), :]
    m = jnp.max(s_t, axis=0, keepdims=True)
    e = jnp.exp(s_t - m)
    inv = jnp.where(m > 0.5 * NEG, 1.0 / jnp.sum(e, axis=0, keepdims=True), 0.0)
    p_t = e * inv
    o_cmp = jnp.dot(vct_ref[...], p_t.astype(BF16), preferred_element_type=F32)
    p_sum = p_t[:, :tq]
    for hh in range(1, NSA_GROUP):
        p_sum = p_sum + p_t[:, hh * tq:(hh + 1) * tq]
    imp = jnp.dot(mselt_ref[...], p_sum, preferred_element_type=F32, precision=lax.Precision.HIGHEST)

    jrow = lax.broadcasted_iota(jnp.int32, (nj, tq), 0)
    pos = qt * tq + lax.broadcasted_iota(jnp.int32, (nj, tq), 1)
    blk_q = jnp.right_shift(pos, int(math.log2(SEL_BLOCK)))
    forced = (jrow == 0) | (jrow == blk_q) | (jrow == blk_q - 1)
    score = jnp.where(forced, FORCED_SCORE, jnp.where(jrow <= blk_q, imp, -1.0))
    score = jnp.where(jrow < n_sel, score, -2.0)
    ranks = []
    for r in range(nj // 8):
        mine = score[r * 8:(r + 1) * 8]
        jmine = jrow[r * 8:(r + 1) * 8]
        rank = jnp.zeros((8, tq), F32)
        for other in range(n_sel):
            row = jnp.broadcast_to(score[other:other + 1], (8, tq))
            if other < r * 8:
                beats = row >= mine
            elif other >= (r + 1) * 8:
                beats = row > mine
            else:
                beats = (row > mine) | ((row == mine) & (jmine > other))
            rank = rank + jnp.where(beats, 1.0, 0.0)
        ranks.append(rank)
    rank = jnp.concatenate(ranks, axis=0)
    drop = jnp.where((rank < SEL_TOPK) & (jrow < n_sel), 0.0, NEG)
    drop_ref[...] = jnp.concatenate([drop] * NSA_GROUP, axis=1)

    blk = SLC_BLOCK_TILES * tq
    col = lax.broadcasted_iota(jnp.int32, (8, cols), 1)
    far_row = jnp.full((8, cols), tbl_ref[REL_BUCKETS - 1, heads[NSA_GROUP - 1]], F32)
    for hh in range(NSA_GROUP - 1):
        far_row = jnp.where(col // tq == hh, tbl_ref[REL_BUCKETS - 1, heads[hh]], far_row)
    row8 = lax.broadcasted_iota(jnp.int32, (8, cols), 0)
    parts = [part.astype(F32) for part in _split3(far_row)]
    far_rows = jnp.where(row8 == 0, parts[0], jnp.where(row8 == 1, parts[1], jnp.where(row8 == 2, parts[2], 0.0)))
    pad_rows = jnp.zeros((HEAD_DIM - SLC_MASK_ROWS - 8, cols), BF16)

    last = qt // SLC_BLOCK_TILES

    def slc_run(n_blocks):
        top = jnp.full((8, cols), NEG, F32)
        for kb in range(n_blocks):
            start = kb * blk
            masks = drop_ref[kb * SLC_MASK_ROWS:(kb + 1) * SLC_MASK_ROWS, :]
            extra = jnp.concatenate([masks, far_rows], axis=0).astype(BF16)
            q_aug = jnp.concatenate([q_t, extra, pad_rows], axis=0)
            s = jnp.dot(ksa_ref[start:start + blk, :], q_aug, preferred_element_type=F32)
            if kb >= n_blocks - 2:
                terms = []
                for i in range(SLC_BLOCK_TILES):
                    delta = qt - (kb * SLC_BLOCK_TILES + i)
                    terms.append(wbd_ref[jnp.where(delta < 0, 3, jnp.minimum(delta, 2))])
                s = s + jnp.concatenate(terms, axis=0)
            s_ref[start:start + blk, :] = s
            top = jnp.maximum(top, jnp.max(s.reshape(blk // 8, 8, cols), axis=0))
        m = jnp.max(top, axis=0, keepdims=True)
        acc = jnp.zeros((HEAD_DIM + AUX_ROWS, cols), F32)
        for kb in range(n_blocks):
            start = kb * blk
            p = jnp.exp(s_ref[start:start + blk, :] - m).astype(BF16)
            acc = acc + jnp.dot(vst_ref[:, start:start + blk], p, preferred_element_type=F32)
        return acc

    n_max = ks_ref.shape[2] // blk
    o_slc = _finish_aug(lax.switch(last, [functools.partial(slc_run, n) for n in range(1, n_max + 1)]))

    w0 = jnp.maximum(qt - (WIN_TILES - 1), 0)
    start = pl.multiple_of(w0 * tq, tq)
    span = WIN_TILES * tq
    s = jnp.dot(kw_ref[0, 0, pl.ds(start, span), :], q_t, preferred_element_type=F32)
    terms = []
    for i in range(WIN_TILES):
        delta = qt - (w0 + i)
        terms.append(wb_ref[jnp.where(delta < 0, WIN_TILES, delta)])
    s = s + jnp.concatenate(terms, axis=0)
    p = jnp.exp(s - jnp.max(s, axis=0, keepdims=True))
    o_win = _finish_aug(jnp.dot(vwt_ref[:, pl.ds(start, span)], p.astype(BF16), preferred_element_type=F32))

    gate = gate_ref[0, 0, 0]
    o_t = gate[0:1] * o_cmp + gate[1:2] * o_slc + gate[2:3] * o_win
    for hh in range(NSA_GROUP):
        o_ref[:, hh * HEAD_DIM:(hh + 1) * HEAD_DIM] = o_t[:, hh * tq:(hh + 1) * tq].T


def _nsa_prompt(rel_table, hm, hm4, kcv, mselt, gates_t, nb, seq, n_sel):
    tq = NSA_TQ
    nq = seq // tq
    ncp = kcv.shape[2]
    nj = mselt.shape[0]
    cmp_back = (nq - 1) * (tq // CMP_STRIDE)
    kern = functools.partial(_nsa_prompt_kernel, n_sel=n_sel, cmp_back=cmp_back)
    kv_spec = lambda head0: pl.BlockSpec((1, 1, seq, HEAD_DIM), lambda b, g, t: (head0 + g, b, 0, 0))
    return pl.pallas_call(
        kern,
        out_shape=jax.ShapeDtypeStruct((nb * seq, NSA_HEADS * HEAD_DIM), F32),
        grid=(nb, NSA_KV_HEADS, nq),
        in_specs=[
            pl.BlockSpec(memory_space=pltpu.SMEM),
            pl.BlockSpec((NSA_GROUP, tq, HEAD_DIM), lambda b, g, t: (g, b * nq + t, 0)),
            pl.BlockSpec((1, 1, ncp, HEAD_DIM), lambda b, g, t: (g, b, 0, 0)),
            pl.BlockSpec((1, 1, ncp, HEAD_DIM), lambda b, g, t: (NSA_KV_HEADS + g, b, 0, 0)),
            pl.BlockSpec((nj, ncp), lambda b, g, t: (0, 0)),
            kv_spec(HM_K_SLC), kv_spec(HM_V_SLC), kv_spec(HM_K_WIN), kv_spec(HM_V_WIN),
            pl.BlockSpec((1, 1, 1, 8, NSA_COLS), lambda b, g, t: (b, g, t, 0, 0)),
        ],
        out_specs=pl.BlockSpec((tq, NSA_GROUP * HEAD_DIM), lambda b, g, t: (b * nq + t, g)),
        scratch_shapes=[
            pltpu.VMEM((HEAD_DIM + AUX_ROWS, seq), BF16),
            pltpu.VMEM((HEAD_DIM + AUX_ROWS, seq), BF16),
            pltpu.VMEM((HEAD_DIM, ncp), BF16),
            pltpu.VMEM((WIN_TILES + 1, tq, NSA_COLS), F32),
            pltpu.VMEM((cmp_back + ncp, NSA_COLS), F32),
            pltpu.VMEM((nj, NSA_COLS), F32),
            pltpu.VMEM((seq, 2 * HEAD_DIM), BF16),
            pltpu.VMEM((4, tq, NSA_COLS), F32),
            pltpu.VMEM((seq, NSA_COLS), F32),
        ],
        compiler_params=_cparams(("parallel", "parallel", "arbitrary")),
        name="nsa_prompt",
    )(rel_table, hm, kcv, kcv, mselt, hm4, hm4, hm4, hm4, gates_t)


CHUNK_ROWS = 512


def _chunkify_kernel(x_ref, o_ref, *, n_cols):
    n = o_ref.shape[2]
    for c in range(o_ref.shape[0]):
        for s in range(CMP_STRIDE):
            o_ref[c, 0, :, s * HEAD_DIM:(s + 1) * HEAD_DIM] = (
                x_ref[0, pl.ds(s * n_cols + c, n, stride=CMP_STRIDE * n_cols), :].astype(BF16))


def _chunkify(rows3, n_cols, n_heads, seq):
    nb = rows3.shape[0]
    tr = _largest_tile(seq, (CHUNK_ROWS, 256))
    return pl.pallas_call(
        functools.partial(_chunkify_kernel, n_cols=n_cols),
        out_shape=jax.ShapeDtypeStruct((n_heads, nb, seq // CMP_STRIDE, CMP_STRIDE * HEAD_DIM), BF16),
        grid=(nb, seq // tr),
        in_specs=[pl.BlockSpec((1, tr * n_cols, HEAD_DIM), lambda b, i: (b, i, 0))],
        out_specs=pl.BlockSpec((n_heads, 1, tr // CMP_STRIDE, CMP_STRIDE * HEAD_DIM), lambda b, i: (0, b, i, 0)),
        compiler_params=_cparams(("parallel", "parallel")),
        name="chunkify",
    )(rows3)


REGROUP_PAGES = 8


N_CACHE_COLS = 16
N_CMP_COLS = 2 * NSA_KV_HEADS


def _regroup_kernel(pt_ref, *refs, n_steps):
    del pt_ref
    npg = REGROUP_PAGES
    nsa_in, lf_in = refs[:npg], refs[npg:2 * npg]
    xc_out, slc_out, lf_out = refs[2 * npg:]
    is_tail = pl.program_id(1) >= n_steps
    chunks = PAGE_SIZE // CMP_STRIDE
    chunk_stride = CMP_STRIDE * N_CACHE_COLS

    @pl.when(is_tail)
    def _():
        slc_out[...] = jnp.zeros(slc_out.shape, slc_out.dtype)
        lf_out[...] = jnp.zeros(lf_out.shape, lf_out.dtype)

    @pl.when(jnp.logical_not(is_tail))
    def _():
        for p in range(npg):
            rows = slice(p * PAGE_SIZE, (p + 1) * PAGE_SIZE)
            for c in range(N_CACHE_COLS - N_CMP_COLS):
                slc_out[c, 0, rows, :] = nsa_in[p][0, pl.ds(N_CMP_COLS + c, PAGE_SIZE, stride=N_CACHE_COLS), :].astype(BF16)
            lf_out[0, rows, :] = lf_in[p][0]
        for pair in range(npg // 2):
            for c in range(N_CMP_COLS):
                for s in range(CMP_STRIDE):
                    first = s * N_CACHE_COLS + c
                    both = [nsa_in[2 * pair + i][0, pl.ds(first, chunks, stride=chunk_stride), :] for i in range(2)]
                    xc_out[c, 0, pair * 2 * chunks:(pair + 1) * 2 * chunks, s * HEAD_DIM:(s + 1) * HEAD_DIM] = (
                        jnp.concatenate(both, axis=0).astype(BF16))


def _regroup(page_table, cache_nsa, cache_logf, lk):
    nb, n_pages = page_table.shape
    npg = REGROUP_PAGES
    n_steps = n_pages // npg
    rows = npg * PAGE_SIZE
    last = n_steps - 1
    n_slc = N_CACHE_COLS - N_CMP_COLS
    chunk_w = CMP_STRIDE * HEAD_DIM

    def page_map(p):
        return lambda b, s, pt: (pt[b, jnp.minimum(s, last) * npg + p], 0, 0)

    def specs(arr):
        return [pl.BlockSpec((1,) + arr.shape[1:], page_map(p)) for p in range(npg)]

    assert (lk - n_pages * PAGE_SIZE) % rows == 0
    n_tail = (lk - n_pages * PAGE_SIZE) // rows
    grid_spec = pltpu.PrefetchScalarGridSpec(
        num_scalar_prefetch=1,
        grid=(nb, n_steps + n_tail),
        in_specs=specs(cache_nsa) + specs(cache_logf),
        out_specs=(
            pl.BlockSpec((N_CMP_COLS, 1, rows // CMP_STRIDE, chunk_w), lambda b, s, pt: (0, b, jnp.minimum(s, last), 0)),
            pl.BlockSpec((n_slc, 1, rows, HEAD_DIM), lambda b, s, pt: (0, b, s, 0)),
            pl.BlockSpec((1, rows, cache_logf.shape[2]), lambda b, s, pt: (b, s, 0)),
        ),
    )
    return pl.pallas_call(
        functools.partial(_regroup_kernel, n_steps=n_steps),
        out_shape=(
            jax.ShapeDtypeStruct((N_CMP_COLS, nb, n_pages * PAGE_SIZE // CMP_STRIDE, chunk_w), BF16),
            jax.ShapeDtypeStruct((n_slc, nb, lk, HEAD_DIM), BF16),
            jax.ShapeDtypeStruct((nb, lk, cache_logf.shape[2]), F32),
        ),
        grid_spec=grid_spec,
        compiler_params=_cparams(("parallel", "arbitrary")),
        name="cache_regroup",
    )(page_table, *([cache_nsa] * npg), *([cache_logf] * npg))


FOX_DEC_PAGES = 8


def _fox_decode_kernel(pt_ref, *refs, dseq):
    del pt_ref
    npg = FOX_DEC_PAGES
    pages = refs[:npg]
    qbd_ref, ccol_ref, cq_ref, knew_ref, vnew_ref, o_ref, m_ref, acc_ref = refs[npg:]
    step = pl.program_id(1)
    is_tail = step == pl.num_programs(1) - 1
    width = FOX_HEADS * HEAD_DIM

    @pl.when(step == 0)
    def _():
        m_ref[...] = jnp.full(m_ref.shape, NEG, F32)
        acc_ref[...] = jnp.zeros(acc_ref.shape, F32)

    def fold(k_blk, v_blk, bias):
        n = k_blk.shape[0]
        s_t = jnp.dot(k_blk, qbd_ref[0], preferred_element_type=F32) + bias
        m_new = jnp.maximum(m_ref[...], jnp.max(s_t, axis=0, keepdims=True))
        alpha = jnp.exp(m_ref[...] - m_new)
        p = jnp.exp(s_t - m_new)
        p_t = jnp.concatenate([p[i * LANE:(i + 1) * LANE].T for i in range(n // LANE)], axis=1).astype(BF16)
        v_aug = jnp.concatenate([v_blk, jnp.ones((n, LANE), BF16)], axis=1)
        upd = jnp.dot(p_t, v_aug, preferred_element_type=F32)
        alpha_col = jnp.broadcast_to(alpha, (LANE, LANE)).T
        acc_ref[...] = acc_ref[...] * jnp.concatenate([alpha_col] * (width // LANE + 1), axis=1) + upd
        m_ref[...] = m_new

    @pl.when(jnp.logical_not(is_tail))
    def _():
        def heads_of(p, first):
            cols = [pages[p][0, pl.ds(first + h, PAGE_SIZE, stride=N_CACHE_COLS), :] for h in range(FOX_HEADS)]
            return jnp.concatenate(cols, axis=1).astype(BF16)

        k_blk = jnp.concatenate([heads_of(p, 0) for p in range(npg)], axis=0)
        v_blk = jnp.concatenate([heads_of(p, FOX_HEADS) for p in range(npg)], axis=0)
        fold(k_blk, v_blk, cq_ref[0] - ccol_ref[0])

    @pl.when(is_tail)
    def _():
        row = lax.broadcasted_iota(jnp.int32, (LANE, LANE), 0)
        t_of_col = lax.broadcasted_iota(jnp.int32, (LANE, LANE), 1) % dseq
        bias = jnp.where(row <= t_of_col, cq_ref[0] - ccol_ref[0, :LANE, :], NEG)
        fold(knew_ref[0], vnew_ref[0], bias)
        acc = acc_ref[...]
        inv = 1.0 / acc[:, width:]
        o_ref[0] = acc[:, :width] * jnp.concatenate([inv] * (width // LANE), axis=1)


def _fox_decode(page_table, cache_fox, qbd, ccols, cq, knew, vnew, dseq):
    nb, n_pages = page_table.shape
    npg = FOX_DEC_PAGES
    n_steps = n_pages // npg
    last = n_steps - 1
    width = FOX_HEADS * HEAD_DIM

    def page_map(p):
        return lambda b, s, pt: (pt[b, jnp.minimum(s, last) * npg + p], 0, 0)

    per_b = lambda shape: pl.BlockSpec((1,) + shape, lambda b, s, pt: (b, 0, 0))
    grid_spec = pltpu.PrefetchScalarGridSpec(
        num_scalar_prefetch=1,
        grid=(nb, n_steps + 1),
        in_specs=[pl.BlockSpec((1,) + cache_fox.shape[1:], page_map(p)) for p in range(npg)] + [
            per_b((width, LANE)),
            pl.BlockSpec((1, npg * PAGE_SIZE, LANE), lambda b, s, pt: (b, s, 0)),
            per_b((1, LANE)),
            per_b((LANE, width)),
            per_b((LANE, width)),
        ],
        out_specs=per_b((LANE, width)),
        scratch_shapes=[pltpu.VMEM((1, LANE), F32), pltpu.VMEM((LANE, width + LANE), F32)],
    )
    return pl.pallas_call(
        functools.partial(_fox_decode_kernel, dseq=dseq),
        out_shape=jax.ShapeDtypeStruct((nb, LANE, width), F32),
        grid_spec=grid_spec,
        compiler_params=_cparams(("parallel", "arbitrary")),
        name="fox_decode",
    )(page_table, *([cache_fox] * npg), qbd, ccols, cq, knew, vnew)


NSA_DEC_KEYS = 2048


def _lane_transpose(p):
    return jnp.concatenate([p[i * LANE:(i + 1) * LANE].T for i in range(p.shape[0] // LANE)], axis=1).astype(BF16)


def _nsa_decode_kernel(qbd_ref, kc_ref, vc_ref, mselt_ref, pair_ref, tcol_ref, ks_ref, vs_ref, kw_ref, vw_ref,
                       gate_ref, o_ref, drop_ref, m_ref, acc_ref, ocmp_ref, *, dseq, past, n_sel):
    step = pl.program_id(1)
    n_tiles = pl.num_programs(1)
    width = NSA_KV_HEADS * HEAD_DIM
    qbd = qbd_ref[0]
    col = lax.broadcasted_iota(jnp.int32, (1, LANE), 1)
    q_pos = past + col % dseq

    def side_by_side(ref, rows=None):
        parts = [ref[g, 0] if rows is None else ref[g, 0, rows, :] for g in range(NSA_KV_HEADS)]
        return jnp.concatenate(parts, axis=1)

    def rel_bias_cols(dist):
        out = jnp.broadcast_to(tcol_ref[0:1, :], dist.shape)
        for k, thr in enumerate(BUCKET_THR, start=1):
            out = jnp.where(dist >= thr, tcol_ref[k:k + 1, :], out)
        return out

    def softmax_pv(s_t, v_all):
        m = jnp.max(s_t, axis=0, keepdims=True)
        e = jnp.exp(s_t - m)
        inv = jnp.where(m > 0.5 * NEG, 1.0 / jnp.sum(e, axis=0, keepdims=True), 0.0)
        p = e * inv
        return p, jnp.dot(_lane_transpose(p), v_all, preferred_element_type=F32)

    @pl.when(step == 0)
    def _():
        m_ref[...] = jnp.full(m_ref.shape, NEG, F32)
        acc_ref[...] = jnp.zeros(acc_ref.shape, F32)
        ncp = kc_ref.shape[2]
        blk_end = lax.broadcasted_iota(jnp.int32, (ncp, LANE), 0) * CMP_STRIDE + (CMP_LEN - 1)
        d = q_pos - blk_end
        s_t = jnp.dot(side_by_side(kc_ref), qbd, preferred_element_type=F32)
        s_t = jnp.where(d >= 0, s_t + rel_bias_cols(jnp.maximum(d, 0)), NEG)
        p, ocmp_ref[...] = softmax_pv(s_t, side_by_side(vc_ref))
        imp = jnp.dot(mselt_ref[...], p, preferred_element_type=F32, precision=lax.Precision.HIGHEST)
        imp = jnp.dot(imp, pair_ref[...], preferred_element_type=F32, precision=lax.Precision.HIGHEST)
        nj = imp.shape[0]
        jrow = lax.broadcasted_iota(jnp.int32, (nj, LANE), 0)
        blk_q = jnp.right_shift(q_pos, int(math.log2(SEL_BLOCK)))
        forced = (jrow == 0) | (jrow == blk_q) | (jrow == blk_q - 1)
        score = jnp.where(forced, FORCED_SCORE, jnp.where(jrow <= blk_q, imp, -1.0))
        score = jnp.where(jrow < n_sel, score, -2.0)
        ranks = []
        for r in range(-(-n_sel // 8)):
            mine = score[r * 8:(r + 1) * 8]
            jmine = jrow[r * 8:(r + 1) * 8]
            rank = jnp.zeros((8, LANE), F32)
            for other in range(n_sel):
                row = jnp.broadcast_to(score[other:other + 1], (8, LANE))
                if other < r * 8:
                    beats = row >= mine
                elif other >= (r + 1) * 8:
                    beats = row > mine
                else:
                    beats = (row > mine) | ((row == mine) & (jmine > other))
                rank = rank + jnp.where(beats, 1.0, 0.0)
            ranks.append(rank)
        if nj > 8 * len(ranks):
            ranks.append(jnp.full((nj - 8 * len(ranks), LANE), float(SEL_TOPK), F32))
        rank = jnp.concatenate(ranks, axis=0)
        far = tcol_ref[REL_BUCKETS - 1:REL_BUCKETS, :]
        drop_ref[...] = jnp.where((rank < SEL_TOPK) & (jrow <= blk_q), far, NEG)

    tile = ks_ref.shape[2]
    per_tile = tile // SEL_BLOCK
    start = step * tile
    first_blk = pl.multiple_of(step * per_tile, 8)
    rows = [jnp.broadcast_to(drop_ref[pl.ds(first_blk + i, 1), :], (SEL_BLOCK, LANE)) for i in range(per_tile)]
    s_t = jnp.dot(side_by_side(ks_ref), qbd, preferred_element_type=F32) + jnp.concatenate(rows, axis=0)

    def near_fix(s_t):
        def fix(rows_at, s_rows):
            key_pos = start + rows_at + lax.broadcasted_iota(jnp.int32, (LANE, LANE), 0)
            d = q_pos - key_pos
            far = tcol_ref[REL_BUCKETS - 1:REL_BUCKETS, :]
            return jnp.where(d >= 0, s_rows + (rel_bias_cols(jnp.maximum(d, 0)) - far), NEG)

        head = fix(0, s_t[:LANE])
        tail = fix(tile - LANE, s_t[tile - LANE:])
        return jnp.concatenate([head, s_t[LANE:tile - LANE], tail], axis=0)

    is_near = (start + tile > past - FAR_DIST)
    s_t = lax.cond(is_near, near_fix, lambda s: s, s_t)
    m_new = jnp.maximum(m_ref[...], jnp.max(s_t, axis=0, keepdims=True))
    alpha = jnp.exp(m_ref[...] - m_new)
    p = jnp.exp(s_t - m_new)
    v_aug = jnp.concatenate([side_by_side(vs_ref), jnp.ones((tile, LANE), BF16)], axis=1)
    upd = jnp.dot(_lane_transpose(p), v_aug, preferred_element_type=F32)
    alpha_col = jnp.broadcast_to(alpha, (LANE, LANE)).T
    acc_ref[...] = acc_ref[...] * jnp.concatenate([alpha_col] * (width // LANE + 1), axis=1) + upd
    m_ref[...] = m_new

    @pl.when(step == n_tiles - 1)
    def _():
        acc = acc_ref[...]
        o_slc = acc[:, :width] * jnp.concatenate([1.0 / acc[:, width:]] * (width // LANE), axis=1)
        span = kw_ref.shape[2]
        key_pos = (past - WINDOW) + lax.broadcasted_iota(jnp.int32, (span, LANE), 0)
        d = q_pos - key_pos
        s_w = jnp.dot(side_by_side(kw_ref), qbd, preferred_element_type=F32)
        s_w = jnp.where((d >= 0) & (d < WINDOW), s_w + rel_bias_cols(jnp.maximum(d, 0)), NEG)
        _, o_win = softmax_pv(s_w, side_by_side(vw_ref))
        g = gate_ref[0]
        tile4 = lambda a: jnp.concatenate([a] * (width // LANE), axis=1)
        o_ref[0] = tile4(g[0]) * ocmp_ref[...] + tile4(g[1]) * o_slc + tile4(g[2]) * o_win


def _nsa_decode(qbd, kcv, mselt, pair, tcols, nsa_dec, win_dec, gcols, *, dseq, past, n_sel):
    nb = qbd.shape[0]
    ncp = kcv.shape[2]
    nj = mselt.shape[0]
    span = win_dec.shape[2]
    tile = NSA_DEC_KEYS
    n_tiles = -(-nsa_dec.shape[2] // tile)
    width = NSA_KV_HEADS * HEAD_DIM
    kern = functools.partial(_nsa_decode_kernel, dseq=dseq, past=past, n_sel=n_sel)
    grp = lambda rows, half, tiled: pl.BlockSpec(
        (NSA_KV_HEADS, 1, rows, HEAD_DIM), (lambda b, s: (half, b, s, 0)) if tiled else (lambda b, s: (half, b, 0, 0)))
    const = lambda shape: pl.BlockSpec(shape, lambda b, s: (0,) * len(shape))
    return pl.pallas_call(
        kern,
        out_shape=jax.ShapeDtypeStruct((nb, LANE, width), F32),
        grid=(nb, n_tiles),
        in_specs=[
            pl.BlockSpec((1, width, LANE), lambda b, s: (b, 0, 0)),
            grp(ncp, 0, False), grp(ncp, 1, False),
            const((nj, ncp)), const((LANE, LANE)), const((REL_BUCKETS, LANE)),
            grp(tile, 0, True), grp(tile, 1, True),
            grp(span, 0, False), grp(span, 1, False),
            pl.BlockSpec((1, 3, LANE, LANE), lambda b, s: (b, 0, 0, 0)),
        ],
        out_specs=pl.BlockSpec((1, LANE, width), lambda b, s: (b, 0, 0)),
        scratch_shapes=[
            pltpu.VMEM((nj, LANE), F32),
            pltpu.VMEM((1, LANE), F32),
            pltpu.VMEM((LANE, width + LANE), F32),
            pltpu.VMEM((LANE, width), F32),
        ],
        compiler_params=_cparams(("parallel", "arbitrary")),
        name="nsa_decode",
    )(qbd, kcv, kcv, mselt, pair, tcols, nsa_dec, nsa_dec, win_dec, win_dec, gcols)


def _largest_tile(n, candidates):
    for c in candidates:
        if n % c == 0:
            return c
    raise ValueError(f"no tile in {candidates} divides {n}")


FFN_ROW_TILES = (1024, 512, 256, 128)
ROW_TILES = (512, 256, 128)


def _token_stage_in(x, p):
    m = x.shape[0]
    tm = m if m < ROW_TILES[-1] else _largest_tile(m, ROW_TILES)
    tm_ffn = m if m < FFN_ROW_TILES[-1] else _largest_tile(m, FFN_ROW_TILES)
    x1 = _ffn(x, p["norm_ffn1"], p["wg1"], p["wu1"], p["wd1"], tm_ffn, p["tf"])
    tm_in = m if m < ROW_TILES[-1] else _largest_tile(m, ROW_TILES[1:])
    nsa_rows, win_rows, fox_rows, hm = _inproj(x1, p["norm_mix"], p["w_main"], p["colgain"], tm_in)
    small = _small(x1, p["norm_mix"], p["w_small"], p["b_small"], tm)
    return x1, nsa_rows, win_rows, fox_rows, hm, small


def _token_stage_out(x1, o_nsa, o_fox, p):
    m = x1.shape[0]
    tm = m if m < ROW_TILES[-1] else _largest_tile(m, ROW_TILES[1:])
    tm_ffn = m if m < FFN_ROW_TILES[-1] else _largest_tile(m, FFN_ROW_TILES)
    x2 = _outproj(o_nsa, o_fox, p["out_norm_nsa"], p["out_norm_fox"], p["w_out"], x1, tm)
    return _ffn(x2, p["norm_ffn2"], p["wg2"], p["wu2"], p["wd2"], tm_ffn, p["tf"])


def kernel(x_prompt, x_sample, cache_nsa_kv, cache_fox_kv, cache_fox_logf, state_win_kv, page_table, rel_table, norm_ffn1, ffn1_gate, ffn1_up, ffn1_down, norm_mix, w_in, nsa_gate_bias, fox_forget_bias, q_norm_nsa, k_norm_nsa, q_norm_fox, k_norm_fox, cmp_pos_k, cmp_w1_k, cmp_w2_k, cmp_pos_v, cmp_w1_v, cmp_w2_v, out_norm_nsa, out_norm_fox, w_out, norm_ffn2, ffn2_gate, ffn2_up, ffn2_down):
    depth = w_in.shape[0]
    assert depth == 1, "single-layer trunk"
    nbp, seq, d_model = x_prompt.shape
    nbd, dseq, _ = x_sample.shape
    n_pages = page_table.shape[1]
    past = n_pages * PAGE_SIZE
    d_ff = ffn1_gate.shape[2]
    nsa_w = NSA_HEADS * HEAD_DIM
    kv6_w = 6 * NSA_KV_HEADS * HEAD_DIM
    fox_w = 3 * FOX_HEADS * HEAD_DIM
    off_gate = nsa_w + kv6_w
    off_fox = off_gate + N_GATE_COLS
    off_forget = off_fox + fox_w
    assert w_in.shape[2] == off_forget + FOX_HEADS and d_model == nsa_w + FOX_HEADS * HEAD_DIM
    assert seq % LANE == 0 and seq >= WINDOW and past % LANE == 0 and n_pages % REGROUP_PAGES == 0
    assert dseq <= 16 and state_win_kv.shape[2] == WINDOW
    assert seq % FOX_KEY_BLOCK == 0 and seq % (SLC_BLOCK_TILES * NSA_TQ) == 0 and seq >= WIN_TILES * NSA_TQ

    w0 = w_in[0]
    ones = lambda n: jnp.ones((n,), F32)
    zeros = lambda n: jnp.zeros((n,), F32)
    kn, kvw = NSA_KV_HEADS, NSA_KV_HEADS * HEAD_DIM
    qk_scale = HEAD_DIM ** -0.5
    p = {
        "tf": _largest_tile(d_ff, (512, 256, 128)),
        "norm_ffn1": norm_ffn1[0][None], "norm_mix": norm_mix[0][None], "norm_ffn2": norm_ffn2[0][None],
        "wg1": ffn1_gate[0].astype(BF16), "wu1": ffn1_up[0].astype(BF16), "wd1": ffn1_down[0].astype(BF16),
        "wg2": ffn2_gate[0].astype(BF16), "wu2": ffn2_up[0].astype(BF16), "wd2": ffn2_down[0].astype(BF16),
        "w_main": jnp.concatenate([w0[:, :off_gate], w0[:, off_fox:off_forget]], axis=1).astype(BF16),
        "w_small": jnp.concatenate([w0[:, off_gate:off_fox], w0[:, off_forget:],
                                    jnp.zeros((d_model, LANE - N_GATE_COLS - FOX_HEADS), F32)], axis=1).astype(BF16),
        "b_small": jnp.concatenate([nsa_gate_bias[0].reshape(-1), fox_forget_bias[0],
                                    zeros(LANE - N_GATE_COLS - FOX_HEADS)])[None],
        "colgain": jnp.concatenate([
            jnp.tile(q_norm_nsa[0] * qk_scale, NSA_HEADS), ones(2 * kvw), jnp.tile(k_norm_nsa[0], kn), ones(kvw),
            jnp.tile(k_norm_nsa[0], kn), ones(kvw), jnp.tile(q_norm_fox[0] * qk_scale, FOX_HEADS),
            jnp.tile(k_norm_fox[0], FOX_HEADS), ones(FOX_HEADS * HEAD_DIM)])[None],
        "out_norm_nsa": out_norm_nsa[0][None], "out_norm_fox": out_norm_fox[0][None],
        "w_out": w_out[0].astype(BF16),
    }
    half = CMP_STRIDE * HEAD_DIM

    def cmp_w1(w):
        return jnp.concatenate([w[0, :half], w[0, half:]], axis=1)

    def cmp_pe(pe):
        return jnp.concatenate([pe[0].reshape(CMP_LEN // CMP_STRIDE, half), jnp.zeros((PE_ROWS - CMP_LEN // CMP_STRIDE, half), F32)], axis=0)

    w1cat = jnp.stack([cmp_w1(cmp_w1_k), cmp_w1(cmp_w1_v)]).astype(BF16)
    w2cat = jnp.stack([cmp_w2_k[0], cmp_w2_v[0]]).astype(BF16)
    pecat = jnp.stack([cmp_pe(cmp_pos_k), cmp_pe(cmp_pos_v)]).astype(BF16)
    k_norm_row = k_norm_nsa[0][None]

    mp = nbp * seq
    x1, nsa_rows, win_rows, fox_rows, hm, small = _token_stage_in(x_prompt.reshape(mp, d_model), p)
    hm4 = hm.reshape(N_HEAD_COLS, nbp, seq, HEAD_DIM)

    logf = small[:, N_GATE_COLS:N_GATE_COLS + FOX_HEADS]
    csum = _cumsum(logf.reshape(nbp, seq, FOX_HEADS).transpose(0, 2, 1))
    o_fox = _fox_prompt(hm, hm4, csum[:, :, None, :], nbp, seq)

    n_chunk = seq // CMP_STRIDE
    n_cmp = (seq - CMP_LEN) // CMP_STRIDE + 1
    n_sel = -(-seq // SEL_BLOCK)
    xc = _chunkify(nsa_rows.reshape(nbp, seq * N_CACHE_COLS, HEAD_DIM), N_CACHE_COLS, N_CMP_COLS, seq)
    kcv = _compress(xc, 0, w1cat, w2cat, pecat, k_norm_row, n_chunk)
    mselt = _cmp_to_sel(n_cmp, n_sel, n_chunk, -(-n_sel // 8) * 8).T
    nq = seq // NSA_TQ
    gates_t = small[:, :N_GATE_COLS].reshape(nbp, nq, NSA_TQ, NSA_KV_HEADS, NSA_GROUP, 3)
    gates_t = gates_t.transpose(0, 3, 1, 5, 4, 2).reshape(nbp, NSA_KV_HEADS, nq, 3, NSA_COLS)
    gates_t = jnp.pad(gates_t, ((0, 0), (0, 0), (0, 0), (0, 8 - 3), (0, 0)))
    o_nsa = _nsa_prompt(rel_table, hm, hm4, kcv, mselt, gates_t, nbp, seq, n_sel)
    y_p = _token_stage_out(x1, o_nsa, o_fox, p)

    ms = nbd * dseq
    lk = past + NSA_DEC_KEYS
    assert n_pages % FOX_DEC_PAGES == 0 and past % NSA_DEC_KEYS == 0 and FOX_HEADS * dseq <= LANE
    xs1, nsa_rows_s, win_rows_s, fox_rows_s, hm_s, small_s = _token_stage_in(x_sample.reshape(ms, d_model), p)
    xc_d, nsa_dec, lf_dec = _regroup(
        page_table,
        cache_nsa_kv.reshape(cache_nsa_kv.shape[1], PAGE_SIZE * N_CACHE_COLS, HEAD_DIM),
        cache_fox_logf[0], lk)
    hm_s4 = hm_s.reshape(N_HEAD_COLS, nbd, dseq, HEAD_DIM)
    nsa_dec = lax.dynamic_update_slice(nsa_dec, hm_s4[HM_K_SLC:HM_K_WIN], (0, 0, past, 0))
    logf_s = small_s[:, N_GATE_COLS:N_GATE_COLS + FOX_HEADS].reshape(nbd, dseq, FOX_HEADS)
    lf_dec = lax.dynamic_update_slice(lf_dec, logf_s, (0, past, 0))

    csum_d = _cumsum(lf_dec.transpose(0, 2, 1))
    n_cols = FOX_HEADS * dseq
    lane_pad = lambda a: jnp.pad(a, [(0, 0)] * (a.ndim - 1) + [(0, LANE - n_cols)])
    head_eye = jnp.eye(FOX_HEADS, dtype=BF16)
    qbd = jnp.einsum("hbtd,hg->bhdgt", hm_s4[HM_Q_FOX:HM_Q_FOX + FOX_HEADS], head_eye)
    qbd = lane_pad(qbd.reshape(nbd, FOX_HEADS * HEAD_DIM, n_cols))
    ccols = lane_pad(jnp.repeat(csum_d.transpose(0, 2, 1), dseq, axis=2))
    cq = lane_pad(csum_d[:, :, past:past + dseq].reshape(nbd, 1, n_cols))

    def new_rows(head0):
        rows = hm_s4[head0:head0 + FOX_HEADS].transpose(1, 2, 0, 3).reshape(nbd, dseq, FOX_HEADS * HEAD_DIM)
        return jnp.pad(rows, ((0, 0), (0, LANE - dseq), (0, 0)))

    o_full = _fox_decode(page_table, cache_fox_kv.reshape(cache_fox_kv.shape[1], PAGE_SIZE * N_CACHE_COLS, HEAD_DIM),
                         qbd, ccols, cq, new_rows(HM_K_FOX), new_rows(HM_V_FOX), dseq)
    o_fox_s = jnp.concatenate([o_full[:, h * dseq:(h + 1) * dseq, h * HEAD_DIM:(h + 1) * HEAD_DIM]
                               for h in range(FOX_HEADS)], axis=2).reshape(ms, FOX_HEADS * HEAD_DIM)

    n_chunk_d = past // CMP_STRIDE
    n_cmp_d = (past + dseq - CMP_LEN) // CMP_STRIDE + 1
    n_sel_d = -(-(past + dseq) // SEL_BLOCK)
    assert n_cmp_d + CMP_LEN // CMP_STRIDE - 1 <= n_chunk_d, "compressed blocks must lie in the cached rows"
    kcv_d = _compress(xc_d, 0, w1cat, w2cat, pecat, k_norm_row, n_chunk_d)
    mselt_d = _cmp_to_sel(n_cmp_d, n_sel_d, n_chunk_d, lk // SEL_BLOCK).T
    win_old = state_win_kv[0].transpose(2, 3, 0, 1, 4).reshape(2 * NSA_KV_HEADS, nbd, WINDOW, HEAD_DIM).astype(BF16)
    win_dec = jnp.concatenate([win_old, hm_s4[HM_K_WIN:HM_V_WIN + NSA_KV_HEADS],
                               jnp.zeros((2 * NSA_KV_HEADS, nbd, LANE - dseq, HEAD_DIM), BF16)], axis=2)
    grp_eye = jnp.eye(NSA_KV_HEADS, dtype=BF16)
    q_grp = hm_s4[HM_Q_NSA:HM_Q_NSA + NSA_HEADS].reshape(NSA_KV_HEADS, NSA_GROUP, nbd, dseq, HEAD_DIM)
    qbd_n = jnp.einsum("gjbtd,gk->bgdkjt", q_grp, grp_eye)
    qbd_n = lane_pad(qbd_n.reshape(nbd, NSA_KV_HEADS * HEAD_DIM, n_cols))
    tcols = lane_pad(jnp.repeat(rel_table, dseq, axis=1))
    col_id = np.arange(n_cols)
    same = ((col_id[:, None] // (NSA_GROUP * dseq) == col_id[None, :] // (NSA_GROUP * dseq))
            & (col_id[:, None] % dseq == col_id[None, :] % dseq))
    pair = jnp.asarray(np.pad(same.astype(np.float32), ((0, LANE - n_cols), (0, LANE - n_cols))))
    gcols = small_s[:, :N_GATE_COLS].reshape(nbd, dseq, NSA_HEADS, 3).transpose(0, 3, 2, 1).reshape(nbd, 3, n_cols)
    gcols = jnp.broadcast_to(lane_pad(gcols)[..., None], (nbd, 3, LANE, LANE))
    o_full_n = _nsa_decode(qbd_n, kcv_d, mselt_d, pair, tcols, nsa_dec, win_dec, gcols,
                           dseq=dseq, past=past, n_sel=n_sel_d)
    o_nsa_s = jnp.concatenate(
        [o_full_n[:, h * dseq:(h + 1) * dseq, (h // NSA_GROUP) * HEAD_DIM:(h // NSA_GROUP + 1) * HEAD_DIM]
         for h in range(NSA_HEADS)], axis=2).reshape(ms, NSA_HEADS * HEAD_DIM)
    y_s = _token_stage_out(xs1, o_nsa_s, o_fox_s, p)

    kvh = (NSA_KV_HEADS, HEAD_DIM)
    win_keep = min(WINDOW, seq)
    win_p = win_rows.reshape(nbp, seq, 2, *kvh)[:, seq - win_keep:]
    win_s = jnp.concatenate([state_win_kv[0], win_rows_s.reshape(nbd, dseq, 2, *kvh)], axis=1)[:, dseq:]
    return (
        y_p.reshape(nbp, seq, d_model),
        y_s.reshape(nbd, dseq, d_model),
        nsa_rows.reshape(1, nbp, seq, 4, *kvh),
        fox_rows.reshape(1, nbp, seq, 2, FOX_HEADS, HEAD_DIM),
        logf.reshape(1, nbp, seq, FOX_HEADS),
        win_p[None],
        nsa_rows_s.reshape(1, nbd, dseq, 4, *kvh),
        fox_rows_s.reshape(1, nbd, dseq, 2, FOX_HEADS, HEAD_DIM),
        logf_s[None],
        win_s[None],
    )
```
